```python
import math
import jax
import jax.numpy as jnp
from jax import lax
import numpy as np

D_MODEL = 2048
BATCH = 8
SEQ = 2048
DEPTH = 2

N_EVEN = (DEPTH + 1) // 2
N_ODD = DEPTH // 2
EPS = 1e-6

HG_WIDTH = D_MODEL // 2
HG_HEAD_DIM = 128
HG_HEADS = HG_WIDTH // HG_HEAD_DIM
HG_CHUNK = 64

S5_WIDTH = D_MODEL // 2
S5_GROUP = 16
S5_GROUPS = S5_WIDTH // S5_GROUP
S5_STATE = 64

RET_WIDTH = (3 * D_MODEL) // 4
RET_HEAD_DIM = 256
RET_HEADS = RET_WIDTH // RET_HEAD_DIM
RET_CHUNK = 128
ROPE_BASE = 10000.0

FNET_WIDTH = D_MODEL // 4
FNET_GROUPS = 4
FNET_GROUP = FNET_WIDTH // FNET_GROUPS

N_EXPERTS = 32
TOP_K = 4
D_FF = D_MODEL
SWIGLU_LIMIT = 7.0
SWIGLU_ALPHA = 1.702
MOE_BLOCK = 128

AB_IN = 5 * HG_WIDTH + S5_WIDTH
AB_SPLITS = [HG_WIDTH, 2 * HG_WIDTH, 3 * HG_WIDTH, 4 * HG_WIDTH, 5 * HG_WIDTH]
CD_IN = 4 * RET_WIDTH + FNET_WIDTH
CD_SPLITS = [RET_WIDTH, 2 * RET_WIDTH, 3 * RET_WIDTH, 4 * RET_WIDTH]

kernel_name = 'hybrid_bidir_hgrn2_s5_retnet_fnet_moe'


def rms_norm(x, gain):
    xf = x.astype(jnp.float32)
    y = xf * lax.rsqrt(jnp.mean(xf * xf, axis=-1, keepdims=True) + EPS)
    return (y * gain.astype(jnp.float32)).astype(x.dtype)


def modulate(h, shift, scale):
    return h * (1.0 + scale[:, None, :]) + shift[:, None, :]


def rotary(t, cos, sin):
    t1, t2 = jnp.split(t, 2, axis=-1)
    return jnp.concatenate([t1 * cos - t2 * sin, t1 * sin + t2 * cos], axis=-1)


def hgrn2_scan(q, k, v, log_f, inclusive):
    bsz, seq, heads, dk = q.shape
    dv = v.shape[-1]
    n_chunks = seq // HG_CHUNK

    def to_chunks(t):
        return t.reshape(bsz, n_chunks, HG_CHUNK, heads, t.shape[-1]).transpose(1, 0, 3, 2, 4)

    pos = jnp.arange(HG_CHUNK)
    mask = (pos[:, None] >= pos[None, :]) if inclusive else (pos[:, None] > pos[None, :])
    mask = mask[None, None, :, :, None]

    def step(state, chunk):
        qc, kc, vc, fc = chunk
        cum = jnp.cumsum(fc, axis=2)
        diff = cum[:, :, :, None, :] - cum[:, :, None, :, :]
        decay = jnp.where(mask, jnp.exp(jnp.minimum(diff, 0.0)), 0.0)
        scores = jnp.einsum('bhtd,bhsd,bhtsd->bhts', qc, kc, decay)
        out = (jnp.einsum('bhts,bhse->bhte', scores, vc)
               + jnp.einsum('bhtd,bhde->bhte', qc * jnp.exp(cum), state))
        last = cum[:, :, -1, :]
        new_state = (jnp.exp(last)[..., None] * state
                     + jnp.einsum('bhsd,bhse->bhde', kc * jnp.exp(last[:, :, None, :] - cum), vc))
        return new_state, out

    state0 = jnp.zeros((bsz, heads, dk, dv), q.dtype)
    _, out = lax.scan(step, state0, (to_chunks(q), to_chunks(k), to_chunks(v), to_chunks(log_f)))
    return out.transpose(1, 0, 3, 2, 4).reshape(bsz, seq, heads, dv)


def s5_scan(u, lam_re, lam_im, log_dt, b_re, b_im, c_re, c_im, reverse):
    lam_re, lam_im, log_dt, b_re, b_im, c_re, c_im = (
        t.astype(jnp.float32) for t in (lam_re, lam_im, log_dt, b_re, b_im, c_re, c_im))
    dt = jnp.exp(log_dt)[:, None]
    mag = jnp.exp(lam_re * dt)
    abar_re = mag * jnp.cos(lam_im * dt)
    abar_im = mag * jnp.sin(lam_im * dt)
    den = lam_re * lam_re + lam_im * lam_im
    num_re = abar_re - 1.0
    coef_re = (num_re * lam_re + abar_im * lam_im) / den
    coef_im = (abar_im * lam_re - num_re * lam_im) / den
    bbar_re = coef_re[..., None] * b_re - coef_im[..., None] * b_im
    bbar_im = coef_re[..., None] * b_im + coef_im[..., None] * b_re
    bu_re = jnp.einsum('bsgc,gpc->bsgp', u, bbar_re)
    bu_im = jnp.einsum('bsgc,gpc->bsgp', u, bbar_im)
    seq = u.shape[1]
    a_re = jnp.broadcast_to(abar_re, (1, seq) + abar_re.shape)
    a_im = jnp.broadcast_to(abar_im, (1, seq) + abar_im.shape)

    def combine(left, right):
        a1r, a1i, b1r, b1i = left
        a2r, a2i, b2r, b2i = right
        return (a2r * a1r - a2i * a1i, a2r * a1i + a2i * a1r,
                a2r * b1r - a2i * b1i + b2r, a2r * b1i + a2i * b1r + b2i)

    _, _, x_re, x_im = lax.associative_scan(combine, (a_re, a_im, bu_re, bu_im), reverse=reverse, axis=1)
    return jnp.einsum('bsgp,gcp->bsgc', x_re, c_re) - jnp.einsum('bsgp,gcp->bsgc', x_im, c_im)


def retention_scan(q, k, v, log_gamma, inclusive):
    bsz, seq, heads, dk = q.shape
    dv = v.shape[-1]
    n_chunks = seq // RET_CHUNK
    qc = q.reshape(bsz, n_chunks, RET_CHUNK, heads, dk)
    kc = k.reshape(bsz, n_chunks, RET_CHUNK, heads, dk)
    vc = v.reshape(bsz, n_chunks, RET_CHUNK, heads, dv)
    pos = jnp.arange(RET_CHUNK, dtype=jnp.float32)
    rel = pos[:, None] - pos[None, :]
    mask = (rel >= 0) if inclusive else (rel > 0)
    d_intra = jnp.where(mask[None], jnp.exp(log_gamma[:, None, None] * jnp.maximum(rel, 0.0)[None]), 0.0)
    scores = jnp.einsum('bcthd,bcshd->bchts', qc, kc) * d_intra[None, None]
    o_intra = jnp.einsum('bchts,bcshe->bcthe', scores, vc)
    zeta = jnp.exp(log_gamma[:, None] * (RET_CHUNK - 1.0 - pos)[None, :])
    local = jnp.einsum('bcshd,hs,bcshe->cbhde', kc, zeta, vc)
    chunk_decay = jnp.exp(log_gamma * RET_CHUNK)[None, :, None, None]

    def step(state, loc):
        return state * chunk_decay + loc, state

    _, r_in = lax.scan(step, jnp.zeros((bsz, heads, dk, dv), q.dtype), local)
    xi = jnp.exp(log_gamma[:, None] * (pos + 1.0)[None, :])
    o_inter = jnp.einsum('bcthd,ht,cbhde->bcthe', qc, xi, r_in)
    return (o_intra + o_inter).reshape(bsz, seq, heads, dv)


def mixer_hgrn2_s5(h, w_in, w_out, lower_bound, hg_gain, lam_re, lam_im, log_dt,
                   b_re, b_im, c_re, c_im, d_skip, glu_w, glu_b):
    bsz, seq, _ = h.shape
    proj = (h @ w_in).astype(jnp.float32)
    q, zf, zb, v, g, u = jnp.split(proj, AB_SPLITS, axis=-1)

    def heads(t):
        return t.reshape(bsz, seq, HG_HEADS, HG_HEAD_DIM)

    lb = lower_bound.astype(jnp.float32).reshape(HG_HEADS, HG_HEAD_DIM)
    f_fwd = lb + (1.0 - lb) * jax.nn.sigmoid(heads(zf))
    f_bwd = jnp.flip(lb + (1.0 - lb) * jax.nn.sigmoid(heads(zb)), 1)
    q, v = heads(q), heads(v)
    o_f = hgrn2_scan(q, 1.0 - f_fwd, v, jnp.log(f_fwd), True)
    o_b = jnp.flip(hgrn2_scan(jnp.flip(q, 1), 1.0 - f_bwd, jnp.flip(v, 1), jnp.log(f_bwd), False), 1)
    hg_out = rms_norm(o_f + o_b, hg_gain).reshape(bsz, seq, HG_WIDTH) * jax.nn.silu(g)

    uu = u.reshape(bsz, seq, S5_GROUPS, S5_GROUP)
    y = (s5_scan(uu, lam_re[0], lam_im[0], log_dt[0], b_re, b_im, c_re[0], c_im[0], False)
         + s5_scan(uu, lam_re[1], lam_im[1], log_dt[1], b_re, b_im, c_re[1], c_im[1], True)
         + d_skip * uu)
    z = jax.nn.gelu(y.reshape(bsz, seq, S5_WIDTH))
    s5_out = z * jax.nn.sigmoid(z @ glu_w + glu_b)

    mixed = jnp.concatenate([hg_out, s5_out], axis=-1).astype(h.dtype)
    return mixed @ w_out


def mixer_retention_fnet(h, w_in, w_out, ret_gain):
    bsz, seq, _ = h.shape
    proj = (h @ w_in).astype(jnp.float32)
    q, k, v, g, fu = jnp.split(proj, CD_SPLITS, axis=-1)

    def heads(t):
        return t.reshape(bsz, seq, RET_HEADS, RET_HEAD_DIM)

    inv_freq = ROPE_BASE ** (-jnp.arange(0, RET_HEAD_DIM, 2, dtype=jnp.float32) / RET_HEAD_DIM)
    ang = jnp.arange(seq, dtype=jnp.float32)[:, None] * inv_freq[None, :]
    cos, sin = jnp.cos(ang)[:, None, :], jnp.sin(ang)[:, None, :]
    q = rotary(heads(q), cos, sin)
    k = rotary(heads(k), cos, sin) * (RET_HEAD_DIM ** -0.5)
    v = heads(v)
    log_gamma = jnp.log1p(-jnp.exp2(-5.0 - jnp.arange(RET_HEADS, dtype=jnp.float32)))
    o_f = retention_scan(q, k, v, log_gamma, True)
    o_b = jnp.flip(retention_scan(jnp.flip(q, 1), jnp.flip(k, 1), jnp.flip(v, 1), log_gamma[::-1], False), 1)
    ret_out = rms_norm(o_f + o_b, ret_gain.reshape(RET_HEADS, RET_HEAD_DIM)).reshape(bsz, seq, RET_WIDTH) * jax.nn.silu(g)

    spec = jnp.fft.fft2(fu.reshape(bsz, seq, FNET_GROUPS, FNET_GROUP), axes=(1, 3), norm='ortho')
    fnet_out = jnp.real(spec).reshape(bsz, seq, FNET_WIDTH)

    mixed = jnp.concatenate([ret_out, fnet_out], axis=-1).astype(h.dtype)
    return mixed @ w_out


def moe_ffn(h, router_w, router_b, w_gu, b_gu, w_dn, b_dn):
    bsz, seq, dm = h.shape
    n_tok = bsz * seq
    xt = h.reshape(n_tok, dm)
    logits = (xt @ router_w + router_b).astype(jnp.float32)
    top_val, top_idx = lax.top_k(logits, TOP_K)
    gates = jax.nn.softmax(top_val, axis=-1).astype(h.dtype)

    n_assign = n_tok * TOP_K
    flat_e = top_idx.reshape(-1)
    flat_tok = jnp.arange(n_assign, dtype=jnp.int32) // TOP_K
    order = jnp.argsort(flat_e)
    e_sorted = flat_e[order]
    counts = jnp.bincount(flat_e, length=N_EXPERTS)
    padded = (counts + MOE_BLOCK - 1) // MOE_BLOCK * MOE_BLOCK
    starts = jnp.cumsum(counts) - counts
    pad_ends = jnp.cumsum(padded)
    pad_starts = pad_ends - padded
    dest = pad_starts[e_sorted] + jnp.arange(n_assign, dtype=jnp.int32) - starts[e_sorted]
    n_blocks = -(-n_assign // MOE_BLOCK) + N_EXPERTS
    n_slots = n_blocks * MOE_BLOCK
    slot_tok = jnp.full((n_slots,), n_tok, jnp.int32).at[dest].set(flat_tok[order])
    slot_gate = jnp.zeros((n_slots,), h.dtype).at[dest].set(gates.reshape(-1)[order])
    block_start = jnp.arange(n_blocks, dtype=jnp.int32) * MOE_BLOCK
    block_expert = jnp.minimum(jnp.searchsorted(pad_ends, block_start, side='right'), N_EXPERTS - 1)
    x_pad = jnp.concatenate([xt, jnp.zeros((1, dm), xt.dtype)], axis=0)

    def run_block(args):
        toks, e = args
        gu = x_pad[toks] @ w_gu[e] + b_gu[e]
        gate, up = jnp.split(gu, 2, axis=-1)
        gate = jnp.minimum(gate, SWIGLU_LIMIT)
        up = jnp.clip(up, -SWIGLU_LIMIT, SWIGLU_LIMIT)
        act = (up + 1.0) * gate * jax.nn.sigmoid(SWIGLU_ALPHA * gate)
        return act @ w_dn[e] + b_dn[e]

    y = lax.map(run_block, (slot_tok.reshape(n_blocks, MOE_BLOCK), block_expert))
    y = y.reshape(n_slots, dm) * slot_gate[:, None]
    out = jnp.zeros((n_tok + 1, dm), h.dtype).at[slot_tok].add(y)[:n_tok]
    return out.reshape(bsz, seq, dm)


def setup_inputs(seed: int = 0) -> dict:
    key = jax.random.key(seed)
    keys = iter(jax.random.split(key, 48))

    def normal(shape, scale):
        return scale * jax.random.normal(next(keys), shape, jnp.float32)

    def gain(shape):
        return 1.0 + 0.05 * jax.random.normal(next(keys), shape, jnp.float32)

    state_idx = jnp.arange(S5_STATE, dtype=jnp.float32)
    return {
        'x': normal((BATCH, SEQ, D_MODEL), 1.0),
        'c': normal((BATCH, D_MODEL), 1.0),
        'ada_w': normal((DEPTH, D_MODEL, 6 * D_MODEL), 0.5 * D_MODEL ** -0.5),
        'ada_b': normal((DEPTH, 6 * D_MODEL), 0.02),
        'norm_mix_gain': gain((DEPTH, D_MODEL)),
        'norm_ffn_gain': gain((DEPTH, D_MODEL)),
        'ab_w_in': normal((N_EVEN, D_MODEL, AB_IN), D_MODEL ** -0.5),
        'ab_w_out': normal((N_EVEN, HG_WIDTH + S5_WIDTH, D_MODEL), (HG_WIDTH + S5_WIDTH) ** -0.5),
        'hg_lb_logits': normal((N_EVEN + 1, HG_WIDTH), 0.5),
        'hg_norm_gain': gain((N_EVEN, HG_HEAD_DIM)),
        's5_lam_re': -0.5 + normal((N_EVEN, 2, S5_GROUPS, S5_STATE), 0.01),
        's5_lam_im': math.pi * state_idx + normal((N_EVEN, 2, S5_GROUPS, S5_STATE), 0.01),
        's5_log_dt': jax.random.uniform(next(keys), (N_EVEN, 2, S5_GROUPS), jnp.float32,
                                        math.log(1e-3), math.log(1e-1)),
        's5_b_re': normal((N_EVEN, S5_GROUPS, S5_STATE, S5_GROUP), (2 * S5_GROUP) ** -0.5),
        's5_b_im': normal((N_EVEN, S5_GROUPS, S5_STATE, S5_GROUP), (2 * S5_GROUP) ** -0.5),
        's5_c_re': normal((N_EVEN, 2, S5_GROUPS, S5_GROUP, S5_STATE), 0.5),
        's5_c_im': normal((N_EVEN, 2, S5_GROUPS, S5_GROUP, S5_STATE), 0.5),
        's5_d': normal((N_EVEN, S5_GROUPS, S5_GROUP), 0.5),
        's5_glu_w': normal((N_EVEN, S5_WIDTH, S5_WIDTH), S5_WIDTH ** -0.5),
        's5_glu_b': normal((N_EVEN, S5_WIDTH), 0.02),
        'cd_w_in': normal((N_ODD, D_MODEL, CD_IN), D_MODEL ** -0.5),
        'cd_w_out': normal((N_ODD, RET_WIDTH + FNET_WIDTH, D_MODEL), (RET_WIDTH + FNET_WIDTH) ** -0.5),
        'ret_norm_gain': gain((N_ODD, RET_WIDTH)),
        'router_w': normal((DEPTH, D_MODEL, N_EXPERTS), D_MODEL ** -0.5),
        'router_b': normal((DEPTH, N_EXPERTS), 0.01),
        'moe_w_gu': normal((DEPTH, N_EXPERTS, D_MODEL, 2 * D_FF), D_MODEL ** -0.5),
        'moe_b_gu': normal((DEPTH, N_EXPERTS, 2 * D_FF), 0.01),
        'moe_w_dn': normal((DEPTH, N_EXPERTS, D_FF, D_MODEL), D_FF ** -0.5),
        'moe_b_dn': normal((DEPTH, N_EXPERTS, D_MODEL), 0.01),
        'final_gain': gain((D_MODEL,)),
    }


def reference(x, c, ada_w, ada_b, norm_mix_gain, norm_ffn_gain, ab_w_in, ab_w_out,
              hg_lb_logits, hg_norm_gain, s5_lam_re, s5_lam_im, s5_log_dt, s5_b_re, s5_b_im,
              s5_c_re, s5_c_im, s5_d, s5_glu_w, s5_glu_b, cd_w_in, cd_w_out, ret_norm_gain,
              router_w, router_b, moe_w_gu, moe_b_gu, moe_w_dn, moe_b_dn, final_gain):
    lower_bounds = jnp.cumsum(jax.nn.softmax(hg_lb_logits.astype(jnp.float32), axis=0), axis=0)
    cond = jax.nn.silu(c)
    for layer in range(DEPTH):
        mod = cond @ ada_w[layer] + ada_b[layer]
        sh_mix, sc_mix, g_mix, sh_ffn, sc_ffn, g_ffn = jnp.split(mod, 6, axis=-1)
        h = modulate(rms_norm(x, norm_mix_gain[layer]), sh_mix, sc_mix)
        j = layer // 2
        if layer % 2 == 0:
            y = mixer_hgrn2_s5(h, ab_w_in[j], ab_w_out[j], lower_bounds[j], hg_norm_gain[j],
                               s5_lam_re[j], s5_lam_im[j], s5_log_dt[j], s5_b_re[j], s5_b_im[j],
                               s5_c_re[j], s5_c_im[j], s5_d[j], s5_glu_w[j], s5_glu_b[j])
        else:
            y = mixer_retention_fnet(h, cd_w_in[j], cd_w_out[j], ret_norm_gain[j])
        x = x + g_mix[:, None, :] * y
        h = modulate(rms_norm(x, norm_ffn_gain[layer]), sh_ffn, sc_ffn)
        x = x + g_ffn[:, None, :] * moe_ffn(h, router_w[layer], router_b[layer], moe_w_gu[layer],
                                            moe_b_gu[layer], moe_w_dn[layer], moe_b_dn[layer])
    return rms_norm(x, final_gain)
```

```python
import functools
import math

import jax
import jax.numpy as jnp
from jax import lax
from jax.experimental import pallas as pl
from jax.experimental.pallas import tpu as pltpu

F32 = jnp.float32
BF16 = jnp.bfloat16
I32 = jnp.int32

EPS = 1e-6
LANES = 128
SUBLANES = 8
VMEM_LIMIT = 56 * 1024 * 1024

HG_HEAD_DIM = 128
HG_CHUNK = 64
HG_EXP_CLAMP = 80.0

S5_GROUP = 16
S5_STATE = 64
S5_TILE_GROUPS = 16
S5_TIME_BLOCK = 512
S5_SCAN_BLOCK = SUBLANES * SUBLANES

RET_HEAD_DIM = 256
RET_Q_TILE = 256
ROPE_BASE = 10000.0

FNET_GROUPS = 4
FNET_ROW_TILE = 512

N_EXPERTS = 32
TOP_K = 4
SWIGLU_LIMIT = 7.0
SWIGLU_ALPHA = 1.702
MOE_ROWS = 256
MOE_UP_TN = 1024
ROUTER_PAD = LANES
NEG_BIG = -1e30


def _cparams(semantics):
    return pltpu.CompilerParams(dimension_semantics=semantics, vmem_limit_bytes=VMEM_LIMIT)


def _ada_kernel(c_ref, w_ref, b_ref, o_ref):
    c = c_ref[...]
    cond = c * jax.nn.sigmoid(c)
    o_ref[0] = jnp.dot(cond.astype(BF16), w_ref[0].astype(BF16),
                       preferred_element_type=F32) + b_ref[0]


def _ada_mod(c, ada_w, ada_b):
    depth, d, n = ada_w.shape
    bsz = c.shape[0]
    tn = 1024
    return pl.pallas_call(
        _ada_kernel,
        grid=(depth, n // tn),
        in_specs=[
            pl.BlockSpec((bsz, d), lambda l, j: (0, 0)),
            pl.BlockSpec((1, d, tn), lambda l, j: (l, 0, j)),
            pl.BlockSpec((1, 1, tn), lambda l, j: (l, 0, j)),
        ],
        out_specs=pl.BlockSpec((1, bsz, tn), lambda l, j: (l, 0, j)),
        out_shape=jax.ShapeDtypeStruct((depth, bsz, n), F32),
        compiler_params=_cparams(("arbitrary", "arbitrary")),
        name="ada_mod",
    )(c, ada_w, ada_b.reshape(depth, 1, n))


def _norm_modulate(x, gain, shift, scale):
    ms = jnp.mean(x * x, axis=-1, keepdims=True)
    y = x * lax.rsqrt(ms + EPS) * gain
    return y * (1.0 + scale) + shift


def _inproj_kernel(x_ref, gain_ref, sh_ref, sc_ref, w_ref, o_ref, h_ref):
    @pl.when(pl.program_id(1) == 0)
    def _():
        h_ref[...] = _norm_modulate(x_ref[...], gain_ref[...], sh_ref[0], sc_ref[0]).astype(BF16)

    o_ref[...] = jnp.dot(h_ref[...], w_ref[...], preferred_element_type=F32)


def _in_proj(x2d, gain, shift, scale, w_bf16, seq):
    t, d = x2d.shape
    n = w_bf16.shape[1]
    tm, tn = 1024, 512
    per_seq = seq // tm
    return pl.pallas_call(
        _inproj_kernel,
        grid=(t // tm, n // tn),
        in_specs=[
            pl.BlockSpec((tm, d), lambda i, j: (i, 0)),
            pl.BlockSpec((1, d), lambda i, j: (0, 0)),
            pl.BlockSpec((1, 1, d), lambda i, j: (i // per_seq, 0, 0)),
            pl.BlockSpec((1, 1, d), lambda i, j: (i // per_seq, 0, 0)),
            pl.BlockSpec((d, tn), lambda i, j: (0, j)),
        ],
        out_specs=pl.BlockSpec((tm, tn), lambda i, j: (i, j)),
        out_shape=jax.ShapeDtypeStruct((t, n), F32),
        scratch_shapes=[pltpu.VMEM((tm, d), BF16)],
        compiler_params=_cparams(("arbitrary", "arbitrary")),
        name="in_proj",
    )(x2d, gain.reshape(1, d), shift, scale, w_bf16)


def _split3(a):
    hi = a.astype(BF16)
    r1 = a - hi.astype(F32)
    mid = r1.astype(BF16)
    lo = (r1 - mid.astype(F32)).astype(BF16)
    return hi, mid, lo


def _tri_sum(tri, a):
    hi, mid, lo = _split3(a)
    return (jnp.dot(tri, hi, preferred_element_type=F32)
            + jnp.dot(tri, mid, preferred_element_type=F32)
            + jnp.dot(tri, lo, preferred_element_type=F32))


def _dot_nt(a, b):
    return lax.dot_general(a, b, (((1,), (1,)), ((), ())), preferred_element_type=F32)


def _dot_tn(a, b):
    return lax.dot_general(a, b, (((0,), (0,)), ((), ())), preferred_element_type=F32)


def _hgrn2_kernel(q_ref, zf_ref, zb_ref, v_ref, g_ref, lb_ref, gain_ref, o_ref, acc_ref):
    seq = q_ref.shape[1]
    ln = HG_CHUNK
    n_chunks = seq // ln
    lb = lb_ref[...]
    gain = gain_ref[...]
    row = lax.broadcasted_iota(I32, (ln, ln), 0)
    col = lax.broadcasted_iota(I32, (ln, ln), 1)
    lower_incl = col <= row
    upper_strict = col > row
    tri_prefix = lower_incl.astype(BF16)
    tri_suffix = (col >= row).astype(BF16)
    mid = ln // 2

    def chunk(c, state_t, z_ref, forward):
        sl = pl.ds(pl.multiple_of(c * ln, ln), ln)
        q = q_ref[0, sl, :]
        v = v_ref[0, sl, :].astype(BF16)
        f = lb + (1.0 - lb) * jax.nn.sigmoid(z_ref[0, sl, :])
        log_f = jnp.log(f)
        k = 1.0 - f
        if forward:
            cum = _tri_sum(tri_prefix, log_f)
            ref_row = cum[mid - 1:mid, :]
            edge = cum[ln - 1:ln, :]
            mask = lower_incl
        else:
            cum = _tri_sum(tri_suffix, log_f)
            ref_row = cum[mid:mid + 1, :]
            edge = cum[0:1, :]
            mask = upper_strict
        qe = q * jnp.exp(jnp.minimum(cum - ref_row, HG_EXP_CLAMP))
        ke = k * jnp.exp(jnp.minimum(ref_row - cum, HG_EXP_CLAMP))
        scores = jnp.where(mask, _dot_nt(qe.astype(BF16), ke.astype(BF16)), 0.0)
        out = (jnp.dot(scores.astype(BF16), v, preferred_element_type=F32)
               + _dot_nt((q * jnp.exp(cum)).astype(BF16), state_t.astype(BF16)))
        kd = (k * jnp.exp(edge - cum)).astype(BF16)
        new_state_t = state_t * jnp.exp(edge) + _dot_tn(v, kd)
        return sl, out, new_state_t

    def fwd_body(c, state_t):
        sl, out, new_state_t = chunk(c, state_t, zf_ref, True)
        acc_ref[sl, :] = out
        return new_state_t

    def bwd_body(i, state_t):
        sl, out, new_state_t = chunk(n_chunks - 1 - i, state_t, zb_ref, False)
        o = acc_ref[sl, :] + out
        y = o * lax.rsqrt(jnp.mean(o * o, axis=-1, keepdims=True) + EPS) * gain
        g = g_ref[0, sl, :]
        o_ref[0, sl, :] = (y * (g * jax.nn.sigmoid(g))).astype(o_ref.dtype)
        return new_state_t

    zero = jnp.zeros((HG_HEAD_DIM, HG_HEAD_DIM), F32)
    lax.fori_loop(0, n_chunks, fwd_body, zero)
    lax.fori_loop(0, n_chunks, bwd_body, zero)


def _hgrn2(proj3, lower_bound, hg_gain, width):
    bsz, seq, _ = proj3.shape
    heads = width // HG_HEAD_DIM
    dh = HG_HEAD_DIM

    def col(k):
        return pl.BlockSpec((1, seq, dh), lambda b, h: (b, 0, k * heads + h))

    return pl.pallas_call(
        _hgrn2_kernel,
        grid=(bsz, heads),
        in_specs=[col(0), col(1), col(2), col(3), col(4),
                  pl.BlockSpec((1, dh), lambda b, h: (0, h)),
                  pl.BlockSpec((1, dh), lambda b, h: (0, 0))],
        out_specs=pl.BlockSpec((1, seq, dh), lambda b, h: (b, 0, h)),
        out_shape=jax.ShapeDtypeStruct((bsz, seq, width), BF16),
        scratch_shapes=[pltpu.VMEM((seq, dh), F32)],
        compiler_params=_cparams(("arbitrary", "arbitrary")),
        name="hgrn2",
    )(proj3, proj3, proj3, proj3, proj3, lower_bound.reshape(1, width), hg_gain.reshape(1, dh))


S5_TBL_STEP = 0
S5_TBL_X16 = 8
S5_TBL_X32 = 9
S5_TBL_CARRY = 10
S5_N_TBL = 11


def _cmul(ar, ai, br, bi):
    return ar * br - ai * bi, ar * bi + ai * br


def _s5_scan_block(bu_ref, carry_ref, pw_ref, direction, base, colblk, width):
    forward = direction == 0
    cre = pl.ds(colblk * LANES, LANES)
    cim = pl.ds(width + colblk * LANES, LANES)
    bre = colblk
    bim = width // LANES + colblk

    def tbl(idx):
        return pw_ref[direction, 0, 0, idx, :, cre], pw_ref[direction, 0, 1, idx, :, cre]

    def rows(k):
        return pl.ds(base + k, SUBLANES, stride=SUBLANES)

    a_r, a_i = tbl(S5_TBL_STEP)
    order = list(range(SUBLANES)) if forward else list(range(SUBLANES - 1, -1, -1))
    xr = [None] * SUBLANES
    xi = [None] * SUBLANES
    pr = pi = None
    for k in order:
        br = bu_ref.at[bre][rows(k), :]
        bi = bu_ref.at[bim][rows(k), :]
        if pr is not None:
            mr, mi = _cmul(a_r, a_i, pr, pi)
            br = br + mr
            bi = bi + mi
        xr[k], xi[k] = br, bi
        pr, pi = br, bi

    er, ei = pr, pi
    sub = lax.broadcasted_iota(I32, (SUBLANES, LANES), 0)
    for step, idx in ((1, S5_TBL_STEP + 7), (2, S5_TBL_X16), (4, S5_TBL_X32)):
        m_r, m_i = tbl(idx)
        if forward:
            shift, keep = step, sub >= step
        else:
            shift, keep = SUBLANES - step, sub < SUBLANES - step
        sr = jnp.where(keep, pltpu.roll(er, shift, 0), 0.0)
        si = jnp.where(keep, pltpu.roll(ei, shift, 0), 0.0)
        dr, di = _cmul(m_r, m_i, sr, si)
        er, ei = er + dr, ei + di
    c_r = carry_ref[:, cre]
    c_i = carry_ref[:, cim]
    t_r, t_i = tbl(S5_TBL_CARRY)
    dr, di = _cmul(t_r, t_i, c_r, c_i)
    er, ei = er + dr, ei + di

    if forward:
        nr = jnp.where(sub >= 1, pltpu.roll(er, 1, 0), c_r)
        ni = jnp.where(sub >= 1, pltpu.roll(ei, 1, 0), c_i)
        last = SUBLANES - 1
    else:
        nr = jnp.where(sub < SUBLANES - 1, pltpu.roll(er, SUBLANES - 1, 0), c_r)
        ni = jnp.where(sub < SUBLANES - 1, pltpu.roll(ei, SUBLANES - 1, 0), c_i)
        last = 0
    carry_ref[:, cre] = jnp.broadcast_to(er[last:last + 1, :], (SUBLANES, LANES))
    carry_ref[:, cim] = jnp.broadcast_to(ei[last:last + 1, :], (SUBLANES, LANES))

    for k in range(SUBLANES):
        f_r, f_i = tbl(S5_TBL_STEP + (k if forward else SUBLANES - 1 - k))
        dr, di = _cmul(f_r, f_i, nr, ni)
        bu_ref.at[bre][rows(k), :] = xr[k] + dr
        bu_ref.at[bim][rows(k), :] = xi[k] + di


def _s5_kernel(u_ref, bm_ref, cm_ref, pw_ref, d_ref, z_ref, bu_ref, y_ref, carry_ref):
    seq = u_ref.shape[1]
    n_cb = int(bu_ref.shape[0])
    width = n_cb * LANES // 2
    tb = S5_TIME_BLOCK
    n_tb = seq // tb
    n_sb = tb // S5_SCAN_BLOCK
    d_skip = d_ref[...]

    for direction in (0, 1):
        carry_ref[...] = jnp.zeros(carry_ref.shape, F32)

        def time_block(it, _, direction=direction):
            blk = it if direction == 0 else n_tb - 1 - it
            sl = pl.ds(pl.multiple_of(blk * tb, tb), tb)
            u = u_ref[0, sl, :]
            bu = jnp.dot(u.astype(BF16), bm_ref[direction, 0], preferred_element_type=F32)
            for cb in range(n_cb):
                bu_ref[cb] = bu[:, cb * LANES:(cb + 1) * LANES]

            def scan_block(js, _):
                sb = js if direction == 0 else n_sb - 1 - js
                base = pl.multiple_of(sb * S5_SCAN_BLOCK, S5_SCAN_BLOCK)
                for colblk in range(width // LANES):
                    _s5_scan_block(bu_ref, carry_ref, pw_ref, direction, base, colblk, width)
                return 0

            lax.fori_loop(0, n_sb, scan_block, 0)
            xs = jnp.concatenate([bu_ref[cb].astype(BF16) for cb in range(n_cb)], axis=-1)
            y = jnp.dot(xs, cm_ref[direction, 0], preferred_element_type=F32)
            if direction == 0:
                y_ref[sl, :] = y
            else:
                z_ref[0, sl, :] = jax.nn.gelu(y_ref[sl, :] + y + d_skip * u).astype(z_ref.dtype)
            return 0

        lax.fori_loop(0, n_tb, time_block, 0)


def _s5_tables(lam_re, lam_im, log_dt, b_re, b_im, c_re, c_im):
    groups, state = lam_re.shape[1], lam_re.shape[2]
    chans = b_re.shape[2]
    tg = S5_TILE_GROUPS
    tiles = groups // tg
    eye = jnp.eye(tg, dtype=F32)
    bms, cms, pws = [], [], []
    sub = jnp.arange(SUBLANES, dtype=F32)
    for direction in (0, 1):
        lr, li = lam_re[direction].astype(F32), lam_im[direction].astype(F32)
        dt = jnp.exp(log_dt[direction].astype(F32))[:, None]
        mag = jnp.exp(lr * dt)
        abar_re = mag * jnp.cos(li * dt)
        abar_im = mag * jnp.sin(li * dt)
        den = lr * lr + li * li
        num_re = abar_re - 1.0
        coef_re = (num_re * lr + abar_im * li) / den
        coef_im = (abar_im * lr - num_re * li) / den
        bbar_re = coef_re[..., None] * b_re - coef_im[..., None] * b_im
        bbar_im = coef_re[..., None] * b_im + coef_im[..., None] * b_re

        def blockdiag_in(bb):
            bb = bb.reshape(tiles, tg, state, chans)
            m = jnp.einsum('gh,tgpc->tgchp', eye, bb)
            return m.reshape(tiles, tg * chans, tg * state)

        def blockdiag_out(cc):
            cc = cc.reshape(tiles, tg, chans, state)
            m = jnp.einsum('gh,tgcp->tgphc', eye, cc)
            return m.reshape(tiles, tg * state, tg * chans)

        bms.append(jnp.concatenate([blockdiag_in(bbar_re), blockdiag_in(bbar_im)], axis=-1))
        cms.append(jnp.concatenate([blockdiag_out(c_re[direction].astype(F32)),
                                    -blockdiag_out(c_im[direction].astype(F32))], axis=1))

        def power(n):
            return jnp.exp(n * lr * dt) * jnp.cos(n * li * dt), jnp.exp(n * lr * dt) * jnp.sin(n * li * dt)

        rows_re, rows_im = [], []
        for n in list(range(1, SUBLANES + 1)) + [2 * SUBLANES, 4 * SUBLANES]:
            p_re, p_im = power(float(n))
            rows_re.append(jnp.broadcast_to(p_re.reshape(tiles, 1, tg * state), (tiles, SUBLANES, tg * state)))
            rows_im.append(jnp.broadcast_to(p_im.reshape(tiles, 1, tg * state), (tiles, SUBLANES, tg * state)))
        carry_n = SUBLANES * (sub + 1.0) if direction == 0 else SUBLANES * (SUBLANES - sub)
        p_re, p_im = power(carry_n[:, None, None])
        rows_re.append(p_re.reshape(SUBLANES, tiles, tg * state).transpose(1, 0, 2))
        rows_im.append(p_im.reshape(SUBLANES, tiles, tg * state).transpose(1, 0, 2))
        pws.append(jnp.stack([jnp.stack(rows_re, axis=1), jnp.stack(rows_im, axis=1)], axis=1))
    return (jnp.stack(bms).astype(BF16), jnp.stack(cms).astype(BF16), jnp.stack(pws))


def _s5(proj3, col0, tables, d_skip):
    bsz, seq, _ = proj3.shape
    bm, cm, pw = tables
    tiles = bm.shape[1]
    tc = bm.shape[2]
    sw = bm.shape[3]
    cb0 = col0 // tc
    return pl.pallas_call(
        _s5_kernel,
        grid=(bsz, tiles),
        in_specs=[
            pl.BlockSpec((1, seq, tc), lambda b, t: (b, 0, cb0 + t)),
            pl.BlockSpec((2, 1, tc, sw), lambda b, t: (0, t, 0, 0)),
            pl.BlockSpec((2, 1, sw, tc), lambda b, t: (0, t, 0, 0)),
            pl.BlockSpec((2, 1, 2, S5_N_TBL, SUBLANES, sw // 2), lambda b, t: (0, t, 0, 0, 0, 0)),
            pl.BlockSpec((1, tc), lambda b, t: (0, t)),
        ],
        out_specs=pl.BlockSpec((1, seq, tc), lambda b, t: (b, 0, t)),
        out_shape=jax.ShapeDtypeStruct((bsz, seq, tiles * tc), BF16),
        scratch_shapes=[pltpu.VMEM((sw // LANES, S5_TIME_BLOCK, LANES), F32),
                        pltpu.VMEM((seq, tc), F32),
                        pltpu.VMEM((SUBLANES, sw), F32)],
        compiler_params=_cparams(("arbitrary", "arbitrary")),
        name="s5",
    )(proj3, bm, cm, pw, d_skip.reshape(1, tiles * tc))


def _retention_kernel(q_ref, k_ref, v_ref, g_ref, cos_ref, sin_ref, lg_ref, gain_ref,
                      o_ref, qs_ref, ks_ref, vs_ref):
    seq = q_ref.shape[1]
    half = RET_HEAD_DIM // 2
    tq = RET_Q_TILE
    cos = cos_ref[...]
    sin = sin_ref[...]

    def rot(t_ref, scale):
        t1 = t_ref[0, :, :half]
        t2 = t_ref[0, :, half:]
        return jnp.concatenate([(t1 * cos - t2 * sin) * scale, (t1 * sin + t2 * cos) * scale], axis=-1)

    qs_ref[...] = rot(q_ref, 1.0).astype(BF16)
    ks_ref[...] = rot(k_ref, RET_HEAD_DIM ** -0.5).astype(BF16)
    vs_ref[...] = v_ref[0].astype(BF16)
    lg_fwd = lg_ref[0, 0:1, :]
    lg_bwd = lg_ref[0, 1:2, :]
    gain = gain_ref[...]

    def q_tile(i, _):
        sl = pl.ds(pl.multiple_of(i * tq, tq), tq)
        scores = _dot_nt(qs_ref[sl, :], ks_ref[...])
        t_idx = lax.broadcasted_iota(I32, (tq, seq), 0) + i * tq
        s_idx = lax.broadcasted_iota(I32, (tq, seq), 1)
        rel = (t_idx - s_idx).astype(F32)
        decay = jnp.exp(jnp.where(rel >= 0.0, lg_fwd * rel, -lg_bwd * rel))
        p = (scores * decay).astype(BF16)
        o = jnp.dot(p, vs_ref[...], preferred_element_type=F32)
        y = o * lax.rsqrt(jnp.mean(o * o, axis=-1, keepdims=True) + EPS) * gain
        g = g_ref[0, sl, :]
        o_ref[0, sl, :] = (y * (g * jax.nn.sigmoid(g))).astype(o_ref.dtype)
        return 0

    lax.fori_loop(0, seq // tq, q_tile, 0)


def _retention(proj3, ret_gain, width):
    bsz, seq, _ = proj3.shape
    dh = RET_HEAD_DIM
    heads = width // dh
    inv_freq = ROPE_BASE ** (-jnp.arange(0, dh, 2, dtype=F32) / dh)
    ang = jnp.arange(seq, dtype=F32)[:, None] * inv_freq[None, :]
    cos, sin = jnp.cos(ang), jnp.sin(ang)
    log_gamma = jnp.log1p(-jnp.exp2(-5.0 - jnp.arange(heads, dtype=F32)))
    lg = jnp.stack([log_gamma, log_gamma[::-1]], axis=1)
    lg = jnp.broadcast_to(lg[:, :, None], (heads, 2, seq))

    def col(k):
        return pl.BlockSpec((1, seq, dh), lambda b, h: (b, 0, k * heads + h))

    return pl.pallas_call(
        _retention_kernel,
        grid=(bsz, heads),
        in_specs=[col(0), col(1), col(2), col(3),
                  pl.BlockSpec((seq, dh // 2), lambda b, h: (0, 0)),
                  pl.BlockSpec((seq, dh // 2), lambda b, h: (0, 0)),
                  pl.BlockSpec((1, 2, seq), lambda b, h: (h, 0, 0)),
                  pl.BlockSpec((1, dh), lambda b, h: (0, h))],
        out_specs=pl.BlockSpec((1, seq, dh), lambda b, h: (b, 0, h)),
        out_shape=jax.ShapeDtypeStruct((bsz, seq, width), BF16),
        scratch_shapes=[pltpu.VMEM((seq, dh), BF16)] * 3,
        compiler_params=_cparams(("arbitrary", "arbitrary")),
        name="retention",
    )(proj3, proj3, proj3, proj3, cos, sin, lg, ret_gain.reshape(1, width))


def _fnet_kernel(x_ref, cc_ref, sc_ref, cs_ref, ss_ref, o_ref, a1_ref, a2_ref):
    @pl.when(pl.program_id(1) == 0)
    def _():
        x = x_ref[0].astype(BF16)
        a1_ref[...] = jnp.dot(x, cc_ref[...], preferred_element_type=F32).astype(BF16)
        a2_ref[...] = jnp.dot(x, sc_ref[...], preferred_element_type=F32).astype(BF16)

    y = (jnp.dot(cs_ref[...], a1_ref[...], preferred_element_type=F32)
         - jnp.dot(ss_ref[...], a2_ref[...], preferred_element_type=F32))
    o_ref[0] = y.astype(o_ref.dtype)


def _dft_mats(n):
    idx = jnp.arange(n, dtype=I32)
    ang = (2.0 * math.pi / n) * ((idx[:, None] * idx[None, :]) % n).astype(F32)
    scale = n ** -0.5
    return jnp.cos(ang) * scale, jnp.sin(ang) * scale


def _fnet(proj3, col0, width):
    bsz, seq, _ = proj3.shape
    gw = width // FNET_GROUPS
    cs, ss = _dft_mats(seq)
    cg, sg = _dft_mats(gw)
    eye = jnp.eye(FNET_GROUPS, dtype=F32)
    cc = jnp.kron(eye, cg)
    sc = jnp.kron(eye, sg)
    tr = FNET_ROW_TILE
    return pl.pallas_call(
        _fnet_kernel,
        grid=(bsz, seq // tr),
        in_specs=[
            pl.BlockSpec((1, seq, width), lambda b, i: (b, 0, col0 // width)),
            pl.BlockSpec((width, width), lambda b, i: (0, 0)),
            pl.BlockSpec((width, width), lambda b, i: (0, 0)),
            pl.BlockSpec((tr, seq), lambda b, i: (i, 0)),
            pl.BlockSpec((tr, seq), lambda b, i: (i, 0)),
        ],
        out_specs=pl.BlockSpec((1, tr, width), lambda b, i: (b, i, 0)),
        out_shape=jax.ShapeDtypeStruct((bsz, seq, width), BF16),
        scratch_shapes=[pltpu.VMEM((seq, width), BF16)] * 2,
        compiler_params=_cparams(("arbitrary", "arbitrary")),
        name="fnet",
    )(proj3, cc.astype(BF16), sc.astype(BF16), cs.astype(BF16), ss.astype(BF16))


def _outproj_kernel(*refs, glu):
    if glu:
        (x_ref, a_ref, b_ref, gw_ref, gb_ref, wa_ref, wb_ref, gm_ref, gain_ref, sh_ref, sc_ref,
         rwh_ref, rwl_ref, rb_ref, x1_ref, h_ref, idx_ref, gate_ref) = refs
    else:
        (x_ref, a_ref, b_ref, wa_ref, wb_ref, gm_ref, gain_ref, sh_ref, sc_ref,
         rwh_ref, rwl_ref, rb_ref, x1_ref, h_ref, idx_ref, gate_ref) = refs
    bm = b_ref[...]
    if glu:
        gl = jnp.dot(bm, gw_ref[...], preferred_element_type=F32) + gb_ref[...]
        bm = (bm.astype(F32) * jax.nn.sigmoid(gl)).astype(BF16)
    y = (jnp.dot(a_ref[...], wa_ref[...], preferred_element_type=F32)
         + jnp.dot(bm, wb_ref[...], preferred_element_type=F32))
    x1 = x_ref[...] + gm_ref[0] * y
    x1_ref[...] = x1
    h = _norm_modulate(x1, gain_ref[...], sh_ref[0], sc_ref[0])
    h_ref[...] = h

    h_hi = h.astype(BF16)
    h_lo = (h - h_hi.astype(F32)).astype(BF16)
    logits = (jnp.dot(h_hi, rwh_ref[...], preferred_element_type=F32)
              + jnp.dot(h_lo, rwh_ref[...], preferred_element_type=F32)
              + jnp.dot(h_hi, rwl_ref[...], preferred_element_type=F32)
              + rb_ref[...])
    lane = lax.broadcasted_iota(I32, logits.shape, 1)
    vals, idxs = [], []
    for _ in range(TOP_K):
        m = jnp.max(logits, axis=-1, keepdims=True)
        ik = jnp.min(jnp.where(logits == m, lane, ROUTER_PAD), axis=-1, keepdims=True)
        vals.append(m)
        idxs.append(ik)
        logits = jnp.where(lane == ik, -jnp.inf, logits)
    exps = [jnp.exp(v - vals[0]) for v in vals]
    denom = exps[0] + exps[1] + exps[2] + exps[3]
    idx_out = jnp.zeros(lane.shape, I32)
    gate_out = jnp.zeros(lane.shape, F32)
    for k in range(TOP_K):
        idx_out = jnp.where(lane == k, idxs[k], idx_out)
        gate_out = jnp.where(lane == k, exps[k] / denom, gate_out)
    idx_ref[...] = idx_out
    gate_ref[...] = gate_out


def _out_proj(x2d, mix_a, mix_b, w_out, g_mix, gain, shift, scale, router_w, router_b, seq,
              glu_w=None, glu_b=None):
    t, d = x2d.shape
    wa_rows = mix_a.shape[1]
    wb_rows = mix_b.shape[1]
    tm = 256
    per_seq = seq // tm
    n_exp = router_w.shape[1]
    rw = jnp.zeros((d, ROUTER_PAD), F32).at[:, :n_exp].set(router_w)
    rw_hi = rw.astype(BF16)
    rw_lo = (rw - rw_hi.astype(F32)).astype(BF16)
    rb = jnp.full((1, ROUTER_PAD), NEG_BIG, F32).at[0, :n_exp].set(router_b)
    w_bf = w_out.astype(BF16)
    glu = glu_w is not None

    def rows(width):
        return pl.BlockSpec((tm, width), lambda i: (i, 0))

    def full(r, c):
        return pl.BlockSpec((r, c), lambda i: (0, 0))

    def per_batch():
        return pl.BlockSpec((1, 1, d), lambda i: (i // per_seq, 0, 0))

    in_specs = [rows(d), rows(wa_rows), rows(wb_rows)]
    args = [x2d, mix_a, mix_b]
    if glu:
        in_specs += [full(wb_rows, wb_rows), full(1, wb_rows)]
        args += [glu_w.astype(BF16), glu_b.reshape(1, wb_rows)]
    in_specs += [pl.BlockSpec((wa_rows, d), lambda i: (0, 0)),
                 pl.BlockSpec((wb_rows, d), lambda i: (wa_rows // wb_rows, 0)),
                 per_batch(), full(1, d), per_batch(), per_batch(),
                 full(d, ROUTER_PAD), full(d, ROUTER_PAD), full(1, ROUTER_PAD)]
    args += [w_bf, w_bf, g_mix, gain.reshape(1, d), shift, scale, rw_hi, rw_lo, rb]
    return pl.pallas_call(
        functools.partial(_outproj_kernel, glu=glu),
        grid=(t // tm,),
        in_specs=in_specs,
        out_specs=[rows(d), rows(d), rows(ROUTER_PAD), rows(ROUTER_PAD)],
        out_shape=[jax.ShapeDtypeStruct((t, d), F32), jax.ShapeDtypeStruct((t, d), F32),
                   jax.ShapeDtypeStruct((t, ROUTER_PAD), I32), jax.ShapeDtypeStruct((t, ROUTER_PAD), F32)],
        compiler_params=_cparams(("arbitrary",)),
        name="out_proj_glu" if glu else "out_proj",
    )(*args)


def _routing_tables(top_idx, gates, n_tok):
    n_assign = n_tok * TOP_K
    r = MOE_ROWS
    n_rb = n_assign // r + N_EXPERTS
    n_slots = n_rb * r
    flat_e = top_idx.reshape(-1)
    onehot = (flat_e[:, None] == jnp.arange(N_EXPERTS, dtype=I32)[None, :]).astype(I32)
    csum = jnp.cumsum(onehot, axis=0)
    rank = jnp.take_along_axis(csum, flat_e[:, None], axis=1)[:, 0] - 1
    counts = csum[-1]
    nblk = (counts + r - 1) // r
    blk_end = jnp.cumsum(nblk)
    blk_start = blk_end - nblk
    dest = blk_start[flat_e] * r + rank
    assign = jnp.arange(n_assign, dtype=I32)
    slot_tok = jnp.zeros((n_slots,), I32).at[dest].set(assign // TOP_K)
    real_dst = (assign % TOP_K) * n_tok + assign // TOP_K
    is_real = jnp.zeros((n_slots,), I32).at[dest].set(1)
    pad_rank = jnp.cumsum(1 - is_real) - 1
    slot_dst = jnp.where(is_real == 1, jnp.zeros((n_slots,), I32).at[dest].set(real_dst),
                         n_assign + pad_rank).astype(I32)
    slot_gate = jnp.zeros((n_slots,), F32).at[dest].set(gates.reshape(-1))
    n_valid_blocks = blk_end[-1]
    rb = jnp.arange(n_rb, dtype=I32)
    rb_e = jnp.minimum(jnp.searchsorted(blk_end, rb, side='right'), N_EXPERTS - 1).astype(I32)
    rb_in_e = rb - blk_start[rb_e]
    rb_nvalid = jnp.clip(counts[rb_e] - rb_in_e * r, 0, r)
    rb_nvalid = jnp.where(rb < n_valid_blocks, rb_nvalid, 0).astype(I32)
    return dict(slot_tok=slot_tok, slot_dst=slot_dst, slot_gate=slot_gate, rb_e=rb_e,
                rb_nvalid=rb_nvalid, n_valid_blocks=n_valid_blocks.reshape(1).astype(I32),
                nblk=nblk.astype(I32), blk_start=blk_start.astype(I32), n_rb=n_rb, n_slots=n_slots)


def _gather_kernel(tok_ref, nvb_ref, h_hbm, o_ref, stage_ref, sem_ref):
    i = pl.program_id(0)
    r = MOE_ROWS
    nvb = nvb_ref[0]

    def issue(blk, slot):
        def body(j, _):
            tok = tok_ref[blk * r + j]
            pltpu.make_async_copy(h_hbm.at[pl.ds(tok, 1), :], stage_ref.at[slot, pl.ds(j, 1), :],
                                  sem_ref.at[slot]).start()
            return 0
        lax.fori_loop(0, r, body, 0)

    @pl.when(i == 0)
    def _():
        issue(0, 0)

    @pl.when(i + 1 < nvb)
    def _():
        issue(i + 1, (i + 1) % 2)

    @pl.when(i < nvb)
    def _():
        slot = i % 2
        pltpu.make_async_copy(h_hbm.at[pl.ds(0, r), :], stage_ref.at[slot], sem_ref.at[slot]).wait()
        o_ref[...] = stage_ref[slot].astype(o_ref.dtype)

    @pl.when(i >= nvb)
    def _():
        o_ref[...] = jnp.zeros(o_ref.shape, o_ref.dtype)


def _moe_gather(h2d, rt):
    t, d = h2d.shape
    r = MOE_ROWS
    grid_spec = pltpu.PrefetchScalarGridSpec(
        num_scalar_prefetch=2,
        grid=(rt['n_rb'],),
        in_specs=[pl.BlockSpec(memory_space=pl.ANY)],
        out_specs=pl.BlockSpec((r, d), lambda i, tok, nvb: (i, 0)),
        scratch_shapes=[pltpu.VMEM((2, r, d), F32), pltpu.SemaphoreType.DMA((2,))],
    )
    return pl.pallas_call(
        _gather_kernel,
        grid_spec=grid_spec,
        out_shape=jax.ShapeDtypeStruct((rt['n_slots'], d), BF16),
        compiler_params=_cparams(("arbitrary",)),
        name="moe_gather",
    )(rt['slot_tok'], rt['n_valid_blocks'], h2d)


def _cast_rows(src_ref, dst_ref, chunk=256):
    n = dst_ref.shape[0] // chunk

    def body(c, _):
        sl = pl.ds(pl.multiple_of(c * chunk, chunk), chunk)
        dst_ref[sl, :] = src_ref[0, sl, :].astype(dst_ref.dtype)
        return 0

    lax.fori_loop(0, n, body, 0)


def _moe_up_kernel(rb_ref, j_ref, e_ref, first_ref, valid_ref,
                   x_ref, wg_ref, wu_ref, bg_ref, bu_ref, o_ref, wgs_ref, wus_ref):
    i = pl.program_id(0)

    @pl.when(first_ref[i] == 1)
    def _():
        _cast_rows(wg_ref, wgs_ref)
        _cast_rows(wu_ref, wus_ref)

    @pl.when(valid_ref[i] == 1)
    def _():
        x = x_ref[...]
        gate = jnp.dot(x, wgs_ref[...], preferred_element_type=F32) + bg_ref[0]
        up = jnp.dot(x, wus_ref[...], preferred_element_type=F32) + bu_ref[0]
        gate = jnp.minimum(gate, SWIGLU_LIMIT)
        up = jnp.clip(up, -SWIGLU_LIMIT, SWIGLU_LIMIT)
        act = (up + 1.0) * gate * jax.nn.sigmoid(SWIGLU_ALPHA * gate)
        o_ref[...] = act.astype(o_ref.dtype)

    @pl.when(valid_ref[i] == 0)
    def _():
        o_ref[...] = jnp.zeros(o_ref.shape, o_ref.dtype)


def _moe_up(x_sorted, w_gu, b_gu, rt):
    n_slots, d = x_sorted.shape
    n_exp, _, two_f = w_gu.shape
    f = two_f // 2
    r, tn = MOE_ROWS, MOE_UP_TN
    nj = f // tn
    n_rb = rt['n_rb']
    n_items = n_rb * nj
    rb = jnp.arange(n_rb, dtype=I32)
    e_of = rt['rb_e']
    nvb = rt['n_valid_blocks'][0]
    nblk = rt['nblk'].at[n_exp - 1].add(n_rb - nvb)
    q = rb - rt['blk_start'][e_of]
    pos = (nj * rt['blk_start'][e_of][:, None]
           + jnp.arange(nj, dtype=I32)[None, :] * nblk[e_of][:, None] + q[:, None]).reshape(-1)
    rb2 = jnp.broadcast_to(rb[:, None], (n_rb, nj)).reshape(-1)
    j2 = jnp.broadcast_to(jnp.arange(nj, dtype=I32)[None, :], (n_rb, nj)).reshape(-1)
    it_rb = jnp.zeros((n_items,), I32).at[pos].set(rb2)
    it_j = jnp.zeros((n_items,), I32).at[pos].set(j2)
    it_valid = (it_rb < nvb).astype(I32)
    it_e = e_of[it_rb]
    prev_e = jnp.concatenate([jnp.full((1,), -1, I32), it_e[:-1]])
    prev_j = jnp.concatenate([jnp.full((1,), -1, I32), it_j[:-1]])
    it_first = ((it_e != prev_e) | (it_j != prev_j)).astype(I32)

    grid_spec = pltpu.PrefetchScalarGridSpec(
        num_scalar_prefetch=5,
        grid=(n_items,),
        in_specs=[
            pl.BlockSpec((r, d), lambda i, rbt, jt, et, ft, vt: (rbt[i], 0)),
            pl.BlockSpec((1, d, tn), lambda i, rbt, jt, et, ft, vt: (et[i], 0, jt[i])),
            pl.BlockSpec((1, d, tn), lambda i, rbt, jt, et, ft, vt: (et[i], 0, nj + jt[i])),
            pl.BlockSpec((1, 1, tn), lambda i, rbt, jt, et, ft, vt: (et[i], 0, jt[i])),
            pl.BlockSpec((1, 1, tn), lambda i, rbt, jt, et, ft, vt: (et[i], 0, nj + jt[i])),
        ],
        out_specs=pl.BlockSpec((r, tn), lambda i, rbt, jt, et, ft, vt: (rbt[i], jt[i])),
        scratch_shapes=[pltpu.VMEM((d, tn), BF16), pltpu.VMEM((d, tn), BF16)],
    )
    return pl.pallas_call(
        _moe_up_kernel,
        grid_spec=grid_spec,
        out_shape=jax.ShapeDtypeStruct((n_slots, f), BF16),
        compiler_params=_cparams(("arbitrary",)),
        name="moe_up",
    )(it_rb, it_j, it_e, it_first, it_valid, x_sorted, w_gu, w_gu,
      b_gu.reshape(n_exp, 1, two_f), b_gu.reshape(n_exp, 1, two_f))


def _moe_dn_kernel(e_ref, nvb_ref, dst_ref,
                   a_ref, w_ref, b_ref, g_ref, out_hbm, ws_ref, y_ref, sem_ref):
    i = pl.program_id(0)
    n_steps = pl.num_programs(0)
    r = MOE_ROWS
    slot = i % 2
    nvb = nvb_ref[0]
    valid = i < nvb
    e = e_ref[i]
    prev_e = e_ref[jnp.maximum(i - 1, 0)]

    def wait_rows(sl):
        pltpu.make_async_copy(y_ref.at[sl], out_hbm.at[pl.ds(0, r), :], sem_ref.at[sl]).wait()

    @pl.when(valid & ((i == 0) | (e != prev_e)))
    def _():
        _cast_rows(w_ref, ws_ref)

    @pl.when(i >= 2)
    def _():
        wait_rows(slot)

    @pl.when(valid)
    def _():
        y = jnp.dot(a_ref[...], ws_ref[...], preferred_element_type=F32) + b_ref[0]
        y_ref[slot] = y * g_ref[...]

    @pl.when(jnp.logical_not(valid))
    def _():
        y_ref[slot] = jnp.zeros((r, y_ref.shape[2]), F32)

    def body(j, _):
        dst = dst_ref[i * r + j]
        pltpu.make_async_copy(y_ref.at[slot, pl.ds(j, 1), :], out_hbm.at[pl.ds(dst, 1), :],
                              sem_ref.at[slot]).start()
        return 0

    lax.fori_loop(0, r, body, 0)

    @pl.when(i == n_steps - 1)
    def _():
        @pl.when(i >= 1)
        def _():
            wait_rows(1 - slot)
        wait_rows(slot)


def _moe_dn(act, w_dn, b_dn, rt):
    n_slots, f = act.shape
    n_exp, _, d = w_dn.shape
    r = MOE_ROWS

    def blk(i, nvb):
        return jnp.minimum(i, nvb[0] - 1)

    grid_spec = pltpu.PrefetchScalarGridSpec(
        num_scalar_prefetch=3,
        grid=(rt['n_rb'],),
        in_specs=[
            pl.BlockSpec((r, f), lambda i, et, nvb, dst: (blk(i, nvb), 0)),
            pl.BlockSpec((1, f, d), lambda i, et, nvb, dst: (et[i], 0, 0)),
            pl.BlockSpec((1, 1, d), lambda i, et, nvb, dst: (et[i], 0, 0)),
            pl.BlockSpec((r, 1), lambda i, et, nvb, dst: (blk(i, nvb), 0)),
        ],
        out_specs=pl.BlockSpec(memory_space=pl.ANY),
        scratch_shapes=[pltpu.VMEM((f, d), BF16), pltpu.VMEM((2, r, d), F32),
                        pltpu.SemaphoreType.DMA((2,))],
    )
    return pl.pallas_call(
        _moe_dn_kernel,
        grid_spec=grid_spec,
        out_shape=jax.ShapeDtypeStruct((n_slots, d), F32),
        compiler_params=_cparams(("arbitrary",)),
        name="moe_dn",
    )(rt['rb_e'], rt['n_valid_blocks'], rt['slot_dst'],
      act, w_dn, b_dn.reshape(n_exp, 1, d), rt['slot_gate'].reshape(n_slots, 1))


def _combine_kernel(x_ref, y0_ref, y1_ref, y2_ref, y3_ref, g_ref, fg_ref, o_ref, *, final):
    y = (y0_ref[...] + y1_ref[...]) + (y2_ref[...] + y3_ref[...])
    x2 = x_ref[...] + g_ref[0] * y
    if final:
        x2 = x2 * lax.rsqrt(jnp.mean(x2 * x2, axis=-1, keepdims=True) + EPS) * fg_ref[...]
    o_ref[...] = x2


def _combine(x1, y4, g_ffn, final_gain, seq, final):
    t, d = x1.shape
    tm = 256
    per_seq = seq // tm
    nb = t // tm

    def yk(k):
        return pl.BlockSpec((tm, d), lambda i: (k * nb + i, 0))

    return pl.pallas_call(
        functools.partial(_combine_kernel, final=final),
        grid=(nb,),
        in_specs=[pl.BlockSpec((tm, d), lambda i: (i, 0)), yk(0), yk(1), yk(2), yk(3),
                  pl.BlockSpec((1, 1, d), lambda i: (i // per_seq, 0, 0)),
                  pl.BlockSpec((1, d), lambda i: (0, 0))],
        out_specs=pl.BlockSpec((tm, d), lambda i: (i, 0)),
        out_shape=jax.ShapeDtypeStruct((t, d), F32),
        compiler_params=_cparams(("arbitrary",)),
        name="combine_final" if final else "combine",
    )(x1, y4, y4, y4, y4, g_ffn, final_gain.reshape(1, d))


def _moe(x1, h, top_idx, gates, w_gu, b_gu, w_dn, b_dn, g_ffn, final_gain, seq, final):
    n_tok = x1.shape[0]
    rt = _routing_tables(top_idx, gates, n_tok)
    x_sorted = _moe_gather(h, rt)
    act = _moe_up(x_sorted, w_gu, b_gu, rt)
    y4 = _moe_dn(act, w_dn, b_dn, rt)
    return _combine(x1, y4, g_ffn, final_gain, seq, final)


def kernel(x, c, ada_w, ada_b, norm_mix_gain, norm_ffn_gain, ab_w_in, ab_w_out, hg_lb_logits, hg_norm_gain, s5_lam_re, s5_lam_im, s5_log_dt, s5_b_re, s5_b_im, s5_c_re, s5_c_im, s5_d, s5_glu_w, s5_glu_b, cd_w_in, cd_w_out, ret_norm_gain, router_w, router_b, moe_w_gu, moe_b_gu, moe_w_dn, moe_b_dn, final_gain):
    bsz, seq, d = x.shape
    depth = ada_w.shape[0]
    n_tok = bsz * seq
    hg_width = hg_lb_logits.shape[1]
    s5_width = s5_glu_w.shape[1]
    ret_width = ret_norm_gain.shape[1]
    fnet_width = cd_w_out.shape[1] - ret_width

    lower_bounds = jnp.cumsum(jax.nn.softmax(hg_lb_logits.astype(F32), axis=0), axis=0)
    mod = _ada_mod(c, ada_w, ada_b)
    xr = x.reshape(n_tok, d)
    for layer in range(depth):
        sh_mix, sc_mix, g_mix, sh_ffn, sc_ffn, g_ffn = (
            mod[layer, :, k * d:(k + 1) * d].reshape(bsz, 1, d) for k in range(6))
        j = layer // 2
        if layer % 2 == 0:
            proj = _in_proj(xr, norm_mix_gain[layer], sh_mix, sc_mix, ab_w_in[j].astype(BF16), seq)
            proj3 = proj.reshape(bsz, seq, proj.shape[1])
            mix_a = _hgrn2(proj3, lower_bounds[j], hg_norm_gain[j], hg_width)
            tables = _s5_tables(s5_lam_re[j], s5_lam_im[j], s5_log_dt[j], s5_b_re[j], s5_b_im[j],
                                s5_c_re[j], s5_c_im[j])
            mix_b = _s5(proj3, 5 * hg_width, tables, s5_d[j].reshape(-1))
            x1, h, idx, gate = _out_proj(
                xr, mix_a.reshape(n_tok, hg_width), mix_b.reshape(n_tok, s5_width), ab_w_out[j],
                g_mix, norm_ffn_gain[layer], sh_ffn, sc_ffn, router_w[layer], router_b[layer], seq,
                glu_w=s5_glu_w[j], glu_b=s5_glu_b[j])
        else:
            proj = _in_proj(xr, norm_mix_gain[layer], sh_mix, sc_mix, cd_w_in[j].astype(BF16), seq)
            proj3 = proj.reshape(bsz, seq, proj.shape[1])
            mix_a = _retention(proj3, ret_norm_gain[j], ret_width)
            mix_b = _fnet(proj3, 4 * ret_width, fnet_width)
            x1, h, idx, gate = _out_proj(
                xr, mix_a.reshape(n_tok, ret_width), mix_b.reshape(n_tok, fnet_width), cd_w_out[j],
                g_mix, norm_ffn_gain[layer], sh_ffn, sc_ffn, router_w[layer], router_b[layer], seq)
        xr = _moe(x1, h, idx[:, :TOP_K], gate[:, :TOP_K], moe_w_gu[layer], moe_b_gu[layer],
                  moe_w_dn[layer], moe_b_dn[layer], g_ffn, final_gain, seq, final=(layer == depth - 1))
    return xr.reshape(bsz, seq, d)
```

```python
import functools
import math

import jax
import jax.numpy as jnp
from jax import lax
from jax.experimental import pallas as pl
from jax.experimental.pallas import tpu as pltpu

F32 = jnp.float32
BF16 = jnp.bfloat16
I32 = jnp.int32

EPS = 1e-6
LANES = 128
SUBLANES = 8
VMEM_LIMIT = 56 * 1024 * 1024

HG_HEAD_DIM = 128
HG_CHUNK = 64
HG_GROUP = 256
HG_EXP_CLAMP = 80.0

S5_GROUP = 16
S5_STATE = 64
S5_TILE_GROUPS = 16
S5_TIME_BLOCK = 512
S5_SCAN_BLOCK = SUBLANES * SUBLANES

RET_HEAD_DIM = 256
RET_Q_TILE = 256
ROPE_BASE = 10000.0

FNET_GROUPS = 4
FNET_ROW_TILE = 512

N_EXPERTS = 32
TOP_K = 4
SWIGLU_LIMIT = 7.0
SWIGLU_ALPHA = 1.702
MOE_ROWS = 256
MOE_UP_TN = 1024
DISPATCH_TOKENS = 512
COMBINE_TOKENS = 128
DMA_UNROLL = 8
ROUTER_PAD = LANES
NEG_BIG = -1e30


def _cparams(semantics):
    return pltpu.CompilerParams(dimension_semantics=semantics, vmem_limit_bytes=VMEM_LIMIT)


def _ada_kernel(c_ref, w_ref, b_ref, o_ref):
    c = c_ref[...]
    cond = c * jax.nn.sigmoid(c)
    o_ref[0] = jnp.dot(cond.astype(BF16), w_ref[0].astype(BF16),
                       preferred_element_type=F32) + b_ref[0]


def _ada_mod(c, ada_w, ada_b):
    depth, d, n = ada_w.shape
    bsz = c.shape[0]
    tn = 1024
    return pl.pallas_call(
        _ada_kernel,
        grid=(depth, n // tn),
        in_specs=[
            pl.BlockSpec((bsz, d), lambda l, j: (0, 0)),
            pl.BlockSpec((1, d, tn), lambda l, j: (l, 0, j)),
            pl.BlockSpec((1, 1, tn), lambda l, j: (l, 0, j)),
        ],
        out_specs=pl.BlockSpec((1, bsz, tn), lambda l, j: (l, 0, j)),
        out_shape=jax.ShapeDtypeStruct((depth, bsz, n), F32),
        compiler_params=_cparams(("arbitrary", "arbitrary")),
        name="ada_mod",
    )(c, ada_w, ada_b.reshape(depth, 1, n))


def _norm_modulate(x, gain, shift, scale):
    ms = jnp.mean(x * x, axis=-1, keepdims=True)
    y = x * lax.rsqrt(ms + EPS) * gain
    return y * (1.0 + scale) + shift


def _inproj_kernel(x_ref, gain_ref, sh_ref, sc_ref, w_ref, o_ref, h_ref):
    @pl.when(pl.program_id(1) == 0)
    def _():
        h_ref[...] = _norm_modulate(x_ref[...], gain_ref[...], sh_ref[0], sc_ref[0]).astype(BF16)

    o_ref[...] = jnp.dot(h_ref[...], w_ref[...], preferred_element_type=F32)


def _in_proj(x2d, gain, shift, scale, w_bf16, seq):
    t, d = x2d.shape
    n = w_bf16.shape[1]
    tm, tn = 1024, 512
    per_seq = seq // tm
    return pl.pallas_call(
        _inproj_kernel,
        grid=(t // tm, n // tn),
        in_specs=[
            pl.BlockSpec((tm, d), lambda i, j: (i, 0)),
            pl.BlockSpec((1, d), lambda i, j: (0, 0)),
            pl.BlockSpec((1, 1, d), lambda i, j: (i // per_seq, 0, 0)),
            pl.BlockSpec((1, 1, d), lambda i, j: (i // per_seq, 0, 0)),
            pl.BlockSpec((d, tn), lambda i, j: (0, j)),
        ],
        out_specs=pl.BlockSpec((tm, tn), lambda i, j: (i, j)),
        out_shape=jax.ShapeDtypeStruct((t, n), F32),
        scratch_shapes=[pltpu.VMEM((tm, d), BF16)],
        compiler_params=_cparams(("arbitrary", "arbitrary")),
        name="in_proj",
    )(x2d, gain.reshape(1, d), shift, scale, w_bf16)


def _split3(a):
    hi = a.astype(BF16)
    r1 = a - hi.astype(F32)
    mid = r1.astype(BF16)
    lo = (r1 - mid.astype(F32)).astype(BF16)
    return hi, mid, lo


def _tri_sum(tri, a):
    hi, mid, lo = _split3(a)
    return (jnp.dot(tri, hi, preferred_element_type=F32)
            + jnp.dot(tri, mid, preferred_element_type=F32)
            + jnp.dot(tri, lo, preferred_element_type=F32))


def _dot_nt(a, b):
    return lax.dot_general(a, b, (((1,), (1,)), ((), ())), preferred_element_type=F32)


def _dot_tn(a, b):
    return lax.dot_general(a, b, (((0,), (0,)), ((), ())), preferred_element_type=F32)


def _hgrn2_kernel(q_ref, zf_ref, zb_ref, v_ref, g_ref, lb_ref, gain_ref, o_ref, acc_ref):
    seq = q_ref.shape[1]
    ln = HG_CHUNK
    gr = HG_GROUP
    n_groups = seq // gr
    per_group = gr // ln
    lb = lb_ref[...]
    gain = gain_ref[...]
    row = lax.broadcasted_iota(I32, (gr, gr), 0)
    col = lax.broadcasted_iota(I32, (gr, gr), 1)
    chunk_lo = (row // ln) * ln
    chunk_hi = chunk_lo + ln
    lower_incl = (col <= row) & (col >= chunk_lo)
    upper_strict = (col > row) & (col < chunk_hi)
    tri_prefix = lower_incl.astype(BF16)
    tri_suffix = ((col >= row) & (col < chunk_hi)).astype(BF16)
    mid = ln // 2

    def per_chunk_rows(a, offset):
        return jnp.concatenate(
            [jnp.broadcast_to(a[j * ln + offset:j * ln + offset + 1, :], (ln, a.shape[1]))
             for j in range(per_group)], axis=0)

    def group(gi, state_t, z_ref, forward):
        sl = pl.ds(pl.multiple_of(gi * gr, gr), gr)
        q = q_ref[0, sl, :]
        v = v_ref[0, sl, :].astype(BF16)
        f = lb + (1.0 - lb) * jax.nn.sigmoid(z_ref[0, sl, :])
        log_f = jnp.log(f)
        k = 1.0 - f
        if forward:
            cum = _tri_sum(tri_prefix, log_f)
            ref_rows = per_chunk_rows(cum, mid - 1)
            edge_off = ln - 1
            mask = lower_incl
        else:
            cum = _tri_sum(tri_suffix, log_f)
            ref_rows = per_chunk_rows(cum, mid)
            edge_off = 0
            mask = upper_strict
        kd = (k * jnp.exp(per_chunk_rows(cum, edge_off) - cum)).astype(BF16)
        rows = [slice(j * ln, (j + 1) * ln) for j in range(per_group)]
        local = [_dot_tn(v[rs], kd[rs]) for rs in rows]
        qe = q * jnp.exp(jnp.minimum(cum - ref_rows, HG_EXP_CLAMP))
        ke = k * jnp.exp(jnp.minimum(ref_rows - cum, HG_EXP_CLAMP))
        scores = jnp.where(mask, _dot_nt(qe.astype(BF16), ke.astype(BF16)), 0.0)
        qc = (q * jnp.exp(cum)).astype(BF16)
        entering = [None] * per_group
        order = range(per_group) if forward else range(per_group - 1, -1, -1)
        for j in order:
            entering[j] = state_t.astype(BF16)
            edge = cum[j * ln + edge_off:j * ln + edge_off + 1, :]
            state_t = state_t * jnp.exp(edge) + local[j]
        inter = [_dot_nt(qc[rs], entering[j]) for j, rs in enumerate(rows)]
        intra = jnp.dot(scores.astype(BF16), v, preferred_element_type=F32)
        return sl, intra + jnp.concatenate(inter, axis=0), state_t

    def fwd_body(gi, state_t):
        sl, out, new_state_t = group(gi, state_t, zf_ref, True)
        acc_ref[sl, :] = out
        return new_state_t

    def bwd_body(i, state_t):
        sl, out, new_state_t = group(n_groups - 1 - i, state_t, zb_ref, False)
        o = acc_ref[sl, :] + out
        y = o * lax.rsqrt(jnp.mean(o * o, axis=-1, keepdims=True) + EPS) * gain
        g = g_ref[0, sl, :]
        o_ref[0, sl, :] = (y * (g * jax.nn.sigmoid(g))).astype(o_ref.dtype)
        return new_state_t

    zero = jnp.zeros((HG_HEAD_DIM, HG_HEAD_DIM), F32)
    lax.fori_loop(0, n_groups, fwd_body, zero)
    lax.fori_loop(0, n_groups, bwd_body, zero)


def _hgrn2(proj3, lower_bound, hg_gain, width):
    bsz, seq, _ = proj3.shape
    heads = width // HG_HEAD_DIM
    dh = HG_HEAD_DIM

    def col(k):
        return pl.BlockSpec((1, seq, dh), lambda b, h: (b, 0, k * heads + h))

    return pl.pallas_call(
        _hgrn2_kernel,
        grid=(bsz, heads),
        in_specs=[col(0), col(1), col(2), col(3), col(4),
                  pl.BlockSpec((1, dh), lambda b, h: (0, h)),
                  pl.BlockSpec((1, dh), lambda b, h: (0, 0))],
        out_specs=pl.BlockSpec((1, seq, dh), lambda b, h: (b, 0, h)),
        out_shape=jax.ShapeDtypeStruct((bsz, seq, width), BF16),
        scratch_shapes=[pltpu.VMEM((seq, dh), F32)],
        compiler_params=_cparams(("arbitrary", "arbitrary")),
        name="hgrn2",
    )(proj3, proj3, proj3, proj3, proj3, lower_bound.reshape(1, width), hg_gain.reshape(1, dh))


S5_TBL_STEP = 0
S5_TBL_X16 = 8
S5_TBL_X32 = 9
S5_TBL_CARRY = 10
S5_N_TBL = 11


def _cmul(ar, ai, br, bi):
    return ar * br - ai * bi, ar * bi + ai * br


def _s5_scan_block(bu_ref, carry_ref, pw_ref, direction, base, colblk, width):
    forward = direction == 0
    cre = pl.ds(colblk * LANES, LANES)
    cim = pl.ds(width + colblk * LANES, LANES)
    bre = colblk
    bim = width // LANES + colblk

    def tbl(idx):
        return pw_ref[direction, 0, 0, idx, :, cre], pw_ref[direction, 0, 1, idx, :, cre]

    def rows(k):
        return pl.ds(base + k, SUBLANES, stride=SUBLANES)

    a_r, a_i = tbl(S5_TBL_STEP)
    order = list(range(SUBLANES)) if forward else list(range(SUBLANES - 1, -1, -1))
    xr = [None] * SUBLANES
    xi = [None] * SUBLANES
    pr = pi = None
    for k in order:
        br = bu_ref.at[bre][rows(k), :]
        bi = bu_ref.at[bim][rows(k), :]
        if pr is not None:
            mr, mi = _cmul(a_r, a_i, pr, pi)
            br = br + mr
            bi = bi + mi
        xr[k], xi[k] = br, bi
        pr, pi = br, bi

    er, ei = pr, pi
    sub = lax.broadcasted_iota(I32, (SUBLANES, LANES), 0)
    for step, idx in ((1, S5_TBL_STEP + 7), (2, S5_TBL_X16), (4, S5_TBL_X32)):
        m_r, m_i = tbl(idx)
        if forward:
            shift, keep = step, sub >= step
        else:
            shift, keep = SUBLANES - step, sub < SUBLANES - step
        sr = jnp.where(keep, pltpu.roll(er, shift, 0), 0.0)
        si = jnp.where(keep, pltpu.roll(ei, shift, 0), 0.0)
        dr, di = _cmul(m_r, m_i, sr, si)
        er, ei = er + dr, ei + di
    c_r = carry_ref[:, cre]
    c_i = carry_ref[:, cim]
    t_r, t_i = tbl(S5_TBL_CARRY)
    dr, di = _cmul(t_r, t_i, c_r, c_i)
    er, ei = er + dr, ei + di

    if forward:
        nr = jnp.where(sub >= 1, pltpu.roll(er, 1, 0), c_r)
        ni = jnp.where(sub >= 1, pltpu.roll(ei, 1, 0), c_i)
        last = SUBLANES - 1
    else:
        nr = jnp.where(sub < SUBLANES - 1, pltpu.roll(er, SUBLANES - 1, 0), c_r)
        ni = jnp.where(sub < SUBLANES - 1, pltpu.roll(ei, SUBLANES - 1, 0), c_i)
        last = 0
    carry_ref[:, cre] = jnp.broadcast_to(er[last:last + 1, :], (SUBLANES, LANES))
    carry_ref[:, cim] = jnp.broadcast_to(ei[last:last + 1, :], (SUBLANES, LANES))

    for k in range(SUBLANES):
        f_r, f_i = tbl(S5_TBL_STEP + (k if forward else SUBLANES - 1 - k))
        dr, di = _cmul(f_r, f_i, nr, ni)
        bu_ref.at[bre][rows(k), :] = xr[k] + dr
        bu_ref.at[bim][rows(k), :] = xi[k] + di


def _s5_kernel(u_ref, bm_ref, cm_ref, pw_ref, d_ref, z_ref, bu_ref, y_ref, carry_ref):
    seq = u_ref.shape[1]
    n_cb = int(bu_ref.shape[0])
    width = n_cb * LANES // 2
    tb = S5_TIME_BLOCK
    n_tb = seq // tb
    n_sb = tb // S5_SCAN_BLOCK
    d_skip = d_ref[...]

    for direction in (0, 1):
        carry_ref[...] = jnp.zeros(carry_ref.shape, F32)

        def time_block(it, _, direction=direction):
            blk = it if direction == 0 else n_tb - 1 - it
            sl = pl.ds(pl.multiple_of(blk * tb, tb), tb)
            u = u_ref[0, sl, :]
            bu = jnp.dot(u.astype(BF16), bm_ref[direction, 0], preferred_element_type=F32)
            for cb in range(n_cb):
                bu_ref[cb] = bu[:, cb * LANES:(cb + 1) * LANES]

            def scan_block(js, _):
                sb = js if direction == 0 else n_sb - 1 - js
                base = pl.multiple_of(sb * S5_SCAN_BLOCK, S5_SCAN_BLOCK)
                for colblk in range(width // LANES):
                    _s5_scan_block(bu_ref, carry_ref, pw_ref, direction, base, colblk, width)
                return 0

            lax.fori_loop(0, n_sb, scan_block, 0)
            xs = jnp.concatenate([bu_ref[cb].astype(BF16) for cb in range(n_cb)], axis=-1)
            y = jnp.dot(xs, cm_ref[direction, 0], preferred_element_type=F32)
            if direction == 0:
                y_ref[sl, :] = y
            else:
                z_ref[0, sl, :] = jax.nn.gelu(y_ref[sl, :] + y + d_skip * u).astype(z_ref.dtype)
            return 0

        lax.fori_loop(0, n_tb, time_block, 0)


def _s5_tables(lam_re, lam_im, log_dt, b_re, b_im, c_re, c_im):
    groups, state = lam_re.shape[1], lam_re.shape[2]
    chans = b_re.shape[2]
    tg = S5_TILE_GROUPS
    tiles = groups // tg
    eye = jnp.eye(tg, dtype=F32)
    bms, cms, pws = [], [], []
    sub = jnp.arange(SUBLANES, dtype=F32)
    for direction in (0, 1):
        lr, li = lam_re[direction].astype(F32), lam_im[direction].astype(F32)
        dt = jnp.exp(log_dt[direction].astype(F32))[:, None]
        mag = jnp.exp(lr * dt)
        abar_re = mag * jnp.cos(li * dt)
        abar_im = mag * jnp.sin(li * dt)
        den = lr * lr + li * li
        num_re = abar_re - 1.0
        coef_re = (num_re * lr + abar_im * li) / den
        coef_im = (abar_im * lr - num_re * li) / den
        bbar_re = coef_re[..., None] * b_re - coef_im[..., None] * b_im
        bbar_im = coef_re[..., None] * b_im + coef_im[..., None] * b_re

        def blockdiag_in(bb):
            bb = bb.reshape(tiles, tg, state, chans)
            m = jnp.einsum('gh,tgpc->tgchp', eye, bb)
            return m.reshape(tiles, tg * chans, tg * state)

        def blockdiag_out(cc):
            cc = cc.reshape(tiles, tg, chans, state)
            m = jnp.einsum('gh,tgcp->tgphc', eye, cc)
            return m.reshape(tiles, tg * state, tg * chans)

        bms.append(jnp.concatenate([blockdiag_in(bbar_re), blockdiag_in(bbar_im)], axis=-1))
        cms.append(jnp.concatenate([blockdiag_out(c_re[direction].astype(F32)),
                                    -blockdiag_out(c_im[direction].astype(F32))], axis=1))

        def power(n):
            return jnp.exp(n * lr * dt) * jnp.cos(n * li * dt), jnp.exp(n * lr * dt) * jnp.sin(n * li * dt)

        rows_re, rows_im = [], []
        for n in list(range(1, SUBLANES + 1)) + [2 * SUBLANES, 4 * SUBLANES]:
            p_re, p_im = power(float(n))
            rows_re.append(jnp.broadcast_to(p_re.reshape(tiles, 1, tg * state), (tiles, SUBLANES, tg * state)))
            rows_im.append(jnp.broadcast_to(p_im.reshape(tiles, 1, tg * state), (tiles, SUBLANES, tg * state)))
        carry_n = SUBLANES * (sub + 1.0) if direction == 0 else SUBLANES * (SUBLANES - sub)
        p_re, p_im = power(carry_n[:, None, None])
        rows_re.append(p_re.reshape(SUBLANES, tiles, tg * state).transpose(1, 0, 2))
        rows_im.append(p_im.reshape(SUBLANES, tiles, tg * state).transpose(1, 0, 2))
        pws.append(jnp.stack([jnp.stack(rows_re, axis=1), jnp.stack(rows_im, axis=1)], axis=1))
    return (jnp.stack(bms).astype(BF16), jnp.stack(cms).astype(BF16), jnp.stack(pws))


def _s5(proj3, col0, tables, d_skip):
    bsz, seq, _ = proj3.shape
    bm, cm, pw = tables
    tiles = bm.shape[1]
    tc = bm.shape[2]
    sw = bm.shape[3]
    cb0 = col0 // tc
    return pl.pallas_call(
        _s5_kernel,
        grid=(bsz, tiles),
        in_specs=[
            pl.BlockSpec((1, seq, tc), lambda b, t: (b, 0, cb0 + t)),
            pl.BlockSpec((2, 1, tc, sw), lambda b, t: (0, t, 0, 0)),
            pl.BlockSpec((2, 1, sw, tc), lambda b, t: (0, t, 0, 0)),
            pl.BlockSpec((2, 1, 2, S5_N_TBL, SUBLANES, sw // 2), lambda b, t: (0, t, 0, 0, 0, 0)),
            pl.BlockSpec((1, tc), lambda b, t: (0, t)),
        ],
        out_specs=pl.BlockSpec((1, seq, tc), lambda b, t: (b, 0, t)),
        out_shape=jax.ShapeDtypeStruct((bsz, seq, tiles * tc), BF16),
        scratch_shapes=[pltpu.VMEM((sw // LANES, S5_TIME_BLOCK, LANES), F32),
                        pltpu.VMEM((seq, tc), F32),
                        pltpu.VMEM((SUBLANES, sw), F32)],
        compiler_params=_cparams(("arbitrary", "arbitrary")),
        name="s5",
    )(proj3, bm, cm, pw, d_skip.reshape(1, tiles * tc))


def _retention_kernel(q_ref, k_ref, v_ref, g_ref, cos_ref, sin_ref, lg_ref, gain_ref,
                      o_ref, qs_ref, ks_ref, vs_ref):
    seq = q_ref.shape[1]
    half = RET_HEAD_DIM // 2
    tq = RET_Q_TILE
    cos = cos_ref[...]
    sin = sin_ref[...]

    def rot(t_ref, scale):
        t1 = t_ref[0, :, :half]
        t2 = t_ref[0, :, half:]
        return jnp.concatenate([(t1 * cos - t2 * sin) * scale, (t1 * sin + t2 * cos) * scale], axis=-1)

    qs_ref[...] = rot(q_ref, 1.0).astype(BF16)
    ks_ref[...] = rot(k_ref, RET_HEAD_DIM ** -0.5).astype(BF16)
    vs_ref[...] = v_ref[0].astype(BF16)
    lg_fwd = lg_ref[0, 0:1, :]
    lg_bwd = lg_ref[0, 1:2, :]
    gain = gain_ref[...]

    def q_tile(i, _):
        sl = pl.ds(pl.multiple_of(i * tq, tq), tq)
        scores = _dot_nt(qs_ref[sl, :], ks_ref[...])
        t_idx = lax.broadcasted_iota(I32, (tq, seq), 0) + i * tq
        s_idx = lax.broadcasted_iota(I32, (tq, seq), 1)
        rel = (t_idx - s_idx).astype(F32)
        decay = jnp.exp(jnp.where(rel >= 0.0, lg_fwd * rel, -lg_bwd * rel))
        p = (scores * decay).astype(BF16)
        o = jnp.dot(p, vs_ref[...], preferred_element_type=F32)
        y = o * lax.rsqrt(jnp.mean(o * o, axis=-1, keepdims=True) + EPS) * gain
        g = g_ref[0, sl, :]
        o_ref[0, sl, :] = (y * (g * jax.nn.sigmoid(g))).astype(o_ref.dtype)
        return 0

    lax.fori_loop(0, seq // tq, q_tile, 0)


def _retention(proj3, ret_gain, width):
    bsz, seq, _ = proj3.shape
    dh = RET_HEAD_DIM
    heads = width // dh
    inv_freq = ROPE_BASE ** (-jnp.arange(0, dh, 2, dtype=F32) / dh)
    ang = jnp.arange(seq, dtype=F32)[:, None] * inv_freq[None, :]
    cos, sin = jnp.cos(ang), jnp.sin(ang)
    log_gamma = jnp.log1p(-jnp.exp2(-5.0 - jnp.arange(heads, dtype=F32)))
    lg = jnp.stack([log_gamma, log_gamma[::-1]], axis=1)
    lg = jnp.broadcast_to(lg[:, :, None], (heads, 2, seq))

    def col(k):
        return pl.BlockSpec((1, seq, dh), lambda b, h: (b, 0, k * heads + h))

    return pl.pallas_call(
        _retention_kernel,
        grid=(bsz, heads),
        in_specs=[col(0), col(1), col(2), col(3),
                  pl.BlockSpec((seq, dh // 2), lambda b, h: (0, 0)),
                  pl.BlockSpec((seq, dh // 2), lambda b, h: (0, 0)),
                  pl.BlockSpec((1, 2, seq), lambda b, h: (h, 0, 0)),
                  pl.BlockSpec((1, dh), lambda b, h: (0, h))],
        out_specs=pl.BlockSpec((1, seq, dh), lambda b, h: (b, 0, h)),
        out_shape=jax.ShapeDtypeStruct((bsz, seq, width), BF16),
        scratch_shapes=[pltpu.VMEM((seq, dh), BF16)] * 3,
        compiler_params=_cparams(("arbitrary", "arbitrary")),
        name="retention",
    )(proj3, proj3, proj3, proj3, cos, sin, lg, ret_gain.reshape(1, width))


def _fnet_kernel(x_ref, cc_ref, sc_ref, cs_ref, ss_ref, o_ref, a1_ref, a2_ref):
    @pl.when(pl.program_id(1) == 0)
    def _():
        x = x_ref[0].astype(BF16)
        a1_ref[...] = jnp.dot(x, cc_ref[...], preferred_element_type=F32).astype(BF16)
        a2_ref[...] = jnp.dot(x, sc_ref[...], preferred_element_type=F32).astype(BF16)

    y = (jnp.dot(cs_ref[...], a1_ref[...], preferred_element_type=F32)
         - jnp.dot(ss_ref[...], a2_ref[...], preferred_element_type=F32))
    o_ref[0] = y.astype(o_ref.dtype)


def _dft_mats(n):
    idx = jnp.arange(n, dtype=I32)
    ang = (2.0 * math.pi / n) * ((idx[:, None] * idx[None, :]) % n).astype(F32)
    scale = n ** -0.5
    return jnp.cos(ang) * scale, jnp.sin(ang) * scale


def _fnet(proj3, col0, width):
    bsz, seq, _ = proj3.shape
    gw = width // FNET_GROUPS
    cs, ss = _dft_mats(seq)
    cg, sg = _dft_mats(gw)
    eye = jnp.eye(FNET_GROUPS, dtype=F32)
    cc = jnp.kron(eye, cg)
    sc = jnp.kron(eye, sg)
    tr = FNET_ROW_TILE
    return pl.pallas_call(
        _fnet_kernel,
        grid=(bsz, seq // tr),
        in_specs=[
            pl.BlockSpec((1, seq, width), lambda b, i: (b, 0, col0 // width)),
            pl.BlockSpec((width, width), lambda b, i: (0, 0)),
            pl.BlockSpec((width, width), lambda b, i: (0, 0)),
            pl.BlockSpec((tr, seq), lambda b, i: (i, 0)),
            pl.BlockSpec((tr, seq), lambda b, i: (i, 0)),
        ],
        out_specs=pl.BlockSpec((1, tr, width), lambda b, i: (b, i, 0)),
        out_shape=jax.ShapeDtypeStruct((bsz, seq, width), BF16),
        scratch_shapes=[pltpu.VMEM((seq, width), BF16)] * 2,
        compiler_params=_cparams(("arbitrary", "arbitrary")),
        name="fnet",
    )(proj3, cc.astype(BF16), sc.astype(BF16), cs.astype(BF16), ss.astype(BF16))


def _outproj_kernel(*refs, glu):
    if glu:
        (x_ref, a_ref, b_ref, gw_ref, gb_ref, wa_ref, wb_ref, gm_ref, gain_ref, sh_ref, sc_ref,
         rwh_ref, rwl_ref, rb_ref, x1_ref, h_ref, idx_ref, gate_ref, cnt_ref, base_ref) = refs
    else:
        (x_ref, a_ref, b_ref, wa_ref, wb_ref, gm_ref, gain_ref, sh_ref, sc_ref,
         rwh_ref, rwl_ref, rb_ref, x1_ref, h_ref, idx_ref, gate_ref, cnt_ref, base_ref) = refs

    @pl.when(pl.program_id(0) == 0)
    def _():
        base_ref[...] = jnp.zeros(base_ref.shape, F32)

    bm = b_ref[...]
    if glu:
        gl = jnp.dot(bm, gw_ref[...], preferred_element_type=F32) + gb_ref[...]
        bm = (bm.astype(F32) * jax.nn.sigmoid(gl)).astype(BF16)
    y = (jnp.dot(a_ref[...], wa_ref[...], preferred_element_type=F32)
         + jnp.dot(bm, wb_ref[...], preferred_element_type=F32))
    x1 = x_ref[...] + gm_ref[0] * y
    x1_ref[...] = x1
    h = _norm_modulate(x1, gain_ref[...], sh_ref[0], sc_ref[0])
    h_ref[...] = h

    h_hi = h.astype(BF16)
    h_lo = (h - h_hi.astype(F32)).astype(BF16)
    logits = (jnp.dot(h_hi, rwh_ref[...], preferred_element_type=F32)
              + jnp.dot(h_lo, rwh_ref[...], preferred_element_type=F32)
              + jnp.dot(h_hi, rwl_ref[...], preferred_element_type=F32)
              + rb_ref[...])
    lane = lax.broadcasted_iota(I32, logits.shape, 1)
    vals, idxs = [], []
    for _ in range(TOP_K):
        m = jnp.max(logits, axis=-1, keepdims=True)
        ik = jnp.min(jnp.where(logits == m, lane, ROUTER_PAD), axis=-1, keepdims=True)
        vals.append(m)
        idxs.append(ik)
        logits = jnp.where(lane == ik, -jnp.inf, logits)
    exps = [jnp.exp(v - vals[0]) for v in vals]
    denom = exps[0] + exps[1] + exps[2] + exps[3]

    tm = lane.shape[0]
    onehot = jnp.zeros(lane.shape, F32)
    for k in range(TOP_K):
        onehot = onehot + (lane == idxs[k]).astype(F32)
    before = (lax.broadcasted_iota(I32, (tm, tm), 1) < lax.broadcasted_iota(I32, (tm, tm), 0)).astype(BF16)
    count = jnp.dot(before, onehot.astype(BF16), preferred_element_type=F32) + base_ref[...]
    new_base = base_ref[...] + jnp.sum(onehot, axis=0, keepdims=True)
    base_ref[...] = new_base
    cnt_ref[...] = new_base.astype(I32)

    idx_out = jnp.zeros(lane.shape, I32)
    gate_out = jnp.zeros(lane.shape, F32)
    for k in range(TOP_K):
        rank = jnp.sum(jnp.where(lane == idxs[k], count, 0.0), axis=-1, keepdims=True).astype(I32)
        idx_out = jnp.where(lane == k, idxs[k], idx_out)
        idx_out = jnp.where(lane == TOP_K + k, rank, idx_out)
        gate_out = jnp.where(lane == k, exps[k] / denom, gate_out)
    idx_ref[...] = idx_out
    gate_ref[...] = gate_out


def _out_proj(x2d, mix_a, mix_b, w_out, g_mix, gain, shift, scale, router_w, router_b, seq,
              glu_w=None, glu_b=None):
    t, d = x2d.shape
    wa_rows = mix_a.shape[1]
    wb_rows = mix_b.shape[1]
    tm = 256
    per_seq = seq // tm
    n_exp = router_w.shape[1]
    rw = jnp.zeros((d, ROUTER_PAD), F32).at[:, :n_exp].set(router_w)
    rw_hi = rw.astype(BF16)
    rw_lo = (rw - rw_hi.astype(F32)).astype(BF16)
    rb = jnp.full((1, ROUTER_PAD), NEG_BIG, F32).at[0, :n_exp].set(router_b)
    w_bf = w_out.astype(BF16)
    glu = glu_w is not None

    def rows(width):
        return pl.BlockSpec((tm, width), lambda i: (i, 0))

    def full(r, c):
        return pl.BlockSpec((r, c), lambda i: (0, 0))

    def per_batch():
        return pl.BlockSpec((1, 1, d), lambda i: (i // per_seq, 0, 0))

    in_specs = [rows(d), rows(wa_rows), rows(wb_rows)]
    args = [x2d, mix_a, mix_b]
    if glu:
        in_specs += [full(wb_rows, wb_rows), full(1, wb_rows)]
        args += [glu_w.astype(BF16), glu_b.reshape(1, wb_rows)]
    in_specs += [pl.BlockSpec((wa_rows, d), lambda i: (0, 0)),
                 pl.BlockSpec((wb_rows, d), lambda i: (wa_rows // wb_rows, 0)),
                 per_batch(), full(1, d), per_batch(), per_batch(),
                 full(d, ROUTER_PAD), full(d, ROUTER_PAD), full(1, ROUTER_PAD)]
    args += [w_bf, w_bf, g_mix, gain.reshape(1, d), shift, scale, rw_hi, rw_lo, rb]
    return pl.pallas_call(
        functools.partial(_outproj_kernel, glu=glu),
        grid=(t // tm,),
        in_specs=in_specs,
        out_specs=[rows(d), rows(d), rows(ROUTER_PAD), rows(ROUTER_PAD), full(1, ROUTER_PAD)],
        out_shape=[jax.ShapeDtypeStruct((t, d), F32), jax.ShapeDtypeStruct((t, d), F32),
                   jax.ShapeDtypeStruct((t, ROUTER_PAD), I32), jax.ShapeDtypeStruct((t, ROUTER_PAD), F32),
                   jax.ShapeDtypeStruct((1, ROUTER_PAD), I32)],
        scratch_shapes=[pltpu.VMEM((1, ROUTER_PAD), F32)],
        compiler_params=_cparams(("arbitrary",)),
        name="out_proj_glu" if glu else "out_proj",
    )(*args)


def _routing_tables(top_idx, rank, counts, n_tok):
    n_assign = n_tok * TOP_K
    r = MOE_ROWS
    n_rb = n_assign // r + N_EXPERTS
    experts = jnp.arange(N_EXPERTS, dtype=I32)
    nblk = (counts + r - 1) // r
    blk_end = jnp.cumsum(nblk)
    blk_start = blk_end - nblk
    row_start = jnp.sum(jnp.where(top_idx[:, :, None] == experts[None, None, :],
                                  (blk_start * r)[None, None, :], 0), axis=-1)
    dest = (row_start + rank).reshape(-1).astype(I32)
    n_valid_blocks = blk_end[-1]
    rb = jnp.arange(n_rb, dtype=I32)
    rb_e = jnp.sum((blk_end[None, :] <= rb[:, None]).astype(I32), axis=1)
    rb_e = jnp.minimum(rb_e, N_EXPERTS - 1).astype(I32)
    tail = jnp.where(nblk > 0, blk_end - 1, -1)
    spare = n_valid_blocks + experts
    spare = jnp.where(spare < n_rb, spare, -1)
    zero_blocks = jnp.concatenate([tail, spare]).astype(I32)
    return dict(dest=dest, rb_e=rb_e, zero_blocks=zero_blocks,
                n_valid_blocks=n_valid_blocks.reshape(1).astype(I32),
                nblk=nblk.astype(I32), blk_start=blk_start.astype(I32), n_rb=n_rb, n_slots=n_rb * r)


def _dispatch_kernel(dest_ref, zb_ref, h_hbm, xs_hbm, zero_ref, zsem_ref, sem_ref):
    i = pl.program_id(0)
    n_steps = pl.num_programs(0)
    r = MOE_ROWS
    td = DISPATCH_TOKENS

    @pl.when(i == 0)
    def _():
        zero_ref[...] = jnp.zeros(zero_ref.shape, zero_ref.dtype)

        def zero_copy(z):
            row0 = pl.multiple_of(zb_ref[z] * r, r)
            return pltpu.make_async_copy(zero_ref, xs_hbm.at[pl.ds(row0, r), :], zsem_ref.at[0])

        def start(z, _):
            @pl.when(zb_ref[z] >= 0)
            def _():
                zero_copy(z).start()
            return 0

        def wait(z, _):
            @pl.when(zb_ref[z] >= 0)
            def _():
                zero_copy(z).wait()
            return 0

        lax.fori_loop(0, zb_ref.shape[0], start, 0)
        lax.fori_loop(0, zb_ref.shape[0], wait, 0)

    slot = i % 2

    def group(g, _):
        t0 = i * td + g * DMA_UNROLL
        for s in range(DMA_UNROLL):
            for k in range(TOP_K):
                dst = dest_ref[(t0 + s) * TOP_K + k]
                pltpu.make_async_copy(h_hbm.at[pl.ds(t0 + s, 1), :], xs_hbm.at[pl.ds(dst, 1), :],
                                      sem_ref.at[slot]).start()
        return 0

    lax.fori_loop(0, td // DMA_UNROLL, group, 0)

    def drain(sl):
        pltpu.make_async_copy(h_hbm.at[pl.ds(0, td * TOP_K), :], xs_hbm.at[pl.ds(0, td * TOP_K), :],
                              sem_ref.at[sl]).wait()

    @pl.when(i >= 1)
    def _():
        drain(1 - slot)

    @pl.when(i == n_steps - 1)
    def _():
        drain(slot)


def _moe_dispatch(h2d, rt):
    t, d = h2d.shape
    grid_spec = pltpu.PrefetchScalarGridSpec(
        num_scalar_prefetch=2,
        grid=(t // DISPATCH_TOKENS,),
        in_specs=[pl.BlockSpec(memory_space=pl.ANY)],
        out_specs=pl.BlockSpec(memory_space=pl.ANY),
        scratch_shapes=[pltpu.VMEM((MOE_ROWS, d), F32), pltpu.SemaphoreType.DMA((1,)),
                        pltpu.SemaphoreType.DMA((2,))],
    )
    return pl.pallas_call(
        _dispatch_kernel,
        grid_spec=grid_spec,
        out_shape=jax.ShapeDtypeStruct((rt['n_slots'], d), F32),
        compiler_params=_cparams(("arbitrary",)),
        name="moe_dispatch",
    )(rt['dest'], rt['zero_blocks'], h2d)


def _cast_rows(src_ref, dst_ref, chunk=256):
    n = dst_ref.shape[0] // chunk

    def body(c, _):
        sl = pl.ds(pl.multiple_of(c * chunk, chunk), chunk)
        dst_ref[sl, :] = src_ref[sl, :].astype(dst_ref.dtype)
        return 0

    lax.fori_loop(0, n, body, 0)


def _moe_up_kernel(rb_ref, j_ref, e_ref, first_ref, valid_ref,
                   x_ref, wg_ref, wu_ref, bg_ref, bu_ref, o_ref, wgs_ref, wus_ref):
    i = pl.program_id(0)

    @pl.when(first_ref[i] == 1)
    def _():
        _cast_rows(wg_ref.at[0, 0], wgs_ref)
        _cast_rows(wu_ref.at[0, 0], wus_ref)

    @pl.when(valid_ref[i] == 1)
    def _():
        x = x_ref[...].astype(BF16)
        gate = jnp.dot(x, wgs_ref[...], preferred_element_type=F32) + bg_ref[0]
        up = jnp.dot(x, wus_ref[...], preferred_element_type=F32) + bu_ref[0]
        gate = jnp.minimum(gate, SWIGLU_LIMIT)
        up = jnp.clip(up, -SWIGLU_LIMIT, SWIGLU_LIMIT)
        act = (up + 1.0) * gate * jax.nn.sigmoid(SWIGLU_ALPHA * gate)
        o_ref[...] = act.astype(o_ref.dtype)

    @pl.when(valid_ref[i] == 0)
    def _():
        o_ref[...] = jnp.zeros(o_ref.shape, o_ref.dtype)


def _moe_up(x_sorted, w_gu, b_gu, layer, rt):
    n_slots, d = x_sorted.shape
    _, n_exp, _, two_f = w_gu.shape
    f = two_f // 2
    r, tn = MOE_ROWS, MOE_UP_TN
    nj = f // tn
    n_rb = rt['n_rb']
    n_items = n_rb * nj
    rb = jnp.arange(n_rb, dtype=I32)
    e_of = rt['rb_e']
    nvb = rt['n_valid_blocks'][0]
    nblk = rt['nblk'].at[n_exp - 1].add(n_rb - nvb)
    q = rb - rt['blk_start'][e_of]
    pos = (nj * rt['blk_start'][e_of][:, None]
           + jnp.arange(nj, dtype=I32)[None, :] * nblk[e_of][:, None] + q[:, None]).reshape(-1)
    rb2 = jnp.broadcast_to(rb[:, None], (n_rb, nj)).reshape(-1)
    j2 = jnp.broadcast_to(jnp.arange(nj, dtype=I32)[None, :], (n_rb, nj)).reshape(-1)
    it_rb = jnp.zeros((n_items,), I32).at[pos].set(rb2)
    it_j = jnp.zeros((n_items,), I32).at[pos].set(j2)
    it_valid = (it_rb < nvb).astype(I32)
    it_e = e_of[it_rb]
    prev_e = jnp.concatenate([jnp.full((1,), -1, I32), it_e[:-1]])
    prev_j = jnp.concatenate([jnp.full((1,), -1, I32), it_j[:-1]])
    it_first = ((it_e != prev_e) | (it_j != prev_j)).astype(I32)

    grid_spec = pltpu.PrefetchScalarGridSpec(
        num_scalar_prefetch=5,
        grid=(n_items,),
        in_specs=[
            pl.BlockSpec((r, d), lambda i, rbt, jt, et, ft, vt: (rbt[i], 0)),
            pl.BlockSpec((1, 1, d, tn), lambda i, rbt, jt, et, ft, vt: (layer, et[i], 0, jt[i])),
            pl.BlockSpec((1, 1, d, tn), lambda i, rbt, jt, et, ft, vt: (layer, et[i], 0, nj + jt[i])),
            pl.BlockSpec((1, 1, tn), lambda i, rbt, jt, et, ft, vt: (et[i], 0, jt[i])),
            pl.BlockSpec((1, 1, tn), lambda i, rbt, jt, et, ft, vt: (et[i], 0, nj + jt[i])),
        ],
        out_specs=pl.BlockSpec((r, tn), lambda i, rbt, jt, et, ft, vt: (rbt[i], jt[i])),
        scratch_shapes=[pltpu.VMEM((d, tn), BF16), pltpu.VMEM((d, tn), BF16)],
    )
    return pl.pallas_call(
        _moe_up_kernel,
        grid_spec=grid_spec,
        out_shape=jax.ShapeDtypeStruct((n_slots, f), BF16),
        compiler_params=_cparams(("arbitrary",)),
        name="moe_up",
    )(it_rb, it_j, it_e, it_first, it_valid, x_sorted, w_gu, w_gu,
      b_gu.reshape(n_exp, 1, two_f), b_gu.reshape(n_exp, 1, two_f))


def _moe_dn_kernel(e_ref, nvb_ref, a_ref, w_ref, b_ref, o_ref, ws_ref):
    i = pl.program_id(0)
    valid = i < nvb_ref[0]
    e = e_ref[i]
    prev_e = e_ref[jnp.maximum(i - 1, 0)]

    @pl.when(valid & ((i == 0) | (e != prev_e)))
    def _():
        _cast_rows(w_ref.at[0, 0], ws_ref)

    @pl.when(valid)
    def _():
        o_ref[...] = jnp.dot(a_ref[...], ws_ref[...], preferred_element_type=F32) + b_ref[0]

    @pl.when(jnp.logical_not(valid))
    def _():
        o_ref[...] = jnp.zeros(o_ref.shape, o_ref.dtype)


def _moe_dn(act, w_dn, b_dn, layer, rt):
    n_slots, f = act.shape
    _, n_exp, _, d = w_dn.shape
    r = MOE_ROWS
    grid_spec = pltpu.PrefetchScalarGridSpec(
        num_scalar_prefetch=2,
        grid=(rt['n_rb'],),
        in_specs=[
            pl.BlockSpec((r, f), lambda i, et, nvb: (jnp.minimum(i, nvb[0] - 1), 0)),
            pl.BlockSpec((1, 1, f, d), lambda i, et, nvb: (layer, et[i], 0, 0)),
            pl.BlockSpec((1, 1, d), lambda i, et, nvb: (et[i], 0, 0)),
        ],
        out_specs=pl.BlockSpec((r, d), lambda i, et, nvb: (i, 0)),
        scratch_shapes=[pltpu.VMEM((f, d), BF16)],
    )
    return pl.pallas_call(
        _moe_dn_kernel,
        grid_spec=grid_spec,
        out_shape=jax.ShapeDtypeStruct((n_slots, d), F32),
        compiler_params=_cparams(("arbitrary",)),
        name="moe_dn",
    )(rt['rb_e'], rt['n_valid_blocks'], act, w_dn, b_dn.reshape(n_exp, 1, d))


def _combine_kernel(dest_ref, x_ref, gate_ref, g_ref, fg_ref, y_hbm, o_ref, buf_ref, sem_ref, *, final):
    i = pl.program_id(0)
    n_steps = pl.num_programs(0)
    tc = COMBINE_TOKENS

    def issue(step, slot):
        def group(g, _):
            for s in range(DMA_UNROLL):
                t = g * DMA_UNROLL + s
                for k in range(TOP_K):
                    src = dest_ref[(step * tc + t) * TOP_K + k]
                    pltpu.make_async_copy(y_hbm.at[pl.ds(src, 1), :], buf_ref.at[slot, k, pl.ds(t, 1), :],
                                          sem_ref.at[slot]).start()
            return 0

        lax.fori_loop(0, tc // DMA_UNROLL, group, 0)

    @pl.when(i == 0)
    def _():
        issue(0, 0)

    @pl.when(i + 1 < n_steps)
    def _():
        issue(i + 1, (i + 1) % 2)

    slot = i % 2
    for k in range(TOP_K):
        pltpu.make_async_copy(y_hbm.at[pl.ds(0, tc), :], buf_ref.at[slot, k], sem_ref.at[slot]).wait()
    gates = gate_ref[...]
    y = gates[:, 0:1] * buf_ref[slot, 0]
    for k in range(1, TOP_K):
        y = y + gates[:, k:k + 1] * buf_ref[slot, k]
    x2 = x_ref[...] + g_ref[0] * y
    if final:
        x2 = x2 * lax.rsqrt(jnp.mean(x2 * x2, axis=-1, keepdims=True) + EPS) * fg_ref[...]
    o_ref[...] = x2


def _combine(x1, y_sorted, gates, rt, g_ffn, final_gain, seq, final):
    t, d = x1.shape
    tc = COMBINE_TOKENS
    per_seq = seq // tc
    grid_spec = pltpu.PrefetchScalarGridSpec(
        num_scalar_prefetch=1,
        grid=(t // tc,),
        in_specs=[pl.BlockSpec((tc, d), lambda i, dst: (i, 0)),
                  pl.BlockSpec((tc, ROUTER_PAD), lambda i, dst: (i, 0)),
                  pl.BlockSpec((1, 1, d), lambda i, dst: (i // per_seq, 0, 0)),
                  pl.BlockSpec((1, d), lambda i, dst: (0, 0)),
                  pl.BlockSpec(memory_space=pl.ANY)],
        out_specs=pl.BlockSpec((tc, d), lambda i, dst: (i, 0)),
        scratch_shapes=[pltpu.VMEM((2, TOP_K, tc, d), F32), pltpu.SemaphoreType.DMA((2,))],
    )
    return pl.pallas_call(
        functools.partial(_combine_kernel, final=final),
        grid_spec=grid_spec,
        out_shape=jax.ShapeDtypeStruct((t, d), F32),
        compiler_params=_cparams(("arbitrary",)),
        name="combine_final" if final else "combine",
    )(rt['dest'], x1, gates, g_ffn, final_gain.reshape(1, d), y_sorted)


def _moe(x1, h, idx, gates, counts, w_gu, b_gu, w_dn, b_dn, layer, g_ffn, final_gain, seq, final):
    n_tok = x1.shape[0]
    rt = _routing_tables(idx[:, :TOP_K], idx[:, TOP_K:2 * TOP_K], counts[0, :N_EXPERTS], n_tok)
    x_sorted = _moe_dispatch(h, rt)
    act = _moe_up(x_sorted, w_gu, b_gu, layer, rt)
    y_sorted = _moe_dn(act, w_dn, b_dn, layer, rt)
    return _combine(x1, y_sorted, gates, rt, g_ffn, final_gain, seq, final)


def kernel(x, c, ada_w, ada_b, norm_mix_gain, norm_ffn_gain, ab_w_in, ab_w_out, hg_lb_logits, hg_norm_gain, s5_lam_re, s5_lam_im, s5_log_dt, s5_b_re, s5_b_im, s5_c_re, s5_c_im, s5_d, s5_glu_w, s5_glu_b, cd_w_in, cd_w_out, ret_norm_gain, router_w, router_b, moe_w_gu, moe_b_gu, moe_w_dn, moe_b_dn, final_gain):
    bsz, seq, d = x.shape
    depth = ada_w.shape[0]
    n_tok = bsz * seq
    hg_width = hg_lb_logits.shape[1]
    s5_width = s5_glu_w.shape[1]
    ret_width = ret_norm_gain.shape[1]
    fnet_width = cd_w_out.shape[1] - ret_width

    lower_bounds = jnp.cumsum(jax.nn.softmax(hg_lb_logits.astype(F32), axis=0), axis=0)
    mod = _ada_mod(c, ada_w, ada_b)
    xr = x.reshape(n_tok, d)
    for layer in range(depth):
        sh_mix, sc_mix, g_mix, sh_ffn, sc_ffn, g_ffn = (
            mod[layer, :, k * d:(k + 1) * d].reshape(bsz, 1, d) for k in range(6))
        j = layer // 2
        if layer % 2 == 0:
            proj = _in_proj(xr, norm_mix_gain[layer], sh_mix, sc_mix, ab_w_in[j].astype(BF16), seq)
            proj3 = proj.reshape(bsz, seq, proj.shape[1])
            mix_a = _hgrn2(proj3, lower_bounds[j], hg_norm_gain[j], hg_width)
            tables = _s5_tables(s5_lam_re[j], s5_lam_im[j], s5_log_dt[j], s5_b_re[j], s5_b_im[j],
                                s5_c_re[j], s5_c_im[j])
            mix_b = _s5(proj3, 5 * hg_width, tables, s5_d[j].reshape(-1))
            x1, h, idx, gate, counts = _out_proj(
                xr, mix_a.reshape(n_tok, hg_width), mix_b.reshape(n_tok, s5_width), ab_w_out[j],
                g_mix, norm_ffn_gain[layer], sh_ffn, sc_ffn, router_w[layer], router_b[layer], seq,
                glu_w=s5_glu_w[j], glu_b=s5_glu_b[j])
        else:
            proj = _in_proj(xr, norm_mix_gain[layer], sh_mix, sc_mix, cd_w_in[j].astype(BF16), seq)
            proj3 = proj.reshape(bsz, seq, proj.shape[1])
            mix_a = _retention(proj3, ret_norm_gain[j], ret_width)
            mix_b = _fnet(proj3, 4 * ret_width, fnet_width)
            x1, h, idx, gate, counts = _out_proj(
                xr, mix_a.reshape(n_tok, ret_width), mix_b.reshape(n_tok, fnet_width), cd_w_out[j],
                g_mix, norm_ffn_gain[layer], sh_ffn, sc_ffn, router_w[layer], router_b[layer], seq)
        xr = _moe(x1, h, idx, gate, counts, moe_w_gu, moe_b_gu[layer], moe_w_dn, moe_b_dn[layer],
                  layer, g_ffn, final_gain, seq, final=(layer == depth - 1))
    return xr.reshape(bsz, seq, d)
```

```python
import functools
import math

import jax
import jax.numpy as jnp
from jax import lax
from jax.experimental import pallas as pl
from jax.experimental.pallas import tpu as pltpu

F32 = jnp.float32
BF16 = jnp.bfloat16
I32 = jnp.int32

EPS = 1e-6
LANES = 128
SUBLANES = 8
VMEM_LIMIT = 56 * 1024 * 1024

HG_HEAD_DIM = 128
HG_CHUNK = 64
HG_GROUP = 256
HG_EXP_CLAMP = 80.0

S5_GROUP = 16
S5_STATE = 64
S5_CHUNK = 16
S5_UNROLL = 8
S5_TILE_GROUPS = 16
S5_TIME_BLOCK = 512
S5_SCAN_BLOCK = SUBLANES * SUBLANES

RET_HEAD_DIM = 256
RET_Q_TILE = 256
ROPE_BASE = 10000.0

FNET_GROUPS = 4
FNET_ROW_TILE = 512

N_EXPERTS = 32
TOP_K = 4
SWIGLU_LIMIT = 7.0
SWIGLU_ALPHA = 1.702
MOE_ROWS = 256
MOE_UP_TN = 1024
DISPATCH_TOKENS = 512
COMBINE_TOKENS = 128
DMA_UNROLL = 8
ROUTER_PAD = LANES
NEG_BIG = -1e30


def _cparams(semantics):
    return pltpu.CompilerParams(dimension_semantics=semantics, vmem_limit_bytes=VMEM_LIMIT)


def _ada_kernel(c_ref, w_ref, b_ref, o_ref):
    c = c_ref[...]
    cond = c * jax.nn.sigmoid(c)
    o_ref[0] = jnp.dot(cond.astype(BF16), w_ref[0].astype(BF16),
                       preferred_element_type=F32) + b_ref[0]


def _ada_mod(c, ada_w, ada_b):
    depth, d, n = ada_w.shape
    bsz = c.shape[0]
    tn = 1024
    return pl.pallas_call(
        _ada_kernel,
        grid=(depth, n // tn),
        in_specs=[
            pl.BlockSpec((bsz, d), lambda l, j: (0, 0)),
            pl.BlockSpec((1, d, tn), lambda l, j: (l, 0, j)),
            pl.BlockSpec((1, 1, tn), lambda l, j: (l, 0, j)),
        ],
        out_specs=pl.BlockSpec((1, bsz, tn), lambda l, j: (l, 0, j)),
        out_shape=jax.ShapeDtypeStruct((depth, bsz, n), F32),
        compiler_params=_cparams(("arbitrary", "arbitrary")),
        name="ada_mod",
    )(c, ada_w, ada_b.reshape(depth, 1, n))


def _norm_modulate(x, gain, shift, scale):
    ms = jnp.mean(x * x, axis=-1, keepdims=True)
    y = x * lax.rsqrt(ms + EPS) * gain
    return y * (1.0 + scale) + shift


def _inproj_kernel(x_ref, gain_ref, sh_ref, sc_ref, w_ref, o_ref, h_ref):
    @pl.when(pl.program_id(1) == 0)
    def _():
        h_ref[...] = _norm_modulate(x_ref[...], gain_ref[...], sh_ref[0], sc_ref[0]).astype(BF16)

    o_ref[...] = jnp.dot(h_ref[...], w_ref[...], preferred_element_type=F32)


def _in_proj(x2d, gain, shift, scale, w_bf16, seq):
    t, d = x2d.shape
    n = w_bf16.shape[1]
    tm, tn = 1024, 512
    per_seq = seq // tm
    return pl.pallas_call(
        _inproj_kernel,
        grid=(t // tm, n // tn),
        in_specs=[
            pl.BlockSpec((tm, d), lambda i, j: (i, 0)),
            pl.BlockSpec((1, d), lambda i, j: (0, 0)),
            pl.BlockSpec((1, 1, d), lambda i, j: (i // per_seq, 0, 0)),
            pl.BlockSpec((1, 1, d), lambda i, j: (i // per_seq, 0, 0)),
            pl.BlockSpec((d, tn), lambda i, j: (0, j)),
        ],
        out_specs=pl.BlockSpec((tm, tn), lambda i, j: (i, j)),
        out_shape=jax.ShapeDtypeStruct((t, n), F32),
        scratch_shapes=[pltpu.VMEM((tm, d), BF16)],
        compiler_params=_cparams(("arbitrary", "arbitrary")),
        name="in_proj",
    )(x2d, gain.reshape(1, d), shift, scale, w_bf16)


def _split3(a):
    hi = a.astype(BF16)
    r1 = a - hi.astype(F32)
    mid = r1.astype(BF16)
    lo = (r1 - mid.astype(F32)).astype(BF16)
    return hi, mid, lo


def _tri_sum(tri, a):
    hi, mid, lo = _split3(a)
    return (jnp.dot(tri, hi, preferred_element_type=F32)
            + jnp.dot(tri, mid, preferred_element_type=F32)
            + jnp.dot(tri, lo, preferred_element_type=F32))


def _dot_nt(a, b):
    return lax.dot_general(a, b, (((1,), (1,)), ((), ())), preferred_element_type=F32)


def _dot_tn(a, b):
    return lax.dot_general(a, b, (((0,), (0,)), ((), ())), preferred_element_type=F32)


def _hgrn2_kernel(q_ref, zf_ref, zb_ref, v_ref, g_ref, lb_ref, gain_ref, o_ref, acc_ref):
    seq = q_ref.shape[1]
    ln = HG_CHUNK
    gr = HG_GROUP
    n_groups = seq // gr
    per_group = gr // ln
    lb = lb_ref[...]
    gain = gain_ref[...]
    row = lax.broadcasted_iota(I32, (gr, gr), 0)
    col = lax.broadcasted_iota(I32, (gr, gr), 1)
    chunk_lo = (row // ln) * ln
    chunk_hi = chunk_lo + ln
    lower_incl = (col <= row) & (col >= chunk_lo)
    upper_strict = (col > row) & (col < chunk_hi)
    tri_prefix = lower_incl.astype(BF16)
    tri_suffix = ((col >= row) & (col < chunk_hi)).astype(BF16)
    mid = ln // 2

    def per_chunk_rows(a, offset):
        return jnp.concatenate(
            [jnp.broadcast_to(a[j * ln + offset:j * ln + offset + 1, :], (ln, a.shape[1]))
             for j in range(per_group)], axis=0)

    def group(gi, state_t, z_ref, forward):
        sl = pl.ds(pl.multiple_of(gi * gr, gr), gr)
        q = q_ref[0, sl, :]
        v = v_ref[0, sl, :].astype(BF16)
        f = lb + (1.0 - lb) * jax.nn.sigmoid(z_ref[0, sl, :])
        log_f = jnp.log(f)
        k = 1.0 - f
        if forward:
            cum = _tri_sum(tri_prefix, log_f)
            ref_rows = per_chunk_rows(cum, mid - 1)
            edge_off = ln - 1
            mask = lower_incl
        else:
            cum = _tri_sum(tri_suffix, log_f)
            ref_rows = per_chunk_rows(cum, mid)
            edge_off = 0
            mask = upper_strict
        kd = (k * jnp.exp(per_chunk_rows(cum, edge_off) - cum)).astype(BF16)
        rows = [slice(j * ln, (j + 1) * ln) for j in range(per_group)]
        local = [_dot_tn(v[rs], kd[rs]) for rs in rows]
        qe = q * jnp.exp(jnp.minimum(cum - ref_rows, HG_EXP_CLAMP))
        ke = k * jnp.exp(jnp.minimum(ref_rows - cum, HG_EXP_CLAMP))
        scores = jnp.where(mask, _dot_nt(qe.astype(BF16), ke.astype(BF16)), 0.0)
        qc = (q * jnp.exp(cum)).astype(BF16)
        entering = [None] * per_group
        order = range(per_group) if forward else range(per_group - 1, -1, -1)
        for j in order:
            entering[j] = state_t.astype(BF16)
            edge = cum[j * ln + edge_off:j * ln + edge_off + 1, :]
            state_t = state_t * jnp.exp(edge) + local[j]
        inter = [_dot_nt(qc[rs], entering[j]) for j, rs in enumerate(rows)]
        intra = jnp.dot(scores.astype(BF16), v, preferred_element_type=F32)
        return sl, intra + jnp.concatenate(inter, axis=0), state_t

    def fwd_body(gi, state_t):
        sl, out, new_state_t = group(gi, state_t, zf_ref, True)
        acc_ref[sl, :] = out
        return new_state_t

    def bwd_body(i, state_t):
        sl, out, new_state_t = group(n_groups - 1 - i, state_t, zb_ref, False)
        o = acc_ref[sl, :] + out
        y = o * lax.rsqrt(jnp.mean(o * o, axis=-1, keepdims=True) + EPS) * gain
        g = g_ref[0, sl, :]
        o_ref[0, sl, :] = (y * (g * jax.nn.sigmoid(g))).astype(o_ref.dtype)
        return new_state_t

    zero = jnp.zeros((HG_HEAD_DIM, HG_HEAD_DIM), F32)
    lax.fori_loop(0, n_groups, fwd_body, zero)
    lax.fori_loop(0, n_groups, bwd_body, zero)


def _hgrn2(proj3, lower_bound, hg_gain, width):
    bsz, seq, _ = proj3.shape
    heads = width // HG_HEAD_DIM
    dh = HG_HEAD_DIM

    def col(k):
        return pl.BlockSpec((1, seq, dh), lambda b, h: (b, 0, k * heads + h))

    return pl.pallas_call(
        _hgrn2_kernel,
        grid=(bsz, heads),
        in_specs=[col(0), col(1), col(2), col(3), col(4),
                  pl.BlockSpec((1, dh), lambda b, h: (0, h)),
                  pl.BlockSpec((1, dh), lambda b, h: (0, 0))],
        out_specs=pl.BlockSpec((1, seq, dh), lambda b, h: (b, 0, h)),
        out_shape=jax.ShapeDtypeStruct((bsz, seq, width), BF16),
        scratch_shapes=[pltpu.VMEM((seq, dh), F32)],
        compiler_params=_cparams(("arbitrary", "arbitrary")),
        name="hgrn2",
    )(proj3, proj3, proj3, proj3, proj3, lower_bound.reshape(1, width), hg_gain.reshape(1, dh))


S5_TBL_STEP = 0
S5_TBL_X16 = 8
S5_TBL_X32 = 9
S5_TBL_CARRY = 10
S5_N_TBL = 11


def _cmul(ar, ai, br, bi):
    return ar * br - ai * bi, ar * bi + ai * br


def _s5_scan_block(bu_ref, carry_ref, pw_ref, direction, base, colblk, width):
    forward = direction == 0
    cre = pl.ds(colblk * LANES, LANES)
    cim = pl.ds(width + colblk * LANES, LANES)
    bre = colblk
    bim = width // LANES + colblk

    def tbl(idx):
        return pw_ref[direction, 0, 0, idx, :, cre], pw_ref[direction, 0, 1, idx, :, cre]

    def rows(k):
        return pl.ds(base + k, SUBLANES, stride=SUBLANES)

    a_r, a_i = tbl(S5_TBL_STEP)
    order = list(range(SUBLANES)) if forward else list(range(SUBLANES - 1, -1, -1))
    xr = [None] * SUBLANES
    xi = [None] * SUBLANES
    pr = pi = None
    for k in order:
        br = bu_ref.at[bre][rows(k), :]
        bi = bu_ref.at[bim][rows(k), :]
        if pr is not None:
            mr, mi = _cmul(a_r, a_i, pr, pi)
            br = br + mr
            bi = bi + mi
        xr[k], xi[k] = br, bi
        pr, pi = br, bi

    er, ei = pr, pi
    sub = lax.broadcasted_iota(I32, (SUBLANES, LANES), 0)
    for step, idx in ((1, S5_TBL_STEP + 7), (2, S5_TBL_X16), (4, S5_TBL_X32)):
        m_r, m_i = tbl(idx)
        if forward:
            shift, keep = step, sub >= step
        else:
            shift, keep = SUBLANES - step, sub < SUBLANES - step
        sr = jnp.where(keep, pltpu.roll(er, shift, 0), 0.0)
        si = jnp.where(keep, pltpu.roll(ei, shift, 0), 0.0)
        dr, di = _cmul(m_r, m_i, sr, si)
        er, ei = er + dr, ei + di
    c_r = carry_ref[:, cre]
    c_i = carry_ref[:, cim]
    t_r, t_i = tbl(S5_TBL_CARRY)
    dr, di = _cmul(t_r, t_i, c_r, c_i)
    er, ei = er + dr, ei + di

    if forward:
        nr = jnp.where(sub >= 1, pltpu.roll(er, 1, 0), c_r)
        ni = jnp.where(sub >= 1, pltpu.roll(ei, 1, 0), c_i)
        last = SUBLANES - 1
    else:
        nr = jnp.where(sub < SUBLANES - 1, pltpu.roll(er, SUBLANES - 1, 0), c_r)
        ni = jnp.where(sub < SUBLANES - 1, pltpu.roll(ei, SUBLANES - 1, 0), c_i)
        last = 0
    carry_ref[:, cre] = jnp.broadcast_to(er[last:last + 1, :], (SUBLANES, LANES))
    carry_ref[:, cim] = jnp.broadcast_to(ei[last:last + 1, :], (SUBLANES, LANES))

    for k in range(SUBLANES):
        f_r, f_i = tbl(S5_TBL_STEP + (k if forward else SUBLANES - 1 - k))
        dr, di = _cmul(f_r, f_i, nr, ni)
        bu_ref.at[bre][rows(k), :] = xr[k] + dr
        bu_ref.at[bim][rows(k), :] = xi[k] + di


def _s5_kernel(u_ref, bm_ref, cm_ref, pw_ref, d_ref, z_ref, bu_ref, y_ref, carry_ref):
    seq = u_ref.shape[1]
    n_cb = int(bu_ref.shape[0])
    width = n_cb * LANES // 2
    tb = S5_TIME_BLOCK
    n_tb = seq // tb
    n_sb = tb // S5_SCAN_BLOCK
    d_skip = d_ref[...]

    for direction in (0, 1):
        carry_ref[...] = jnp.zeros(carry_ref.shape, F32)

        def time_block(it, _, direction=direction):
            blk = it if direction == 0 else n_tb - 1 - it
            sl = pl.ds(pl.multiple_of(blk * tb, tb), tb)
            u = u_ref[0, sl, :]
            bu = jnp.dot(u.astype(BF16), bm_ref[direction, 0], preferred_element_type=F32)
            for cb in range(n_cb):
                bu_ref[cb] = bu[:, cb * LANES:(cb + 1) * LANES]

            def scan_block(js, _):
                sb = js if direction == 0 else n_sb - 1 - js
                base = pl.multiple_of(sb * S5_SCAN_BLOCK, S5_SCAN_BLOCK)
                for colblk in range(width // LANES):
                    _s5_scan_block(bu_ref, carry_ref, pw_ref, direction, base, colblk, width)
                return 0

            lax.fori_loop(0, n_sb, scan_block, 0)
            xs = jnp.concatenate([bu_ref[cb].astype(BF16) for cb in range(n_cb)], axis=-1)
            y = jnp.dot(xs, cm_ref[direction, 0], preferred_element_type=F32)
            if direction == 0:
                y_ref[sl, :] = y
            else:
                z_ref[0, sl, :] = jax.nn.gelu(y_ref[sl, :] + y + d_skip * u).astype(z_ref.dtype)
            return 0

        lax.fori_loop(0, n_tb, time_block, 0)


def _s5_tables(lam_re, lam_im, log_dt, b_re, b_im, c_re, c_im):
    groups, state = lam_re.shape[1], lam_re.shape[2]
    chans = b_re.shape[2]
    tg = S5_TILE_GROUPS
    tiles = groups // tg
    eye = jnp.eye(tg, dtype=F32)
    bms, cms, pws = [], [], []
    sub = jnp.arange(SUBLANES, dtype=F32)
    for direction in (0, 1):
        lr, li = lam_re[direction].astype(F32), lam_im[direction].astype(F32)
        dt = jnp.exp(log_dt[direction].astype(F32))[:, None]
        mag = jnp.exp(lr * dt)
        abar_re = mag * jnp.cos(li * dt)
        abar_im = mag * jnp.sin(li * dt)
        den = lr * lr + li * li
        num_re = abar_re - 1.0
        coef_re = (num_re * lr + abar_im * li) / den
        coef_im = (abar_im * lr - num_re * li) / den
        bbar_re = coef_re[..., None] * b_re - coef_im[..., None] * b_im
        bbar_im = coef_re[..., None] * b_im + coef_im[..., None] * b_re

        def blockdiag_in(bb):
            bb = bb.reshape(tiles, tg, state, chans)
            m = jnp.einsum('gh,tgpc->tgchp', eye, bb)
            return m.reshape(tiles, tg * chans, tg * state)

        def blockdiag_out(cc):
            cc = cc.reshape(tiles, tg, chans, state)
            m = jnp.einsum('gh,tgcp->tgphc', eye, cc)
            return m.reshape(tiles, tg * state, tg * chans)

        bms.append(jnp.concatenate([blockdiag_in(bbar_re), blockdiag_in(bbar_im)], axis=-1))
        cms.append(jnp.concatenate([blockdiag_out(c_re[direction].astype(F32)),
                                    -blockdiag_out(c_im[direction].astype(F32))], axis=1))

        def power(n):
            return jnp.exp(n * lr * dt) * jnp.cos(n * li * dt), jnp.exp(n * lr * dt) * jnp.sin(n * li * dt)

        rows_re, rows_im = [], []
        for n in list(range(1, SUBLANES + 1)) + [2 * SUBLANES, 4 * SUBLANES]:
            p_re, p_im = power(float(n))
            rows_re.append(jnp.broadcast_to(p_re.reshape(tiles, 1, tg * state), (tiles, SUBLANES, tg * state)))
            rows_im.append(jnp.broadcast_to(p_im.reshape(tiles, 1, tg * state), (tiles, SUBLANES, tg * state)))
        carry_n = SUBLANES * (sub + 1.0) if direction == 0 else SUBLANES * (SUBLANES - sub)
        p_re, p_im = power(carry_n[:, None, None])
        rows_re.append(p_re.reshape(SUBLANES, tiles, tg * state).transpose(1, 0, 2))
        rows_im.append(p_im.reshape(SUBLANES, tiles, tg * state).transpose(1, 0, 2))
        pws.append(jnp.stack([jnp.stack(rows_re, axis=1), jnp.stack(rows_im, axis=1)], axis=1))
    return (jnp.stack(bms).astype(BF16), jnp.stack(cms).astype(BF16), jnp.stack(pws))


def _s5(proj3, col0, tables, d_skip):
    bsz, seq, _ = proj3.shape
    bm, cm, pw = tables
    tiles = bm.shape[1]
    tc = bm.shape[2]
    sw = bm.shape[3]
    cb0 = col0 // tc
    return pl.pallas_call(
        _s5_kernel,
        grid=(bsz, tiles),
        in_specs=[
            pl.BlockSpec((1, seq, tc), lambda b, t: (b, 0, cb0 + t)),
            pl.BlockSpec((2, 1, tc, sw), lambda b, t: (0, t, 0, 0)),
            pl.BlockSpec((2, 1, sw, tc), lambda b, t: (0, t, 0, 0)),
            pl.BlockSpec((2, 1, 2, S5_N_TBL, SUBLANES, sw // 2), lambda b, t: (0, t, 0, 0, 0, 0)),
            pl.BlockSpec((1, tc), lambda b, t: (0, t)),
        ],
        out_specs=pl.BlockSpec((1, seq, tc), lambda b, t: (b, 0, t)),
        out_shape=jax.ShapeDtypeStruct((bsz, seq, tiles * tc), BF16),
        scratch_shapes=[pltpu.VMEM((sw // LANES, S5_TIME_BLOCK, LANES), F32),
                        pltpu.VMEM((seq, tc), F32),
                        pltpu.VMEM((SUBLANES, sw), F32)],
        compiler_params=_cparams(("arbitrary", "arbitrary")),
        name="s5",
    )(proj3, bm, cm, pw, d_skip.reshape(1, tiles * tc))


def _s5c_kernel(u_ref, t_ref, w_ref, v_ref, a_ref, z_ref, e_ref, p_ref):
    rows = u_ref.shape[1]
    n_chunks = rows // SUBLANES
    u = u_ref[0]
    e_ref[...] = jnp.dot(u, w_ref[0], preferred_element_type=F32)
    af_r, af_i, ab_r, ab_i = a_ref[0, 0], a_ref[0, 1], a_ref[0, 2], a_ref[0, 3]
    seg = [pl.ds(k * LANES, LANES) for k in range(4)]

    def step(m, carry):
        xr, xi, yr, yi = carry
        rf = pl.ds(pl.multiple_of(m * SUBLANES, SUBLANES), SUBLANES)
        rb = pl.ds(pl.multiple_of((n_chunks - 1 - m) * SUBLANES, SUBLANES), SUBLANES)
        p_ref[rf, seg[0]] = xr
        p_ref[rf, seg[1]] = xi
        p_ref[rb, seg[2]] = yr
        p_ref[rb, seg[3]] = yi
        dr, di = _cmul(af_r, af_i, xr, xi)
        gr, gi = _cmul(ab_r, ab_i, yr, yi)
        return (dr + e_ref[rf, seg[0]], di + e_ref[rf, seg[1]],
                gr + e_ref[rb, seg[2]], gi + e_ref[rb, seg[3]])

    zero = jnp.zeros((SUBLANES, LANES), F32)
    lax.fori_loop(0, n_chunks, step, (zero, zero, zero, zero), unroll=S5_UNROLL)
    y = (jnp.dot(u, t_ref[0], preferred_element_type=F32)
         + jnp.dot(p_ref[...].astype(BF16), v_ref[0], preferred_element_type=F32))
    z_ref[0] = jax.nn.gelu(y).astype(z_ref.dtype)


def _s5c_tables(lam_re, lam_im, log_dt, b_re, b_im, c_re, c_im, d_skip):
    groups, state = lam_re.shape[1], lam_re.shape[2]
    chans = b_re.shape[2]
    ck = S5_CHUNK
    pairs = groups // 2
    lag = jnp.arange(ck + 1, dtype=F32)[:, None, None]
    kern, w_parts, v_parts, a_parts = [], [], [], []
    for direction in (0, 1):
        lr, li = lam_re[direction].astype(F32), lam_im[direction].astype(F32)
        dt = jnp.exp(log_dt[direction].astype(F32))[:, None]
        mag = jnp.exp(lr * dt)
        abar_re = mag * jnp.cos(li * dt)
        abar_im = mag * jnp.sin(li * dt)
        den = lr * lr + li * li
        num_re = abar_re - 1.0
        coef_re = (num_re * lr + abar_im * li) / den
        coef_im = (abar_im * lr - num_re * li) / den
        bbar_re = coef_re[..., None] * b_re - coef_im[..., None] * b_im
        bbar_im = coef_re[..., None] * b_im + coef_im[..., None] * b_re
        pw_re = jnp.exp(lag * lr * dt) * jnp.cos(lag * li * dt)
        pw_im = jnp.exp(lag * lr * dt) * jnp.sin(lag * li * dt)
        ab_re = pw_re[..., None] * bbar_re - pw_im[..., None] * bbar_im
        ab_im = pw_re[..., None] * bbar_im + pw_im[..., None] * bbar_re
        cr, ci = c_re[direction].astype(F32), c_im[direction].astype(F32)
        kern.append(jnp.einsum('gcp,ngpd->ngcd', cr, ab_re[:ck]) - jnp.einsum('gcp,ngpd->ngcd', ci, ab_im[:ck]))
        order = jnp.arange(ck - 1, -1, -1) if direction == 0 else jnp.arange(ck)
        w_parts.append((ab_re[order], ab_im[order]))
        order = jnp.arange(1, ck + 1) if direction == 0 else jnp.arange(ck, 0, -1)
        a_r, a_i = pw_re[order], pw_im[order]
        v_from_re = jnp.einsum('gcp,jgp->gpjc', cr, a_r) - jnp.einsum('gcp,jgp->gpjc', ci, a_i)
        v_from_im = -(jnp.einsum('gcp,jgp->gpjc', cr, a_i) + jnp.einsum('gcp,jgp->gpjc', ci, a_r))
        v_parts.append((v_from_re, v_from_im))
        a_parts.append((pw_re[ck], pw_im[ck]))

    s_idx = jnp.arange(ck)[:, None]
    t_idx = jnp.arange(ck)[None, :]
    fwd = jnp.where((t_idx >= s_idx)[:, :, None, None, None], kern[0][jnp.clip(t_idx - s_idx, 0, ck - 1)], 0.0)
    bwd = jnp.where((s_idx >= t_idx)[:, :, None, None, None], kern[1][jnp.clip(s_idx - t_idx, 0, ck - 1)], 0.0)
    skip = (jnp.eye(ck, dtype=F32)[:, :, None, None, None]
            * (jnp.eye(chans, dtype=F32)[None, None, None] * d_skip.reshape(groups, chans)[None, None, :, :, None]))
    toep = (fwd + bwd + skip).transpose(2, 0, 4, 1, 3)
    toep = toep.reshape(pairs, 2, ck * chans, ck * chans)
    eye2 = jnp.eye(2, dtype=F32)
    t_mat = jnp.einsum('ab,paxy->paxby', eye2, toep).reshape(pairs, 2 * ck * chans, 2 * ck * chans)

    w_seg = [w_parts[0][0], w_parts[0][1], w_parts[1][0], w_parts[1][1]]
    w_stack = jnp.stack(w_seg, axis=0).transpose(2, 1, 4, 0, 3)
    w_stack = w_stack.reshape(pairs, 2, ck * chans, 4, state)
    w_mat = jnp.einsum('ab,gaxkq->gaxkbq', eye2, w_stack).reshape(pairs, 2 * ck * chans, 4 * 2 * state)

    v_seg = [v_parts[0][0], v_parts[0][1], v_parts[1][0], v_parts[1][1]]
    v_stack = jnp.stack(v_seg, axis=0).reshape(4, pairs, 2, state, ck * chans)
    v_mat = jnp.einsum('ab,kgaqy->gkaqby', eye2, v_stack).reshape(pairs, 4 * 2 * state, 2 * ck * chans)

    a_seg = jnp.stack([a_parts[0][0], a_parts[0][1], a_parts[1][0], a_parts[1][1]], axis=0)
    a_seg = a_seg.reshape(4, pairs, 2 * state).transpose(1, 0, 2)
    a_tbl = jnp.broadcast_to(a_seg[:, :, None, :], (pairs, 4, SUBLANES, 2 * state))
    return t_mat.astype(BF16), w_mat.astype(BF16), v_mat.astype(BF16), a_tbl


def _s5c(u3, tables):
    bsz, seq, width = u3.shape
    assert bsz == SUBLANES, "one chunk's rows (the batch) must fill one sublane tile"
    t_mat, w_mat, v_mat, a_tbl = tables
    pairs = t_mat.shape[0]
    ck = S5_CHUNK
    pc = width // pairs
    n_chunks = seq // ck
    cols = ck * pc
    rows = n_chunks * bsz
    u = u3.astype(BF16).reshape(bsz, n_chunks, ck, pairs, 2, pc // 2)
    u = u.transpose(3, 1, 0, 4, 2, 5).reshape(pairs, rows, cols)
    z = pl.pallas_call(
        _s5c_kernel,
        grid=(pairs,),
        in_specs=[
            pl.BlockSpec((1, rows, cols), lambda p: (p, 0, 0)),
            pl.BlockSpec((1, cols, cols), lambda p: (p, 0, 0)),
            pl.BlockSpec((1, cols, cols), lambda p: (p, 0, 0)),
            pl.BlockSpec((1, cols, cols), lambda p: (p, 0, 0)),
            pl.BlockSpec((1, 4, SUBLANES, LANES), lambda p: (p, 0, 0, 0)),
        ],
        out_specs=pl.BlockSpec((1, rows, cols), lambda p: (p, 0, 0)),
        out_shape=jax.ShapeDtypeStruct((pairs, rows, cols), BF16),
        scratch_shapes=[pltpu.VMEM((rows, cols), F32), pltpu.VMEM((rows, cols), F32)],
        compiler_params=_cparams(("arbitrary",)),
        name="s5",
    )(u, t_mat, w_mat, v_mat, a_tbl)
    z = z.reshape(pairs, n_chunks, bsz, 2, ck, pc // 2).transpose(2, 1, 4, 0, 3, 5)
    return z.reshape(bsz, seq, width)


def _retention_kernel(q_ref, k_ref, v_ref, g_ref, cos_ref, sin_ref, lg_ref, gain_ref,
                      o_ref, qs_ref, ks_ref, vs_ref):
    seq = q_ref.shape[1]
    half = RET_HEAD_DIM // 2
    tq = RET_Q_TILE
    cos = cos_ref[...]
    sin = sin_ref[...]

    def rot(t_ref, scale):
        t1 = t_ref[0, :, :half]
        t2 = t_ref[0, :, half:]
        return jnp.concatenate([(t1 * cos - t2 * sin) * scale, (t1 * sin + t2 * cos) * scale], axis=-1)

    qs_ref[...] = rot(q_ref, 1.0).astype(BF16)
    ks_ref[...] = rot(k_ref, RET_HEAD_DIM ** -0.5).astype(BF16)
    vs_ref[...] = v_ref[0].astype(BF16)
    lg_fwd = lg_ref[0, 0:1, :]
    lg_bwd = lg_ref[0, 1:2, :]
    gain = gain_ref[...]

    def q_tile(i, _):
        sl = pl.ds(pl.multiple_of(i * tq, tq), tq)
        scores = _dot_nt(qs_ref[sl, :], ks_ref[...])
        t_idx = lax.broadcasted_iota(I32, (tq, seq), 0) + i * tq
        s_idx = lax.broadcasted_iota(I32, (tq, seq), 1)
        rel = (t_idx - s_idx).astype(F32)
        decay = jnp.exp(jnp.where(rel >= 0.0, lg_fwd * rel, -lg_bwd * rel))
        p = (scores * decay).astype(BF16)
        o = jnp.dot(p, vs_ref[...], preferred_element_type=F32)
        y = o * lax.rsqrt(jnp.mean(o * o, axis=-1, keepdims=True) + EPS) * gain
        g = g_ref[0, sl, :]
        o_ref[0, sl, :] = (y * (g * jax.nn.sigmoid(g))).astype(o_ref.dtype)
        return 0

    lax.fori_loop(0, seq // tq, q_tile, 0)


def _retention(proj3, ret_gain, width):
    bsz, seq, _ = proj3.shape
    dh = RET_HEAD_DIM
    heads = width // dh
    inv_freq = ROPE_BASE ** (-jnp.arange(0, dh, 2, dtype=F32) / dh)
    ang = jnp.arange(seq, dtype=F32)[:, None] * inv_freq[None, :]
    cos, sin = jnp.cos(ang), jnp.sin(ang)
    log_gamma = jnp.log1p(-jnp.exp2(-5.0 - jnp.arange(heads, dtype=F32)))
    lg = jnp.stack([log_gamma, log_gamma[::-1]], axis=1)
    lg = jnp.broadcast_to(lg[:, :, None], (heads, 2, seq))

    def col(k):
        return pl.BlockSpec((1, seq, dh), lambda b, h: (b, 0, k * heads + h))

    return pl.pallas_call(
        _retention_kernel,
        grid=(bsz, heads),
        in_specs=[col(0), col(1), col(2), col(3),
                  pl.BlockSpec((seq, dh // 2), lambda b, h: (0, 0)),
                  pl.BlockSpec((seq, dh // 2), lambda b, h: (0, 0)),
                  pl.BlockSpec((1, 2, seq), lambda b, h: (h, 0, 0)),
                  pl.BlockSpec((1, dh), lambda b, h: (0, h))],
        out_specs=pl.BlockSpec((1, seq, dh), lambda b, h: (b, 0, h)),
        out_shape=jax.ShapeDtypeStruct((bsz, seq, width), BF16),
        scratch_shapes=[pltpu.VMEM((seq, dh), BF16)] * 3,
        compiler_params=_cparams(("arbitrary", "arbitrary")),
        name="retention",
    )(proj3, proj3, proj3, proj3, cos, sin, lg, ret_gain.reshape(1, width))


def _fnet_kernel(x_ref, cc_ref, sc_ref, cs_ref, ss_ref, o_ref, a1_ref, a2_ref):
    @pl.when(pl.program_id(1) == 0)
    def _():
        x = x_ref[0].astype(BF16)
        a1_ref[...] = jnp.dot(x, cc_ref[...], preferred_element_type=F32).astype(BF16)
        a2_ref[...] = jnp.dot(x, sc_ref[...], preferred_element_type=F32).astype(BF16)

    y = (jnp.dot(cs_ref[...], a1_ref[...], preferred_element_type=F32)
         - jnp.dot(ss_ref[...], a2_ref[...], preferred_element_type=F32))
    o_ref[0] = y.astype(o_ref.dtype)


def _dft_mats(n):
    idx = jnp.arange(n, dtype=I32)
    ang = (2.0 * math.pi / n) * ((idx[:, None] * idx[None, :]) % n).astype(F32)
    scale = n ** -0.5
    return jnp.cos(ang) * scale, jnp.sin(ang) * scale


def _fnet(proj3, col0, width):
    bsz, seq, _ = proj3.shape
    gw = width // FNET_GROUPS
    cs, ss = _dft_mats(seq)
    cg, sg = _dft_mats(gw)
    eye = jnp.eye(FNET_GROUPS, dtype=F32)
    cc = jnp.kron(eye, cg)
    sc = jnp.kron(eye, sg)
    tr = FNET_ROW_TILE
    return pl.pallas_call(
        _fnet_kernel,
        grid=(bsz, seq // tr),
        in_specs=[
            pl.BlockSpec((1, seq, width), lambda b, i: (b, 0, col0 // width)),
            pl.BlockSpec((width, width), lambda b, i: (0, 0)),
            pl.BlockSpec((width, width), lambda b, i: (0, 0)),
            pl.BlockSpec((tr, seq), lambda b, i: (i, 0)),
            pl.BlockSpec((tr, seq), lambda b, i: (i, 0)),
        ],
        out_specs=pl.BlockSpec((1, tr, width), lambda b, i: (b, i, 0)),
        out_shape=jax.ShapeDtypeStruct((bsz, seq, width), BF16),
        scratch_shapes=[pltpu.VMEM((seq, width), BF16)] * 2,
        compiler_params=_cparams(("arbitrary", "arbitrary")),
        name="fnet",
    )(proj3, cc.astype(BF16), sc.astype(BF16), cs.astype(BF16), ss.astype(BF16))


def _outproj_kernel(*refs, glu):
    if glu:
        (x_ref, a_ref, b_ref, gw_ref, gb_ref, wa_ref, wb_ref, gm_ref, gain_ref, sh_ref, sc_ref,
         rwh_ref, rwl_ref, rb_ref, x1_ref, h_ref, idx_ref, gate_ref, cnt_ref, base_ref) = refs
    else:
        (x_ref, a_ref, b_ref, wa_ref, wb_ref, gm_ref, gain_ref, sh_ref, sc_ref,
         rwh_ref, rwl_ref, rb_ref, x1_ref, h_ref, idx_ref, gate_ref, cnt_ref, base_ref) = refs

    @pl.when(pl.program_id(0) == 0)
    def _():
        base_ref[...] = jnp.zeros(base_ref.shape, F32)

    bm = b_ref[...]
    if glu:
        gl = jnp.dot(bm, gw_ref[...], preferred_element_type=F32) + gb_ref[...]
        bm = (bm.astype(F32) * jax.nn.sigmoid(gl)).astype(BF16)
    y = (jnp.dot(a_ref[...], wa_ref[...], preferred_element_type=F32)
         + jnp.dot(bm, wb_ref[...], preferred_element_type=F32))
    x1 = x_ref[...] + gm_ref[0] * y
    x1_ref[...] = x1
    h = _norm_modulate(x1, gain_ref[...], sh_ref[0], sc_ref[0])
    h_ref[...] = h

    h_hi = h.astype(BF16)
    h_lo = (h - h_hi.astype(F32)).astype(BF16)
    logits = (jnp.dot(h_hi, rwh_ref[...], preferred_element_type=F32)
              + jnp.dot(h_lo, rwh_ref[...], preferred_element_type=F32)
              + jnp.dot(h_hi, rwl_ref[...], preferred_element_type=F32)
              + rb_ref[...])
    lane = lax.broadcasted_iota(I32, logits.shape, 1)
    vals, idxs = [], []
    for _ in range(TOP_K):
        m = jnp.max(logits, axis=-1, keepdims=True)
        ik = jnp.min(jnp.where(logits == m, lane, ROUTER_PAD), axis=-1, keepdims=True)
        vals.append(m)
        idxs.append(ik)
        logits = jnp.where(lane == ik, -jnp.inf, logits)
    exps = [jnp.exp(v - vals[0]) for v in vals]
    denom = exps[0] + exps[1] + exps[2] + exps[3]

    tm = lane.shape[0]
    onehot = jnp.zeros(lane.shape, F32)
    for k in range(TOP_K):
        onehot = onehot + (lane == idxs[k]).astype(F32)
    before = (lax.broadcasted_iota(I32, (tm, tm), 1) < lax.broadcasted_iota(I32, (tm, tm), 0)).astype(BF16)
    count = jnp.dot(before, onehot.astype(BF16), preferred_element_type=F32) + base_ref[...]
    new_base = base_ref[...] + jnp.sum(onehot, axis=0, keepdims=True)
    base_ref[...] = new_base
    cnt_ref[...] = new_base.astype(I32)

    idx_out = jnp.zeros(lane.shape, I32)
    gate_out = jnp.zeros(lane.shape, F32)
    for k in range(TOP_K):
        rank = jnp.sum(jnp.where(lane == idxs[k], count, 0.0), axis=-1, keepdims=True).astype(I32)
        idx_out = jnp.where(lane == k, idxs[k], idx_out)
        idx_out = jnp.where(lane == TOP_K + k, rank, idx_out)
        gate_out = jnp.where(lane == k, exps[k] / denom, gate_out)
    idx_ref[...] = idx_out
    gate_ref[...] = gate_out


def _out_proj(x2d, mix_a, mix_b, w_out, g_mix, gain, shift, scale, router_w, router_b, seq,
              glu_w=None, glu_b=None):
    t, d = x2d.shape
    wa_rows = mix_a.shape[1]
    wb_rows = mix_b.shape[1]
    tm = 256
    per_seq = seq // tm
    n_exp = router_w.shape[1]
    rw = jnp.zeros((d, ROUTER_PAD), F32).at[:, :n_exp].set(router_w)
    rw_hi = rw.astype(BF16)
    rw_lo = (rw - rw_hi.astype(F32)).astype(BF16)
    rb = jnp.full((1, ROUTER_PAD), NEG_BIG, F32).at[0, :n_exp].set(router_b)
    w_bf = w_out.astype(BF16)
    glu = glu_w is not None

    def rows(width):
        return pl.BlockSpec((tm, width), lambda i: (i, 0))

    def full(r, c):
        return pl.BlockSpec((r, c), lambda i: (0, 0))

    def per_batch():
        return pl.BlockSpec((1, 1, d), lambda i: (i // per_seq, 0, 0))

    in_specs = [rows(d), rows(wa_rows), rows(wb_rows)]
    args = [x2d, mix_a, mix_b]
    if glu:
        in_specs += [full(wb_rows, wb_rows), full(1, wb_rows)]
        args += [glu_w.astype(BF16), glu_b.reshape(1, wb_rows)]
    in_specs += [pl.BlockSpec((wa_rows, d), lambda i: (0, 0)),
                 pl.BlockSpec((wb_rows, d), lambda i: (wa_rows // wb_rows, 0)),
                 per_batch(), full(1, d), per_batch(), per_batch(),
                 full(d, ROUTER_PAD), full(d, ROUTER_PAD), full(1, ROUTER_PAD)]
    args += [w_bf, w_bf, g_mix, gain.reshape(1, d), shift, scale, rw_hi, rw_lo, rb]
    return pl.pallas_call(
        functools.partial(_outproj_kernel, glu=glu),
        grid=(t // tm,),
        in_specs=in_specs,
        out_specs=[rows(d), rows(d), rows(ROUTER_PAD), rows(ROUTER_PAD), full(1, ROUTER_PAD)],
        out_shape=[jax.ShapeDtypeStruct((t, d), F32), jax.ShapeDtypeStruct((t, d), F32),
                   jax.ShapeDtypeStruct((t, ROUTER_PAD), I32), jax.ShapeDtypeStruct((t, ROUTER_PAD), F32),
                   jax.ShapeDtypeStruct((1, ROUTER_PAD), I32)],
        scratch_shapes=[pltpu.VMEM((1, ROUTER_PAD), F32)],
        compiler_params=_cparams(("arbitrary",)),
        name="out_proj_glu" if glu else "out_proj",
    )(*args)


def _routing_tables(top_idx, rank, counts, n_tok):
    n_assign = n_tok * TOP_K
    r = MOE_ROWS
    n_rb = n_assign // r + N_EXPERTS
    experts = jnp.arange(N_EXPERTS, dtype=I32)
    nblk = (counts + r - 1) // r
    blk_end = jnp.cumsum(nblk)
    blk_start = blk_end - nblk
    row_start = jnp.sum(jnp.where(top_idx[:, :, None] == experts[None, None, :],
                                  (blk_start * r)[None, None, :], 0), axis=-1)
    dest = (row_start + rank).reshape(-1).astype(I32)
    n_valid_blocks = blk_end[-1]
    rb = jnp.arange(n_rb, dtype=I32)
    rb_e = jnp.sum((blk_end[None, :] <= rb[:, None]).astype(I32), axis=1)
    rb_e = jnp.minimum(rb_e, N_EXPERTS - 1).astype(I32)
    tail = jnp.where(nblk > 0, blk_end - 1, -1)
    spare = n_valid_blocks + experts
    spare = jnp.where(spare < n_rb, spare, -1)
    zero_blocks = jnp.concatenate([tail, spare]).astype(I32)
    return dict(dest=dest, rb_e=rb_e, zero_blocks=zero_blocks,
                n_valid_blocks=n_valid_blocks.reshape(1).astype(I32),
                nblk=nblk.astype(I32), blk_start=blk_start.astype(I32), n_rb=n_rb, n_slots=n_rb * r)


def _dispatch_kernel(dest_ref, zb_ref, h_ref, xs_hbm, zero_ref, zsem_ref, sem_ref):
    i = pl.program_id(0)
    r = MOE_ROWS
    td = DISPATCH_TOKENS

    @pl.when(i == 0)
    def _():
        zero_ref[...] = jnp.zeros(zero_ref.shape, zero_ref.dtype)

        def zero_copy(z):
            row0 = pl.multiple_of(zb_ref[z] * r, r)
            return pltpu.make_async_copy(zero_ref, xs_hbm.at[pl.ds(row0, r), :], zsem_ref.at[0])

        def start(z, _):
            @pl.when(zb_ref[z] >= 0)
            def _():
                zero_copy(z).start()
            return 0

        def wait(z, _):
            @pl.when(zb_ref[z] >= 0)
            def _():
                zero_copy(z).wait()
            return 0

        lax.fori_loop(0, zb_ref.shape[0], start, 0)
        lax.fori_loop(0, zb_ref.shape[0], wait, 0)

    def group(g, _):
        for s in range(DMA_UNROLL):
            t = g * DMA_UNROLL + s
            for k in range(TOP_K):
                dst = dest_ref[(i * td + t) * TOP_K + k]
                pltpu.make_async_copy(h_ref.at[pl.ds(t, 1), :], xs_hbm.at[pl.ds(dst, 1), :],
                                      sem_ref.at[0]).start()
        return 0

    lax.fori_loop(0, td // DMA_UNROLL, group, 0)
    for _ in range(TOP_K):
        pltpu.make_async_copy(h_ref, xs_hbm.at[pl.ds(0, td), :], sem_ref.at[0]).wait()


def _moe_dispatch(h2d, rt):
    t, d = h2d.shape
    td = DISPATCH_TOKENS
    grid_spec = pltpu.PrefetchScalarGridSpec(
        num_scalar_prefetch=2,
        grid=(t // td,),
        in_specs=[pl.BlockSpec((td, d), lambda i, dst, zb: (i, 0))],
        out_specs=pl.BlockSpec(memory_space=pl.ANY),
        scratch_shapes=[pltpu.VMEM((MOE_ROWS, d), F32), pltpu.SemaphoreType.DMA((1,)),
                        pltpu.SemaphoreType.DMA((1,))],
    )
    return pl.pallas_call(
        _dispatch_kernel,
        grid_spec=grid_spec,
        out_shape=jax.ShapeDtypeStruct((rt['n_slots'], d), F32),
        compiler_params=_cparams(("arbitrary",)),
        name="moe_dispatch",
    )(rt['dest'], rt['zero_blocks'], h2d)


def _cast_rows(src_ref, dst_ref, chunk=256):
    n = dst_ref.shape[0] // chunk

    def body(c, _):
        sl = pl.ds(pl.multiple_of(c * chunk, chunk), chunk)
        dst_ref[sl, :] = src_ref[sl, :].astype(dst_ref.dtype)
        return 0

    lax.fori_loop(0, n, body, 0)


def _moe_up_kernel(rb_ref, j_ref, e_ref, first_ref, valid_ref,
                   x_ref, wg_ref, wu_ref, bg_ref, bu_ref, o_ref, wgs_ref, wus_ref):
    i = pl.program_id(0)

    @pl.when(first_ref[i] == 1)
    def _():
        _cast_rows(wg_ref.at[0, 0], wgs_ref)
        _cast_rows(wu_ref.at[0, 0], wus_ref)

    @pl.when(valid_ref[i] == 1)
    def _():
        x = x_ref[...].astype(BF16)
        gate = jnp.dot(x, wgs_ref[...], preferred_element_type=F32) + bg_ref[0]
        up = jnp.dot(x, wus_ref[...], preferred_element_type=F32) + bu_ref[0]
        gate = jnp.minimum(gate, SWIGLU_LIMIT)
        up = jnp.clip(up, -SWIGLU_LIMIT, SWIGLU_LIMIT)
        act = (up + 1.0) * gate * jax.nn.sigmoid(SWIGLU_ALPHA * gate)
        o_ref[...] = act.astype(o_ref.dtype)

    @pl.when(valid_ref[i] == 0)
    def _():
        o_ref[...] = jnp.zeros(o_ref.shape, o_ref.dtype)


def _moe_up(x_sorted, w_gu, b_gu, layer, rt):
    n_slots, d = x_sorted.shape
    _, n_exp, _, two_f = w_gu.shape
    f = two_f // 2
    r, tn = MOE_ROWS, MOE_UP_TN
    nj = f // tn
    n_rb = rt['n_rb']
    n_items = n_rb * nj
    rb = jnp.arange(n_rb, dtype=I32)
    e_of = rt['rb_e']
    nvb = rt['n_valid_blocks'][0]
    nblk = rt['nblk'].at[n_exp - 1].add(n_rb - nvb)
    q = rb - rt['blk_start'][e_of]
    pos = (nj * rt['blk_start'][e_of][:, None]
           + jnp.arange(nj, dtype=I32)[None, :] * nblk[e_of][:, None] + q[:, None]).reshape(-1)
    rb2 = jnp.broadcast_to(rb[:, None], (n_rb, nj)).reshape(-1)
    j2 = jnp.broadcast_to(jnp.arange(nj, dtype=I32)[None, :], (n_rb, nj)).reshape(-1)
    it_rb = jnp.zeros((n_items,), I32).at[pos].set(rb2)
    it_j = jnp.zeros((n_items,), I32).at[pos].set(j2)
    it_valid = (it_rb < nvb).astype(I32)
    it_e = e_of[it_rb]
    prev_e = jnp.concatenate([jnp.full((1,), -1, I32), it_e[:-1]])
    prev_j = jnp.concatenate([jnp.full((1,), -1, I32), it_j[:-1]])
    it_first = ((it_e != prev_e) | (it_j != prev_j)).astype(I32)

    grid_spec = pltpu.PrefetchScalarGridSpec(
        num_scalar_prefetch=5,
        grid=(n_items,),
        in_specs=[
            pl.BlockSpec((r, d), lambda i, rbt, jt, et, ft, vt: (rbt[i], 0)),
            pl.BlockSpec((1, 1, d, tn), lambda i, rbt, jt, et, ft, vt: (layer, et[i], 0, jt[i])),
            pl.BlockSpec((1, 1, d, tn), lambda i, rbt, jt, et, ft, vt: (layer, et[i], 0, nj + jt[i])),
            pl.BlockSpec((1, 1, tn), lambda i, rbt, jt, et, ft, vt: (et[i], 0, jt[i])),
            pl.BlockSpec((1, 1, tn), lambda i, rbt, jt, et, ft, vt: (et[i], 0, nj + jt[i])),
        ],
        out_specs=pl.BlockSpec((r, tn), lambda i, rbt, jt, et, ft, vt: (rbt[i], jt[i])),
        scratch_shapes=[pltpu.VMEM((d, tn), BF16), pltpu.VMEM((d, tn), BF16)],
    )
    return pl.pallas_call(
        _moe_up_kernel,
        grid_spec=grid_spec,
        out_shape=jax.ShapeDtypeStruct((n_slots, f), BF16),
        compiler_params=_cparams(("arbitrary",)),
        name="moe_up",
    )(it_rb, it_j, it_e, it_first, it_valid, x_sorted, w_gu, w_gu,
      b_gu.reshape(n_exp, 1, two_f), b_gu.reshape(n_exp, 1, two_f))


def _moe_dn_kernel(e_ref, nvb_ref, a_ref, w_ref, b_ref, o_ref, ws_ref):
    i = pl.program_id(0)
    valid = i < nvb_ref[0]
    e = e_ref[i]
    prev_e = e_ref[jnp.maximum(i - 1, 0)]

    @pl.when(valid & ((i == 0) | (e != prev_e)))
    def _():
        _cast_rows(w_ref.at[0, 0], ws_ref)

    @pl.when(valid)
    def _():
        o_ref[...] = jnp.dot(a_ref[...], ws_ref[...], preferred_element_type=F32) + b_ref[0]

    @pl.when(jnp.logical_not(valid))
    def _():
        o_ref[...] = jnp.zeros(o_ref.shape, o_ref.dtype)


def _moe_dn(act, w_dn, b_dn, layer, rt):
    n_slots, f = act.shape
    _, n_exp, _, d = w_dn.shape
    r = MOE_ROWS
    grid_spec = pltpu.PrefetchScalarGridSpec(
        num_scalar_prefetch=2,
        grid=(rt['n_rb'],),
        in_specs=[
            pl.BlockSpec((r, f), lambda i, et, nvb: (jnp.minimum(i, nvb[0] - 1), 0)),
            pl.BlockSpec((1, 1, f, d), lambda i, et, nvb: (layer, et[i], 0, 0)),
            pl.BlockSpec((1, 1, d), lambda i, et, nvb: (et[i], 0, 0)),
        ],
        out_specs=pl.BlockSpec((r, d), lambda i, et, nvb: (i, 0)),
        scratch_shapes=[pltpu.VMEM((f, d), BF16)],
    )
    return pl.pallas_call(
        _moe_dn_kernel,
        grid_spec=grid_spec,
        out_shape=jax.ShapeDtypeStruct((n_slots, d), F32),
        compiler_params=_cparams(("arbitrary",)),
        name="moe_dn",
    )(rt['rb_e'], rt['n_valid_blocks'], act, w_dn, b_dn.reshape(n_exp, 1, d))


def _combine_kernel(dest_ref, x_ref, gate_ref, g_ref, fg_ref, y_hbm, o_ref, buf_ref, sem_ref, *, final):
    i = pl.program_id(0)
    n_steps = pl.num_programs(0)
    tc = COMBINE_TOKENS

    def issue(step, slot):
        def group(g, _):
            for s in range(DMA_UNROLL):
                t = g * DMA_UNROLL + s
                for k in range(TOP_K):
                    src = dest_ref[(step * tc + t) * TOP_K + k]
                    pltpu.make_async_copy(y_hbm.at[pl.ds(src, 1), :], buf_ref.at[slot, k, pl.ds(t, 1), :],
                                          sem_ref.at[slot]).start()
            return 0

        lax.fori_loop(0, tc // DMA_UNROLL, group, 0)

    @pl.when(i == 0)
    def _():
        issue(0, 0)

    @pl.when(i + 1 < n_steps)
    def _():
        issue(i + 1, (i + 1) % 2)

    slot = i % 2
    for k in range(TOP_K):
        pltpu.make_async_copy(y_hbm.at[pl.ds(0, tc), :], buf_ref.at[slot, k], sem_ref.at[slot]).wait()
    gates = gate_ref[...]
    y = gates[:, 0:1] * buf_ref[slot, 0]
    for k in range(1, TOP_K):
        y = y + gates[:, k:k + 1] * buf_ref[slot, k]
    x2 = x_ref[...] + g_ref[0] * y
    if final:
        x2 = x2 * lax.rsqrt(jnp.mean(x2 * x2, axis=-1, keepdims=True) + EPS) * fg_ref[...]
    o_ref[...] = x2


def _combine(x1, y_sorted, gates, rt, g_ffn, final_gain, seq, final):
    t, d = x1.shape
    tc = COMBINE_TOKENS
    per_seq = seq // tc
    grid_spec = pltpu.PrefetchScalarGridSpec(
        num_scalar_prefetch=1,
        grid=(t // tc,),
        in_specs=[pl.BlockSpec((tc, d), lambda i, dst: (i, 0)),
                  pl.BlockSpec((tc, ROUTER_PAD), lambda i, dst: (i, 0)),
                  pl.BlockSpec((1, 1, d), lambda i, dst: (i // per_seq, 0, 0)),
                  pl.BlockSpec((1, d), lambda i, dst: (0, 0)),
                  pl.BlockSpec(memory_space=pl.ANY)],
        out_specs=pl.BlockSpec((tc, d), lambda i, dst: (i, 0)),
        scratch_shapes=[pltpu.VMEM((2, TOP_K, tc, d), F32), pltpu.SemaphoreType.DMA((2,))],
    )
    return pl.pallas_call(
        functools.partial(_combine_kernel, final=final),
        grid_spec=grid_spec,
        out_shape=jax.ShapeDtypeStruct((t, d), F32),
        compiler_params=_cparams(("arbitrary",)),
        name="combine_final" if final else "combine",
    )(rt['dest'], x1, gates, g_ffn, final_gain.reshape(1, d), y_sorted)


def _moe(x1, h, idx, gates, counts, w_gu, b_gu, w_dn, b_dn, layer, g_ffn, final_gain, seq, final):
    n_tok = x1.shape[0]
    rt = _routing_tables(idx[:, :TOP_K], idx[:, TOP_K:2 * TOP_K], counts[0, :N_EXPERTS], n_tok)
    x_sorted = _moe_dispatch(h, rt)
    act = _moe_up(x_sorted, w_gu, b_gu, layer, rt)
    y_sorted = _moe_dn(act, w_dn, b_dn, layer, rt)
    return _combine(x1, y_sorted, gates, rt, g_ffn, final_gain, seq, final)


def kernel(x, c, ada_w, ada_b, norm_mix_gain, norm_ffn_gain, ab_w_in, ab_w_out, hg_lb_logits, hg_norm_gain, s5_lam_re, s5_lam_im, s5_log_dt, s5_b_re, s5_b_im, s5_c_re, s5_c_im, s5_d, s5_glu_w, s5_glu_b, cd_w_in, cd_w_out, ret_norm_gain, router_w, router_b, moe_w_gu, moe_b_gu, moe_w_dn, moe_b_dn, final_gain):
    bsz, seq, d = x.shape
    depth = ada_w.shape[0]
    n_tok = bsz * seq
    hg_width = hg_lb_logits.shape[1]
    s5_width = s5_glu_w.shape[1]
    ret_width = ret_norm_gain.shape[1]
    fnet_width = cd_w_out.shape[1] - ret_width

    lower_bounds = jnp.cumsum(jax.nn.softmax(hg_lb_logits.astype(F32), axis=0), axis=0)
    mod = _ada_mod(c, ada_w, ada_b)
    xr = x.reshape(n_tok, d)
    for layer in range(depth):
        sh_mix, sc_mix, g_mix, sh_ffn, sc_ffn, g_ffn = (
            mod[layer, :, k * d:(k + 1) * d].reshape(bsz, 1, d) for k in range(6))
        j = layer // 2
        if layer % 2 == 0:
            proj = _in_proj(xr, norm_mix_gain[layer], sh_mix, sc_mix, ab_w_in[j].astype(BF16), seq)
            proj3 = proj.reshape(bsz, seq, proj.shape[1])
            mix_a = _hgrn2(proj3, lower_bounds[j], hg_norm_gain[j], hg_width)
            tables = _s5c_tables(s5_lam_re[j], s5_lam_im[j], s5_log_dt[j], s5_b_re[j], s5_b_im[j],
                                 s5_c_re[j], s5_c_im[j], s5_d[j])
            mix_b = _s5c(proj3[:, :, 5 * hg_width:], tables)
            x1, h, idx, gate, counts = _out_proj(
                xr, mix_a.reshape(n_tok, hg_width), mix_b.reshape(n_tok, s5_width), ab_w_out[j],
                g_mix, norm_ffn_gain[layer], sh_ffn, sc_ffn, router_w[layer], router_b[layer], seq,
                glu_w=s5_glu_w[j], glu_b=s5_glu_b[j])
        else:
            proj = _in_proj(xr, norm_mix_gain[layer], sh_mix, sc_mix, cd_w_in[j].astype(BF16), seq)
            proj3 = proj.reshape(bsz, seq, proj.shape[1])
            mix_a = _retention(proj3, ret_norm_gain[j], ret_width)
            mix_b = _fnet(proj3, 4 * ret_width, fnet_width)
            x1, h, idx, gate, counts = _out_proj(
                xr, mix_a.reshape(n_tok, ret_width), mix_b.reshape(n_tok, fnet_width), cd_w_out[j],
                g_mix, norm_ffn_gain[layer], sh_ffn, sc_ffn, router_w[layer], router_b[layer], seq)
        xr = _moe(x1, h, idx, gate, counts, moe_w_gu, moe_b_gu[layer], moe_w_dn, moe_b_dn[layer],
                  layer, g_ffn, final_gain, seq, final=(layer == depth - 1))
    return xr.reshape(bsz, seq, d)
```

```python
import functools
import math

import jax
import jax.numpy as jnp
from jax import lax
from jax.experimental import pallas as pl
from jax.experimental.pallas import tpu as pltpu

F32 = jnp.float32
BF16 = jnp.bfloat16
I32 = jnp.int32

EPS = 1e-6
LANES = 128
SUBLANES = 8
VMEM_LIMIT = 56 * 1024 * 1024

HG_HEAD_DIM = 128
HG_CHUNK = 64
HG_GROUP = 256
HG_EXP_CLAMP = 80.0

S5_GROUP = 16
S5_STATE = 64
S5_CHUNK = 16
S5_UNROLL = 8
S5_TILE_GROUPS = 16
S5_TIME_BLOCK = 512
S5_SCAN_BLOCK = SUBLANES * SUBLANES

RET_HEAD_DIM = 256
RET_Q_TILE = 256
ROPE_BASE = 10000.0

FNET_GROUPS = 4
FNET_ROW_TILE = 512

N_EXPERTS = 32
TOP_K = 4
SWIGLU_LIMIT = 7.0
SWIGLU_ALPHA = 1.702
MOE_ROWS = 256
MOE_UP_TN = 1024
DISPATCH_TOKENS = 512
COMBINE_TOKENS = 128
DMA_UNROLL = 8
ROUTER_PAD = LANES
NEG_BIG = -1e30


def _cparams(semantics):
    return pltpu.CompilerParams(dimension_semantics=semantics, vmem_limit_bytes=VMEM_LIMIT)


def _ada_kernel(c_ref, w_ref, b_ref, o_ref):
    c = c_ref[...]
    cond = c * jax.nn.sigmoid(c)
    o_ref[0] = jnp.dot(cond.astype(BF16), w_ref[0].astype(BF16),
                       preferred_element_type=F32) + b_ref[0]


def _ada_mod(c, ada_w, ada_b):
    depth, d, n = ada_w.shape
    bsz = c.shape[0]
    tn = 1024
    return pl.pallas_call(
        _ada_kernel,
        grid=(depth, n // tn),
        in_specs=[
            pl.BlockSpec((bsz, d), lambda l, j: (0, 0)),
            pl.BlockSpec((1, d, tn), lambda l, j: (l, 0, j)),
            pl.BlockSpec((1, 1, tn), lambda l, j: (l, 0, j)),
        ],
        out_specs=pl.BlockSpec((1, bsz, tn), lambda l, j: (l, 0, j)),
        out_shape=jax.ShapeDtypeStruct((depth, bsz, n), F32),
        compiler_params=_cparams(("arbitrary", "arbitrary")),
        name="ada_mod",
    )(c, ada_w, ada_b.reshape(depth, 1, n))


def _norm_modulate(x, gain, shift, scale):
    ms = jnp.mean(x * x, axis=-1, keepdims=True)
    y = x * lax.rsqrt(ms + EPS) * gain
    return y * (1.0 + scale) + shift


def _inproj_kernel(x_ref, gain_ref, sh_ref, sc_ref, w_ref, o_ref, h_ref):
    @pl.when(pl.program_id(1) == 0)
    def _():
        h_ref[...] = _norm_modulate(x_ref[...], gain_ref[...], sh_ref[0], sc_ref[0]).astype(BF16)

    o_ref[...] = jnp.dot(h_ref[...], w_ref[...], preferred_element_type=F32)


def _in_proj(x2d, gain, shift, scale, w_bf16, seq):
    t, d = x2d.shape
    n = w_bf16.shape[1]
    tm, tn = 1024, 512
    per_seq = seq // tm
    return pl.pallas_call(
        _inproj_kernel,
        grid=(t // tm, n // tn),
        in_specs=[
            pl.BlockSpec((tm, d), lambda i, j: (i, 0)),
            pl.BlockSpec((1, d), lambda i, j: (0, 0)),
            pl.BlockSpec((1, 1, d), lambda i, j: (i // per_seq, 0, 0)),
            pl.BlockSpec((1, 1, d), lambda i, j: (i // per_seq, 0, 0)),
            pl.BlockSpec((d, tn), lambda i, j: (0, j)),
        ],
        out_specs=pl.BlockSpec((tm, tn), lambda i, j: (i, j)),
        out_shape=jax.ShapeDtypeStruct((t, n), F32),
        scratch_shapes=[pltpu.VMEM((tm, d), BF16)],
        compiler_params=_cparams(("arbitrary", "arbitrary")),
        name="in_proj",
    )(x2d, gain.reshape(1, d), shift, scale, w_bf16)


def _split3(a):
    hi = a.astype(BF16)
    r1 = a - hi.astype(F32)
    mid = r1.astype(BF16)
    lo = (r1 - mid.astype(F32)).astype(BF16)
    return hi, mid, lo


def _tri_sum(tri, a):
    hi, mid, lo = _split3(a)
    return (jnp.dot(tri, hi, preferred_element_type=F32)
            + jnp.dot(tri, mid, preferred_element_type=F32)
            + jnp.dot(tri, lo, preferred_element_type=F32))


def _dot_nt(a, b):
    return lax.dot_general(a, b, (((1,), (1,)), ((), ())), preferred_element_type=F32)


def _dot_tn(a, b):
    return lax.dot_general(a, b, (((0,), (0,)), ((), ())), preferred_element_type=F32)


def _hgrn2_kernel(q_ref, zf_ref, zb_ref, v_ref, g_ref, lb_ref, gain_ref, o_ref, acc_ref):
    seq = q_ref.shape[1]
    ln = HG_CHUNK
    gr = HG_GROUP
    n_groups = seq // gr
    per_group = gr // ln
    lb = lb_ref[...]
    gain = gain_ref[...]
    row = lax.broadcasted_iota(I32, (gr, gr), 0)
    col = lax.broadcasted_iota(I32, (gr, gr), 1)
    chunk_lo = (row // ln) * ln
    chunk_hi = chunk_lo + ln
    lower_incl = (col <= row) & (col >= chunk_lo)
    upper_strict = (col > row) & (col < chunk_hi)
    tri_prefix = lower_incl.astype(BF16)
    tri_suffix = ((col >= row) & (col < chunk_hi)).astype(BF16)
    mid = ln // 2

    def per_chunk_rows(a, offset):
        return jnp.concatenate(
            [jnp.broadcast_to(a[j * ln + offset:j * ln + offset + 1, :], (ln, a.shape[1]))
             for j in range(per_group)], axis=0)

    def group(gi, state_t, z_ref, forward):
        sl = pl.ds(pl.multiple_of(gi * gr, gr), gr)
        q = q_ref[0, sl, :]
        v = v_ref[0, sl, :].astype(BF16)
        f = lb + (1.0 - lb) * jax.nn.sigmoid(z_ref[0, sl, :])
        log_f = jnp.log(f)
        k = 1.0 - f
        if forward:
            cum = _tri_sum(tri_prefix, log_f)
            ref_rows = per_chunk_rows(cum, mid - 1)
            edge_off = ln - 1
            mask = lower_incl
        else:
            cum = _tri_sum(tri_suffix, log_f)
            ref_rows = per_chunk_rows(cum, mid)
            edge_off = 0
            mask = upper_strict
        kd = (k * jnp.exp(per_chunk_rows(cum, edge_off) - cum)).astype(BF16)
        rows = [slice(j * ln, (j + 1) * ln) for j in range(per_group)]
        local = [_dot_tn(v[rs], kd[rs]) for rs in rows]
        qe = q * jnp.exp(jnp.minimum(cum - ref_rows, HG_EXP_CLAMP))
        ke = k * jnp.exp(jnp.minimum(ref_rows - cum, HG_EXP_CLAMP))
        scores = jnp.where(mask, _dot_nt(qe.astype(BF16), ke.astype(BF16)), 0.0)
        qc = (q * jnp.exp(cum)).astype(BF16)
        entering = [None] * per_group
        order = range(per_group) if forward else range(per_group - 1, -1, -1)
        for j in order:
            entering[j] = state_t.astype(BF16)
            edge = cum[j * ln + edge_off:j * ln + edge_off + 1, :]
            state_t = state_t * jnp.exp(edge) + local[j]
        inter = [_dot_nt(qc[rs], entering[j]) for j, rs in enumerate(rows)]
        intra = jnp.dot(scores.astype(BF16), v, preferred_element_type=F32)
        return sl, intra + jnp.concatenate(inter, axis=0), state_t

    def fwd_body(gi, state_t):
        sl, out, new_state_t = group(gi, state_t, zf_ref, True)
        acc_ref[sl, :] = out
        return new_state_t

    def bwd_body(i, state_t):
        sl, out, new_state_t = group(n_groups - 1 - i, state_t, zb_ref, False)
        o = acc_ref[sl, :] + out
        y = o * lax.rsqrt(jnp.mean(o * o, axis=-1, keepdims=True) + EPS) * gain
        g = g_ref[0, sl, :]
        o_ref[0, sl, :] = (y * (g * jax.nn.sigmoid(g))).astype(o_ref.dtype)
        return new_state_t

    zero = jnp.zeros((HG_HEAD_DIM, HG_HEAD_DIM), F32)
    lax.fori_loop(0, n_groups, fwd_body, zero, unroll=2)
    lax.fori_loop(0, n_groups, bwd_body, zero, unroll=2)


def _hgrn2(proj3, lower_bound, hg_gain, width):
    bsz, seq, _ = proj3.shape
    heads = width // HG_HEAD_DIM
    dh = HG_HEAD_DIM

    def col(k):
        return pl.BlockSpec((1, seq, dh), lambda b, h: (b, 0, k * heads + h))

    return pl.pallas_call(
        _hgrn2_kernel,
        grid=(bsz, heads),
        in_specs=[col(0), col(1), col(2), col(3), col(4),
                  pl.BlockSpec((1, dh), lambda b, h: (0, h)),
                  pl.BlockSpec((1, dh), lambda b, h: (0, 0))],
        out_specs=pl.BlockSpec((1, seq, dh), lambda b, h: (b, 0, h)),
        out_shape=jax.ShapeDtypeStruct((bsz, seq, width), BF16),
        scratch_shapes=[pltpu.VMEM((seq, dh), F32)],
        compiler_params=_cparams(("arbitrary", "arbitrary")),
        name="hgrn2",
    )(proj3, proj3, proj3, proj3, proj3, lower_bound.reshape(1, width), hg_gain.reshape(1, dh))


S5_TBL_STEP = 0
S5_TBL_X16 = 8
S5_TBL_X32 = 9
S5_TBL_CARRY = 10
S5_N_TBL = 11


def _cmul(ar, ai, br, bi):
    return ar * br - ai * bi, ar * bi + ai * br


def _s5_scan_block(bu_ref, carry_ref, pw_ref, direction, base, colblk, width):
    forward = direction == 0
    cre = pl.ds(colblk * LANES, LANES)
    cim = pl.ds(width + colblk * LANES, LANES)
    bre = colblk
    bim = width // LANES + colblk

    def tbl(idx):
        return pw_ref[direction, 0, 0, idx, :, cre], pw_ref[direction, 0, 1, idx, :, cre]

    def rows(k):
        return pl.ds(base + k, SUBLANES, stride=SUBLANES)

    a_r, a_i = tbl(S5_TBL_STEP)
    order = list(range(SUBLANES)) if forward else list(range(SUBLANES - 1, -1, -1))
    xr = [None] * SUBLANES
    xi = [None] * SUBLANES
    pr = pi = None
    for k in order:
        br = bu_ref.at[bre][rows(k), :]
        bi = bu_ref.at[bim][rows(k), :]
        if pr is not None:
            mr, mi = _cmul(a_r, a_i, pr, pi)
            br = br + mr
            bi = bi + mi
        xr[k], xi[k] = br, bi
        pr, pi = br, bi

    er, ei = pr, pi
    sub = lax.broadcasted_iota(I32, (SUBLANES, LANES), 0)
    for step, idx in ((1, S5_TBL_STEP + 7), (2, S5_TBL_X16), (4, S5_TBL_X32)):
        m_r, m_i = tbl(idx)
        if forward:
            shift, keep = step, sub >= step
        else:
            shift, keep = SUBLANES - step, sub < SUBLANES - step
        sr = jnp.where(keep, pltpu.roll(er, shift, 0), 0.0)
        si = jnp.where(keep, pltpu.roll(ei, shift, 0), 0.0)
        dr, di = _cmul(m_r, m_i, sr, si)
        er, ei = er + dr, ei + di
    c_r = carry_ref[:, cre]
    c_i = carry_ref[:, cim]
    t_r, t_i = tbl(S5_TBL_CARRY)
    dr, di = _cmul(t_r, t_i, c_r, c_i)
    er, ei = er + dr, ei + di

    if forward:
        nr = jnp.where(sub >= 1, pltpu.roll(er, 1, 0), c_r)
        ni = jnp.where(sub >= 1, pltpu.roll(ei, 1, 0), c_i)
        last = SUBLANES - 1
    else:
        nr = jnp.where(sub < SUBLANES - 1, pltpu.roll(er, SUBLANES - 1, 0), c_r)
        ni = jnp.where(sub < SUBLANES - 1, pltpu.roll(ei, SUBLANES - 1, 0), c_i)
        last = 0
    carry_ref[:, cre] = jnp.broadcast_to(er[last:last + 1, :], (SUBLANES, LANES))
    carry_ref[:, cim] = jnp.broadcast_to(ei[last:last + 1, :], (SUBLANES, LANES))

    for k in range(SUBLANES):
        f_r, f_i = tbl(S5_TBL_STEP + (k if forward else SUBLANES - 1 - k))
        dr, di = _cmul(f_r, f_i, nr, ni)
        bu_ref.at[bre][rows(k), :] = xr[k] + dr
        bu_ref.at[bim][rows(k), :] = xi[k] + di


def _s5_kernel(u_ref, bm_ref, cm_ref, pw_ref, d_ref, z_ref, bu_ref, y_ref, carry_ref):
    seq = u_ref.shape[1]
    n_cb = int(bu_ref.shape[0])
    width = n_cb * LANES // 2
    tb = S5_TIME_BLOCK
    n_tb = seq // tb
    n_sb = tb // S5_SCAN_BLOCK
    d_skip = d_ref[...]

    for direction in (0, 1):
        carry_ref[...] = jnp.zeros(carry_ref.shape, F32)

        def time_block(it, _, direction=direction):
            blk = it if direction == 0 else n_tb - 1 - it
            sl = pl.ds(pl.multiple_of(blk * tb, tb), tb)
            u = u_ref[0, sl, :]
            bu = jnp.dot(u.astype(BF16), bm_ref[direction, 0], preferred_element_type=F32)
            for cb in range(n_cb):
                bu_ref[cb] = bu[:, cb * LANES:(cb + 1) * LANES]

            def scan_block(js, _):
                sb = js if direction == 0 else n_sb - 1 - js
                base = pl.multiple_of(sb * S5_SCAN_BLOCK, S5_SCAN_BLOCK)
                for colblk in range(width // LANES):
                    _s5_scan_block(bu_ref, carry_ref, pw_ref, direction, base, colblk, width)
                return 0

            lax.fori_loop(0, n_sb, scan_block, 0)
            xs = jnp.concatenate([bu_ref[cb].astype(BF16) for cb in range(n_cb)], axis=-1)
            y = jnp.dot(xs, cm_ref[direction, 0], preferred_element_type=F32)
            if direction == 0:
                y_ref[sl, :] = y
            else:
                z_ref[0, sl, :] = jax.nn.gelu(y_ref[sl, :] + y + d_skip * u).astype(z_ref.dtype)
            return 0

        lax.fori_loop(0, n_tb, time_block, 0)


def _s5_tables(lam_re, lam_im, log_dt, b_re, b_im, c_re, c_im):
    groups, state = lam_re.shape[1], lam_re.shape[2]
    chans = b_re.shape[2]
    tg = S5_TILE_GROUPS
    tiles = groups // tg
    eye = jnp.eye(tg, dtype=F32)
    bms, cms, pws = [], [], []
    sub = jnp.arange(SUBLANES, dtype=F32)
    for direction in (0, 1):
        lr, li = lam_re[direction].astype(F32), lam_im[direction].astype(F32)
        dt = jnp.exp(log_dt[direction].astype(F32))[:, None]
        mag = jnp.exp(lr * dt)
        abar_re = mag * jnp.cos(li * dt)
        abar_im = mag * jnp.sin(li * dt)
        den = lr * lr + li * li
        num_re = abar_re - 1.0
        coef_re = (num_re * lr + abar_im * li) / den
        coef_im = (abar_im * lr - num_re * li) / den
        bbar_re = coef_re[..., None] * b_re - coef_im[..., None] * b_im
        bbar_im = coef_re[..., None] * b_im + coef_im[..., None] * b_re

        def blockdiag_in(bb):
            bb = bb.reshape(tiles, tg, state, chans)
            m = jnp.einsum('gh,tgpc->tgchp', eye, bb)
            return m.reshape(tiles, tg * chans, tg * state)

        def blockdiag_out(cc):
            cc = cc.reshape(tiles, tg, chans, state)
            m = jnp.einsum('gh,tgcp->tgphc', eye, cc)
            return m.reshape(tiles, tg * state, tg * chans)

        bms.append(jnp.concatenate([blockdiag_in(bbar_re), blockdiag_in(bbar_im)], axis=-1))
        cms.append(jnp.concatenate([blockdiag_out(c_re[direction].astype(F32)),
                                    -blockdiag_out(c_im[direction].astype(F32))], axis=1))

        def power(n):
            return jnp.exp(n * lr * dt) * jnp.cos(n * li * dt), jnp.exp(n * lr * dt) * jnp.sin(n * li * dt)

        rows_re, rows_im = [], []
        for n in list(range(1, SUBLANES + 1)) + [2 * SUBLANES, 4 * SUBLANES]:
            p_re, p_im = power(float(n))
            rows_re.append(jnp.broadcast_to(p_re.reshape(tiles, 1, tg * state), (tiles, SUBLANES, tg * state)))
            rows_im.append(jnp.broadcast_to(p_im.reshape(tiles, 1, tg * state), (tiles, SUBLANES, tg * state)))
        carry_n = SUBLANES * (sub + 1.0) if direction == 0 else SUBLANES * (SUBLANES - sub)
        p_re, p_im = power(carry_n[:, None, None])
        rows_re.append(p_re.reshape(SUBLANES, tiles, tg * state).transpose(1, 0, 2))
        rows_im.append(p_im.reshape(SUBLANES, tiles, tg * state).transpose(1, 0, 2))
        pws.append(jnp.stack([jnp.stack(rows_re, axis=1), jnp.stack(rows_im, axis=1)], axis=1))
    return (jnp.stack(bms).astype(BF16), jnp.stack(cms).astype(BF16), jnp.stack(pws))


def _s5(proj3, col0, tables, d_skip):
    bsz, seq, _ = proj3.shape
    bm, cm, pw = tables
    tiles = bm.shape[1]
    tc = bm.shape[2]
    sw = bm.shape[3]
    cb0 = col0 // tc
    return pl.pallas_call(
        _s5_kernel,
        grid=(bsz, tiles),
        in_specs=[
            pl.BlockSpec((1, seq, tc), lambda b, t: (b, 0, cb0 + t)),
            pl.BlockSpec((2, 1, tc, sw), lambda b, t: (0, t, 0, 0)),
            pl.BlockSpec((2, 1, sw, tc), lambda b, t: (0, t, 0, 0)),
            pl.BlockSpec((2, 1, 2, S5_N_TBL, SUBLANES, sw // 2), lambda b, t: (0, t, 0, 0, 0, 0)),
            pl.BlockSpec((1, tc), lambda b, t: (0, t)),
        ],
        out_specs=pl.BlockSpec((1, seq, tc), lambda b, t: (b, 0, t)),
        out_shape=jax.ShapeDtypeStruct((bsz, seq, tiles * tc), BF16),
        scratch_shapes=[pltpu.VMEM((sw // LANES, S5_TIME_BLOCK, LANES), F32),
                        pltpu.VMEM((seq, tc), F32),
                        pltpu.VMEM((SUBLANES, sw), F32)],
        compiler_params=_cparams(("arbitrary", "arbitrary")),
        name="s5",
    )(proj3, bm, cm, pw, d_skip.reshape(1, tiles * tc))


S5_TILE_PAIRS = 4
S5_ROW_SPLIT = 4


def _s5c_kernel(u_ref, t_ref, w_ref, v_ref, a_ref, z_ref, uc_ref, e_ref, p_ref):
    ck = S5_CHUNK
    n_chunks = u_ref.shape[1] // ck
    slab = n_chunks // S5_ROW_SPLIT
    gpt = 2 * S5_TILE_PAIRS
    half_tok = ck // 2
    lane_grp = lax.broadcasted_iota(I32, (slab, LANES), 1) // S5_GROUP

    def token_rows(rq, tok):
        return pl.ds(rq * slab * ck + tok, slab, stride=ck)

    def roll_lanes(a, groups):
        shift = (groups * S5_GROUP) % LANES
        return pltpu.roll(a, shift, 1) if shift else a

    for half in range(2):
        for rq in range(S5_ROW_SPLIT):
            toks = [u_ref.at[0][token_rows(rq, half * half_tok + sl), :] for sl in range(half_tok)]
            for gl in range(gpt):
                acc = jnp.zeros((slab, LANES), F32)
                for sl in range(half_tok):
                    acc = jnp.where(lane_grp == sl, roll_lanes(toks[sl], sl - gl), acc)
                uc_ref[gl * 2 + half, rq * slab:(rq + 1) * slab, :] = acc

    seg = [pl.ds(k * LANES, LANES) for k in range(4)]

    def pair_inputs(pp):
        return jnp.concatenate([uc_ref[pp * 4 + k] for k in range(4)], axis=-1).astype(BF16)

    for pp in range(S5_TILE_PAIRS):
        e_ref[:, pp, :] = jnp.dot(pair_inputs(pp), w_ref[pp], preferred_element_type=F32)

    af_r, af_i, ab_r, ab_i = a_ref[0, 0], a_ref[0, 1], a_ref[0, 2], a_ref[0, 3]

    def step(m, carry):
        xr, xi, yr, yi = carry
        mb = n_chunks - 1 - m
        p_ref[m, :, seg[0]] = xr
        p_ref[m, :, seg[1]] = xi
        p_ref[mb, :, seg[2]] = yr
        p_ref[mb, :, seg[3]] = yi
        dr, di = _cmul(af_r, af_i, xr, xi)
        gr, gi = _cmul(ab_r, ab_i, yr, yi)
        return (dr + e_ref[m, :, seg[0]], di + e_ref[m, :, seg[1]],
                gr + e_ref[mb, :, seg[2]], gi + e_ref[mb, :, seg[3]])

    zero = jnp.zeros((S5_TILE_PAIRS, LANES), F32)
    lax.fori_loop(0, n_chunks, step, (zero, zero, zero, zero), unroll=S5_UNROLL)

    for pp in range(S5_TILE_PAIRS):
        y = (jnp.dot(pair_inputs(pp), t_ref[pp], preferred_element_type=F32)
             + jnp.dot(p_ref[:, pp, :].astype(BF16), v_ref[pp], preferred_element_type=F32))
        zt = jax.nn.gelu(y)
        for k in range(4):
            uc_ref[pp * 4 + k] = zt[:, k * LANES:(k + 1) * LANES]

    for half in range(2):
        for rq in range(S5_ROW_SPLIT):
            cols = [uc_ref[gl * 2 + half, rq * slab:(rq + 1) * slab, :] for gl in range(gpt)]
            for sl in range(half_tok):
                acc = jnp.zeros((slab, LANES), F32)
                for gl in range(gpt):
                    acc = jnp.where(lane_grp == gl, roll_lanes(cols[gl], gl - sl), acc)
                z_ref.at[0][token_rows(rq, half * half_tok + sl), :] = acc


_einsum_f32 = functools.partial(jnp.einsum, precision=lax.Precision.HIGHEST)


def _s5c_tables(lam_re, lam_im, log_dt, b_re, b_im, c_re, c_im, d_skip):
    groups, state = lam_re.shape[1], lam_re.shape[2]
    chans = b_re.shape[2]
    ck = S5_CHUNK
    pairs = groups // 2
    lag = jnp.arange(ck + 1, dtype=F32)[:, None, None]
    kern, w_parts, v_parts, a_parts = [], [], [], []
    for direction in (0, 1):
        lr, li = lam_re[direction].astype(F32), lam_im[direction].astype(F32)
        dt = jnp.exp(log_dt[direction].astype(F32))[:, None]
        mag = jnp.exp(lr * dt)
        abar_re = mag * jnp.cos(li * dt)
        abar_im = mag * jnp.sin(li * dt)
        den = lr * lr + li * li
        num_re = abar_re - 1.0
        coef_re = (num_re * lr + abar_im * li) / den
        coef_im = (abar_im * lr - num_re * li) / den
        bbar_re = coef_re[..., None] * b_re - coef_im[..., None] * b_im
        bbar_im = coef_re[..., None] * b_im + coef_im[..., None] * b_re
        pw_re = jnp.exp(lag * lr * dt) * jnp.cos(lag * li * dt)
        pw_im = jnp.exp(lag * lr * dt) * jnp.sin(lag * li * dt)
        ab_re = pw_re[..., None] * bbar_re - pw_im[..., None] * bbar_im
        ab_im = pw_re[..., None] * bbar_im + pw_im[..., None] * bbar_re
        cr, ci = c_re[direction].astype(F32), c_im[direction].astype(F32)
        kern.append(_einsum_f32('gcp,ngpd->ngcd', cr, ab_re[:ck]) - _einsum_f32('gcp,ngpd->ngcd', ci, ab_im[:ck]))
        order = jnp.arange(ck - 1, -1, -1) if direction == 0 else jnp.arange(ck)
        w_parts.append((ab_re[order], ab_im[order]))
        order = jnp.arange(1, ck + 1) if direction == 0 else jnp.arange(ck, 0, -1)
        a_r, a_i = pw_re[order], pw_im[order]
        v_from_re = jnp.einsum('gcp,jgp->gpjc', cr, a_r) - jnp.einsum('gcp,jgp->gpjc', ci, a_i)
        v_from_im = -(jnp.einsum('gcp,jgp->gpjc', cr, a_i) + jnp.einsum('gcp,jgp->gpjc', ci, a_r))
        v_parts.append((v_from_re, v_from_im))
        a_parts.append((pw_re[ck], pw_im[ck]))

    s_idx = jnp.arange(ck)[None, :, None]
    t_idx = jnp.arange(ck)[None, None, :]
    n_idx = jnp.arange(ck)[:, None, None]
    sel_f = (t_idx - s_idx == n_idx).astype(F32)
    sel_b = (s_idx - t_idx == n_idx).astype(F32)
    skip = jnp.eye(chans, dtype=F32)[None] * d_skip.reshape(groups, chans)[:, :, None]
    toep = (_einsum_f32('nst,ngcd->gsdtc', sel_f, kern[0]) + _einsum_f32('nst,ngcd->gsdtc', sel_b, kern[1])
            + jnp.einsum('st,gcd->gsdtc', jnp.eye(ck, dtype=F32), skip))
    toep = toep.reshape(pairs, 2, ck * chans, ck * chans)
    eye2 = jnp.eye(2, dtype=F32)
    t_mat = jnp.einsum('ab,paxy->paxby', eye2, toep).reshape(pairs, 2 * ck * chans, 2 * ck * chans)

    w_seg = [w_parts[0][0], w_parts[0][1], w_parts[1][0], w_parts[1][1]]
    w_stack = jnp.stack(w_seg, axis=0).transpose(2, 1, 4, 0, 3)
    w_stack = w_stack.reshape(pairs, 2, ck * chans, 4, state)
    w_mat = jnp.einsum('ab,gaxkq->gaxkbq', eye2, w_stack).reshape(pairs, 2 * ck * chans, 4 * 2 * state)

    v_seg = [v_parts[0][0], v_parts[0][1], v_parts[1][0], v_parts[1][1]]
    v_stack = jnp.stack(v_seg, axis=0).reshape(4, pairs, 2, state, ck * chans)
    v_mat = jnp.einsum('ab,kgaqy->gkaqby', eye2, v_stack).reshape(pairs, 4 * 2 * state, 2 * ck * chans)

    a_seg = jnp.stack([a_parts[0][0], a_parts[0][1], a_parts[1][0], a_parts[1][1]], axis=0)
    tp = S5_TILE_PAIRS
    a_tbl = a_seg.reshape(4, pairs // tp, tp, 2 * state).transpose(1, 0, 2, 3)
    return t_mat.astype(BF16), w_mat.astype(BF16), v_mat.astype(BF16), a_tbl


def _s5c(proj3, col0, width, tables):
    bsz, seq, _ = proj3.shape
    t_mat, w_mat, v_mat, a_tbl = tables
    tp = S5_TILE_PAIRS
    tiles = t_mat.shape[0] // tp
    cols = t_mat.shape[1]
    tc = width // tiles
    assert tc == LANES and cols == 4 * LANES
    n_chunks = seq // S5_CHUNK
    return pl.pallas_call(
        _s5c_kernel,
        grid=(tiles, bsz),
        in_specs=[
            pl.BlockSpec((1, seq, tc), lambda t, b: (b, 0, col0 // tc + t)),
            pl.BlockSpec((tp, cols, cols), lambda t, b: (t, 0, 0)),
            pl.BlockSpec((tp, cols, cols), lambda t, b: (t, 0, 0)),
            pl.BlockSpec((tp, cols, cols), lambda t, b: (t, 0, 0)),
            pl.BlockSpec((1, 4, tp, LANES), lambda t, b: (t, 0, 0, 0)),
        ],
        out_specs=pl.BlockSpec((1, seq, tc), lambda t, b: (b, 0, t)),
        out_shape=jax.ShapeDtypeStruct((bsz, seq, width), F32),
        scratch_shapes=[pltpu.VMEM((4 * tp, n_chunks, LANES), F32),
                        pltpu.VMEM((n_chunks, tp, cols), F32),
                        pltpu.VMEM((n_chunks, tp, cols), F32)],
        compiler_params=_cparams(("arbitrary", "arbitrary")),
        name="s5",
    )(proj3, t_mat, w_mat, v_mat, a_tbl)


def _retention_kernel(q_ref, k_ref, v_ref, g_ref, cos_ref, sin_ref, lg_ref, gain_ref,
                      o_ref, qs_ref, ks_ref, vs_ref):
    seq = q_ref.shape[1]
    half = RET_HEAD_DIM // 2
    tq = RET_Q_TILE
    cos = cos_ref[...]
    sin = sin_ref[...]

    def rot(t_ref, scale):
        t1 = t_ref[0, :, :half]
        t2 = t_ref[0, :, half:]
        return jnp.concatenate([(t1 * cos - t2 * sin) * scale, (t1 * sin + t2 * cos) * scale], axis=-1)

    qs_ref[...] = rot(q_ref, 1.0).astype(BF16)
    ks_ref[...] = rot(k_ref, RET_HEAD_DIM ** -0.5).astype(BF16)
    vs_ref[...] = v_ref[0].astype(BF16)
    lg_fwd = lg_ref[0, 0:1, :]
    lg_bwd = lg_ref[0, 1:2, :]
    gain = gain_ref[...]

    def q_tile(i, _):
        sl = pl.ds(pl.multiple_of(i * tq, tq), tq)
        scores = _dot_nt(qs_ref[sl, :], ks_ref[...])
        t_idx = lax.broadcasted_iota(I32, (tq, seq), 0) + i * tq
        s_idx = lax.broadcasted_iota(I32, (tq, seq), 1)
        rel = (t_idx - s_idx).astype(F32)
        decay = jnp.exp(jnp.where(rel >= 0.0, lg_fwd * rel, -lg_bwd * rel))
        p = (scores * decay).astype(BF16)
        o = jnp.dot(p, vs_ref[...], preferred_element_type=F32)
        y = o * lax.rsqrt(jnp.mean(o * o, axis=-1, keepdims=True) + EPS) * gain
        g = g_ref[0, sl, :]
        o_ref[0, sl, :] = (y * (g * jax.nn.sigmoid(g))).astype(o_ref.dtype)
        return 0

    lax.fori_loop(0, seq // tq, q_tile, 0)


def _retention(proj3, ret_gain, width):
    bsz, seq, _ = proj3.shape
    dh = RET_HEAD_DIM
    heads = width // dh
    inv_freq = ROPE_BASE ** (-jnp.arange(0, dh, 2, dtype=F32) / dh)
    ang = jnp.arange(seq, dtype=F32)[:, None] * inv_freq[None, :]
    cos, sin = jnp.cos(ang), jnp.sin(ang)
    log_gamma = jnp.log1p(-jnp.exp2(-5.0 - jnp.arange(heads, dtype=F32)))
    lg = jnp.stack([log_gamma, log_gamma[::-1]], axis=1)
    lg = jnp.broadcast_to(lg[:, :, None], (heads, 2, seq))

    def col(k):
        return pl.BlockSpec((1, seq, dh), lambda b, h: (b, 0, k * heads + h))

    return pl.pallas_call(
        _retention_kernel,
        grid=(bsz, heads),
        in_specs=[col(0), col(1), col(2), col(3),
                  pl.BlockSpec((seq, dh // 2), lambda b, h: (0, 0)),
                  pl.BlockSpec((seq, dh // 2), lambda b, h: (0, 0)),
                  pl.BlockSpec((1, 2, seq), lambda b, h: (h, 0, 0)),
                  pl.BlockSpec((1, dh), lambda b, h: (0, h))],
        out_specs=pl.BlockSpec((1, seq, dh), lambda b, h: (b, 0, h)),
        out_shape=jax.ShapeDtypeStruct((bsz, seq, width), BF16),
        scratch_shapes=[pltpu.VMEM((seq, dh), BF16)] * 3,
        compiler_params=_cparams(("arbitrary", "arbitrary")),
        name="retention",
    )(proj3, proj3, proj3, proj3, cos, sin, lg, ret_gain.reshape(1, width))


def _fnet_kernel(x_ref, cc_ref, sc_ref, cs_ref, ss_ref, o_ref, a1_ref, a2_ref):
    @pl.when(pl.program_id(1) == 0)
    def _():
        x = x_ref[0].astype(BF16)
        a1_ref[...] = jnp.dot(x, cc_ref[...], preferred_element_type=F32).astype(BF16)
        a2_ref[...] = jnp.dot(x, sc_ref[...], preferred_element_type=F32).astype(BF16)

    y = (jnp.dot(cs_ref[...], a1_ref[...], preferred_element_type=F32)
         - jnp.dot(ss_ref[...], a2_ref[...], preferred_element_type=F32))
    o_ref[0] = y.astype(o_ref.dtype)


def _dft_mats(n):
    idx = jnp.arange(n, dtype=I32)
    ang = (2.0 * math.pi / n) * ((idx[:, None] * idx[None, :]) % n).astype(F32)
    scale = n ** -0.5
    return jnp.cos(ang) * scale, jnp.sin(ang) * scale


def _fnet(proj3, col0, width):
    bsz, seq, _ = proj3.shape
    gw = width // FNET_GROUPS
    cs, ss = _dft_mats(seq)
    cg, sg = _dft_mats(gw)
    eye = jnp.eye(FNET_GROUPS, dtype=F32)
    cc = jnp.kron(eye, cg)
    sc = jnp.kron(eye, sg)
    tr = FNET_ROW_TILE
    return pl.pallas_call(
        _fnet_kernel,
        grid=(bsz, seq // tr),
        in_specs=[
            pl.BlockSpec((1, seq, width), lambda b, i: (b, 0, col0 // width)),
            pl.BlockSpec((width, width), lambda b, i: (0, 0)),
            pl.BlockSpec((width, width), lambda b, i: (0, 0)),
            pl.BlockSpec((tr, seq), lambda b, i: (i, 0)),
            pl.BlockSpec((tr, seq), lambda b, i: (i, 0)),
        ],
        out_specs=pl.BlockSpec((1, tr, width), lambda b, i: (b, i, 0)),
        out_shape=jax.ShapeDtypeStruct((bsz, seq, width), BF16),
        scratch_shapes=[pltpu.VMEM((seq, width), BF16)] * 2,
        compiler_params=_cparams(("arbitrary", "arbitrary")),
        name="fnet",
    )(proj3, cc.astype(BF16), sc.astype(BF16), cs.astype(BF16), ss.astype(BF16))


def _outproj_kernel(*refs, glu):
    if glu:
        (x_ref, a_ref, b_ref, gw_ref, gb_ref, wa_ref, wb_ref, gm_ref, gain_ref, sh_ref, sc_ref,
         rwh_ref, rwl_ref, rb_ref, x1_ref, h_ref, idx_ref, gate_ref, cnt_ref, base_ref) = refs
    else:
        (x_ref, a_ref, b_ref, wa_ref, wb_ref, gm_ref, gain_ref, sh_ref, sc_ref,
         rwh_ref, rwl_ref, rb_ref, x1_ref, h_ref, idx_ref, gate_ref, cnt_ref, base_ref) = refs

    @pl.when(pl.program_id(0) == 0)
    def _():
        base_ref[...] = jnp.zeros(base_ref.shape, F32)

    bm = b_ref[...]
    if glu:
        gl = jnp.dot(bm.astype(BF16), gw_ref[...], preferred_element_type=F32) + gb_ref[...]
        bm = (bm.astype(F32) * jax.nn.sigmoid(gl)).astype(BF16)
    y = (jnp.dot(a_ref[...], wa_ref[...], preferred_element_type=F32)
         + jnp.dot(bm, wb_ref[...], preferred_element_type=F32))
    x1 = x_ref[...] + gm_ref[0] * y
    x1_ref[...] = x1
    h = _norm_modulate(x1, gain_ref[...], sh_ref[0], sc_ref[0])
    h_ref[...] = h

    h_hi = h.astype(BF16)
    h_lo = (h - h_hi.astype(F32)).astype(BF16)
    logits = (jnp.dot(h_hi, rwh_ref[...], preferred_element_type=F32)
              + jnp.dot(h_lo, rwh_ref[...], preferred_element_type=F32)
              + jnp.dot(h_hi, rwl_ref[...], preferred_element_type=F32)
              + rb_ref[...])
    lane = lax.broadcasted_iota(I32, logits.shape, 1)
    vals, idxs = [], []
    for _ in range(TOP_K):
        m = jnp.max(logits, axis=-1, keepdims=True)
        ik = jnp.min(jnp.where(logits == m, lane, ROUTER_PAD), axis=-1, keepdims=True)
        vals.append(m)
        idxs.append(ik)
        logits = jnp.where(lane == ik, -jnp.inf, logits)
    exps = [jnp.exp(v - vals[0]) for v in vals]
    denom = exps[0] + exps[1] + exps[2] + exps[3]

    tm = lane.shape[0]
    onehot = jnp.zeros(lane.shape, F32)
    for k in range(TOP_K):
        onehot = onehot + (lane == idxs[k]).astype(F32)
    before = (lax.broadcasted_iota(I32, (tm, tm), 1) < lax.broadcasted_iota(I32, (tm, tm), 0)).astype(BF16)
    count = jnp.dot(before, onehot.astype(BF16), preferred_element_type=F32) + base_ref[...]
    new_base = base_ref[...] + jnp.sum(onehot, axis=0, keepdims=True)
    base_ref[...] = new_base
    cnt_ref[...] = new_base.astype(I32)

    idx_out = jnp.zeros(lane.shape, I32)
    gate_out = jnp.zeros(lane.shape, F32)
    for k in range(TOP_K):
        rank = jnp.sum(jnp.where(lane == idxs[k], count, 0.0), axis=-1, keepdims=True).astype(I32)
        idx_out = jnp.where(lane == k, idxs[k], idx_out)
        idx_out = jnp.where(lane == TOP_K + k, rank, idx_out)
        gate_out = jnp.where(lane == k, exps[k] / denom, gate_out)
    idx_ref[...] = idx_out
    gate_ref[...] = gate_out


def _out_proj(x2d, mix_a, mix_b, w_out, g_mix, gain, shift, scale, router_w, router_b, seq,
              glu_w=None, glu_b=None):
    t, d = x2d.shape
    wa_rows = mix_a.shape[1]
    wb_rows = mix_b.shape[1]
    tm = 256
    per_seq = seq // tm
    n_exp = router_w.shape[1]
    rw = jnp.zeros((d, ROUTER_PAD), F32).at[:, :n_exp].set(router_w)
    rw_hi = rw.astype(BF16)
    rw_lo = (rw - rw_hi.astype(F32)).astype(BF16)
    rb = jnp.full((1, ROUTER_PAD), NEG_BIG, F32).at[0, :n_exp].set(router_b)
    w_bf = w_out.astype(BF16)
    glu = glu_w is not None

    def rows(width):
        return pl.BlockSpec((tm, width), lambda i: (i, 0))

    def full(r, c):
        return pl.BlockSpec((r, c), lambda i: (0, 0))

    def per_batch():
        return pl.BlockSpec((1, 1, d), lambda i: (i // per_seq, 0, 0))

    in_specs = [rows(d), rows(wa_rows), rows(wb_rows)]
    args = [x2d, mix_a, mix_b]
    if glu:
        in_specs += [full(wb_rows, wb_rows), full(1, wb_rows)]
        args += [glu_w.astype(BF16), glu_b.reshape(1, wb_rows)]
    in_specs += [pl.BlockSpec((wa_rows, d), lambda i: (0, 0)),
                 pl.BlockSpec((wb_rows, d), lambda i: (wa_rows // wb_rows, 0)),
                 per_batch(), full(1, d), per_batch(), per_batch(),
                 full(d, ROUTER_PAD), full(d, ROUTER_PAD), full(1, ROUTER_PAD)]
    args += [w_bf, w_bf, g_mix, gain.reshape(1, d), shift, scale, rw_hi, rw_lo, rb]
    return pl.pallas_call(
        functools.partial(_outproj_kernel, glu=glu),
        grid=(t // tm,),
        in_specs=in_specs,
        out_specs=[rows(d), rows(d), rows(ROUTER_PAD), rows(ROUTER_PAD), full(1, ROUTER_PAD)],
        out_shape=[jax.ShapeDtypeStruct((t, d), F32), jax.ShapeDtypeStruct((t, d), F32),
                   jax.ShapeDtypeStruct((t, ROUTER_PAD), I32), jax.ShapeDtypeStruct((t, ROUTER_PAD), F32),
                   jax.ShapeDtypeStruct((1, ROUTER_PAD), I32)],
        scratch_shapes=[pltpu.VMEM((1, ROUTER_PAD), F32)],
        compiler_params=_cparams(("arbitrary",)),
        name="out_proj_glu" if glu else "out_proj",
    )(*args)


def _routing_tables(top_idx, rank, counts, n_tok):
    n_assign = n_tok * TOP_K
    r = MOE_ROWS
    n_rb = n_assign // r + N_EXPERTS
    experts = jnp.arange(N_EXPERTS, dtype=I32)
    nblk = (counts + r - 1) // r
    blk_end = jnp.cumsum(nblk)
    blk_start = blk_end - nblk
    row_start = jnp.sum(jnp.where(top_idx[:, :, None] == experts[None, None, :],
                                  (blk_start * r)[None, None, :], 0), axis=-1)
    dest = (row_start + rank).reshape(-1).astype(I32)
    n_valid_blocks = blk_end[-1]
    rb = jnp.arange(n_rb, dtype=I32)
    rb_e = jnp.sum((blk_end[None, :] <= rb[:, None]).astype(I32), axis=1)
    rb_e = jnp.minimum(rb_e, N_EXPERTS - 1).astype(I32)
    tail = jnp.where(nblk > 0, blk_end - 1, -1)
    spare = n_valid_blocks + experts
    spare = jnp.where(spare < n_rb, spare, -1)
    zero_blocks = jnp.concatenate([tail, spare]).astype(I32)
    return dict(dest=dest, rb_e=rb_e, zero_blocks=zero_blocks,
                n_valid_blocks=n_valid_blocks.reshape(1).astype(I32),
                nblk=nblk.astype(I32), blk_start=blk_start.astype(I32), n_rb=n_rb, n_slots=n_rb * r)


def _dispatch_kernel(dest_ref, zb_ref, h_ref, xs_hbm, zero_ref, zsem_ref, sem_ref):
    i = pl.program_id(0)
    r = MOE_ROWS
    td = DISPATCH_TOKENS

    @pl.when(i == 0)
    def _():
        zero_ref[...] = jnp.zeros(zero_ref.shape, zero_ref.dtype)

        def zero_copy(z):
            row0 = pl.multiple_of(zb_ref[z] * r, r)
            return pltpu.make_async_copy(zero_ref, xs_hbm.at[pl.ds(row0, r), :], zsem_ref.at[0])

        def start(z, _):
            @pl.when(zb_ref[z] >= 0)
            def _():
                zero_copy(z).start()
            return 0

        def wait(z, _):
            @pl.when(zb_ref[z] >= 0)
            def _():
                zero_copy(z).wait()
            return 0

        lax.fori_loop(0, zb_ref.shape[0], start, 0)
        lax.fori_loop(0, zb_ref.shape[0], wait, 0)

    def group(g, _):
        for s in range(DMA_UNROLL):
            t = g * DMA_UNROLL + s
            for k in range(TOP_K):
                dst = dest_ref[(i * td + t) * TOP_K + k]
                pltpu.make_async_copy(h_ref.at[pl.ds(t, 1), :], xs_hbm.at[pl.ds(dst, 1), :],
                                      sem_ref.at[0]).start()
        return 0

    lax.fori_loop(0, td // DMA_UNROLL, group, 0)
    for _ in range(TOP_K):
        pltpu.make_async_copy(h_ref, xs_hbm.at[pl.ds(0, td), :], sem_ref.at[0]).wait()


def _moe_dispatch(h2d, rt):
    t, d = h2d.shape
    td = DISPATCH_TOKENS
    grid_spec = pltpu.PrefetchScalarGridSpec(
        num_scalar_prefetch=2,
        grid=(t // td,),
        in_specs=[pl.BlockSpec((td, d), lambda i, dst, zb: (i, 0))],
        out_specs=pl.BlockSpec(memory_space=pl.ANY),
        scratch_shapes=[pltpu.VMEM((MOE_ROWS, d), F32), pltpu.SemaphoreType.DMA((1,)),
                        pltpu.SemaphoreType.DMA((1,))],
    )
    return pl.pallas_call(
        _dispatch_kernel,
        grid_spec=grid_spec,
        out_shape=jax.ShapeDtypeStruct((rt['n_slots'], d), F32),
        compiler_params=_cparams(("arbitrary",)),
        name="moe_dispatch",
    )(rt['dest'], rt['zero_blocks'], h2d)


def _cast_rows(src_ref, dst_ref, chunk=256):
    n = dst_ref.shape[0] // chunk

    def body(c, _):
        sl = pl.ds(pl.multiple_of(c * chunk, chunk), chunk)
        dst_ref[sl, :] = src_ref[sl, :].astype(dst_ref.dtype)
        return 0

    lax.fori_loop(0, n, body, 0)


def _moe_up_kernel(rb_ref, j_ref, e_ref, first_ref, valid_ref,
                   x_ref, wg_ref, wu_ref, bg_ref, bu_ref, o_ref, wgs_ref, wus_ref):
    i = pl.program_id(0)

    @pl.when(first_ref[i] == 1)
    def _():
        _cast_rows(wg_ref.at[0, 0], wgs_ref)
        _cast_rows(wu_ref.at[0, 0], wus_ref)

    @pl.when(valid_ref[i] == 1)
    def _():
        x = x_ref[...].astype(BF16)
        gate = jnp.dot(x, wgs_ref[...], preferred_element_type=F32) + bg_ref[0]
        up = jnp.dot(x, wus_ref[...], preferred_element_type=F32) + bu_ref[0]
        gate = jnp.minimum(gate, SWIGLU_LIMIT)
        up = jnp.clip(up, -SWIGLU_LIMIT, SWIGLU_LIMIT)
        act = (up + 1.0) * gate * jax.nn.sigmoid(SWIGLU_ALPHA * gate)
        o_ref[...] = act.astype(o_ref.dtype)

    @pl.when(valid_ref[i] == 0)
    def _():
        o_ref[...] = jnp.zeros(o_ref.shape, o_ref.dtype)


def _moe_up(x_sorted, w_gu, b_gu, layer, rt):
    n_slots, d = x_sorted.shape
    _, n_exp, _, two_f = w_gu.shape
    f = two_f // 2
    r, tn = MOE_ROWS, MOE_UP_TN
    nj = f // tn
    n_rb = rt['n_rb']
    n_items = n_rb * nj
    rb = jnp.arange(n_rb, dtype=I32)
    e_of = rt['rb_e']
    nvb = rt['n_valid_blocks'][0]
    nblk = rt['nblk'].at[n_exp - 1].add(n_rb - nvb)
    q = rb - rt['blk_start'][e_of]
    pos = (nj * rt['blk_start'][e_of][:, None]
           + jnp.arange(nj, dtype=I32)[None, :] * nblk[e_of][:, None] + q[:, None]).reshape(-1)
    rb2 = jnp.broadcast_to(rb[:, None], (n_rb, nj)).reshape(-1)
    j2 = jnp.broadcast_to(jnp.arange(nj, dtype=I32)[None, :], (n_rb, nj)).reshape(-1)
    it_rb = jnp.zeros((n_items,), I32).at[pos].set(rb2)
    it_j = jnp.zeros((n_items,), I32).at[pos].set(j2)
    it_valid = (it_rb < nvb).astype(I32)
    it_e = e_of[it_rb]
    prev_e = jnp.concatenate([jnp.full((1,), -1, I32), it_e[:-1]])
    prev_j = jnp.concatenate([jnp.full((1,), -1, I32), it_j[:-1]])
    it_first = ((it_e != prev_e) | (it_j != prev_j)).astype(I32)

    grid_spec = pltpu.PrefetchScalarGridSpec(
        num_scalar_prefetch=5,
        grid=(n_items,),
        in_specs=[
            pl.BlockSpec((r, d), lambda i, rbt, jt, et, ft, vt: (rbt[i], 0)),
            pl.BlockSpec((1, 1, d, tn), lambda i, rbt, jt, et, ft, vt: (layer, et[i], 0, jt[i])),
            pl.BlockSpec((1, 1, d, tn), lambda i, rbt, jt, et, ft, vt: (layer, et[i], 0, nj + jt[i])),
            pl.BlockSpec((1, 1, tn), lambda i, rbt, jt, et, ft, vt: (et[i], 0, jt[i])),
            pl.BlockSpec((1, 1, tn), lambda i, rbt, jt, et, ft, vt: (et[i], 0, nj + jt[i])),
        ],
        out_specs=pl.BlockSpec((r, tn), lambda i, rbt, jt, et, ft, vt: (rbt[i], jt[i])),
        scratch_shapes=[pltpu.VMEM((d, tn), BF16), pltpu.VMEM((d, tn), BF16)],
    )
    return pl.pallas_call(
        _moe_up_kernel,
        grid_spec=grid_spec,
        out_shape=jax.ShapeDtypeStruct((n_slots, f), BF16),
        compiler_params=_cparams(("arbitrary",)),
        name="moe_up",
    )(it_rb, it_j, it_e, it_first, it_valid, x_sorted, w_gu, w_gu,
      b_gu.reshape(n_exp, 1, two_f), b_gu.reshape(n_exp, 1, two_f))


def _moe_dn_kernel(e_ref, nvb_ref, a_ref, w_ref, b_ref, o_ref, ws_ref):
    i = pl.program_id(0)
    valid = i < nvb_ref[0]
    e = e_ref[i]
    prev_e = e_ref[jnp.maximum(i - 1, 0)]

    @pl.when(valid & ((i == 0) | (e != prev_e)))
    def _():
        _cast_rows(w_ref.at[0, 0], ws_ref)

    @pl.when(valid)
    def _():
        o_ref[...] = jnp.dot(a_ref[...], ws_ref[...], preferred_element_type=F32) + b_ref[0]

    @pl.when(jnp.logical_not(valid))
    def _():
        o_ref[...] = jnp.zeros(o_ref.shape, o_ref.dtype)


def _moe_dn(act, w_dn, b_dn, layer, rt):
    n_slots, f = act.shape
    _, n_exp, _, d = w_dn.shape
    r = MOE_ROWS
    grid_spec = pltpu.PrefetchScalarGridSpec(
        num_scalar_prefetch=2,
        grid=(rt['n_rb'],),
        in_specs=[
            pl.BlockSpec((r, f), lambda i, et, nvb: (jnp.minimum(i, nvb[0] - 1), 0)),
            pl.BlockSpec((1, 1, f, d), lambda i, et, nvb: (layer, et[i], 0, 0)),
            pl.BlockSpec((1, 1, d), lambda i, et, nvb: (et[i], 0, 0)),
        ],
        out_specs=pl.BlockSpec((r, d), lambda i, et, nvb: (i, 0)),
        scratch_shapes=[pltpu.VMEM((f, d), BF16)],
    )
    return pl.pallas_call(
        _moe_dn_kernel,
        grid_spec=grid_spec,
        out_shape=jax.ShapeDtypeStruct((n_slots, d), F32),
        compiler_params=_cparams(("arbitrary",)),
        name="moe_dn",
    )(rt['rb_e'], rt['n_valid_blocks'], act, w_dn, b_dn.reshape(n_exp, 1, d))


def _combine_kernel(dest_ref, x_ref, gate_ref, g_ref, fg_ref, y_hbm, o_ref, buf_ref, sem_ref, *, final):
    i = pl.program_id(0)
    n_steps = pl.num_programs(0)
    tc = COMBINE_TOKENS

    def issue(step, slot):
        def group(g, _):
            for s in range(DMA_UNROLL):
                t = g * DMA_UNROLL + s
                for k in range(TOP_K):
                    src = dest_ref[(step * tc + t) * TOP_K + k]
                    pltpu.make_async_copy(y_hbm.at[pl.ds(src, 1), :], buf_ref.at[slot, k, pl.ds(t, 1), :],
                                          sem_ref.at[slot]).start()
            return 0

        lax.fori_loop(0, tc // DMA_UNROLL, group, 0)

    @pl.when(i == 0)
    def _():
        issue(0, 0)

    @pl.when(i + 1 < n_steps)
    def _():
        issue(i + 1, (i + 1) % 2)

    slot = i % 2
    for k in range(TOP_K):
        pltpu.make_async_copy(y_hbm.at[pl.ds(0, tc), :], buf_ref.at[slot, k], sem_ref.at[slot]).wait()
    gates = gate_ref[...]
    y = gates[:, 0:1] * buf_ref[slot, 0]
    for k in range(1, TOP_K):
        y = y + gates[:, k:k + 1] * buf_ref[slot, k]
    x2 = x_ref[...] + g_ref[0] * y
    if final:
        x2 = x2 * lax.rsqrt(jnp.mean(x2 * x2, axis=-1, keepdims=True) + EPS) * fg_ref[...]
    o_ref[...] = x2


def _combine(x1, y_sorted, gates, rt, g_ffn, final_gain, seq, final):
    t, d = x1.shape
    tc = COMBINE_TOKENS
    per_seq = seq // tc
    grid_spec = pltpu.PrefetchScalarGridSpec(
        num_scalar_prefetch=1,
        grid=(t // tc,),
        in_specs=[pl.BlockSpec((tc, d), lambda i, dst: (i, 0)),
                  pl.BlockSpec((tc, ROUTER_PAD), lambda i, dst: (i, 0)),
                  pl.BlockSpec((1, 1, d), lambda i, dst: (i // per_seq, 0, 0)),
                  pl.BlockSpec((1, d), lambda i, dst: (0, 0)),
                  pl.BlockSpec(memory_space=pl.ANY)],
        out_specs=pl.BlockSpec((tc, d), lambda i, dst: (i, 0)),
        scratch_shapes=[pltpu.VMEM((2, TOP_K, tc, d), F32), pltpu.SemaphoreType.DMA((2,))],
    )
    return pl.pallas_call(
        functools.partial(_combine_kernel, final=final),
        grid_spec=grid_spec,
        out_shape=jax.ShapeDtypeStruct((t, d), F32),
        compiler_params=_cparams(("arbitrary",)),
        name="combine_final" if final else "combine",
    )(rt['dest'], x1, gates, g_ffn, final_gain.reshape(1, d), y_sorted)


def _moe(x1, h, idx, gates, counts, w_gu, b_gu, w_dn, b_dn, layer, g_ffn, final_gain, seq, final):
    n_tok = x1.shape[0]
    rt = _routing_tables(idx[:, :TOP_K], idx[:, TOP_K:2 * TOP_K], counts[0, :N_EXPERTS], n_tok)
    x_sorted = _moe_dispatch(h, rt)
    act = _moe_up(x_sorted, w_gu, b_gu, layer, rt)
    y_sorted = _moe_dn(act, w_dn, b_dn, layer, rt)
    return _combine(x1, y_sorted, gates, rt, g_ffn, final_gain, seq, final)


def kernel(x, c, ada_w, ada_b, norm_mix_gain, norm_ffn_gain, ab_w_in, ab_w_out, hg_lb_logits, hg_norm_gain, s5_lam_re, s5_lam_im, s5_log_dt, s5_b_re, s5_b_im, s5_c_re, s5_c_im, s5_d, s5_glu_w, s5_glu_b, cd_w_in, cd_w_out, ret_norm_gain, router_w, router_b, moe_w_gu, moe_b_gu, moe_w_dn, moe_b_dn, final_gain):
    bsz, seq, d = x.shape
    depth = ada_w.shape[0]
    n_tok = bsz * seq
    hg_width = hg_lb_logits.shape[1]
    s5_width = s5_glu_w.shape[1]
    ret_width = ret_norm_gain.shape[1]
    fnet_width = cd_w_out.shape[1] - ret_width

    lower_bounds = jnp.cumsum(jax.nn.softmax(hg_lb_logits.astype(F32), axis=0), axis=0)
    mod = _ada_mod(c, ada_w, ada_b)
    xr = x.reshape(n_tok, d)
    for layer in range(depth):
        sh_mix, sc_mix, g_mix, sh_ffn, sc_ffn, g_ffn = (
            mod[layer, :, k * d:(k + 1) * d].reshape(bsz, 1, d) for k in range(6))
        j = layer // 2
        if layer % 2 == 0:
            proj = _in_proj(xr, norm_mix_gain[layer], sh_mix, sc_mix, ab_w_in[j].astype(BF16), seq)
            proj3 = proj.reshape(bsz, seq, proj.shape[1])
            mix_a = _hgrn2(proj3, lower_bounds[j], hg_norm_gain[j], hg_width)
            tables = _s5c_tables(s5_lam_re[j], s5_lam_im[j], s5_log_dt[j], s5_b_re[j], s5_b_im[j],
                                 s5_c_re[j], s5_c_im[j], s5_d[j])
            mix_b = _s5c(proj3, 5 * hg_width, s5_width, tables)
            x1, h, idx, gate, counts = _out_proj(
                xr, mix_a.reshape(n_tok, hg_width), mix_b.reshape(n_tok, s5_width), ab_w_out[j],
                g_mix, norm_ffn_gain[layer], sh_ffn, sc_ffn, router_w[layer], router_b[layer], seq,
                glu_w=s5_glu_w[j], glu_b=s5_glu_b[j])
        else:
            proj = _in_proj(xr, norm_mix_gain[layer], sh_mix, sc_mix, cd_w_in[j].astype(BF16), seq)
            proj3 = proj.reshape(bsz, seq, proj.shape[1])
            mix_a = _retention(proj3, ret_norm_gain[j], ret_width)
            mix_b = _fnet(proj3, 4 * ret_width, fnet_width)
            x1, h, idx, gate, counts = _out_proj(
                xr, mix_a.reshape(n_tok, ret_width), mix_b.reshape(n_tok, fnet_width), cd_w_out[j],
                g_mix, norm_ffn_gain[layer], sh_ffn, sc_ffn, router_w[layer], router_b[layer], seq)
        xr = _moe(x1, h, idx, gate, counts, moe_w_gu, moe_b_gu[layer], moe_w_dn, moe_b_dn[layer],
                  layer, g_ffn, final_gain, seq, final=(layer == depth - 1))
    return xr.reshape(bsz, seq, d)
```

```python
import functools
import math

import jax
import jax.numpy as jnp
from jax import lax
from jax.experimental import pallas as pl
from jax.experimental.pallas import tpu as pltpu

F32 = jnp.float32
BF16 = jnp.bfloat16
I32 = jnp.int32

EPS = 1e-6
LANES = 128
SUBLANES = 8
VMEM_LIMIT = 56 * 1024 * 1024

HG_HEAD_DIM = 128
HG_CHUNK = 64
HG_GROUP = 256
HG_EXP_CLAMP = 80.0

S5_GROUP = 16
S5_STATE = 64
S5_CHUNK = 16
S5_UNROLL = 8
S5_TILE_PAIRS = 4
S5_ROW_SPLIT = 4

RET_HEAD_DIM = 256
RET_Q_TILE = 256
ROPE_BASE = 10000.0

FNET_GROUPS = 4
FNET_ROW_TILE = 512

N_EXPERTS = 32
TOP_K = 4
SWIGLU_LIMIT = 7.0
SWIGLU_ALPHA = 1.702
MOE_ROWS = 256
MOE_UP_TN = 1024
MOE_W_PARTS = 4
DISPATCH_TOKENS = 512
COMBINE_TOKENS = 128
DMA_UNROLL = 8
ROUTER_PAD = LANES
NEG_BIG = -1e30


def _cparams(semantics):
    return pltpu.CompilerParams(dimension_semantics=semantics, vmem_limit_bytes=VMEM_LIMIT)


def _ada_kernel(c_ref, w_ref, b_ref, o_ref):
    c = c_ref[...]
    cond = c * jax.nn.sigmoid(c)
    o_ref[0] = jnp.dot(cond.astype(BF16), w_ref[0].astype(BF16),
                       preferred_element_type=F32) + b_ref[0]


def _ada_mod(c, ada_w, ada_b):
    depth, d, n = ada_w.shape
    bsz = c.shape[0]
    tn = 1024
    return pl.pallas_call(
        _ada_kernel,
        grid=(depth, n // tn),
        in_specs=[
            pl.BlockSpec((bsz, d), lambda l, j: (0, 0)),
            pl.BlockSpec((1, d, tn), lambda l, j: (l, 0, j)),
            pl.BlockSpec((1, 1, tn), lambda l, j: (l, 0, j)),
        ],
        out_specs=pl.BlockSpec((1, bsz, tn), lambda l, j: (l, 0, j)),
        out_shape=jax.ShapeDtypeStruct((depth, bsz, n), F32),
        compiler_params=_cparams(("arbitrary", "arbitrary")),
        name="ada_mod",
    )(c, ada_w, ada_b.reshape(depth, 1, n))


def _norm_modulate(x, gain, shift, scale):
    ms = jnp.mean(x * x, axis=-1, keepdims=True)
    y = x * lax.rsqrt(ms + EPS) * gain
    return y * (1.0 + scale) + shift


def _inproj_kernel(x_ref, gain_ref, sh_ref, sc_ref, w_ref, o_ref, h_ref):
    @pl.when(pl.program_id(1) == 0)
    def _():
        h_ref[...] = _norm_modulate(x_ref[...], gain_ref[...], sh_ref[0], sc_ref[0]).astype(BF16)

    o_ref[...] = jnp.dot(h_ref[...], w_ref[...], preferred_element_type=F32)


def _in_proj(x2d, gain, shift, scale, w_bf16, seq):
    t, d = x2d.shape
    n = w_bf16.shape[1]
    tm, tn = 1024, 512
    per_seq = seq // tm
    return pl.pallas_call(
        _inproj_kernel,
        grid=(t // tm, n // tn),
        in_specs=[
            pl.BlockSpec((tm, d), lambda i, j: (i, 0)),
            pl.BlockSpec((1, d), lambda i, j: (0, 0)),
            pl.BlockSpec((1, 1, d), lambda i, j: (i // per_seq, 0, 0)),
            pl.BlockSpec((1, 1, d), lambda i, j: (i // per_seq, 0, 0)),
            pl.BlockSpec((d, tn), lambda i, j: (0, j)),
        ],
        out_specs=pl.BlockSpec((tm, tn), lambda i, j: (i, j)),
        out_shape=jax.ShapeDtypeStruct((t, n), F32),
        scratch_shapes=[pltpu.VMEM((tm, d), BF16)],
        compiler_params=_cparams(("arbitrary", "arbitrary")),
        name="in_proj",
    )(x2d, gain.reshape(1, d), shift, scale, w_bf16)


def _split3(a):
    hi = a.astype(BF16)
    r1 = a - hi.astype(F32)
    mid = r1.astype(BF16)
    lo = (r1 - mid.astype(F32)).astype(BF16)
    return hi, mid, lo


def _tri_sum(tri, a):
    hi, mid, lo = _split3(a)
    return (jnp.dot(tri, hi, preferred_element_type=F32)
            + jnp.dot(tri, mid, preferred_element_type=F32)
            + jnp.dot(tri, lo, preferred_element_type=F32))


def _dot_nt(a, b):
    return lax.dot_general(a, b, (((1,), (1,)), ((), ())), preferred_element_type=F32)


def _dot_tn(a, b):
    return lax.dot_general(a, b, (((0,), (0,)), ((), ())), preferred_element_type=F32)


def _hgrn2_kernel(q_ref, zf_ref, zb_ref, v_ref, g_ref, lb_ref, gain_ref, o_ref, acc_ref):
    seq = q_ref.shape[1]
    ln = HG_CHUNK
    gr = HG_GROUP
    n_groups = seq // gr
    per_group = gr // ln
    lb = lb_ref[...]
    gain = gain_ref[...]
    row = lax.broadcasted_iota(I32, (gr, gr), 0)
    col = lax.broadcasted_iota(I32, (gr, gr), 1)
    chunk_lo = (row // ln) * ln
    chunk_hi = chunk_lo + ln
    lower_incl = (col <= row) & (col >= chunk_lo)
    upper_strict = (col > row) & (col < chunk_hi)
    tri_prefix = lower_incl.astype(BF16)
    tri_suffix = ((col >= row) & (col < chunk_hi)).astype(BF16)
    mid = ln // 2

    def per_chunk_rows(a, offset):
        return jnp.concatenate(
            [jnp.broadcast_to(a[j * ln + offset:j * ln + offset + 1, :], (ln, a.shape[1]))
             for j in range(per_group)], axis=0)

    def group(gi, state_t, z_ref, forward):
        sl = pl.ds(pl.multiple_of(gi * gr, gr), gr)
        q = q_ref[0, sl, :]
        v = v_ref[0, sl, :].astype(BF16)
        f = lb + (1.0 - lb) * jax.nn.sigmoid(z_ref[0, sl, :])
        log_f = jnp.log(f)
        k = 1.0 - f
        if forward:
            cum = _tri_sum(tri_prefix, log_f)
            ref_rows = per_chunk_rows(cum, mid - 1)
            edge_off = ln - 1
            mask = lower_incl
        else:
            cum = _tri_sum(tri_suffix, log_f)
            ref_rows = per_chunk_rows(cum, mid)
            edge_off = 0
            mask = upper_strict
        kd = (k * jnp.exp(per_chunk_rows(cum, edge_off) - cum)).astype(BF16)
        rows = [slice(j * ln, (j + 1) * ln) for j in range(per_group)]
        local = [_dot_tn(v[rs], kd[rs]) for rs in rows]
        qe = q * jnp.exp(jnp.minimum(cum - ref_rows, HG_EXP_CLAMP))
        ke = k * jnp.exp(jnp.minimum(ref_rows - cum, HG_EXP_CLAMP))
        scores = jnp.where(mask, _dot_nt(qe.astype(BF16), ke.astype(BF16)), 0.0)
        qc = (q * jnp.exp(cum)).astype(BF16)
        entering = [None] * per_group
        order = range(per_group) if forward else range(per_group - 1, -1, -1)
        for j in order:
            entering[j] = state_t.astype(BF16)
            edge = cum[j * ln + edge_off:j * ln + edge_off + 1, :]
            state_t = state_t * jnp.exp(edge) + local[j]
        inter = [_dot_nt(qc[rs], entering[j]) for j, rs in enumerate(rows)]
        intra = jnp.dot(scores.astype(BF16), v, preferred_element_type=F32)
        return sl, intra + jnp.concatenate(inter, axis=0), state_t

    def fwd_body(gi, state_t):
        sl, out, new_state_t = group(gi, state_t, zf_ref, True)
        acc_ref[sl, :] = out
        return new_state_t

    def bwd_body(i, state_t):
        sl, out, new_state_t = group(n_groups - 1 - i, state_t, zb_ref, False)
        o = acc_ref[sl, :] + out
        y = o * lax.rsqrt(jnp.mean(o * o, axis=-1, keepdims=True) + EPS) * gain
        g = g_ref[0, sl, :]
        o_ref[0, sl, :] = (y * (g * jax.nn.sigmoid(g))).astype(o_ref.dtype)
        return new_state_t

    zero = jnp.zeros((HG_HEAD_DIM, HG_HEAD_DIM), F32)
    lax.fori_loop(0, n_groups, fwd_body, zero, unroll=2)
    lax.fori_loop(0, n_groups, bwd_body, zero, unroll=2)


def _hgrn2(proj3, lower_bound, hg_gain, width):
    bsz, seq, _ = proj3.shape
    heads = width // HG_HEAD_DIM
    dh = HG_HEAD_DIM

    def col(k):
        return pl.BlockSpec((1, seq, dh), lambda b, h: (b, 0, k * heads + h))

    return pl.pallas_call(
        _hgrn2_kernel,
        grid=(bsz, heads),
        in_specs=[col(0), col(1), col(2), col(3), col(4),
                  pl.BlockSpec((1, dh), lambda b, h: (0, h)),
                  pl.BlockSpec((1, dh), lambda b, h: (0, 0))],
        out_specs=pl.BlockSpec((1, seq, dh), lambda b, h: (b, 0, h)),
        out_shape=jax.ShapeDtypeStruct((bsz, seq, width), BF16),
        scratch_shapes=[pltpu.VMEM((seq, dh), F32)],
        compiler_params=_cparams(("arbitrary", "arbitrary")),
        name="hgrn2",
    )(proj3, proj3, proj3, proj3, proj3, lower_bound.reshape(1, width), hg_gain.reshape(1, dh))


def _cmul(ar, ai, br, bi):
    return ar * br - ai * bi, ar * bi + ai * br


def _s5c_kernel(u_ref, t_ref, w_ref, v_ref, a_ref, z_ref, uc_ref, e_ref, p_ref):
    ck = S5_CHUNK
    n_chunks = u_ref.shape[1] // ck
    slab = n_chunks // S5_ROW_SPLIT
    gpt = 2 * S5_TILE_PAIRS
    half_tok = ck // 2
    lane_grp = lax.broadcasted_iota(I32, (slab, LANES), 1) // S5_GROUP

    def token_rows(rq, tok):
        return pl.ds(rq * slab * ck + tok, slab, stride=ck)

    def roll_lanes(a, groups):
        shift = (groups * S5_GROUP) % LANES
        return pltpu.roll(a, shift, 1) if shift else a

    for half in range(2):
        for rq in range(S5_ROW_SPLIT):
            toks = [u_ref.at[0][token_rows(rq, half * half_tok + sl), :] for sl in range(half_tok)]
            for gl in range(gpt):
                acc = jnp.zeros((slab, LANES), F32)
                for sl in range(half_tok):
                    acc = jnp.where(lane_grp == sl, roll_lanes(toks[sl], sl - gl), acc)
                uc_ref[gl * 2 + half, rq * slab:(rq + 1) * slab, :] = acc

    seg = [pl.ds(k * LANES, LANES) for k in range(4)]

    def pair_inputs(pp):
        return jnp.concatenate([uc_ref[pp * 4 + k] for k in range(4)], axis=-1).astype(BF16)

    for pp in range(S5_TILE_PAIRS):
        e_ref[:, pp, :] = jnp.dot(pair_inputs(pp), w_ref[pp], preferred_element_type=F32)

    af_r, af_i, ab_r, ab_i = a_ref[0, 0], a_ref[0, 1], a_ref[0, 2], a_ref[0, 3]

    def step(m, carry):
        xr, xi, yr, yi = carry
        mb = n_chunks - 1 - m
        p_ref[m, :, seg[0]] = xr
        p_ref[m, :, seg[1]] = xi
        p_ref[mb, :, seg[2]] = yr
        p_ref[mb, :, seg[3]] = yi
        dr, di = _cmul(af_r, af_i, xr, xi)
        gr, gi = _cmul(ab_r, ab_i, yr, yi)
        return (dr + e_ref[m, :, seg[0]], di + e_ref[m, :, seg[1]],
                gr + e_ref[mb, :, seg[2]], gi + e_ref[mb, :, seg[3]])

    zero = jnp.zeros((S5_TILE_PAIRS, LANES), F32)
    lax.fori_loop(0, n_chunks, step, (zero, zero, zero, zero), unroll=S5_UNROLL)

    for pp in range(S5_TILE_PAIRS):
        y = (jnp.dot(pair_inputs(pp), t_ref[pp], preferred_element_type=F32)
             + jnp.dot(p_ref[:, pp, :].astype(BF16), v_ref[pp], preferred_element_type=F32))
        zt = jax.nn.gelu(y)
        for k in range(4):
            uc_ref[pp * 4 + k] = zt[:, k * LANES:(k + 1) * LANES]

    for half in range(2):
        for rq in range(S5_ROW_SPLIT):
            cols = [uc_ref[gl * 2 + half, rq * slab:(rq + 1) * slab, :] for gl in range(gpt)]
            for sl in range(half_tok):
                acc = jnp.zeros((slab, LANES), F32)
                for gl in range(gpt):
                    acc = jnp.where(lane_grp == gl, roll_lanes(cols[gl], gl - sl), acc)
                z_ref.at[0][token_rows(rq, half * half_tok + sl), :] = acc


_einsum_f32 = functools.partial(jnp.einsum, precision=lax.Precision.HIGHEST)


def _s5c_tables(lam_re, lam_im, log_dt, b_re, b_im, c_re, c_im, d_skip):
    groups, state = lam_re.shape[1], lam_re.shape[2]
    chans = b_re.shape[2]
    ck = S5_CHUNK
    pairs = groups // 2
    lag = jnp.arange(ck + 1, dtype=F32)[:, None, None]
    kern, w_parts, v_parts, a_parts = [], [], [], []
    for direction in (0, 1):
        lr, li = lam_re[direction].astype(F32), lam_im[direction].astype(F32)
        dt = jnp.exp(log_dt[direction].astype(F32))[:, None]
        mag = jnp.exp(lr * dt)
        abar_re = mag * jnp.cos(li * dt)
        abar_im = mag * jnp.sin(li * dt)
        den = lr * lr + li * li
        num_re = abar_re - 1.0
        coef_re = (num_re * lr + abar_im * li) / den
        coef_im = (abar_im * lr - num_re * li) / den
        bbar_re = coef_re[..., None] * b_re - coef_im[..., None] * b_im
        bbar_im = coef_re[..., None] * b_im + coef_im[..., None] * b_re
        pw_re = jnp.exp(lag * lr * dt) * jnp.cos(lag * li * dt)
        pw_im = jnp.exp(lag * lr * dt) * jnp.sin(lag * li * dt)
        ab_re = pw_re[..., None] * bbar_re - pw_im[..., None] * bbar_im
        ab_im = pw_re[..., None] * bbar_im + pw_im[..., None] * bbar_re
        cr, ci = c_re[direction].astype(F32), c_im[direction].astype(F32)
        kern.append(_einsum_f32('gcp,ngpd->ngcd', cr, ab_re[:ck]) - _einsum_f32('gcp,ngpd->ngcd', ci, ab_im[:ck]))
        order = jnp.arange(ck - 1, -1, -1) if direction == 0 else jnp.arange(ck)
        w_parts.append((ab_re[order], ab_im[order]))
        order = jnp.arange(1, ck + 1) if direction == 0 else jnp.arange(ck, 0, -1)
        a_r, a_i = pw_re[order], pw_im[order]
        v_from_re = jnp.einsum('gcp,jgp->gpjc', cr, a_r) - jnp.einsum('gcp,jgp->gpjc', ci, a_i)
        v_from_im = -(jnp.einsum('gcp,jgp->gpjc', cr, a_i) + jnp.einsum('gcp,jgp->gpjc', ci, a_r))
        v_parts.append((v_from_re, v_from_im))
        a_parts.append((pw_re[ck], pw_im[ck]))

    s_idx = jnp.arange(ck)[None, :, None]
    t_idx = jnp.arange(ck)[None, None, :]
    n_idx = jnp.arange(ck)[:, None, None]
    sel_f = (t_idx - s_idx == n_idx).astype(F32)
    sel_b = (s_idx - t_idx == n_idx).astype(F32)
    skip = jnp.eye(chans, dtype=F32)[None] * d_skip.reshape(groups, chans)[:, :, None]
    toep = (_einsum_f32('nst,ngcd->gsdtc', sel_f, kern[0]) + _einsum_f32('nst,ngcd->gsdtc', sel_b, kern[1])
            + jnp.einsum('st,gcd->gsdtc', jnp.eye(ck, dtype=F32), skip))
    toep = toep.reshape(pairs, 2, ck * chans, ck * chans)
    eye2 = jnp.eye(2, dtype=F32)
    t_mat = jnp.einsum('ab,paxy->paxby', eye2, toep).reshape(pairs, 2 * ck * chans, 2 * ck * chans)

    w_seg = [w_parts[0][0], w_parts[0][1], w_parts[1][0], w_parts[1][1]]
    w_stack = jnp.stack(w_seg, axis=0).transpose(2, 1, 4, 0, 3)
    w_stack = w_stack.reshape(pairs, 2, ck * chans, 4, state)
    w_mat = jnp.einsum('ab,gaxkq->gaxkbq', eye2, w_stack).reshape(pairs, 2 * ck * chans, 4 * 2 * state)

    v_seg = [v_parts[0][0], v_parts[0][1], v_parts[1][0], v_parts[1][1]]
    v_stack = jnp.stack(v_seg, axis=0).reshape(4, pairs, 2, state, ck * chans)
    v_mat = jnp.einsum('ab,kgaqy->gkaqby', eye2, v_stack).reshape(pairs, 4 * 2 * state, 2 * ck * chans)

    a_seg = jnp.stack([a_parts[0][0], a_parts[0][1], a_parts[1][0], a_parts[1][1]], axis=0)
    tp = S5_TILE_PAIRS
    a_tbl = a_seg.reshape(4, pairs // tp, tp, 2 * state).transpose(1, 0, 2, 3)
    return t_mat.astype(BF16), w_mat.astype(BF16), v_mat.astype(BF16), a_tbl


def _s5c(proj3, col0, width, tables):
    bsz, seq, _ = proj3.shape
    t_mat, w_mat, v_mat, a_tbl = tables
    tp = S5_TILE_PAIRS
    tiles = t_mat.shape[0] // tp
    cols = t_mat.shape[1]
    tc = width // tiles
    assert tc == LANES and cols == 4 * LANES
    n_chunks = seq // S5_CHUNK
    return pl.pallas_call(
        _s5c_kernel,
        grid=(tiles, bsz),
        in_specs=[
            pl.BlockSpec((1, seq, tc), lambda t, b: (b, 0, col0 // tc + t)),
            pl.BlockSpec((tp, cols, cols), lambda t, b: (t, 0, 0)),
            pl.BlockSpec((tp, cols, cols), lambda t, b: (t, 0, 0)),
            pl.BlockSpec((tp, cols, cols), lambda t, b: (t, 0, 0)),
            pl.BlockSpec((1, 4, tp, LANES), lambda t, b: (t, 0, 0, 0)),
        ],
        out_specs=pl.BlockSpec((1, seq, tc), lambda t, b: (b, 0, t)),
        out_shape=jax.ShapeDtypeStruct((bsz, seq, width), F32),
        scratch_shapes=[pltpu.VMEM((4 * tp, n_chunks, LANES), F32),
                        pltpu.VMEM((n_chunks, tp, cols), F32),
                        pltpu.VMEM((n_chunks, tp, cols), F32)],
        compiler_params=_cparams(("arbitrary", "arbitrary")),
        name="s5",
    )(proj3, t_mat, w_mat, v_mat, a_tbl)


def _retention_kernel(q_ref, k_ref, v_ref, g_ref, cos_ref, sin_ref, lg_ref, gain_ref,
                      o_ref, qs_ref, ks_ref, vs_ref, decay_ref):
    seq = q_ref.shape[1]
    half = RET_HEAD_DIM // 2
    tq = RET_Q_TILE
    cos = cos_ref[...]
    sin = sin_ref[...]

    @pl.when(pl.program_id(1) == 0)
    def _():
        lg_fwd = lg_ref[0, 0:1, :]
        lg_bwd = lg_ref[0, 1:2, :]

        def fill(i, _):
            sl = pl.ds(pl.multiple_of(i * tq, tq), tq)
            t_idx = lax.broadcasted_iota(I32, (tq, seq), 0) + i * tq
            s_idx = lax.broadcasted_iota(I32, (tq, seq), 1)
            rel = (t_idx - s_idx).astype(F32)
            decay_ref[sl, :] = jnp.exp(jnp.where(rel >= 0.0, lg_fwd * rel, -lg_bwd * rel))
            return 0

        lax.fori_loop(0, seq // tq, fill, 0)

    def rot(t_ref, scale):
        t1 = t_ref[0, :, :half]
        t2 = t_ref[0, :, half:]
        return jnp.concatenate([(t1 * cos - t2 * sin) * scale, (t1 * sin + t2 * cos) * scale], axis=-1)

    qs_ref[...] = rot(q_ref, 1.0).astype(BF16)
    ks_ref[...] = rot(k_ref, RET_HEAD_DIM ** -0.5).astype(BF16)
    vs_ref[...] = v_ref[0].astype(BF16)
    gain = gain_ref[...]

    def q_tile(i, _):
        sl = pl.ds(pl.multiple_of(i * tq, tq), tq)
        scores = _dot_nt(qs_ref[sl, :], ks_ref[...])
        p = (scores * decay_ref[sl, :]).astype(BF16)
        o = jnp.dot(p, vs_ref[...], preferred_element_type=F32)
        y = o * lax.rsqrt(jnp.mean(o * o, axis=-1, keepdims=True) + EPS) * gain
        g = g_ref[0, sl, :]
        o_ref[0, sl, :] = (y * (g * jax.nn.sigmoid(g))).astype(o_ref.dtype)
        return 0

    lax.fori_loop(0, seq // tq, q_tile, 0)


def _retention(proj3, ret_gain, width):
    bsz, seq, _ = proj3.shape
    dh = RET_HEAD_DIM
    heads = width // dh
    inv_freq = ROPE_BASE ** (-jnp.arange(0, dh, 2, dtype=F32) / dh)
    ang = jnp.arange(seq, dtype=F32)[:, None] * inv_freq[None, :]
    cos, sin = jnp.cos(ang), jnp.sin(ang)
    log_gamma = jnp.log1p(-jnp.exp2(-5.0 - jnp.arange(heads, dtype=F32)))
    lg = jnp.stack([log_gamma, log_gamma[::-1]], axis=1)
    lg = jnp.broadcast_to(lg[:, :, None], (heads, 2, seq))

    def col(k):
        return pl.BlockSpec((1, seq, dh), lambda h, b: (b, 0, k * heads + h))

    return pl.pallas_call(
        _retention_kernel,
        grid=(heads, bsz),
        in_specs=[col(0), col(1), col(2), col(3),
                  pl.BlockSpec((seq, dh // 2), lambda h, b: (0, 0)),
                  pl.BlockSpec((seq, dh // 2), lambda h, b: (0, 0)),
                  pl.BlockSpec((1, 2, seq), lambda h, b: (h, 0, 0)),
                  pl.BlockSpec((1, dh), lambda h, b: (0, h))],
        out_specs=pl.BlockSpec((1, seq, dh), lambda h, b: (b, 0, h)),
        out_shape=jax.ShapeDtypeStruct((bsz, seq, width), BF16),
        scratch_shapes=[pltpu.VMEM((seq, dh), BF16)] * 3 + [pltpu.VMEM((seq, seq), F32)],
        compiler_params=_cparams(("arbitrary", "arbitrary")),
        name="retention",
    )(proj3, proj3, proj3, proj3, cos, sin, lg, ret_gain.reshape(1, width))


def _fnet_kernel(x_ref, cc_ref, sc_ref, cs_ref, ss_ref, o_ref, a1_ref, a2_ref):
    @pl.when(pl.program_id(1) == 0)
    def _():
        x = x_ref[0].astype(BF16)
        a1_ref[...] = jnp.dot(x, cc_ref[...], preferred_element_type=F32).astype(BF16)
        a2_ref[...] = jnp.dot(x, sc_ref[...], preferred_element_type=F32).astype(BF16)

    y = (jnp.dot(cs_ref[...], a1_ref[...], preferred_element_type=F32)
         - jnp.dot(ss_ref[...], a2_ref[...], preferred_element_type=F32))
    o_ref[0] = y.astype(o_ref.dtype)


def _dft_mats(n):
    idx = jnp.arange(n, dtype=I32)
    ang = (2.0 * math.pi / n) * ((idx[:, None] * idx[None, :]) % n).astype(F32)
    scale = n ** -0.5
    return jnp.cos(ang) * scale, jnp.sin(ang) * scale


def _fnet(proj3, col0, width):
    bsz, seq, _ = proj3.shape
    gw = width // FNET_GROUPS
    cs, ss = _dft_mats(seq)
    cg, sg = _dft_mats(gw)
    eye = jnp.eye(FNET_GROUPS, dtype=F32)
    cc = jnp.kron(eye, cg)
    sc = jnp.kron(eye, sg)
    tr = FNET_ROW_TILE
    return pl.pallas_call(
        _fnet_kernel,
        grid=(bsz, seq // tr),
        in_specs=[
            pl.BlockSpec((1, seq, width), lambda b, i: (b, 0, col0 // width)),
            pl.BlockSpec((width, width), lambda b, i: (0, 0)),
            pl.BlockSpec((width, width), lambda b, i: (0, 0)),
            pl.BlockSpec((tr, seq), lambda b, i: (i, 0)),
            pl.BlockSpec((tr, seq), lambda b, i: (i, 0)),
        ],
        out_specs=pl.BlockSpec((1, tr, width), lambda b, i: (b, i, 0)),
        out_shape=jax.ShapeDtypeStruct((bsz, seq, width), BF16),
        scratch_shapes=[pltpu.VMEM((seq, width), BF16)] * 2,
        compiler_params=_cparams(("arbitrary", "arbitrary")),
        name="fnet",
    )(proj3, cc.astype(BF16), sc.astype(BF16), cs.astype(BF16), ss.astype(BF16))


def _outproj_kernel(*refs, glu):
    if glu:
        (x_ref, a_ref, b_ref, gw_ref, gb_ref, wa_ref, wb_ref, gm_ref, gain_ref, sh_ref, sc_ref,
         rwh_ref, rwl_ref, rb_ref, x1_ref, h_ref, idx_ref, gate_ref, cnt_ref, base_ref) = refs
    else:
        (x_ref, a_ref, b_ref, wa_ref, wb_ref, gm_ref, gain_ref, sh_ref, sc_ref,
         rwh_ref, rwl_ref, rb_ref, x1_ref, h_ref, idx_ref, gate_ref, cnt_ref, base_ref) = refs

    @pl.when(pl.program_id(0) == 0)
    def _():
        base_ref[...] = jnp.zeros(base_ref.shape, F32)

    bm = b_ref[...]
    if glu:
        gl = jnp.dot(bm.astype(BF16), gw_ref[...], preferred_element_type=F32) + gb_ref[...]
        bm = (bm.astype(F32) * jax.nn.sigmoid(gl)).astype(BF16)
    y = (jnp.dot(a_ref[...], wa_ref[...], preferred_element_type=F32)
         + jnp.dot(bm, wb_ref[...], preferred_element_type=F32))
    x1 = x_ref[...] + gm_ref[0] * y
    x1_ref[...] = x1
    h = _norm_modulate(x1, gain_ref[...], sh_ref[0], sc_ref[0])
    h_ref[...] = h

    h_hi = h.astype(BF16)
    h_lo = (h - h_hi.astype(F32)).astype(BF16)
    logits = (jnp.dot(h_hi, rwh_ref[...], preferred_element_type=F32)
              + jnp.dot(h_lo, rwh_ref[...], preferred_element_type=F32)
              + jnp.dot(h_hi, rwl_ref[...], preferred_element_type=F32)
              + rb_ref[...])
    lane = lax.broadcasted_iota(I32, logits.shape, 1)
    vals, idxs = [], []
    for _ in range(TOP_K):
        m = jnp.max(logits, axis=-1, keepdims=True)
        ik = jnp.min(jnp.where(logits == m, lane, ROUTER_PAD), axis=-1, keepdims=True)
        vals.append(m)
        idxs.append(ik)
        logits = jnp.where(lane == ik, -jnp.inf, logits)
    exps = [jnp.exp(v - vals[0]) for v in vals]
    denom = exps[0] + exps[1] + exps[2] + exps[3]

    tm = lane.shape[0]
    onehot = jnp.zeros(lane.shape, F32)
    for k in range(TOP_K):
        onehot = onehot + (lane == idxs[k]).astype(F32)
    before = (lax.broadcasted_iota(I32, (tm, tm), 1) < lax.broadcasted_iota(I32, (tm, tm), 0)).astype(BF16)
    count = jnp.dot(before, onehot.astype(BF16), preferred_element_type=F32) + base_ref[...]
    new_base = base_ref[...] + jnp.sum(onehot, axis=0, keepdims=True)
    base_ref[...] = new_base
    cnt_ref[...] = new_base.astype(I32)

    idx_out = jnp.zeros(lane.shape, I32)
    gate_out = jnp.zeros(lane.shape, F32)
    for k in range(TOP_K):
        rank = jnp.sum(jnp.where(lane == idxs[k], count, 0.0), axis=-1, keepdims=True).astype(I32)
        idx_out = jnp.where(lane == k, idxs[k], idx_out)
        idx_out = jnp.where(lane == TOP_K + k, rank, idx_out)
        gate_out = jnp.where(lane == k, exps[k] / denom, gate_out)
    idx_ref[...] = idx_out
    gate_ref[...] = gate_out


def _out_proj(x2d, mix_a, mix_b, w_out, g_mix, gain, shift, scale, router_w, router_b, seq,
              glu_w=None, glu_b=None):
    t, d = x2d.shape
    wa_rows = mix_a.shape[1]
    wb_rows = mix_b.shape[1]
    tm = 256
    per_seq = seq // tm
    n_exp = router_w.shape[1]
    rw = jnp.zeros((d, ROUTER_PAD), F32).at[:, :n_exp].set(router_w)
    rw_hi = rw.astype(BF16)
    rw_lo = (rw - rw_hi.astype(F32)).astype(BF16)
    rb = jnp.full((1, ROUTER_PAD), NEG_BIG, F32).at[0, :n_exp].set(router_b)
    w_bf = w_out.astype(BF16)
    glu = glu_w is not None

    def rows(width):
        return pl.BlockSpec((tm, width), lambda i: (i, 0))

    def full(r, c):
        return pl.BlockSpec((r, c), lambda i: (0, 0))

    def per_batch():
        return pl.BlockSpec((1, 1, d), lambda i: (i // per_seq, 0, 0))

    in_specs = [rows(d), rows(wa_rows), rows(wb_rows)]
    args = [x2d, mix_a, mix_b]
    if glu:
        in_specs += [full(wb_rows, wb_rows), full(1, wb_rows)]
        args += [glu_w.astype(BF16), glu_b.reshape(1, wb_rows)]
    in_specs += [pl.BlockSpec((wa_rows, d), lambda i: (0, 0)),
                 pl.BlockSpec((wb_rows, d), lambda i: (wa_rows // wb_rows, 0)),
                 per_batch(), full(1, d), per_batch(), per_batch(),
                 full(d, ROUTER_PAD), full(d, ROUTER_PAD), full(1, ROUTER_PAD)]
    args += [w_bf, w_bf, g_mix, gain.reshape(1, d), shift, scale, rw_hi, rw_lo, rb]
    return pl.pallas_call(
        functools.partial(_outproj_kernel, glu=glu),
        grid=(t // tm,),
        in_specs=in_specs,
        out_specs=[rows(d), rows(d), rows(ROUTER_PAD), rows(ROUTER_PAD), full(1, ROUTER_PAD)],
        out_shape=[jax.ShapeDtypeStruct((t, d), F32), jax.ShapeDtypeStruct((t, d), F32),
                   jax.ShapeDtypeStruct((t, ROUTER_PAD), I32), jax.ShapeDtypeStruct((t, ROUTER_PAD), F32),
                   jax.ShapeDtypeStruct((1, ROUTER_PAD), I32)],
        scratch_shapes=[pltpu.VMEM((1, ROUTER_PAD), F32)],
        compiler_params=_cparams(("arbitrary",)),
        name="out_proj_glu" if glu else "out_proj",
    )(*args)


def _routing_tables(top_idx, rank, counts, n_tok):
    n_assign = n_tok * TOP_K
    r = MOE_ROWS
    n_rb = n_assign // r + N_EXPERTS
    experts = jnp.arange(N_EXPERTS, dtype=I32)
    nblk = (counts + r - 1) // r
    blk_end = jnp.cumsum(nblk)
    blk_start = blk_end - nblk
    row_start = jnp.sum(jnp.where(top_idx[:, :, None] == experts[None, None, :],
                                  (blk_start * r)[None, None, :], 0), axis=-1)
    dest = (row_start + rank).reshape(-1).astype(I32)
    n_valid_blocks = blk_end[-1]
    rb = jnp.arange(n_rb, dtype=I32)
    rb_e = jnp.sum((blk_end[None, :] <= rb[:, None]).astype(I32), axis=1)
    rb_e = jnp.minimum(rb_e, N_EXPERTS - 1).astype(I32)
    tail = jnp.where(nblk > 0, blk_end - 1, -1)
    spare = n_valid_blocks + experts
    spare = jnp.where(spare < n_rb, spare, -1)
    zero_blocks = jnp.concatenate([tail, spare]).astype(I32)
    return dict(dest=dest, rb_e=rb_e, zero_blocks=zero_blocks,
                n_valid_blocks=n_valid_blocks.reshape(1).astype(I32),
                nblk=nblk.astype(I32), blk_start=blk_start.astype(I32), n_rb=n_rb, n_slots=n_rb * r)


def _dispatch_kernel(dest_ref, zb_ref, h_ref, xs_hbm, zero_ref, zsem_ref, sem_ref):
    i = pl.program_id(0)
    r = MOE_ROWS
    td = DISPATCH_TOKENS

    @pl.when(i == 0)
    def _():
        zero_ref[...] = jnp.zeros(zero_ref.shape, zero_ref.dtype)

        def zero_copy(z):
            row0 = pl.multiple_of(zb_ref[z] * r, r)
            return pltpu.make_async_copy(zero_ref, xs_hbm.at[pl.ds(row0, r), :], zsem_ref.at[0])

        def start(z, _):
            @pl.when(zb_ref[z] >= 0)
            def _():
                zero_copy(z).start()
            return 0

        def wait(z, _):
            @pl.when(zb_ref[z] >= 0)
            def _():
                zero_copy(z).wait()
            return 0

        lax.fori_loop(0, zb_ref.shape[0], start, 0)
        lax.fori_loop(0, zb_ref.shape[0], wait, 0)

    def group(g, _):
        for s in range(DMA_UNROLL):
            t = g * DMA_UNROLL + s
            for k in range(TOP_K):
                dst = dest_ref[(i * td + t) * TOP_K + k]
                pltpu.make_async_copy(h_ref.at[pl.ds(t, 1), :], xs_hbm.at[pl.ds(dst, 1), :],
                                      sem_ref.at[0]).start()
        return 0

    lax.fori_loop(0, td // DMA_UNROLL, group, 0)
    for _ in range(TOP_K):
        pltpu.make_async_copy(h_ref, xs_hbm.at[pl.ds(0, td), :], sem_ref.at[0]).wait()


def _moe_dispatch(h2d, rt):
    t, d = h2d.shape
    td = DISPATCH_TOKENS
    grid_spec = pltpu.PrefetchScalarGridSpec(
        num_scalar_prefetch=2,
        grid=(t // td,),
        in_specs=[pl.BlockSpec((td, d), lambda i, dst, zb: (i, 0))],
        out_specs=pl.BlockSpec(memory_space=pl.ANY),
        scratch_shapes=[pltpu.VMEM((MOE_ROWS, d), F32), pltpu.SemaphoreType.DMA((1,)),
                        pltpu.SemaphoreType.DMA((1,))],
    )
    return pl.pallas_call(
        _dispatch_kernel,
        grid_spec=grid_spec,
        out_shape=jax.ShapeDtypeStruct((rt['n_slots'], d), F32),
        compiler_params=_cparams(("arbitrary",)),
        name="moe_dispatch",
    )(rt['dest'], rt['zero_blocks'], h2d)


def _cast_rows(src_ref, dst_ref, chunk=256):
    chunk = min(chunk, dst_ref.shape[0])
    n = dst_ref.shape[0] // chunk

    def body(c, _):
        sl = pl.ds(pl.multiple_of(c * chunk, chunk), chunk)
        dst_ref[sl, :] = src_ref[sl, :].astype(dst_ref.dtype)
        return 0

    lax.fori_loop(0, n, body, 0)


def _stagger_tables(group, cols, parts):
    n = group.shape[0]
    idx = jnp.arange(n, dtype=I32)
    is_first = jnp.concatenate([jnp.ones((1,), bool), group[1:] != group[:-1]])
    first_cur = lax.cummax(jnp.where(is_first, idx, 0))
    nxt = jnp.where(is_first, idx, n)
    first_next = jnp.concatenate([lax.cummin(nxt[::-1])[::-1][1:], jnp.full((1,), n, I32)])
    out = [[] for _ in cols]
    for p in range(parts):
        switch = (idx >= jnp.maximum(first_next - p, first_cur + 1)) & (first_next < n)
        eff = jnp.where(switch, first_next, idx)
        for k, c in enumerate(cols):
            out[k].append(c[eff])
    return [jnp.stack(o) for o in out]


def _moe_up_kernel(rb_ref, j_ref, e_ref, first_ref, valid_ref, pe_ref, pj_ref, x_ref, *refs):
    parts = MOE_W_PARTS
    wg_refs = refs[:parts]
    wu_refs = refs[parts:2 * parts]
    bg_ref, bu_ref, o_ref, wgs_ref, wus_ref = refs[2 * parts:]
    i = pl.program_id(0)
    rows = wgs_ref.shape[0] // parts

    @pl.when(first_ref[i] == 1)
    def _():
        for p in range(parts):
            _cast_rows(wg_refs[p].at[0, 0], wgs_ref.at[pl.ds(p * rows, rows)])
            _cast_rows(wu_refs[p].at[0, 0], wus_ref.at[pl.ds(p * rows, rows)])

    @pl.when(valid_ref[i] == 1)
    def _():
        x = x_ref[...].astype(BF16)
        gate = jnp.dot(x, wgs_ref[...], preferred_element_type=F32) + bg_ref[0]
        up = jnp.dot(x, wus_ref[...], preferred_element_type=F32) + bu_ref[0]
        gate = jnp.minimum(gate, SWIGLU_LIMIT)
        up = jnp.clip(up, -SWIGLU_LIMIT, SWIGLU_LIMIT)
        act = (up + 1.0) * gate * jax.nn.sigmoid(SWIGLU_ALPHA * gate)
        o_ref[...] = act.astype(o_ref.dtype)

    @pl.when(valid_ref[i] == 0)
    def _():
        o_ref[...] = jnp.zeros(o_ref.shape, o_ref.dtype)


def _moe_up(x_sorted, w_gu, b_gu, layer, rt):
    n_slots, d = x_sorted.shape
    _, n_exp, _, two_f = w_gu.shape
    f = two_f // 2
    r, tn = MOE_ROWS, MOE_UP_TN
    nj = f // tn
    n_rb = rt['n_rb']
    n_items = n_rb * nj
    rb = jnp.arange(n_rb, dtype=I32)
    e_of = rt['rb_e']
    nvb = rt['n_valid_blocks'][0]
    nblk = rt['nblk'].at[n_exp - 1].add(n_rb - nvb)
    q = rb - rt['blk_start'][e_of]
    pos = (nj * rt['blk_start'][e_of][:, None]
           + jnp.arange(nj, dtype=I32)[None, :] * nblk[e_of][:, None] + q[:, None]).reshape(-1)
    rb2 = jnp.broadcast_to(rb[:, None], (n_rb, nj)).reshape(-1)
    j2 = jnp.broadcast_to(jnp.arange(nj, dtype=I32)[None, :], (n_rb, nj)).reshape(-1)
    it_rb = jnp.zeros((n_items,), I32).at[pos].set(rb2)
    it_j = jnp.zeros((n_items,), I32).at[pos].set(j2)
    it_valid = (it_rb < nvb).astype(I32)
    it_e = e_of[it_rb]
    prev_e = jnp.concatenate([jnp.full((1,), -1, I32), it_e[:-1]])
    prev_j = jnp.concatenate([jnp.full((1,), -1, I32), it_j[:-1]])
    is_first = (it_e != prev_e) | (it_j != prev_j)
    it_first = is_first.astype(I32)
    parts = MOE_W_PARTS
    part_e, part_j = _stagger_tables(jnp.cumsum(it_first), [it_e, it_j], parts)

    def w_spec(p, col0):
        return pl.BlockSpec((1, 1, d // parts, tn),
                            lambda i, rbt, jt, et, ft, vt, pe, pj: (layer, pe[p, i], p, col0 + pj[p, i]))

    grid_spec = pltpu.PrefetchScalarGridSpec(
        num_scalar_prefetch=7,
        grid=(n_items,),
        in_specs=([pl.BlockSpec((r, d), lambda i, rbt, jt, et, ft, vt, pe, pj: (rbt[i], 0))]
                  + [w_spec(p, 0) for p in range(parts)] + [w_spec(p, nj) for p in range(parts)]
                  + [pl.BlockSpec((1, 1, tn), lambda i, rbt, jt, et, ft, vt, pe, pj: (et[i], 0, jt[i])),
                     pl.BlockSpec((1, 1, tn), lambda i, rbt, jt, et, ft, vt, pe, pj: (et[i], 0, nj + jt[i]))]),
        out_specs=pl.BlockSpec((r, tn), lambda i, rbt, jt, et, ft, vt, pe, pj: (rbt[i], jt[i])),
        scratch_shapes=[pltpu.VMEM((d, tn), BF16), pltpu.VMEM((d, tn), BF16)],
    )
    b3 = b_gu.reshape(n_exp, 1, two_f)
    return pl.pallas_call(
        _moe_up_kernel,
        grid_spec=grid_spec,
        out_shape=jax.ShapeDtypeStruct((n_slots, f), BF16),
        compiler_params=_cparams(("arbitrary",)),
        name="moe_up",
    )(it_rb, it_j, it_e, it_first, it_valid, part_e, part_j, x_sorted,
      *([w_gu] * (2 * parts)), b3, b3)


def _moe_dn_kernel(e_ref, nvb_ref, pe_ref, a_ref, *refs):
    parts = MOE_W_PARTS
    w_refs = refs[:parts]
    b_ref, o_ref, ws_ref = refs[parts:]
    i = pl.program_id(0)
    valid = i < nvb_ref[0]
    e = e_ref[i]
    prev_e = e_ref[jnp.maximum(i - 1, 0)]
    rows = ws_ref.shape[0] // parts

    @pl.when(valid & ((i == 0) | (e != prev_e)))
    def _():
        for p in range(parts):
            _cast_rows(w_refs[p].at[0, 0], ws_ref.at[pl.ds(p * rows, rows)])

    @pl.when(valid)
    def _():
        o_ref[...] = jnp.dot(a_ref[...], ws_ref[...], preferred_element_type=F32) + b_ref[0]

    @pl.when(jnp.logical_not(valid))
    def _():
        o_ref[...] = jnp.zeros(o_ref.shape, o_ref.dtype)


def _moe_dn(act, w_dn, b_dn, layer, rt):
    n_slots, f = act.shape
    _, n_exp, _, d = w_dn.shape
    r = MOE_ROWS
    parts = MOE_W_PARTS
    (part_e,) = _stagger_tables(rt['rb_e'], [rt['rb_e']], parts)

    def w_spec(p):
        return pl.BlockSpec((1, 1, f // parts, d), lambda i, et, nvb, pe: (layer, pe[p, i], p, 0))

    grid_spec = pltpu.PrefetchScalarGridSpec(
        num_scalar_prefetch=3,
        grid=(rt['n_rb'],),
        in_specs=([pl.BlockSpec((r, f), lambda i, et, nvb, pe: (jnp.minimum(i, nvb[0] - 1), 0))]
                  + [w_spec(p) for p in range(parts)]
                  + [pl.BlockSpec((1, 1, d), lambda i, et, nvb, pe: (et[i], 0, 0))]),
        out_specs=pl.BlockSpec((r, d), lambda i, et, nvb, pe: (i, 0)),
        scratch_shapes=[pltpu.VMEM((f, d), BF16)],
    )
    return pl.pallas_call(
        _moe_dn_kernel,
        grid_spec=grid_spec,
        out_shape=jax.ShapeDtypeStruct((n_slots, d), F32),
        compiler_params=_cparams(("arbitrary",)),
        name="moe_dn",
    )(rt['rb_e'], rt['n_valid_blocks'], part_e, act, *([w_dn] * parts), b_dn.reshape(n_exp, 1, d))


def _combine_kernel(dest_ref, x_ref, gate_ref, g_ref, fg_ref, y_hbm, o_ref, buf_ref, sem_ref, *, final):
    i = pl.program_id(0)
    n_steps = pl.num_programs(0)
    tc = COMBINE_TOKENS

    def issue(step, slot):
        def group(g, _):
            for s in range(DMA_UNROLL):
                t = g * DMA_UNROLL + s
                for k in range(TOP_K):
                    src = dest_ref[(step * tc + t) * TOP_K + k]
                    pltpu.make_async_copy(y_hbm.at[pl.ds(src, 1), :], buf_ref.at[slot, k, pl.ds(t, 1), :],
                                          sem_ref.at[slot]).start()
            return 0

        lax.fori_loop(0, tc // DMA_UNROLL, group, 0)

    @pl.when(i == 0)
    def _():
        issue(0, 0)

    @pl.when(i + 1 < n_steps)
    def _():
        issue(i + 1, (i + 1) % 2)

    slot = i % 2
    for k in range(TOP_K):
        pltpu.make_async_copy(y_hbm.at[pl.ds(0, tc), :], buf_ref.at[slot, k], sem_ref.at[slot]).wait()
    gates = gate_ref[...]
    y = gates[:, 0:1] * buf_ref[slot, 0]
    for k in range(1, TOP_K):
        y = y + gates[:, k:k + 1] * buf_ref[slot, k]
    x2 = x_ref[...] + g_ref[0] * y
    if final:
        x2 = x2 * lax.rsqrt(jnp.mean(x2 * x2, axis=-1, keepdims=True) + EPS) * fg_ref[...]
    o_ref[...] = x2


def _combine(x1, y_sorted, gates, rt, g_ffn, final_gain, seq, final):
    t, d = x1.shape
    tc = COMBINE_TOKENS
    per_seq = seq // tc
    grid_spec = pltpu.PrefetchScalarGridSpec(
        num_scalar_prefetch=1,
        grid=(t // tc,),
        in_specs=[pl.BlockSpec((tc, d), lambda i, dst: (i, 0)),
                  pl.BlockSpec((tc, ROUTER_PAD), lambda i, dst: (i, 0)),
                  pl.BlockSpec((1, 1, d), lambda i, dst: (i // per_seq, 0, 0)),
                  pl.BlockSpec((1, d), lambda i, dst: (0, 0)),
                  pl.BlockSpec(memory_space=pl.ANY)],
        out_specs=pl.BlockSpec((tc, d), lambda i, dst: (i, 0)),
        scratch_shapes=[pltpu.VMEM((2, TOP_K, tc, d), F32), pltpu.SemaphoreType.DMA((2,))],
    )
    return pl.pallas_call(
        functools.partial(_combine_kernel, final=final),
        grid_spec=grid_spec,
        out_shape=jax.ShapeDtypeStruct((t, d), F32),
        compiler_params=_cparams(("arbitrary",)),
        name="combine_final" if final else "combine",
    )(rt['dest'], x1, gates, g_ffn, final_gain.reshape(1, d), y_sorted)


def _moe(x1, h, idx, gates, counts, w_gu, b_gu, w_dn, b_dn, layer, g_ffn, final_gain, seq, final):
    n_tok = x1.shape[0]
    rt = _routing_tables(idx[:, :TOP_K], idx[:, TOP_K:2 * TOP_K], counts[0, :N_EXPERTS], n_tok)
    x_sorted = _moe_dispatch(h, rt)
    act = _moe_up(x_sorted, w_gu, b_gu, layer, rt)
    y_sorted = _moe_dn(act, w_dn, b_dn, layer, rt)
    return _combine(x1, y_sorted, gates, rt, g_ffn, final_gain, seq, final)


def kernel(x, c, ada_w, ada_b, norm_mix_gain, norm_ffn_gain, ab_w_in, ab_w_out, hg_lb_logits, hg_norm_gain, s5_lam_re, s5_lam_im, s5_log_dt, s5_b_re, s5_b_im, s5_c_re, s5_c_im, s5_d, s5_glu_w, s5_glu_b, cd_w_in, cd_w_out, ret_norm_gain, router_w, router_b, moe_w_gu, moe_b_gu, moe_w_dn, moe_b_dn, final_gain):
    bsz, seq, d = x.shape
    depth = ada_w.shape[0]
    n_tok = bsz * seq
    hg_width = hg_lb_logits.shape[1]
    s5_width = s5_glu_w.shape[1]
    ret_width = ret_norm_gain.shape[1]
    fnet_width = cd_w_out.shape[1] - ret_width

    lower_bounds = jnp.cumsum(jax.nn.softmax(hg_lb_logits.astype(F32), axis=0), axis=0)
    mod = _ada_mod(c, ada_w, ada_b)
    xr = x.reshape(n_tok, d)
    for layer in range(depth):
        sh_mix, sc_mix, g_mix, sh_ffn, sc_ffn, g_ffn = (
            mod[layer, :, k * d:(k + 1) * d].reshape(bsz, 1, d) for k in range(6))
        j = layer // 2
        if layer % 2 == 0:
            proj = _in_proj(xr, norm_mix_gain[layer], sh_mix, sc_mix, ab_w_in[j].astype(BF16), seq)
            proj3 = proj.reshape(bsz, seq, proj.shape[1])
            mix_a = _hgrn2(proj3, lower_bounds[j], hg_norm_gain[j], hg_width)
            tables = _s5c_tables(s5_lam_re[j], s5_lam_im[j], s5_log_dt[j], s5_b_re[j], s5_b_im[j],
                                 s5_c_re[j], s5_c_im[j], s5_d[j])
            mix_b = _s5c(proj3, 5 * hg_width, s5_width, tables)
            x1, h, idx, gate, counts = _out_proj(
                xr, mix_a.reshape(n_tok, hg_width), mix_b.reshape(n_tok, s5_width), ab_w_out[j],
                g_mix, norm_ffn_gain[layer], sh_ffn, sc_ffn, router_w[layer], router_b[layer], seq,
                glu_w=s5_glu_w[j], glu_b=s5_glu_b[j])
        else:
            proj = _in_proj(xr, norm_mix_gain[layer], sh_mix, sc_mix, cd_w_in[j].astype(BF16), seq)
            proj3 = proj.reshape(bsz, seq, proj.shape[1])
            mix_a = _retention(proj3, ret_norm_gain[j], ret_width)
            mix_b = _fnet(proj3, 4 * ret_width, fnet_width)
            x1, h, idx, gate, counts = _out_proj(
                xr, mix_a.reshape(n_tok, ret_width), mix_b.reshape(n_tok, fnet_width), cd_w_out[j],
                g_mix, norm_ffn_gain[layer], sh_ffn, sc_ffn, router_w[layer], router_b[layer], seq)
        xr = _moe(x1, h, idx, gate, counts, moe_w_gu, moe_b_gu[layer], moe_w_dn, moe_b_dn[layer],
                  layer, g_ffn, final_gain, seq, final=(layer == depth - 1))
    return xr.reshape(bsz, seq, d)
```

```python
import functools
import math

import jax
import jax.numpy as jnp
from jax import lax
from jax.experimental import pallas as pl
from jax.experimental.pallas import tpu as pltpu

F32 = jnp.float32
BF16 = jnp.bfloat16
I32 = jnp.int32

EPS = 1e-6
LANES = 128
SUBLANES = 8
VMEM_LIMIT = 56 * 1024 * 1024

HG_HEAD_DIM = 128
HG_CHUNK = 64
HG_GROUP = 256
HG_EXP_CLAMP = 80.0

S5_GROUP = 16
S5_STATE = 64
S5_CHUNK = 16
S5_UNROLL = 8
S5_TILE_PAIRS = 4
S5_ROW_SPLIT = 4

RET_HEAD_DIM = 256
RET_Q_TILE = 256
ROPE_BASE = 10000.0

FNET_GROUPS = 4
FNET_ROW_TILE = 512

N_EXPERTS = 32
TOP_K = 4
SWIGLU_LIMIT = 7.0
SWIGLU_ALPHA = 1.702
MOE_ROWS = 256
MOE_UP_TN = 1024
MOE_W_PARTS = 4
DISPATCH_TOKENS = 512
COMBINE_TOKENS = 128
DMA_UNROLL = 8
ROUTER_PAD = LANES
NEG_BIG = -1e30


def _cparams(semantics):
    return pltpu.CompilerParams(dimension_semantics=semantics, vmem_limit_bytes=VMEM_LIMIT)


def _ada_kernel(c_ref, w_ref, b_ref, o_ref):
    c = c_ref[...]
    cond = c * jax.nn.sigmoid(c)
    o_ref[0] = jnp.dot(cond.astype(BF16), w_ref[0].astype(BF16),
                       preferred_element_type=F32) + b_ref[0]


def _ada_mod(c, ada_w, ada_b):
    depth, d, n = ada_w.shape
    bsz = c.shape[0]
    tn = 1024
    return pl.pallas_call(
        _ada_kernel,
        grid=(depth, n // tn),
        in_specs=[
            pl.BlockSpec((bsz, d), lambda l, j: (0, 0)),
            pl.BlockSpec((1, d, tn), lambda l, j: (l, 0, j)),
            pl.BlockSpec((1, 1, tn), lambda l, j: (l, 0, j)),
        ],
        out_specs=pl.BlockSpec((1, bsz, tn), lambda l, j: (l, 0, j)),
        out_shape=jax.ShapeDtypeStruct((depth, bsz, n), F32),
        compiler_params=_cparams(("arbitrary", "arbitrary")),
        name="ada_mod",
    )(c, ada_w, ada_b.reshape(depth, 1, n))


def _norm_modulate(x, gain, shift, scale):
    ms = jnp.mean(x * x, axis=-1, keepdims=True)
    y = x * lax.rsqrt(ms + EPS) * gain
    return y * (1.0 + scale) + shift


def _inproj_kernel(x_ref, gain_ref, sh_ref, sc_ref, w_ref, o_ref, h_ref):
    @pl.when(pl.program_id(1) == 0)
    def _():
        h_ref[...] = _norm_modulate(x_ref[...], gain_ref[...], sh_ref[0], sc_ref[0]).astype(BF16)

    o_ref[...] = jnp.dot(h_ref[...], w_ref[...], preferred_element_type=F32)


def _in_proj(x2d, gain, shift, scale, w_bf16, seq):
    t, d = x2d.shape
    n = w_bf16.shape[1]
    tm, tn = 1024, 512
    per_seq = seq // tm
    return pl.pallas_call(
        _inproj_kernel,
        grid=(t // tm, n // tn),
        in_specs=[
            pl.BlockSpec((tm, d), lambda i, j: (i, 0)),
            pl.BlockSpec((1, d), lambda i, j: (0, 0)),
            pl.BlockSpec((1, 1, d), lambda i, j: (i // per_seq, 0, 0)),
            pl.BlockSpec((1, 1, d), lambda i, j: (i // per_seq, 0, 0)),
            pl.BlockSpec((d, tn), lambda i, j: (0, j)),
        ],
        out_specs=pl.BlockSpec((tm, tn), lambda i, j: (i, j)),
        out_shape=jax.ShapeDtypeStruct((t, n), F32),
        scratch_shapes=[pltpu.VMEM((tm, d), BF16)],
        compiler_params=_cparams(("arbitrary", "arbitrary")),
        name="in_proj",
    )(x2d, gain.reshape(1, d), shift, scale, w_bf16)


def _split3(a):
    hi = a.astype(BF16)
    r1 = a - hi.astype(F32)
    mid = r1.astype(BF16)
    lo = (r1 - mid.astype(F32)).astype(BF16)
    return hi, mid, lo


def _tri_sum(tri, a):
    hi, mid, lo = _split3(a)
    return (jnp.dot(tri, hi, preferred_element_type=F32)
            + jnp.dot(tri, mid, preferred_element_type=F32)
            + jnp.dot(tri, lo, preferred_element_type=F32))


def _dot_nt(a, b):
    return lax.dot_general(a, b, (((1,), (1,)), ((), ())), preferred_element_type=F32)


def _dot_tn(a, b):
    return lax.dot_general(a, b, (((0,), (0,)), ((), ())), preferred_element_type=F32)


def _hgrn2_kernel(q_ref, zf_ref, zb_ref, v_ref, g_ref, lb_ref, gain_ref, o_ref, acc_ref, accb_ref):
    seq = q_ref.shape[1]
    ln = HG_CHUNK
    gr = HG_GROUP
    n_groups = seq // gr
    per_group = gr // ln
    lb = lb_ref[...]
    gain = gain_ref[...]
    row = lax.broadcasted_iota(I32, (gr, gr), 0)
    col = lax.broadcasted_iota(I32, (gr, gr), 1)
    chunk_lo = (row // ln) * ln
    chunk_hi = chunk_lo + ln
    lower_incl = (col <= row) & (col >= chunk_lo)
    upper_strict = (col > row) & (col < chunk_hi)
    tri_prefix = lower_incl.astype(BF16)
    tri_suffix = ((col >= row) & (col < chunk_hi)).astype(BF16)
    mid = ln // 2

    def per_chunk_rows(a, offset):
        return jnp.concatenate(
            [jnp.broadcast_to(a[j * ln + offset:j * ln + offset + 1, :], (ln, a.shape[1]))
             for j in range(per_group)], axis=0)

    rows = [slice(j * ln, (j + 1) * ln) for j in range(per_group)]

    def load(gi, z_ref, forward):
        sl = pl.ds(pl.multiple_of(gi * gr, gr), gr)
        f = lb + (1.0 - lb) * jax.nn.sigmoid(z_ref[0, sl, :])
        log_f = jnp.log(f)
        d = dict(sl=sl, forward=forward, q=q_ref[0, sl, :], v=v_ref[0, sl, :].astype(BF16), k=1.0 - f)
        if forward:
            d['cum'] = _tri_sum(tri_prefix, log_f)
        else:
            d['cum'] = _tri_sum(tri_suffix, log_f)
        return d

    def local_states(d):
        edge_off = ln - 1 if d['forward'] else 0
        kd = (d['k'] * jnp.exp(per_chunk_rows(d['cum'], edge_off) - d['cum'])).astype(BF16)
        d['local'] = [_dot_tn(d['v'][rs], kd[rs]) for rs in rows]

    def scores(d):
        ref_rows = per_chunk_rows(d['cum'], mid - 1 if d['forward'] else mid)
        qe = d['q'] * jnp.exp(jnp.minimum(d['cum'] - ref_rows, HG_EXP_CLAMP))
        ke = d['k'] * jnp.exp(jnp.minimum(ref_rows - d['cum'], HG_EXP_CLAMP))
        mask = lower_incl if d['forward'] else upper_strict
        d['scores'] = jnp.where(mask, _dot_nt(qe.astype(BF16), ke.astype(BF16)), 0.0)

    def inter(d, state_t):
        edge_off = ln - 1 if d['forward'] else 0
        qc = (d['q'] * jnp.exp(d['cum'])).astype(BF16)
        entering = [None] * per_group
        order = range(per_group) if d['forward'] else range(per_group - 1, -1, -1)
        for j in order:
            entering[j] = state_t.astype(BF16)
            edge = d['cum'][j * ln + edge_off:j * ln + edge_off + 1, :]
            state_t = state_t * jnp.exp(edge) + d['local'][j]
        d['inter'] = jnp.concatenate([_dot_nt(qc[rs], entering[j]) for j, rs in enumerate(rows)], axis=0)
        return state_t

    def body(i, states):
        st_f, st_b = states
        both = [load(i, zf_ref, True), load(n_groups - 1 - i, zb_ref, False)]
        for d in both:
            local_states(d)
        for d in both:
            scores(d)
        st_f = inter(both[0], st_f)
        st_b = inter(both[1], st_b)
        for d, ref in zip(both, (acc_ref, accb_ref)):
            ref[d['sl'], :] = jnp.dot(d['scores'].astype(BF16), d['v'], preferred_element_type=F32) + d['inter']
        return st_f, st_b

    zero = jnp.zeros((HG_HEAD_DIM, HG_HEAD_DIM), F32)
    lax.fori_loop(0, n_groups, body, (zero, zero))

    def finish(gi, _):
        sl = pl.ds(pl.multiple_of(gi * gr, gr), gr)
        o = acc_ref[sl, :] + accb_ref[sl, :]
        y = o * lax.rsqrt(jnp.mean(o * o, axis=-1, keepdims=True) + EPS) * gain
        g = g_ref[0, sl, :]
        o_ref[0, sl, :] = (y * (g * jax.nn.sigmoid(g))).astype(o_ref.dtype)
        return 0

    lax.fori_loop(0, n_groups, finish, 0)


def _hgrn2(proj3, lower_bound, hg_gain, width):
    bsz, seq, _ = proj3.shape
    heads = width // HG_HEAD_DIM
    dh = HG_HEAD_DIM

    def col(k):
        return pl.BlockSpec((1, seq, dh), lambda b, h: (b, 0, k * heads + h))

    return pl.pallas_call(
        _hgrn2_kernel,
        grid=(bsz, heads),
        in_specs=[col(0), col(1), col(2), col(3), col(4),
                  pl.BlockSpec((1, dh), lambda b, h: (0, h)),
                  pl.BlockSpec((1, dh), lambda b, h: (0, 0))],
        out_specs=pl.BlockSpec((1, seq, dh), lambda b, h: (b, 0, h)),
        out_shape=jax.ShapeDtypeStruct((bsz, seq, width), BF16),
        scratch_shapes=[pltpu.VMEM((seq, dh), F32), pltpu.VMEM((seq, dh), F32)],
        compiler_params=_cparams(("arbitrary", "arbitrary")),
        name="hgrn2",
    )(proj3, proj3, proj3, proj3, proj3, lower_bound.reshape(1, width), hg_gain.reshape(1, dh))


def _cmul(ar, ai, br, bi):
    return ar * br - ai * bi, ar * bi + ai * br


def _s5c_kernel(u_ref, t_ref, w_ref, v_ref, a_ref, z_ref, uc_ref, e_ref, p_ref):
    ck = S5_CHUNK
    n_chunks = u_ref.shape[1] // ck
    slab = n_chunks // S5_ROW_SPLIT
    gpt = 2 * S5_TILE_PAIRS
    half_tok = ck // 2
    lane_grp = lax.broadcasted_iota(I32, (slab, LANES), 1) // S5_GROUP

    def token_rows(rq, tok):
        return pl.ds(rq * slab * ck + tok, slab, stride=ck)

    def roll_lanes(a, groups):
        shift = (groups * S5_GROUP) % LANES
        return pltpu.roll(a, shift, 1) if shift else a

    for half in range(2):
        for rq in range(S5_ROW_SPLIT):
            toks = [u_ref.at[0][token_rows(rq, half * half_tok + sl), :] for sl in range(half_tok)]
            for gl in range(gpt):
                acc = jnp.zeros((slab, LANES), F32)
                for sl in range(half_tok):
                    acc = jnp.where(lane_grp == sl, roll_lanes(toks[sl], sl - gl), acc)
                uc_ref[gl * 2 + half, rq * slab:(rq + 1) * slab, :] = acc

    seg = [pl.ds(k * LANES, LANES) for k in range(4)]

    def pair_inputs(pp):
        return jnp.concatenate([uc_ref[pp * 4 + k] for k in range(4)], axis=-1).astype(BF16)

    for pp in range(S5_TILE_PAIRS):
        e_ref[:, pp, :] = jnp.dot(pair_inputs(pp), w_ref[pp], preferred_element_type=F32)

    af_r, af_i, ab_r, ab_i = a_ref[0, 0], a_ref[0, 1], a_ref[0, 2], a_ref[0, 3]

    def step(m, carry):
        xr, xi, yr, yi = carry
        mb = n_chunks - 1 - m
        p_ref[m, :, seg[0]] = xr
        p_ref[m, :, seg[1]] = xi
        p_ref[mb, :, seg[2]] = yr
        p_ref[mb, :, seg[3]] = yi
        dr, di = _cmul(af_r, af_i, xr, xi)
        gr, gi = _cmul(ab_r, ab_i, yr, yi)
        return (dr + e_ref[m, :, seg[0]], di + e_ref[m, :, seg[1]],
                gr + e_ref[mb, :, seg[2]], gi + e_ref[mb, :, seg[3]])

    zero = jnp.zeros((S5_TILE_PAIRS, LANES), F32)
    lax.fori_loop(0, n_chunks, step, (zero, zero, zero, zero), unroll=S5_UNROLL)

    for pp in range(S5_TILE_PAIRS):
        y = (jnp.dot(pair_inputs(pp), t_ref[pp], preferred_element_type=F32)
             + jnp.dot(p_ref[:, pp, :].astype(BF16), v_ref[pp], preferred_element_type=F32))
        zt = jax.nn.gelu(y)
        for k in range(4):
            uc_ref[pp * 4 + k] = zt[:, k * LANES:(k + 1) * LANES]

    for half in range(2):
        for rq in range(S5_ROW_SPLIT):
            cols = [uc_ref[gl * 2 + half, rq * slab:(rq + 1) * slab, :] for gl in range(gpt)]
            for sl in range(half_tok):
                acc = jnp.zeros((slab, LANES), F32)
                for gl in range(gpt):
                    acc = jnp.where(lane_grp == gl, roll_lanes(cols[gl], gl - sl), acc)
                z_ref.at[0][token_rows(rq, half * half_tok + sl), :] = acc


_einsum_f32 = functools.partial(jnp.einsum, precision=lax.Precision.HIGHEST)


def _s5c_tables(lam_re, lam_im, log_dt, b_re, b_im, c_re, c_im, d_skip):
    groups, state = lam_re.shape[1], lam_re.shape[2]
    chans = b_re.shape[2]
    ck = S5_CHUNK
    pairs = groups // 2
    lag = jnp.arange(ck + 1, dtype=F32)[:, None, None]
    kern, w_parts, v_parts, a_parts = [], [], [], []
    for direction in (0, 1):
        lr, li = lam_re[direction].astype(F32), lam_im[direction].astype(F32)
        dt = jnp.exp(log_dt[direction].astype(F32))[:, None]
        mag = jnp.exp(lr * dt)
        abar_re = mag * jnp.cos(li * dt)
        abar_im = mag * jnp.sin(li * dt)
        den = lr * lr + li * li
        num_re = abar_re - 1.0
        coef_re = (num_re * lr + abar_im * li) / den
        coef_im = (abar_im * lr - num_re * li) / den
        bbar_re = coef_re[..., None] * b_re - coef_im[..., None] * b_im
        bbar_im = coef_re[..., None] * b_im + coef_im[..., None] * b_re
        pw_re = jnp.exp(lag * lr * dt) * jnp.cos(lag * li * dt)
        pw_im = jnp.exp(lag * lr * dt) * jnp.sin(lag * li * dt)
        ab_re = pw_re[..., None] * bbar_re - pw_im[..., None] * bbar_im
        ab_im = pw_re[..., None] * bbar_im + pw_im[..., None] * bbar_re
        cr, ci = c_re[direction].astype(F32), c_im[direction].astype(F32)
        kern.append(_einsum_f32('gcp,ngpd->ngcd', cr, ab_re[:ck]) - _einsum_f32('gcp,ngpd->ngcd', ci, ab_im[:ck]))
        order = jnp.arange(ck - 1, -1, -1) if direction == 0 else jnp.arange(ck)
        w_parts.append((ab_re[order], ab_im[order]))
        order = jnp.arange(1, ck + 1) if direction == 0 else jnp.arange(ck, 0, -1)
        a_r, a_i = pw_re[order], pw_im[order]
        v_from_re = jnp.einsum('gcp,jgp->gpjc', cr, a_r) - jnp.einsum('gcp,jgp->gpjc', ci, a_i)
        v_from_im = -(jnp.einsum('gcp,jgp->gpjc', cr, a_i) + jnp.einsum('gcp,jgp->gpjc', ci, a_r))
        v_parts.append((v_from_re, v_from_im))
        a_parts.append((pw_re[ck], pw_im[ck]))

    s_idx = jnp.arange(ck)[None, :, None]
    t_idx = jnp.arange(ck)[None, None, :]
    n_idx = jnp.arange(ck)[:, None, None]
    sel_f = (t_idx - s_idx == n_idx).astype(F32)
    sel_b = (s_idx - t_idx == n_idx).astype(F32)
    skip = jnp.eye(chans, dtype=F32)[None] * d_skip.reshape(groups, chans)[:, :, None]
    toep = (_einsum_f32('nst,ngcd->gsdtc', sel_f, kern[0]) + _einsum_f32('nst,ngcd->gsdtc', sel_b, kern[1])
            + jnp.einsum('st,gcd->gsdtc', jnp.eye(ck, dtype=F32), skip))
    toep = toep.reshape(pairs, 2, ck * chans, ck * chans)
    eye2 = jnp.eye(2, dtype=F32)
    t_mat = jnp.einsum('ab,paxy->paxby', eye2, toep).reshape(pairs, 2 * ck * chans, 2 * ck * chans)

    w_seg = [w_parts[0][0], w_parts[0][1], w_parts[1][0], w_parts[1][1]]
    w_stack = jnp.stack(w_seg, axis=0).transpose(2, 1, 4, 0, 3)
    w_stack = w_stack.reshape(pairs, 2, ck * chans, 4, state)
    w_mat = jnp.einsum('ab,gaxkq->gaxkbq', eye2, w_stack).reshape(pairs, 2 * ck * chans, 4 * 2 * state)

    v_seg = [v_parts[0][0], v_parts[0][1], v_parts[1][0], v_parts[1][1]]
    v_stack = jnp.stack(v_seg, axis=0).reshape(4, pairs, 2, state, ck * chans)
    v_mat = jnp.einsum('ab,kgaqy->gkaqby', eye2, v_stack).reshape(pairs, 4 * 2 * state, 2 * ck * chans)

    a_seg = jnp.stack([a_parts[0][0], a_parts[0][1], a_parts[1][0], a_parts[1][1]], axis=0)
    tp = S5_TILE_PAIRS
    a_tbl = a_seg.reshape(4, pairs // tp, tp, 2 * state).transpose(1, 0, 2, 3)
    return t_mat.astype(BF16), w_mat.astype(BF16), v_mat.astype(BF16), a_tbl


def _s5c(proj3, col0, width, tables):
    bsz, seq, _ = proj3.shape
    t_mat, w_mat, v_mat, a_tbl = tables
    tp = S5_TILE_PAIRS
    tiles = t_mat.shape[0] // tp
    cols = t_mat.shape[1]
    tc = width // tiles
    assert tc == LANES and cols == 4 * LANES
    n_chunks = seq // S5_CHUNK
    return pl.pallas_call(
        _s5c_kernel,
        grid=(tiles, bsz),
        in_specs=[
            pl.BlockSpec((1, seq, tc), lambda t, b: (b, 0, col0 // tc + t)),
            pl.BlockSpec((tp, cols, cols), lambda t, b: (t, 0, 0)),
            pl.BlockSpec((tp, cols, cols), lambda t, b: (t, 0, 0)),
            pl.BlockSpec((tp, cols, cols), lambda t, b: (t, 0, 0)),
            pl.BlockSpec((1, 4, tp, LANES), lambda t, b: (t, 0, 0, 0)),
        ],
        out_specs=pl.BlockSpec((1, seq, tc), lambda t, b: (b, 0, t)),
        out_shape=jax.ShapeDtypeStruct((bsz, seq, width), F32),
        scratch_shapes=[pltpu.VMEM((4 * tp, n_chunks, LANES), F32),
                        pltpu.VMEM((n_chunks, tp, cols), F32),
                        pltpu.VMEM((n_chunks, tp, cols), F32)],
        compiler_params=_cparams(("arbitrary", "arbitrary")),
        name="s5",
    )(proj3, t_mat, w_mat, v_mat, a_tbl)


def _retention_kernel(q_ref, k_ref, v_ref, g_ref, cos_ref, sin_ref, lg_ref, gain_ref,
                      o_ref, qs_ref, ks_ref, vs_ref, decay_ref):
    seq = q_ref.shape[1]
    half = RET_HEAD_DIM // 2
    tq = RET_Q_TILE
    cos = cos_ref[...]
    sin = sin_ref[...]

    @pl.when(pl.program_id(1) == 0)
    def _():
        lg_fwd = lg_ref[0, 0:1, :]
        lg_bwd = lg_ref[0, 1:2, :]

        def fill(i, _):
            sl = pl.ds(pl.multiple_of(i * tq, tq), tq)
            t_idx = lax.broadcasted_iota(I32, (tq, seq), 0) + i * tq
            s_idx = lax.broadcasted_iota(I32, (tq, seq), 1)
            rel = (t_idx - s_idx).astype(F32)
            decay_ref[sl, :] = jnp.exp(jnp.where(rel >= 0.0, lg_fwd * rel, -lg_bwd * rel))
            return 0

        lax.fori_loop(0, seq // tq, fill, 0)

    def rot(t_ref, scale):
        t1 = t_ref[0, :, :half]
        t2 = t_ref[0, :, half:]
        return jnp.concatenate([(t1 * cos - t2 * sin) * scale, (t1 * sin + t2 * cos) * scale], axis=-1)

    qs_ref[...] = rot(q_ref, 1.0).astype(BF16)
    ks_ref[...] = rot(k_ref, RET_HEAD_DIM ** -0.5).astype(BF16)
    vs_ref[...] = v_ref[0].astype(BF16)
    gain = gain_ref[...]

    def q_tile(i, _):
        sl = pl.ds(pl.multiple_of(i * tq, tq), tq)
        scores = _dot_nt(qs_ref[sl, :], ks_ref[...])
        p = (scores * decay_ref[sl, :]).astype(BF16)
        o = jnp.dot(p, vs_ref[...], preferred_element_type=F32)
        y = o * lax.rsqrt(jnp.mean(o * o, axis=-1, keepdims=True) + EPS) * gain
        g = g_ref[0, sl, :]
        o_ref[0, sl, :] = (y * (g * jax.nn.sigmoid(g))).astype(o_ref.dtype)
        return 0

    lax.fori_loop(0, seq // tq, q_tile, 0)


def _retention(proj3, ret_gain, width):
    bsz, seq, _ = proj3.shape
    dh = RET_HEAD_DIM
    heads = width // dh
    inv_freq = ROPE_BASE ** (-jnp.arange(0, dh, 2, dtype=F32) / dh)
    ang = jnp.arange(seq, dtype=F32)[:, None] * inv_freq[None, :]
    cos, sin = jnp.cos(ang), jnp.sin(ang)
    log_gamma = jnp.log1p(-jnp.exp2(-5.0 - jnp.arange(heads, dtype=F32)))
    lg = jnp.stack([log_gamma, log_gamma[::-1]], axis=1)
    lg = jnp.broadcast_to(lg[:, :, None], (heads, 2, seq))

    def col(k):
        return pl.BlockSpec((1, seq, dh), lambda h, b: (b, 0, k * heads + h))

    return pl.pallas_call(
        _retention_kernel,
        grid=(heads, bsz),
        in_specs=[col(0), col(1), col(2), col(3),
                  pl.BlockSpec((seq, dh // 2), lambda h, b: (0, 0)),
                  pl.BlockSpec((seq, dh // 2), lambda h, b: (0, 0)),
                  pl.BlockSpec((1, 2, seq), lambda h, b: (h, 0, 0)),
                  pl.BlockSpec((1, dh), lambda h, b: (0, h))],
        out_specs=pl.BlockSpec((1, seq, dh), lambda h, b: (b, 0, h)),
        out_shape=jax.ShapeDtypeStruct((bsz, seq, width), BF16),
        scratch_shapes=[pltpu.VMEM((seq, dh), BF16)] * 3 + [pltpu.VMEM((seq, seq), F32)],
        compiler_params=_cparams(("arbitrary", "arbitrary")),
        name="retention",
    )(proj3, proj3, proj3, proj3, cos, sin, lg, ret_gain.reshape(1, width))


def _fnet_kernel(x_ref, cc_ref, sc_ref, cs_ref, ss_ref, o_ref, a1_ref, a2_ref):
    @pl.when(pl.program_id(1) == 0)
    def _():
        x = x_ref[0].astype(BF16)
        a1_ref[...] = jnp.dot(x, cc_ref[...], preferred_element_type=F32).astype(BF16)
        a2_ref[...] = jnp.dot(x, sc_ref[...], preferred_element_type=F32).astype(BF16)

    y = (jnp.dot(cs_ref[...], a1_ref[...], preferred_element_type=F32)
         - jnp.dot(ss_ref[...], a2_ref[...], preferred_element_type=F32))
    o_ref[0] = y.astype(o_ref.dtype)


def _dft_mats(n):
    idx = jnp.arange(n, dtype=I32)
    ang = (2.0 * math.pi / n) * ((idx[:, None] * idx[None, :]) % n).astype(F32)
    scale = n ** -0.5
    return jnp.cos(ang) * scale, jnp.sin(ang) * scale


def _fnet(proj3, col0, width):
    bsz, seq, _ = proj3.shape
    gw = width // FNET_GROUPS
    cs, ss = _dft_mats(seq)
    cg, sg = _dft_mats(gw)
    eye = jnp.eye(FNET_GROUPS, dtype=F32)
    cc = jnp.kron(eye, cg)
    sc = jnp.kron(eye, sg)
    tr = FNET_ROW_TILE
    return pl.pallas_call(
        _fnet_kernel,
        grid=(bsz, seq // tr),
        in_specs=[
            pl.BlockSpec((1, seq, width), lambda b, i: (b, 0, col0 // width)),
            pl.BlockSpec((width, width), lambda b, i: (0, 0)),
            pl.BlockSpec((width, width), lambda b, i: (0, 0)),
            pl.BlockSpec((tr, seq), lambda b, i: (i, 0)),
            pl.BlockSpec((tr, seq), lambda b, i: (i, 0)),
        ],
        out_specs=pl.BlockSpec((1, tr, width), lambda b, i: (b, i, 0)),
        out_shape=jax.ShapeDtypeStruct((bsz, seq, width), BF16),
        scratch_shapes=[pltpu.VMEM((seq, width), BF16)] * 2,
        compiler_params=_cparams(("arbitrary", "arbitrary")),
        name="fnet",
    )(proj3, cc.astype(BF16), sc.astype(BF16), cs.astype(BF16), ss.astype(BF16))


HIGH_HALF = -65536


def _slab_rows(s, rows, per_row):
    return pl.ds(s, rows, stride=per_row)


def _pack_rows(h, slab_ref):
    rows, d = h.shape
    half = d // 2
    per_row = half // LANES
    for s in range(per_row):
        lo = h[:, s * LANES:(s + 1) * LANES].astype(BF16).astype(F32)
        hi = h[:, half + s * LANES:half + (s + 1) * LANES].astype(BF16).astype(F32)
        lo_bits = lax.shift_right_logical(lax.bitcast_convert_type(lo, I32), jnp.int32(16))
        hi_bits = lax.bitcast_convert_type(hi, I32) & jnp.int32(HIGH_HALF)
        slab_ref[_slab_rows(s, rows, per_row), :] = hi_bits | lo_bits


def _unpack_rows(slab_ref, per_row):
    rows = slab_ref.shape[0] // per_row
    los, his = [], []
    for s in range(per_row):
        w = slab_ref[_slab_rows(s, rows, per_row), :]
        los.append(lax.bitcast_convert_type(lax.shift_left(w, jnp.int32(16)), F32))
        his.append(lax.bitcast_convert_type(w & jnp.int32(HIGH_HALF), F32))
    return jnp.concatenate(los, axis=-1).astype(BF16), jnp.concatenate(his, axis=-1).astype(BF16)


def _outproj_kernel(*refs, glu):
    if glu:
        (x_ref, a_ref, b_ref, gw_ref, gb_ref, wa_ref, wb_ref, gm_ref, gain_ref, sh_ref, sc_ref,
         rwh_ref, rwl_ref, rb_ref, x1_ref, h_ref, idx_ref, gate_ref, cnt_ref, base_ref) = refs
    else:
        (x_ref, a_ref, b_ref, wa_ref, wb_ref, gm_ref, gain_ref, sh_ref, sc_ref,
         rwh_ref, rwl_ref, rb_ref, x1_ref, h_ref, idx_ref, gate_ref, cnt_ref, base_ref) = refs

    @pl.when(pl.program_id(0) == 0)
    def _():
        base_ref[...] = jnp.zeros(base_ref.shape, F32)

    bm = b_ref[...]
    if glu:
        gl = jnp.dot(bm.astype(BF16), gw_ref[...], preferred_element_type=F32) + gb_ref[...]
        bm = (bm.astype(F32) * jax.nn.sigmoid(gl)).astype(BF16)
    y = (jnp.dot(a_ref[...], wa_ref[...], preferred_element_type=F32)
         + jnp.dot(bm, wb_ref[...], preferred_element_type=F32))
    x1 = x_ref[...] + gm_ref[0] * y
    x1_ref[...] = x1
    h = _norm_modulate(x1, gain_ref[...], sh_ref[0], sc_ref[0])
    _pack_rows(h, h_ref)

    h_hi = h.astype(BF16)
    h_lo = (h - h_hi.astype(F32)).astype(BF16)
    logits = (jnp.dot(h_hi, rwh_ref[...], preferred_element_type=F32)
              + jnp.dot(h_lo, rwh_ref[...], preferred_element_type=F32)
              + jnp.dot(h_hi, rwl_ref[...], preferred_element_type=F32)
              + rb_ref[...])
    lane = lax.broadcasted_iota(I32, logits.shape, 1)
    vals, idxs = [], []
    for _ in range(TOP_K):
        m = jnp.max(logits, axis=-1, keepdims=True)
        ik = jnp.min(jnp.where(logits == m, lane, ROUTER_PAD), axis=-1, keepdims=True)
        vals.append(m)
        idxs.append(ik)
        logits = jnp.where(lane == ik, -jnp.inf, logits)
    exps = [jnp.exp(v - vals[0]) for v in vals]
    denom = exps[0] + exps[1] + exps[2] + exps[3]

    tm = lane.shape[0]
    onehot = jnp.zeros(lane.shape, F32)
    for k in range(TOP_K):
        onehot = onehot + (lane == idxs[k]).astype(F32)
    before = (lax.broadcasted_iota(I32, (tm, tm), 1) < lax.broadcasted_iota(I32, (tm, tm), 0)).astype(BF16)
    count = jnp.dot(before, onehot.astype(BF16), preferred_element_type=F32) + base_ref[...]
    new_base = base_ref[...] + jnp.sum(onehot, axis=0, keepdims=True)
    base_ref[...] = new_base
    cnt_ref[...] = new_base.astype(I32)

    idx_out = jnp.zeros(lane.shape, I32)
    gate_out = jnp.zeros(lane.shape, F32)
    for k in range(TOP_K):
        rank = jnp.sum(jnp.where(lane == idxs[k], count, 0.0), axis=-1, keepdims=True).astype(I32)
        idx_out = jnp.where(lane == k, idxs[k], idx_out)
        idx_out = jnp.where(lane == TOP_K + k, rank, idx_out)
        gate_out = jnp.where(lane == k, exps[k] / denom, gate_out)
    idx_ref[...] = idx_out
    gate_ref[...] = gate_out


def _out_proj(x2d, mix_a, mix_b, w_out, g_mix, gain, shift, scale, router_w, router_b, seq,
              glu_w=None, glu_b=None):
    t, d = x2d.shape
    wa_rows = mix_a.shape[1]
    wb_rows = mix_b.shape[1]
    tm = 256
    per_seq = seq // tm
    n_exp = router_w.shape[1]
    rw = jnp.zeros((d, ROUTER_PAD), F32).at[:, :n_exp].set(router_w)
    rw_hi = rw.astype(BF16)
    rw_lo = (rw - rw_hi.astype(F32)).astype(BF16)
    rb = jnp.full((1, ROUTER_PAD), NEG_BIG, F32).at[0, :n_exp].set(router_b)
    w_bf = w_out.astype(BF16)
    glu = glu_w is not None

    def rows(width):
        return pl.BlockSpec((tm, width), lambda i: (i, 0))

    def full(r, c):
        return pl.BlockSpec((r, c), lambda i: (0, 0))

    def per_batch():
        return pl.BlockSpec((1, 1, d), lambda i: (i // per_seq, 0, 0))

    in_specs = [rows(d), rows(wa_rows), rows(wb_rows)]
    args = [x2d, mix_a, mix_b]
    if glu:
        in_specs += [full(wb_rows, wb_rows), full(1, wb_rows)]
        args += [glu_w.astype(BF16), glu_b.reshape(1, wb_rows)]
    in_specs += [pl.BlockSpec((wa_rows, d), lambda i: (0, 0)),
                 pl.BlockSpec((wb_rows, d), lambda i: (wa_rows // wb_rows, 0)),
                 per_batch(), full(1, d), per_batch(), per_batch(),
                 full(d, ROUTER_PAD), full(d, ROUTER_PAD), full(1, ROUTER_PAD)]
    args += [w_bf, w_bf, g_mix, gain.reshape(1, d), shift, scale, rw_hi, rw_lo, rb]
    return pl.pallas_call(
        functools.partial(_outproj_kernel, glu=glu),
        grid=(t // tm,),
        in_specs=in_specs,
        out_specs=[rows(d), pl.BlockSpec((tm * (d // (2 * LANES)), LANES), lambda i: (i, 0)),
                   rows(ROUTER_PAD), rows(ROUTER_PAD), full(1, ROUTER_PAD)],
        out_shape=[jax.ShapeDtypeStruct((t, d), F32), jax.ShapeDtypeStruct((t * (d // (2 * LANES)), LANES), I32),
                   jax.ShapeDtypeStruct((t, ROUTER_PAD), I32), jax.ShapeDtypeStruct((t, ROUTER_PAD), F32),
                   jax.ShapeDtypeStruct((1, ROUTER_PAD), I32)],
        scratch_shapes=[pltpu.VMEM((1, ROUTER_PAD), F32)],
        compiler_params=_cparams(("arbitrary",)),
        name="out_proj_glu" if glu else "out_proj",
    )(*args)


def _routing_tables(top_idx, rank, counts, n_tok):
    n_assign = n_tok * TOP_K
    r = MOE_ROWS
    n_rb = n_assign // r + N_EXPERTS
    experts = jnp.arange(N_EXPERTS, dtype=I32)
    nblk = (counts + r - 1) // r
    blk_end = jnp.cumsum(nblk)
    blk_start = blk_end - nblk
    row_start = jnp.sum(jnp.where(top_idx[:, :, None] == experts[None, None, :],
                                  (blk_start * r)[None, None, :], 0), axis=-1)
    dest = (row_start + rank).reshape(-1).astype(I32)
    n_valid_blocks = blk_end[-1]
    rb = jnp.arange(n_rb, dtype=I32)
    rb_e = jnp.sum((blk_end[None, :] <= rb[:, None]).astype(I32), axis=1)
    rb_e = jnp.minimum(rb_e, N_EXPERTS - 1).astype(I32)
    tail = jnp.where(nblk > 0, blk_end - 1, -1)
    spare = n_valid_blocks + experts
    spare = jnp.where(spare < n_rb, spare, -1)
    zero_blocks = jnp.concatenate([tail, spare]).astype(I32)
    return dict(dest=dest, rb_e=rb_e, zero_blocks=zero_blocks,
                n_valid_blocks=n_valid_blocks.reshape(1).astype(I32),
                nblk=nblk.astype(I32), blk_start=blk_start.astype(I32), n_rb=n_rb, n_slots=n_rb * r)


def _dispatch_kernel(dest_ref, zb_ref, h_ref, xs_hbm, zero_ref, zsem_ref, sem_ref):
    i = pl.program_id(0)
    td = DISPATCH_TOKENS
    sr = h_ref.shape[0] // td
    blk = MOE_ROWS * sr

    def slab(ref, row):
        return ref.at[pl.ds(pl.multiple_of(row * sr, sr), sr), :]

    @pl.when(i == 0)
    def _():
        zero_ref[...] = jnp.zeros(zero_ref.shape, zero_ref.dtype)

        def zero_copy(z):
            row0 = pl.multiple_of(zb_ref[z] * blk, blk)
            return pltpu.make_async_copy(zero_ref, xs_hbm.at[pl.ds(row0, blk), :], zsem_ref.at[0])

        def start(z, _):
            @pl.when(zb_ref[z] >= 0)
            def _():
                zero_copy(z).start()
            return 0

        def wait(z, _):
            @pl.when(zb_ref[z] >= 0)
            def _():
                zero_copy(z).wait()
            return 0

        lax.fori_loop(0, zb_ref.shape[0], start, 0)
        lax.fori_loop(0, zb_ref.shape[0], wait, 0)

    def group(g, _):
        for s in range(DMA_UNROLL):
            t = g * DMA_UNROLL + s
            for k in range(TOP_K):
                dst = dest_ref[(i * td + t) * TOP_K + k]
                pltpu.make_async_copy(slab(h_ref, t), slab(xs_hbm, dst), sem_ref.at[0]).start()
        return 0

    lax.fori_loop(0, td // DMA_UNROLL, group, 0)
    for _ in range(TOP_K):
        pltpu.make_async_copy(h_ref, xs_hbm.at[pl.ds(0, td * sr), :], sem_ref.at[0]).wait()


def _moe_dispatch(h_slabs, n_tok, rt):
    sr = h_slabs.shape[0] // n_tok
    td = DISPATCH_TOKENS
    grid_spec = pltpu.PrefetchScalarGridSpec(
        num_scalar_prefetch=2,
        grid=(n_tok // td,),
        in_specs=[pl.BlockSpec((td * sr, LANES), lambda i, dst, zb: (i, 0))],
        out_specs=pl.BlockSpec(memory_space=pl.ANY),
        scratch_shapes=[pltpu.VMEM((MOE_ROWS * sr, LANES), h_slabs.dtype), pltpu.SemaphoreType.DMA((1,)),
                        pltpu.SemaphoreType.DMA((1,))],
    )
    return pl.pallas_call(
        _dispatch_kernel,
        grid_spec=grid_spec,
        out_shape=jax.ShapeDtypeStruct((rt['n_slots'] * sr, LANES), h_slabs.dtype),
        compiler_params=_cparams(("arbitrary",)),
        name="moe_dispatch",
    )(rt['dest'], rt['zero_blocks'], h_slabs)


def _cast_rows(src_ref, dst_ref, chunk=256):
    chunk = min(chunk, dst_ref.shape[0])
    n = dst_ref.shape[0] // chunk

    def body(c, _):
        sl = pl.ds(pl.multiple_of(c * chunk, chunk), chunk)
        dst_ref[sl, :] = src_ref[sl, :].astype(dst_ref.dtype)
        return 0

    lax.fori_loop(0, n, body, 0)


def _stagger_tables(group, cols, parts):
    n = group.shape[0]
    idx = jnp.arange(n, dtype=I32)
    is_first = jnp.concatenate([jnp.ones((1,), bool), group[1:] != group[:-1]])
    first_cur = lax.cummax(jnp.where(is_first, idx, 0))
    nxt = jnp.where(is_first, idx, n)
    first_next = jnp.concatenate([lax.cummin(nxt[::-1])[::-1][1:], jnp.full((1,), n, I32)])
    out = [[] for _ in cols]
    for p in range(parts):
        switch = (idx >= jnp.maximum(first_next - p, first_cur + 1)) & (first_next < n)
        eff = jnp.where(switch, first_next, idx)
        for k, c in enumerate(cols):
            out[k].append(c[eff])
    return [jnp.stack(o) for o in out]


def _moe_up_kernel(rb_ref, j_ref, e_ref, first_ref, valid_ref, pe_ref, pj_ref, x_ref, *refs):
    parts = MOE_W_PARTS
    wg_refs = refs[:parts]
    wu_refs = refs[parts:2 * parts]
    bg_ref, bu_ref, o_ref, wgs_ref, wus_ref = refs[2 * parts:]
    i = pl.program_id(0)
    rows = wgs_ref.shape[0] // parts

    @pl.when(first_ref[i] == 1)
    def _():
        for p in range(parts):
            _cast_rows(wg_refs[p].at[0, 0], wgs_ref.at[pl.ds(p * rows, rows)])
            _cast_rows(wu_refs[p].at[0, 0], wus_ref.at[pl.ds(p * rows, rows)])

    @pl.when(valid_ref[i] == 1)
    def _():
        x_lo, x_hi = _unpack_rows(x_ref, x_ref.shape[0] // MOE_ROWS)
        half = wgs_ref.shape[0] // 2

        def proj(ws_ref, b_ref):
            return (jnp.dot(x_lo, ws_ref[:half, :], preferred_element_type=F32)
                    + jnp.dot(x_hi, ws_ref[half:, :], preferred_element_type=F32) + b_ref[0])

        gate = proj(wgs_ref, bg_ref)
        up = proj(wus_ref, bu_ref)
        gate = jnp.minimum(gate, SWIGLU_LIMIT)
        up = jnp.clip(up, -SWIGLU_LIMIT, SWIGLU_LIMIT)
        act = (up + 1.0) * gate * jax.nn.sigmoid(SWIGLU_ALPHA * gate)
        o_ref[...] = act.astype(o_ref.dtype)

    @pl.when(valid_ref[i] == 0)
    def _():
        o_ref[...] = jnp.zeros(o_ref.shape, o_ref.dtype)


def _moe_up(x_sorted, w_gu, b_gu, layer, rt):
    _, n_exp, d, two_f = w_gu.shape
    n_slots = rt['n_slots']
    sr = x_sorted.shape[0] // n_slots
    f = two_f // 2
    r, tn = MOE_ROWS, MOE_UP_TN
    nj = f // tn
    n_rb = rt['n_rb']
    n_items = n_rb * nj
    rb = jnp.arange(n_rb, dtype=I32)
    e_of = rt['rb_e']
    nvb = rt['n_valid_blocks'][0]
    nblk = rt['nblk'].at[n_exp - 1].add(n_rb - nvb)
    q = rb - rt['blk_start'][e_of]
    pos = (nj * rt['blk_start'][e_of][:, None]
           + jnp.arange(nj, dtype=I32)[None, :] * nblk[e_of][:, None] + q[:, None]).reshape(-1)
    rb2 = jnp.broadcast_to(rb[:, None], (n_rb, nj)).reshape(-1)
    j2 = jnp.broadcast_to(jnp.arange(nj, dtype=I32)[None, :], (n_rb, nj)).reshape(-1)
    it_rb = jnp.zeros((n_items,), I32).at[pos].set(rb2)
    it_j = jnp.zeros((n_items,), I32).at[pos].set(j2)
    it_valid = (it_rb < nvb).astype(I32)
    it_e = e_of[it_rb]
    prev_e = jnp.concatenate([jnp.full((1,), -1, I32), it_e[:-1]])
    prev_j = jnp.concatenate([jnp.full((1,), -1, I32), it_j[:-1]])
    is_first = (it_e != prev_e) | (it_j != prev_j)
    it_first = is_first.astype(I32)
    parts = MOE_W_PARTS
    part_e, part_j = _stagger_tables(jnp.cumsum(it_first), [it_e, it_j], parts)

    def w_spec(p, col0):
        return pl.BlockSpec((1, 1, d // parts, tn),
                            lambda i, rbt, jt, et, ft, vt, pe, pj: (layer, pe[p, i], p, col0 + pj[p, i]))

    grid_spec = pltpu.PrefetchScalarGridSpec(
        num_scalar_prefetch=7,
        grid=(n_items,),
        in_specs=([pl.BlockSpec((r * sr, LANES), lambda i, rbt, jt, et, ft, vt, pe, pj: (rbt[i], 0))]
                  + [w_spec(p, 0) for p in range(parts)] + [w_spec(p, nj) for p in range(parts)]
                  + [pl.BlockSpec((1, 1, tn), lambda i, rbt, jt, et, ft, vt, pe, pj: (et[i], 0, jt[i])),
                     pl.BlockSpec((1, 1, tn), lambda i, rbt, jt, et, ft, vt, pe, pj: (et[i], 0, nj + jt[i]))]),
        out_specs=pl.BlockSpec((r, tn), lambda i, rbt, jt, et, ft, vt, pe, pj: (rbt[i], jt[i])),
        scratch_shapes=[pltpu.VMEM((d, tn), BF16), pltpu.VMEM((d, tn), BF16)],
    )
    b3 = b_gu.reshape(n_exp, 1, two_f)
    return pl.pallas_call(
        _moe_up_kernel,
        grid_spec=grid_spec,
        out_shape=jax.ShapeDtypeStruct((n_slots, f), BF16),
        compiler_params=_cparams(("arbitrary",)),
        name="moe_up",
    )(it_rb, it_j, it_e, it_first, it_valid, part_e, part_j, x_sorted,
      *([w_gu] * (2 * parts)), b3, b3)


def _moe_dn_kernel(e_ref, nvb_ref, pe_ref, a_ref, *refs):
    parts = MOE_W_PARTS
    w_refs = refs[:parts]
    b_ref, o_ref, ws_ref = refs[parts:]
    i = pl.program_id(0)
    valid = i < nvb_ref[0]
    e = e_ref[i]
    prev_e = e_ref[jnp.maximum(i - 1, 0)]
    rows = ws_ref.shape[0] // parts

    @pl.when(valid & ((i == 0) | (e != prev_e)))
    def _():
        for p in range(parts):
            _cast_rows(w_refs[p].at[0, 0], ws_ref.at[pl.ds(p * rows, rows)])

    @pl.when(valid)
    def _():
        o_ref[...] = jnp.dot(a_ref[...], ws_ref[...], preferred_element_type=F32) + b_ref[0]

    @pl.when(jnp.logical_not(valid))
    def _():
        o_ref[...] = jnp.zeros(o_ref.shape, o_ref.dtype)


def _moe_dn(act, w_dn, b_dn, layer, rt):
    n_slots, f = act.shape
    _, n_exp, _, d = w_dn.shape
    r = MOE_ROWS
    parts = MOE_W_PARTS
    (part_e,) = _stagger_tables(rt['rb_e'], [rt['rb_e']], parts)

    def w_spec(p):
        return pl.BlockSpec((1, 1, f // parts, d), lambda i, et, nvb, pe: (layer, pe[p, i], p, 0))

    grid_spec = pltpu.PrefetchScalarGridSpec(
        num_scalar_prefetch=3,
        grid=(rt['n_rb'],),
        in_specs=([pl.BlockSpec((r, f), lambda i, et, nvb, pe: (jnp.minimum(i, nvb[0] - 1), 0))]
                  + [w_spec(p) for p in range(parts)]
                  + [pl.BlockSpec((1, 1, d), lambda i, et, nvb, pe: (et[i], 0, 0))]),
        out_specs=pl.BlockSpec((r, d), lambda i, et, nvb, pe: (i, 0)),
        scratch_shapes=[pltpu.VMEM((f, d), BF16)],
    )
    return pl.pallas_call(
        _moe_dn_kernel,
        grid_spec=grid_spec,
        out_shape=jax.ShapeDtypeStruct((n_slots, d), F32),
        compiler_params=_cparams(("arbitrary",)),
        name="moe_dn",
    )(rt['rb_e'], rt['n_valid_blocks'], part_e, act, *([w_dn] * parts), b_dn.reshape(n_exp, 1, d))


def _combine_kernel(dest_ref, x_ref, gate_ref, g_ref, fg_ref, y_hbm, o_ref, buf_ref, sem_ref, *, final):
    i = pl.program_id(0)
    n_steps = pl.num_programs(0)
    tc = COMBINE_TOKENS

    def issue(step, slot):
        def group(g, _):
            for s in range(DMA_UNROLL):
                t = g * DMA_UNROLL + s
                for k in range(TOP_K):
                    src = dest_ref[(step * tc + t) * TOP_K + k]
                    pltpu.make_async_copy(y_hbm.at[pl.ds(src, 1), :], buf_ref.at[slot, k, pl.ds(t, 1), :],
                                          sem_ref.at[slot]).start()
            return 0

        lax.fori_loop(0, tc // DMA_UNROLL, group, 0)

    @pl.when(i == 0)
    def _():
        issue(0, 0)

    @pl.when(i + 1 < n_steps)
    def _():
        issue(i + 1, (i + 1) % 2)

    slot = i % 2
    for k in range(TOP_K):
        pltpu.make_async_copy(y_hbm.at[pl.ds(0, tc), :], buf_ref.at[slot, k], sem_ref.at[slot]).wait()
    gates = gate_ref[...]
    y = gates[:, 0:1] * buf_ref[slot, 0]
    for k in range(1, TOP_K):
        y = y + gates[:, k:k + 1] * buf_ref[slot, k]
    x2 = x_ref[...] + g_ref[0] * y
    if final:
        x2 = x2 * lax.rsqrt(jnp.mean(x2 * x2, axis=-1, keepdims=True) + EPS) * fg_ref[...]
    o_ref[...] = x2


def _combine(x1, y_sorted, gates, rt, g_ffn, final_gain, seq, final):
    t, d = x1.shape
    tc = COMBINE_TOKENS
    per_seq = seq // tc
    grid_spec = pltpu.PrefetchScalarGridSpec(
        num_scalar_prefetch=1,
        grid=(t // tc,),
        in_specs=[pl.BlockSpec((tc, d), lambda i, dst: (i, 0)),
                  pl.BlockSpec((tc, ROUTER_PAD), lambda i, dst: (i, 0)),
                  pl.BlockSpec((1, 1, d), lambda i, dst: (i // per_seq, 0, 0)),
                  pl.BlockSpec((1, d), lambda i, dst: (0, 0)),
                  pl.BlockSpec(memory_space=pl.ANY)],
        out_specs=pl.BlockSpec((tc, d), lambda i, dst: (i, 0)),
        scratch_shapes=[pltpu.VMEM((2, TOP_K, tc, d), F32), pltpu.SemaphoreType.DMA((2,))],
    )
    return pl.pallas_call(
        functools.partial(_combine_kernel, final=final),
        grid_spec=grid_spec,
        out_shape=jax.ShapeDtypeStruct((t, d), F32),
        compiler_params=_cparams(("arbitrary",)),
        name="combine_final" if final else "combine",
    )(rt['dest'], x1, gates, g_ffn, final_gain.reshape(1, d), y_sorted)


def _moe(x1, h, idx, gates, counts, w_gu, b_gu, w_dn, b_dn, layer, g_ffn, final_gain, seq, final):
    n_tok = x1.shape[0]
    rt = _routing_tables(idx[:, :TOP_K], idx[:, TOP_K:2 * TOP_K], counts[0, :N_EXPERTS], n_tok)
    x_sorted = _moe_dispatch(h, n_tok, rt)
    act = _moe_up(x_sorted, w_gu, b_gu, layer, rt)
    y_sorted = _moe_dn(act, w_dn, b_dn, layer, rt)
    return _combine(x1, y_sorted, gates, rt, g_ffn, final_gain, seq, final)


def kernel(x, c, ada_w, ada_b, norm_mix_gain, norm_ffn_gain, ab_w_in, ab_w_out, hg_lb_logits, hg_norm_gain, s5_lam_re, s5_lam_im, s5_log_dt, s5_b_re, s5_b_im, s5_c_re, s5_c_im, s5_d, s5_glu_w, s5_glu_b, cd_w_in, cd_w_out, ret_norm_gain, router_w, router_b, moe_w_gu, moe_b_gu, moe_w_dn, moe_b_dn, final_gain):
    bsz, seq, d = x.shape
    depth = ada_w.shape[0]
    n_tok = bsz * seq
    hg_width = hg_lb_logits.shape[1]
    s5_width = s5_glu_w.shape[1]
    ret_width = ret_norm_gain.shape[1]
    fnet_width = cd_w_out.shape[1] - ret_width

    lower_bounds = jnp.cumsum(jax.nn.softmax(hg_lb_logits.astype(F32), axis=0), axis=0)
    mod = _ada_mod(c, ada_w, ada_b)
    xr = x.reshape(n_tok, d)
    for layer in range(depth):
        sh_mix, sc_mix, g_mix, sh_ffn, sc_ffn, g_ffn = (
            mod[layer, :, k * d:(k + 1) * d].reshape(bsz, 1, d) for k in range(6))
        j = layer // 2
        if layer % 2 == 0:
            proj = _in_proj(xr, norm_mix_gain[layer], sh_mix, sc_mix, ab_w_in[j].astype(BF16), seq)
            proj3 = proj.reshape(bsz, seq, proj.shape[1])
            mix_a = _hgrn2(proj3, lower_bounds[j], hg_norm_gain[j], hg_width)
            tables = _s5c_tables(s5_lam_re[j], s5_lam_im[j], s5_log_dt[j], s5_b_re[j], s5_b_im[j],
                                 s5_c_re[j], s5_c_im[j], s5_d[j])
            mix_b = _s5c(proj3, 5 * hg_width, s5_width, tables)
            x1, h, idx, gate, counts = _out_proj(
                xr, mix_a.reshape(n_tok, hg_width), mix_b.reshape(n_tok, s5_width), ab_w_out[j],
                g_mix, norm_ffn_gain[layer], sh_ffn, sc_ffn, router_w[layer], router_b[layer], seq,
                glu_w=s5_glu_w[j], glu_b=s5_glu_b[j])
        else:
            proj = _in_proj(xr, norm_mix_gain[layer], sh_mix, sc_mix, cd_w_in[j].astype(BF16), seq)
            proj3 = proj.reshape(bsz, seq, proj.shape[1])
            mix_a = _retention(proj3, ret_norm_gain[j], ret_width)
            mix_b = _fnet(proj3, 4 * ret_width, fnet_width)
            x1, h, idx, gate, counts = _out_proj(
                xr, mix_a.reshape(n_tok, ret_width), mix_b.reshape(n_tok, fnet_width), cd_w_out[j],
                g_mix, norm_ffn_gain[layer], sh_ffn, sc_ffn, router_w[layer], router_b[layer], seq)
        xr = _moe(x1, h, idx, gate, counts, moe_w_gu, moe_b_gu[layer], moe_w_dn, moe_b_dn[layer],
                  layer, g_ffn, final_gain, seq, final=(layer == depth - 1))
    return xr.reshape(bsz, seq, d)
```

```python
import functools
import math

import jax
import jax.numpy as jnp
from jax import lax
from jax.experimental import pallas as pl
from jax.experimental.pallas import tpu as pltpu

F32 = jnp.float32
BF16 = jnp.bfloat16
I32 = jnp.int32

EPS = 1e-6
LANES = 128
SUBLANES = 8
VMEM_LIMIT = 56 * 1024 * 1024

HG_HEAD_DIM = 128
HG_CHUNK = 64
HG_GROUP = 256
HG_EXP_CLAMP = 80.0

S5_GROUP = 16
S5_STATE = 64
S5_CHUNK = 16
S5_UNROLL = 8
S5_TILE_PAIRS = 4
S5_ROW_SPLIT = 4

RET_HEAD_DIM = 256
RET_Q_TILE = 256
ROPE_BASE = 10000.0

FNET_GROUPS = 4
FNET_ROW_TILE = 512

N_EXPERTS = 32
TOP_K = 4
SWIGLU_LIMIT = 7.0
SWIGLU_ALPHA = 1.702
MOE_ROWS = 256
MOE_UP_TN = 1024
MOE_W_PARTS = 4
DISPATCH_TOKENS = 512
COMBINE_TOKENS = 128
DMA_UNROLL = 8
ROUTER_PAD = LANES
NEG_BIG = -1e30


def _cparams(semantics):
    return pltpu.CompilerParams(dimension_semantics=semantics, vmem_limit_bytes=VMEM_LIMIT)


def _ada_kernel(c_ref, w_ref, b_ref, o_ref):
    c = c_ref[...]
    cond = c * jax.nn.sigmoid(c)
    o_ref[0] = jnp.dot(cond.astype(BF16), w_ref[0].astype(BF16),
                       preferred_element_type=F32) + b_ref[0]


def _ada_mod(c, ada_w, ada_b):
    depth, d, n = ada_w.shape
    bsz = c.shape[0]
    tn = 1024
    return pl.pallas_call(
        _ada_kernel,
        grid=(depth, n // tn),
        in_specs=[
            pl.BlockSpec((bsz, d), lambda l, j: (0, 0)),
            pl.BlockSpec((1, d, tn), lambda l, j: (l, 0, j)),
            pl.BlockSpec((1, 1, tn), lambda l, j: (l, 0, j)),
        ],
        out_specs=pl.BlockSpec((1, bsz, tn), lambda l, j: (l, 0, j)),
        out_shape=jax.ShapeDtypeStruct((depth, bsz, n), F32),
        compiler_params=_cparams(("arbitrary", "arbitrary")),
        name="ada_mod",
    )(c, ada_w, ada_b.reshape(depth, 1, n))


def _norm_modulate(x, gain, shift, scale):
    ms = jnp.mean(x * x, axis=-1, keepdims=True)
    y = x * lax.rsqrt(ms + EPS) * gain
    return y * (1.0 + scale) + shift


def _inproj_kernel(x_ref, gain_ref, sh_ref, sc_ref, w_ref, o_ref, h_ref):
    @pl.when(pl.program_id(1) == 0)
    def _():
        rows = 128

        def slab(c, _):
            sl = pl.ds(pl.multiple_of(c * rows, rows), rows)
            h_ref[sl, :] = _norm_modulate(x_ref[sl, :], gain_ref[...], sh_ref[0], sc_ref[0]).astype(BF16)
            return 0

        lax.fori_loop(0, h_ref.shape[0] // rows, slab, 0)

    o_ref[...] = jnp.dot(h_ref[...], w_ref[...], preferred_element_type=F32)


def _in_proj(x2d, gain, shift, scale, w_bf16, seq):
    t, d = x2d.shape
    n = w_bf16.shape[1]
    tm, tn = 1024, 512
    per_seq = seq // tm
    return pl.pallas_call(
        _inproj_kernel,
        grid=(t // tm, n // tn),
        in_specs=[
            pl.BlockSpec((tm, d), lambda i, j: (i, 0)),
            pl.BlockSpec((1, d), lambda i, j: (0, 0)),
            pl.BlockSpec((1, 1, d), lambda i, j: (i // per_seq, 0, 0)),
            pl.BlockSpec((1, 1, d), lambda i, j: (i // per_seq, 0, 0)),
            pl.BlockSpec((d, tn), lambda i, j: (0, j)),
        ],
        out_specs=pl.BlockSpec((tm, tn), lambda i, j: (i, j)),
        out_shape=jax.ShapeDtypeStruct((t, n), F32),
        scratch_shapes=[pltpu.VMEM((tm, d), BF16)],
        compiler_params=_cparams(("arbitrary", "arbitrary")),
        name="in_proj",
    )(x2d, gain.reshape(1, d), shift, scale, w_bf16)


def _split3(a):
    hi = a.astype(BF16)
    r1 = a - hi.astype(F32)
    mid = r1.astype(BF16)
    lo = (r1 - mid.astype(F32)).astype(BF16)
    return hi, mid, lo


def _tri_sum(tri, a):
    hi, mid, lo = _split3(a)
    return (jnp.dot(tri, hi, preferred_element_type=F32)
            + jnp.dot(tri, mid, preferred_element_type=F32)
            + jnp.dot(tri, lo, preferred_element_type=F32))


def _dot_nt(a, b):
    return lax.dot_general(a, b, (((1,), (1,)), ((), ())), preferred_element_type=F32)


def _dot_tn(a, b):
    return lax.dot_general(a, b, (((0,), (0,)), ((), ())), preferred_element_type=F32)


def _hgrn2_kernel(q_ref, zf_ref, zb_ref, v_ref, g_ref, lb_ref, gain_ref, o_ref, acc_ref, accb_ref):
    seq = q_ref.shape[1]
    ln = HG_CHUNK
    gr = HG_GROUP
    n_groups = seq // gr
    per_group = gr // ln
    lb = lb_ref[...]
    gain = gain_ref[...]
    row = lax.broadcasted_iota(I32, (gr, gr), 0)
    col = lax.broadcasted_iota(I32, (gr, gr), 1)
    chunk_lo = (row // ln) * ln
    chunk_hi = chunk_lo + ln
    lower_incl = (col <= row) & (col >= chunk_lo)
    upper_strict = (col > row) & (col < chunk_hi)
    tri_prefix = lower_incl.astype(BF16)
    tri_suffix = ((col >= row) & (col < chunk_hi)).astype(BF16)
    mid = ln // 2

    def per_chunk_rows(a, offset):
        return jnp.concatenate(
            [jnp.broadcast_to(a[j * ln + offset:j * ln + offset + 1, :], (ln, a.shape[1]))
             for j in range(per_group)], axis=0)

    rows = [slice(j * ln, (j + 1) * ln) for j in range(per_group)]

    def load(gi, z_ref, forward):
        sl = pl.ds(pl.multiple_of(gi * gr, gr), gr)
        f = lb + (1.0 - lb) * jax.nn.sigmoid(z_ref[0, sl, :])
        log_f = jnp.log(f)
        d = dict(sl=sl, forward=forward, q=q_ref[0, sl, :], v=v_ref[0, sl, :].astype(BF16), k=1.0 - f)
        if forward:
            d['cum'] = _tri_sum(tri_prefix, log_f)
        else:
            d['cum'] = _tri_sum(tri_suffix, log_f)
        return d

    def local_states(d):
        edge_off = ln - 1 if d['forward'] else 0
        kd = (d['k'] * jnp.exp(per_chunk_rows(d['cum'], edge_off) - d['cum'])).astype(BF16)
        d['local'] = [_dot_tn(d['v'][rs], kd[rs]) for rs in rows]

    def scores(d):
        ref_rows = per_chunk_rows(d['cum'], mid - 1 if d['forward'] else mid)
        qe = d['q'] * jnp.exp(jnp.minimum(d['cum'] - ref_rows, HG_EXP_CLAMP))
        ke = d['k'] * jnp.exp(jnp.minimum(ref_rows - d['cum'], HG_EXP_CLAMP))
        mask = lower_incl if d['forward'] else upper_strict
        d['scores'] = jnp.where(mask, _dot_nt(qe.astype(BF16), ke.astype(BF16)), 0.0)

    def inter(d, state_t):
        edge_off = ln - 1 if d['forward'] else 0
        qc = (d['q'] * jnp.exp(d['cum'])).astype(BF16)
        entering = [None] * per_group
        order = range(per_group) if d['forward'] else range(per_group - 1, -1, -1)
        for j in order:
            entering[j] = state_t.astype(BF16)
            edge = d['cum'][j * ln + edge_off:j * ln + edge_off + 1, :]
            state_t = state_t * jnp.exp(edge) + d['local'][j]
        d['inter'] = jnp.concatenate([_dot_nt(qc[rs], entering[j]) for j, rs in enumerate(rows)], axis=0)
        return state_t

    def body(i, states):
        st_f, st_b = states
        both = [load(i, zf_ref, True), load(n_groups - 1 - i, zb_ref, False)]
        for d in both:
            local_states(d)
        for d in both:
            scores(d)
        st_f = inter(both[0], st_f)
        st_b = inter(both[1], st_b)
        for d, ref in zip(both, (acc_ref, accb_ref)):
            ref[d['sl'], :] = jnp.dot(d['scores'].astype(BF16), d['v'], preferred_element_type=F32) + d['inter']
        return st_f, st_b

    zero = jnp.zeros((HG_HEAD_DIM, HG_HEAD_DIM), F32)
    lax.fori_loop(0, n_groups, body, (zero, zero))

    def finish(gi, _):
        sl = pl.ds(pl.multiple_of(gi * gr, gr), gr)
        o = acc_ref[sl, :] + accb_ref[sl, :]
        y = o * lax.rsqrt(jnp.mean(o * o, axis=-1, keepdims=True) + EPS) * gain
        g = g_ref[0, sl, :]
        o_ref[0, sl, :] = (y * (g * jax.nn.sigmoid(g))).astype(o_ref.dtype)
        return 0

    lax.fori_loop(0, n_groups, finish, 0)


def _hgrn2(proj3, lower_bound, hg_gain, width):
    bsz, seq, _ = proj3.shape
    heads = width // HG_HEAD_DIM
    dh = HG_HEAD_DIM

    def col(k):
        return pl.BlockSpec((1, seq, dh), lambda b, h: (b, 0, k * heads + h))

    return pl.pallas_call(
        _hgrn2_kernel,
        grid=(bsz, heads),
        in_specs=[col(0), col(1), col(2), col(3), col(4),
                  pl.BlockSpec((1, dh), lambda b, h: (0, h)),
                  pl.BlockSpec((1, dh), lambda b, h: (0, 0))],
        out_specs=pl.BlockSpec((1, seq, dh), lambda b, h: (b, 0, h)),
        out_shape=jax.ShapeDtypeStruct((bsz, seq, width), BF16),
        scratch_shapes=[pltpu.VMEM((seq, dh), F32), pltpu.VMEM((seq, dh), F32)],
        compiler_params=_cparams(("arbitrary", "arbitrary")),
        name="hgrn2",
    )(proj3, proj3, proj3, proj3, proj3, lower_bound.reshape(1, width), hg_gain.reshape(1, dh))


def _cmul(ar, ai, br, bi):
    return ar * br - ai * bi, ar * bi + ai * br


def _s5c_kernel(u_ref, t_ref, w_ref, v_ref, a_ref, z_ref, uc_ref, e_ref, p_ref):
    ck = S5_CHUNK
    n_chunks = u_ref.shape[1] // ck
    slab = n_chunks // S5_ROW_SPLIT
    gpt = 2 * S5_TILE_PAIRS
    half_tok = ck // 2
    lane_grp = lax.broadcasted_iota(I32, (slab, LANES), 1) // S5_GROUP

    def token_rows(rq, tok):
        return pl.ds(rq * slab * ck + tok, slab, stride=ck)

    def roll_lanes(a, groups):
        shift = (groups * S5_GROUP) % LANES
        return pltpu.roll(a, shift, 1) if shift else a

    for half in range(2):
        for rq in range(S5_ROW_SPLIT):
            toks = [u_ref.at[0][token_rows(rq, half * half_tok + sl), :] for sl in range(half_tok)]
            for gl in range(gpt):
                acc = jnp.zeros((slab, LANES), F32)
                for sl in range(half_tok):
                    acc = jnp.where(lane_grp == sl, roll_lanes(toks[sl], sl - gl), acc)
                uc_ref[gl * 2 + half, rq * slab:(rq + 1) * slab, :] = acc

    seg = [pl.ds(k * LANES, LANES) for k in range(4)]

    def pair_inputs(pp):
        return jnp.concatenate([uc_ref[pp * 4 + k] for k in range(4)], axis=-1).astype(BF16)

    for pp in range(S5_TILE_PAIRS):
        e_ref[:, pp, :] = jnp.dot(pair_inputs(pp), w_ref[pp], preferred_element_type=F32)

    af_r, af_i, ab_r, ab_i = a_ref[0, 0], a_ref[0, 1], a_ref[0, 2], a_ref[0, 3]

    def step(m, carry):
        xr, xi, yr, yi = carry
        mb = n_chunks - 1 - m
        p_ref[m, :, seg[0]] = xr
        p_ref[m, :, seg[1]] = xi
        p_ref[mb, :, seg[2]] = yr
        p_ref[mb, :, seg[3]] = yi
        dr, di = _cmul(af_r, af_i, xr, xi)
        gr, gi = _cmul(ab_r, ab_i, yr, yi)
        return (dr + e_ref[m, :, seg[0]], di + e_ref[m, :, seg[1]],
                gr + e_ref[mb, :, seg[2]], gi + e_ref[mb, :, seg[3]])

    zero = jnp.zeros((S5_TILE_PAIRS, LANES), F32)
    lax.fori_loop(0, n_chunks, step, (zero, zero, zero, zero), unroll=S5_UNROLL)

    for pp in range(S5_TILE_PAIRS):
        y = (jnp.dot(pair_inputs(pp), t_ref[pp], preferred_element_type=F32)
             + jnp.dot(p_ref[:, pp, :].astype(BF16), v_ref[pp], preferred_element_type=F32))
        zt = jax.nn.gelu(y)
        for k in range(4):
            uc_ref[pp * 4 + k] = zt[:, k * LANES:(k + 1) * LANES]

    for half in range(2):
        for rq in range(S5_ROW_SPLIT):
            cols = [uc_ref[gl * 2 + half, rq * slab:(rq + 1) * slab, :] for gl in range(gpt)]
            for sl in range(half_tok):
                acc = jnp.zeros((slab, LANES), F32)
                for gl in range(gpt):
                    acc = jnp.where(lane_grp == gl, roll_lanes(cols[gl], gl - sl), acc)
                z_ref.at[0][token_rows(rq, half * half_tok + sl), :] = acc


_einsum_f32 = functools.partial(jnp.einsum, precision=lax.Precision.HIGHEST)


def _s5c_tables(lam_re, lam_im, log_dt, b_re, b_im, c_re, c_im, d_skip):
    groups, state = lam_re.shape[1], lam_re.shape[2]
    chans = b_re.shape[2]
    ck = S5_CHUNK
    pairs = groups // 2
    lag = jnp.arange(ck + 1, dtype=F32)[:, None, None]
    kern, w_parts, v_parts, a_parts = [], [], [], []
    for direction in (0, 1):
        lr, li = lam_re[direction].astype(F32), lam_im[direction].astype(F32)
        dt = jnp.exp(log_dt[direction].astype(F32))[:, None]
        mag = jnp.exp(lr * dt)
        abar_re = mag * jnp.cos(li * dt)
        abar_im = mag * jnp.sin(li * dt)
        den = lr * lr + li * li
        num_re = abar_re - 1.0
        coef_re = (num_re * lr + abar_im * li) / den
        coef_im = (abar_im * lr - num_re * li) / den
        bbar_re = coef_re[..., None] * b_re - coef_im[..., None] * b_im
        bbar_im = coef_re[..., None] * b_im + coef_im[..., None] * b_re
        pw_re = jnp.exp(lag * lr * dt) * jnp.cos(lag * li * dt)
        pw_im = jnp.exp(lag * lr * dt) * jnp.sin(lag * li * dt)
        ab_re = pw_re[..., None] * bbar_re - pw_im[..., None] * bbar_im
        ab_im = pw_re[..., None] * bbar_im + pw_im[..., None] * bbar_re
        cr, ci = c_re[direction].astype(F32), c_im[direction].astype(F32)
        kern.append(_einsum_f32('gcp,ngpd->ngcd', cr, ab_re[:ck]) - _einsum_f32('gcp,ngpd->ngcd', ci, ab_im[:ck]))
        order = jnp.arange(ck - 1, -1, -1) if direction == 0 else jnp.arange(ck)
        w_parts.append((ab_re[order], ab_im[order]))
        order = jnp.arange(1, ck + 1) if direction == 0 else jnp.arange(ck, 0, -1)
        a_r, a_i = pw_re[order], pw_im[order]
        v_from_re = jnp.einsum('gcp,jgp->gpjc', cr, a_r) - jnp.einsum('gcp,jgp->gpjc', ci, a_i)
        v_from_im = -(jnp.einsum('gcp,jgp->gpjc', cr, a_i) + jnp.einsum('gcp,jgp->gpjc', ci, a_r))
        v_parts.append((v_from_re, v_from_im))
        a_parts.append((pw_re[ck], pw_im[ck]))

    s_idx = jnp.arange(ck)[None, :, None]
    t_idx = jnp.arange(ck)[None, None, :]
    n_idx = jnp.arange(ck)[:, None, None]
    sel_f = (t_idx - s_idx == n_idx).astype(F32)
    sel_b = (s_idx - t_idx == n_idx).astype(F32)
    skip = jnp.eye(chans, dtype=F32)[None] * d_skip.reshape(groups, chans)[:, :, None]
    toep = (_einsum_f32('nst,ngcd->gsdtc', sel_f, kern[0]) + _einsum_f32('nst,ngcd->gsdtc', sel_b, kern[1])
            + jnp.einsum('st,gcd->gsdtc', jnp.eye(ck, dtype=F32), skip))
    toep = toep.reshape(pairs, 2, ck * chans, ck * chans)
    eye2 = jnp.eye(2, dtype=F32)
    t_mat = jnp.einsum('ab,paxy->paxby', eye2, toep).reshape(pairs, 2 * ck * chans, 2 * ck * chans)

    w_seg = [w_parts[0][0], w_parts[0][1], w_parts[1][0], w_parts[1][1]]
    w_stack = jnp.stack(w_seg, axis=0).transpose(2, 1, 4, 0, 3)
    w_stack = w_stack.reshape(pairs, 2, ck * chans, 4, state)
    w_mat = jnp.einsum('ab,gaxkq->gaxkbq', eye2, w_stack).reshape(pairs, 2 * ck * chans, 4 * 2 * state)

    v_seg = [v_parts[0][0], v_parts[0][1], v_parts[1][0], v_parts[1][1]]
    v_stack = jnp.stack(v_seg, axis=0).reshape(4, pairs, 2, state, ck * chans)
    v_mat = jnp.einsum('ab,kgaqy->gkaqby', eye2, v_stack).reshape(pairs, 4 * 2 * state, 2 * ck * chans)

    a_seg = jnp.stack([a_parts[0][0], a_parts[0][1], a_parts[1][0], a_parts[1][1]], axis=0)
    tp = S5_TILE_PAIRS
    a_tbl = a_seg.reshape(4, pairs // tp, tp, 2 * state).transpose(1, 0, 2, 3)
    return t_mat.astype(BF16), w_mat.astype(BF16), v_mat.astype(BF16), a_tbl


def _s5c(proj3, col0, width, tables):
    bsz, seq, _ = proj3.shape
    t_mat, w_mat, v_mat, a_tbl = tables
    tp = S5_TILE_PAIRS
    tiles = t_mat.shape[0] // tp
    cols = t_mat.shape[1]
    tc = width // tiles
    assert tc == LANES and cols == 4 * LANES
    n_chunks = seq // S5_CHUNK
    return pl.pallas_call(
        _s5c_kernel,
        grid=(tiles, bsz),
        in_specs=[
            pl.BlockSpec((1, seq, tc), lambda t, b: (b, 0, col0 // tc + t)),
            pl.BlockSpec((tp, cols, cols), lambda t, b: (t, 0, 0)),
            pl.BlockSpec((tp, cols, cols), lambda t, b: (t, 0, 0)),
            pl.BlockSpec((tp, cols, cols), lambda t, b: (t, 0, 0)),
            pl.BlockSpec((1, 4, tp, LANES), lambda t, b: (t, 0, 0, 0)),
        ],
        out_specs=pl.BlockSpec((1, seq, tc), lambda t, b: (b, 0, t)),
        out_shape=jax.ShapeDtypeStruct((bsz, seq, width), F32),
        scratch_shapes=[pltpu.VMEM((4 * tp, n_chunks, LANES), F32),
                        pltpu.VMEM((n_chunks, tp, cols), F32),
                        pltpu.VMEM((n_chunks, tp, cols), F32)],
        compiler_params=_cparams(("arbitrary", "arbitrary")),
        name="s5",
    )(proj3, t_mat, w_mat, v_mat, a_tbl)


def _retention_kernel(q_ref, k_ref, v_ref, g_ref, cos_ref, sin_ref, lg_ref, gain_ref,
                      o_ref, qs_ref, ks_ref, vs_ref, decay_ref):
    seq = q_ref.shape[1]
    half = RET_HEAD_DIM // 2
    tq = RET_Q_TILE
    cos = cos_ref[...]
    sin = sin_ref[...]

    @pl.when(pl.program_id(1) == 0)
    def _():
        lg_fwd = lg_ref[0, 0:1, :]
        lg_bwd = lg_ref[0, 1:2, :]

        def fill(i, _):
            sl = pl.ds(pl.multiple_of(i * tq, tq), tq)
            t_idx = lax.broadcasted_iota(I32, (tq, seq), 0) + i * tq
            s_idx = lax.broadcasted_iota(I32, (tq, seq), 1)
            rel = (t_idx - s_idx).astype(F32)
            decay_ref[sl, :] = jnp.exp(jnp.where(rel >= 0.0, lg_fwd * rel, -lg_bwd * rel))
            return 0

        lax.fori_loop(0, seq // tq, fill, 0)

    def rot(t_ref, scale):
        t1 = t_ref[0, :, :half]
        t2 = t_ref[0, :, half:]
        return jnp.concatenate([(t1 * cos - t2 * sin) * scale, (t1 * sin + t2 * cos) * scale], axis=-1)

    qs_ref[...] = rot(q_ref, 1.0).astype(BF16)
    ks_ref[...] = rot(k_ref, RET_HEAD_DIM ** -0.5).astype(BF16)
    vs_ref[...] = v_ref[0].astype(BF16)
    gain = gain_ref[...]

    def q_tiles(i, _):
        sls = [pl.ds(pl.multiple_of((2 * i + a) * tq, tq), tq) for a in range(2)]
        scores = [_dot_nt(qs_ref[sl, :], ks_ref[...]) for sl in sls]
        ps = [(s * decay_ref[sl, :]).astype(BF16) for s, sl in zip(scores, sls)]
        outs = [jnp.dot(p, vs_ref[...], preferred_element_type=F32) for p in ps]
        for o, sl in zip(outs, sls):
            y = o * lax.rsqrt(jnp.mean(o * o, axis=-1, keepdims=True) + EPS) * gain
            g = g_ref[0, sl, :]
            o_ref[0, sl, :] = (y * (g * jax.nn.sigmoid(g))).astype(o_ref.dtype)
        return 0

    lax.fori_loop(0, seq // (2 * tq), q_tiles, 0)


def _retention(proj3, ret_gain, width):
    bsz, seq, _ = proj3.shape
    dh = RET_HEAD_DIM
    heads = width // dh
    inv_freq = ROPE_BASE ** (-jnp.arange(0, dh, 2, dtype=F32) / dh)
    ang = jnp.arange(seq, dtype=F32)[:, None] * inv_freq[None, :]
    cos, sin = jnp.cos(ang), jnp.sin(ang)
    log_gamma = jnp.log1p(-jnp.exp2(-5.0 - jnp.arange(heads, dtype=F32)))
    lg = jnp.stack([log_gamma, log_gamma[::-1]], axis=1)
    lg = jnp.broadcast_to(lg[:, :, None], (heads, 2, seq))

    def col(k):
        return pl.BlockSpec((1, seq, dh), lambda h, b: (b, 0, k * heads + h))

    return pl.pallas_call(
        _retention_kernel,
        grid=(heads, bsz),
        in_specs=[col(0), col(1), col(2), col(3),
                  pl.BlockSpec((seq, dh // 2), lambda h, b: (0, 0)),
                  pl.BlockSpec((seq, dh // 2), lambda h, b: (0, 0)),
                  pl.BlockSpec((1, 2, seq), lambda h, b: (h, 0, 0)),
                  pl.BlockSpec((1, dh), lambda h, b: (0, h))],
        out_specs=pl.BlockSpec((1, seq, dh), lambda h, b: (b, 0, h)),
        out_shape=jax.ShapeDtypeStruct((bsz, seq, width), BF16),
        scratch_shapes=[pltpu.VMEM((seq, dh), BF16)] * 3 + [pltpu.VMEM((seq, seq), F32)],
        compiler_params=_cparams(("arbitrary", "arbitrary")),
        name="retention",
    )(proj3, proj3, proj3, proj3, cos, sin, lg, ret_gain.reshape(1, width))


def _fnet_kernel(x_ref, cc_ref, sc_ref, cs_ref, ss_ref, o_ref, a1_ref, a2_ref):
    @pl.when(pl.program_id(1) == 0)
    def _():
        x = x_ref[0].astype(BF16)
        a1_ref[...] = jnp.dot(x, cc_ref[...], preferred_element_type=F32).astype(BF16)
        a2_ref[...] = jnp.dot(x, sc_ref[...], preferred_element_type=F32).astype(BF16)

    y = (jnp.dot(cs_ref[...], a1_ref[...], preferred_element_type=F32)
         - jnp.dot(ss_ref[...], a2_ref[...], preferred_element_type=F32))
    o_ref[0] = y.astype(o_ref.dtype)


def _dft_mats(n):
    idx = jnp.arange(n, dtype=I32)
    ang = (2.0 * math.pi / n) * ((idx[:, None] * idx[None, :]) % n).astype(F32)
    scale = n ** -0.5
    return jnp.cos(ang) * scale, jnp.sin(ang) * scale


def _fnet(proj3, col0, width):
    bsz, seq, _ = proj3.shape
    gw = width // FNET_GROUPS
    cs, ss = _dft_mats(seq)
    cg, sg = _dft_mats(gw)
    eye = jnp.eye(FNET_GROUPS, dtype=F32)
    cc = jnp.kron(eye, cg)
    sc = jnp.kron(eye, sg)
    tr = FNET_ROW_TILE
    return pl.pallas_call(
        _fnet_kernel,
        grid=(bsz, seq // tr),
        in_specs=[
            pl.BlockSpec((1, seq, width), lambda b, i: (b, 0, col0 // width)),
            pl.BlockSpec((width, width), lambda b, i: (0, 0)),
            pl.BlockSpec((width, width), lambda b, i: (0, 0)),
            pl.BlockSpec((tr, seq), lambda b, i: (i, 0)),
            pl.BlockSpec((tr, seq), lambda b, i: (i, 0)),
        ],
        out_specs=pl.BlockSpec((1, tr, width), lambda b, i: (b, i, 0)),
        out_shape=jax.ShapeDtypeStruct((bsz, seq, width), BF16),
        scratch_shapes=[pltpu.VMEM((seq, width), BF16)] * 2,
        compiler_params=_cparams(("arbitrary", "arbitrary")),
        name="fnet",
    )(proj3, cc.astype(BF16), sc.astype(BF16), cs.astype(BF16), ss.astype(BF16))


HIGH_HALF = -65536


def _slab_rows(s, rows, per_row):
    return pl.ds(s, rows, stride=per_row)


def _pack_rows(h, slab_ref):
    rows, d = h.shape
    half = d // 2
    per_row = half // LANES
    for s in range(per_row):
        lo = h[:, s * LANES:(s + 1) * LANES].astype(BF16).astype(F32)
        hi = h[:, half + s * LANES:half + (s + 1) * LANES].astype(BF16).astype(F32)
        lo_bits = lax.shift_right_logical(lax.bitcast_convert_type(lo, I32), jnp.int32(16))
        hi_bits = lax.bitcast_convert_type(hi, I32) & jnp.int32(HIGH_HALF)
        slab_ref[_slab_rows(s, rows, per_row), :] = hi_bits | lo_bits


def _unpack_rows(slab_ref, per_row):
    rows = slab_ref.shape[0] // per_row
    los, his = [], []
    for s in range(per_row):
        w = slab_ref[_slab_rows(s, rows, per_row), :]
        los.append(lax.bitcast_convert_type(lax.shift_left(w, jnp.int32(16)), F32))
        his.append(lax.bitcast_convert_type(w & jnp.int32(HIGH_HALF), F32))
    return jnp.concatenate(los, axis=-1).astype(BF16), jnp.concatenate(his, axis=-1).astype(BF16)


def _outproj_kernel(*refs, glu):
    if glu:
        (x_ref, a_ref, b_ref, gw_ref, gb_ref, wa_ref, wb_ref, gm_ref, gain_ref, sh_ref, sc_ref,
         rwh_ref, rwl_ref, rb_ref, x1_ref, h_ref, idx_ref, gate_ref, cnt_ref, base_ref) = refs
    else:
        (x_ref, a_ref, b_ref, wa_ref, wb_ref, gm_ref, gain_ref, sh_ref, sc_ref,
         rwh_ref, rwl_ref, rb_ref, x1_ref, h_ref, idx_ref, gate_ref, cnt_ref, base_ref) = refs

    @pl.when(pl.program_id(0) == 0)
    def _():
        base_ref[...] = jnp.zeros(base_ref.shape, F32)

    bm = b_ref[...]
    if glu:
        gl = jnp.dot(bm.astype(BF16), gw_ref[...], preferred_element_type=F32) + gb_ref[...]
        bm = (bm.astype(F32) * jax.nn.sigmoid(gl)).astype(BF16)
    y = (jnp.dot(a_ref[...], wa_ref[...], preferred_element_type=F32)
         + jnp.dot(bm, wb_ref[...], preferred_element_type=F32))
    x1 = x_ref[...] + gm_ref[0] * y
    x1_ref[...] = x1
    h = _norm_modulate(x1, gain_ref[...], sh_ref[0], sc_ref[0])
    _pack_rows(h, h_ref)

    h_hi = h.astype(BF16)
    h_lo = (h - h_hi.astype(F32)).astype(BF16)
    logits = (jnp.dot(h_hi, rwh_ref[...], preferred_element_type=F32)
              + jnp.dot(h_lo, rwh_ref[...], preferred_element_type=F32)
              + jnp.dot(h_hi, rwl_ref[...], preferred_element_type=F32)
              + rb_ref[...])
    lane = lax.broadcasted_iota(I32, logits.shape, 1)
    vals, idxs = [], []
    for _ in range(TOP_K):
        m = jnp.max(logits, axis=-1, keepdims=True)
        ik = jnp.min(jnp.where(logits == m, lane, ROUTER_PAD), axis=-1, keepdims=True)
        vals.append(m)
        idxs.append(ik)
        logits = jnp.where(lane == ik, -jnp.inf, logits)
    exps = [jnp.exp(v - vals[0]) for v in vals]
    denom = exps[0] + exps[1] + exps[2] + exps[3]

    tm = lane.shape[0]
    onehot = jnp.zeros(lane.shape, F32)
    for k in range(TOP_K):
        onehot = onehot + (lane == idxs[k]).astype(F32)
    before = (lax.broadcasted_iota(I32, (tm, tm), 1) < lax.broadcasted_iota(I32, (tm, tm), 0)).astype(BF16)
    count = jnp.dot(before, onehot.astype(BF16), preferred_element_type=F32) + base_ref[...]
    new_base = base_ref[...] + jnp.sum(onehot, axis=0, keepdims=True)
    base_ref[...] = new_base
    cnt_ref[...] = new_base.astype(I32)

    idx_out = jnp.zeros(lane.shape, I32)
    gate_out = jnp.zeros(lane.shape, F32)
    for k in range(TOP_K):
        rank = jnp.sum(jnp.where(lane == idxs[k], count, 0.0), axis=-1, keepdims=True).astype(I32)
        idx_out = jnp.where(lane == k, idxs[k], idx_out)
        idx_out = jnp.where(lane == TOP_K + k, rank, idx_out)
        gate_out = jnp.where(lane == k, exps[k] / denom, gate_out)
    idx_ref[...] = idx_out
    gate_ref[...] = gate_out


def _out_proj(x2d, mix_a, mix_b, w_out, g_mix, gain, shift, scale, router_w, router_b, seq,
              glu_w=None, glu_b=None):
    t, d = x2d.shape
    wa_rows = mix_a.shape[1]
    wb_rows = mix_b.shape[1]
    tm = 256
    per_seq = seq // tm
    n_exp = router_w.shape[1]
    rw = jnp.zeros((d, ROUTER_PAD), F32).at[:, :n_exp].set(router_w)
    rw_hi = rw.astype(BF16)
    rw_lo = (rw - rw_hi.astype(F32)).astype(BF16)
    rb = jnp.full((1, ROUTER_PAD), NEG_BIG, F32).at[0, :n_exp].set(router_b)
    w_bf = w_out.astype(BF16)
    glu = glu_w is not None

    def rows(width):
        return pl.BlockSpec((tm, width), lambda i: (i, 0))

    def full(r, c):
        return pl.BlockSpec((r, c), lambda i: (0, 0))

    def per_batch():
        return pl.BlockSpec((1, 1, d), lambda i: (i // per_seq, 0, 0))

    in_specs = [rows(d), rows(wa_rows), rows(wb_rows)]
    args = [x2d, mix_a, mix_b]
    if glu:
        in_specs += [full(wb_rows, wb_rows), full(1, wb_rows)]
        args += [glu_w.astype(BF16), glu_b.reshape(1, wb_rows)]
    in_specs += [pl.BlockSpec((wa_rows, d), lambda i: (0, 0)),
                 pl.BlockSpec((wb_rows, d), lambda i: (wa_rows // wb_rows, 0)),
                 per_batch(), full(1, d), per_batch(), per_batch(),
                 full(d, ROUTER_PAD), full(d, ROUTER_PAD), full(1, ROUTER_PAD)]
    args += [w_bf, w_bf, g_mix, gain.reshape(1, d), shift, scale, rw_hi, rw_lo, rb]
    return pl.pallas_call(
        functools.partial(_outproj_kernel, glu=glu),
        grid=(t // tm,),
        in_specs=in_specs,
        out_specs=[rows(d), pl.BlockSpec((tm * (d // (2 * LANES)), LANES), lambda i: (i, 0)),
                   rows(ROUTER_PAD), rows(ROUTER_PAD), full(1, ROUTER_PAD)],
        out_shape=[jax.ShapeDtypeStruct((t, d), F32), jax.ShapeDtypeStruct((t * (d // (2 * LANES)), LANES), I32),
                   jax.ShapeDtypeStruct((t, ROUTER_PAD), I32), jax.ShapeDtypeStruct((t, ROUTER_PAD), F32),
                   jax.ShapeDtypeStruct((1, ROUTER_PAD), I32)],
        scratch_shapes=[pltpu.VMEM((1, ROUTER_PAD), F32)],
        compiler_params=_cparams(("arbitrary",)),
        name="out_proj_glu" if glu else "out_proj",
    )(*args)


def _routing_tables(top_idx, rank, counts, n_tok):
    n_assign = n_tok * TOP_K
    r = MOE_ROWS
    n_rb = n_assign // r + N_EXPERTS
    experts = jnp.arange(N_EXPERTS, dtype=I32)
    nblk = (counts + r - 1) // r
    blk_end = jnp.cumsum(nblk)
    blk_start = blk_end - nblk
    row_start = jnp.sum(jnp.where(top_idx[:, :, None] == experts[None, None, :],
                                  (blk_start * r)[None, None, :], 0), axis=-1)
    dest = (row_start + rank).reshape(-1).astype(I32)
    n_valid_blocks = blk_end[-1]
    rb = jnp.arange(n_rb, dtype=I32)
    rb_e = jnp.sum((blk_end[None, :] <= rb[:, None]).astype(I32), axis=1)
    rb_e = jnp.minimum(rb_e, N_EXPERTS - 1).astype(I32)
    tail = jnp.where(nblk > 0, blk_end - 1, -1)
    spare = n_valid_blocks + experts
    spare = jnp.where(spare < n_rb, spare, -1)
    zero_blocks = jnp.concatenate([tail, spare]).astype(I32)
    return dict(dest=dest, rb_e=rb_e, zero_blocks=zero_blocks,
                n_valid_blocks=n_valid_blocks.reshape(1).astype(I32),
                nblk=nblk.astype(I32), blk_start=blk_start.astype(I32), n_rb=n_rb, n_slots=n_rb * r)


def _dispatch_kernel(dest_ref, zb_ref, h_ref, xs_hbm, zero_ref, zsem_ref, sem_ref):
    i = pl.program_id(0)
    td = DISPATCH_TOKENS
    sr = h_ref.shape[0] // td
    blk = MOE_ROWS * sr

    def slab(ref, row):
        return ref.at[pl.ds(pl.multiple_of(row * sr, sr), sr), :]

    @pl.when(i == 0)
    def _():
        zero_ref[...] = jnp.zeros(zero_ref.shape, zero_ref.dtype)

        def zero_copy(z):
            row0 = pl.multiple_of(zb_ref[z] * blk, blk)
            return pltpu.make_async_copy(zero_ref, xs_hbm.at[pl.ds(row0, blk), :], zsem_ref.at[0])

        def start(z, _):
            @pl.when(zb_ref[z] >= 0)
            def _():
                zero_copy(z).start()
            return 0

        def wait(z, _):
            @pl.when(zb_ref[z] >= 0)
            def _():
                zero_copy(z).wait()
            return 0

        lax.fori_loop(0, zb_ref.shape[0], start, 0)
        lax.fori_loop(0, zb_ref.shape[0], wait, 0)

    def group(g, _):
        for s in range(DMA_UNROLL):
            t = g * DMA_UNROLL + s
            for k in range(TOP_K):
                dst = dest_ref[(i * td + t) * TOP_K + k]
                pltpu.make_async_copy(slab(h_ref, t), slab(xs_hbm, dst), sem_ref.at[0]).start()
        return 0

    lax.fori_loop(0, td // DMA_UNROLL, group, 0)
    for _ in range(TOP_K):
        pltpu.make_async_copy(h_ref, xs_hbm.at[pl.ds(0, td * sr), :], sem_ref.at[0]).wait()


def _moe_dispatch(h_slabs, n_tok, rt):
    sr = h_slabs.shape[0] // n_tok
    td = DISPATCH_TOKENS
    grid_spec = pltpu.PrefetchScalarGridSpec(
        num_scalar_prefetch=2,
        grid=(n_tok // td,),
        in_specs=[pl.BlockSpec((td * sr, LANES), lambda i, dst, zb: (i, 0))],
        out_specs=pl.BlockSpec(memory_space=pl.ANY),
        scratch_shapes=[pltpu.VMEM((MOE_ROWS * sr, LANES), h_slabs.dtype), pltpu.SemaphoreType.DMA((1,)),
                        pltpu.SemaphoreType.DMA((1,))],
    )
    return pl.pallas_call(
        _dispatch_kernel,
        grid_spec=grid_spec,
        out_shape=jax.ShapeDtypeStruct((rt['n_slots'] * sr, LANES), h_slabs.dtype),
        compiler_params=_cparams(("arbitrary",)),
        name="moe_dispatch",
    )(rt['dest'], rt['zero_blocks'], h_slabs)


def _cast_rows(src_ref, dst_ref, chunk=256):
    chunk = min(chunk, dst_ref.shape[0])
    n = dst_ref.shape[0] // chunk

    def body(c, _):
        sl = pl.ds(pl.multiple_of(c * chunk, chunk), chunk)
        dst_ref[sl, :] = src_ref[sl, :].astype(dst_ref.dtype)
        return 0

    lax.fori_loop(0, n, body, 0)


def _stagger_tables(group, cols, parts):
    n = group.shape[0]
    idx = jnp.arange(n, dtype=I32)
    is_first = jnp.concatenate([jnp.ones((1,), bool), group[1:] != group[:-1]])
    first_cur = lax.cummax(jnp.where(is_first, idx, 0))
    nxt = jnp.where(is_first, idx, n)
    first_next = jnp.concatenate([lax.cummin(nxt[::-1])[::-1][1:], jnp.full((1,), n, I32)])
    out = [[] for _ in cols]
    for p in range(parts):
        switch = (idx >= jnp.maximum(first_next - p, first_cur + 1)) & (first_next < n)
        eff = jnp.where(switch, first_next, idx)
        for k, c in enumerate(cols):
            out[k].append(c[eff])
    return [jnp.stack(o) for o in out]


def _moe_up_kernel(rb_ref, j_ref, e_ref, first_ref, valid_ref, pe_ref, pj_ref, x_ref, *refs):
    parts = MOE_W_PARTS
    wg_refs = refs[:parts]
    wu_refs = refs[parts:2 * parts]
    bg_ref, bu_ref, o_ref, wgs_ref, wus_ref = refs[2 * parts:]
    i = pl.program_id(0)
    rows = wgs_ref.shape[0] // parts

    @pl.when(first_ref[i] == 1)
    def _():
        for p in range(parts):
            _cast_rows(wg_refs[p].at[0, 0], wgs_ref.at[pl.ds(p * rows, rows)])
            _cast_rows(wu_refs[p].at[0, 0], wus_ref.at[pl.ds(p * rows, rows)])

    @pl.when(valid_ref[i] == 1)
    def _():
        x_lo, x_hi = _unpack_rows(x_ref, x_ref.shape[0] // MOE_ROWS)
        half = wgs_ref.shape[0] // 2

        def proj(ws_ref, b_ref):
            return (jnp.dot(x_lo, ws_ref[:half, :], preferred_element_type=F32)
                    + jnp.dot(x_hi, ws_ref[half:, :], preferred_element_type=F32) + b_ref[0])

        gate = proj(wgs_ref, bg_ref)
        up = proj(wus_ref, bu_ref)
        gate = jnp.minimum(gate, SWIGLU_LIMIT)
        up = jnp.clip(up, -SWIGLU_LIMIT, SWIGLU_LIMIT)
        act = (up + 1.0) * gate * jax.nn.sigmoid(SWIGLU_ALPHA * gate)
        o_ref[...] = act.astype(o_ref.dtype)

    @pl.when(valid_ref[i] == 0)
    def _():
        o_ref[...] = jnp.zeros(o_ref.shape, o_ref.dtype)


def _moe_up(x_sorted, w_gu, b_gu, layer, rt):
    _, n_exp, d, two_f = w_gu.shape
    n_slots = rt['n_slots']
    sr = x_sorted.shape[0] // n_slots
    f = two_f // 2
    r, tn = MOE_ROWS, MOE_UP_TN
    nj = f // tn
    n_rb = rt['n_rb']
    n_items = n_rb * nj
    rb = jnp.arange(n_rb, dtype=I32)
    e_of = rt['rb_e']
    nvb = rt['n_valid_blocks'][0]
    nblk = rt['nblk'].at[n_exp - 1].add(n_rb - nvb)
    q = rb - rt['blk_start'][e_of]
    pos = (nj * rt['blk_start'][e_of][:, None]
           + jnp.arange(nj, dtype=I32)[None, :] * nblk[e_of][:, None] + q[:, None]).reshape(-1)
    rb2 = jnp.broadcast_to(rb[:, None], (n_rb, nj)).reshape(-1)
    j2 = jnp.broadcast_to(jnp.arange(nj, dtype=I32)[None, :], (n_rb, nj)).reshape(-1)
    it_rb = jnp.zeros((n_items,), I32).at[pos].set(rb2)
    it_j = jnp.zeros((n_items,), I32).at[pos].set(j2)
    it_valid = (it_rb < nvb).astype(I32)
    it_e = e_of[it_rb]
    prev_e = jnp.concatenate([jnp.full((1,), -1, I32), it_e[:-1]])
    prev_j = jnp.concatenate([jnp.full((1,), -1, I32), it_j[:-1]])
    is_first = (it_e != prev_e) | (it_j != prev_j)
    it_first = is_first.astype(I32)
    parts = MOE_W_PARTS
    part_e, part_j = _stagger_tables(jnp.cumsum(it_first), [it_e, it_j], parts)

    def w_spec(p, col0):
        return pl.BlockSpec((1, 1, d // parts, tn),
                            lambda i, rbt, jt, et, ft, vt, pe, pj: (layer, pe[p, i], p, col0 + pj[p, i]))

    grid_spec = pltpu.PrefetchScalarGridSpec(
        num_scalar_prefetch=7,
        grid=(n_items,),
        in_specs=([pl.BlockSpec((r * sr, LANES), lambda i, rbt, jt, et, ft, vt, pe, pj: (rbt[i], 0))]
                  + [w_spec(p, 0) for p in range(parts)] + [w_spec(p, nj) for p in range(parts)]
                  + [pl.BlockSpec((1, 1, tn), lambda i, rbt, jt, et, ft, vt, pe, pj: (et[i], 0, jt[i])),
                     pl.BlockSpec((1, 1, tn), lambda i, rbt, jt, et, ft, vt, pe, pj: (et[i], 0, nj + jt[i]))]),
        out_specs=pl.BlockSpec((r, tn), lambda i, rbt, jt, et, ft, vt, pe, pj: (rbt[i], jt[i])),
        scratch_shapes=[pltpu.VMEM((d, tn), BF16), pltpu.VMEM((d, tn), BF16)],
    )
    b3 = b_gu.reshape(n_exp, 1, two_f)
    return pl.pallas_call(
        _moe_up_kernel,
        grid_spec=grid_spec,
        out_shape=jax.ShapeDtypeStruct((n_slots, f), BF16),
        compiler_params=_cparams(("arbitrary",)),
        name="moe_up",
    )(it_rb, it_j, it_e, it_first, it_valid, part_e, part_j, x_sorted,
      *([w_gu] * (2 * parts)), b3, b3)


def _moe_dn_kernel(e_ref, nvb_ref, pe_ref, a_ref, *refs):
    parts = MOE_W_PARTS
    w_refs = refs[:parts]
    b_ref, o_ref, ws_ref = refs[parts:]
    i = pl.program_id(0)
    valid = i < nvb_ref[0]
    e = e_ref[i]
    prev_e = e_ref[jnp.maximum(i - 1, 0)]
    rows = ws_ref.shape[0] // parts

    @pl.when(valid & ((i == 0) | (e != prev_e)))
    def _():
        for p in range(parts):
            _cast_rows(w_refs[p].at[0, 0], ws_ref.at[pl.ds(p * rows, rows)])

    @pl.when(valid)
    def _():
        o_ref[...] = jnp.dot(a_ref[...], ws_ref[...], preferred_element_type=F32) + b_ref[0]

    @pl.when(jnp.logical_not(valid))
    def _():
        o_ref[...] = jnp.zeros(o_ref.shape, o_ref.dtype)


def _moe_dn(act, w_dn, b_dn, layer, rt):
    n_slots, f = act.shape
    _, n_exp, _, d = w_dn.shape
    r = MOE_ROWS
    parts = MOE_W_PARTS
    (part_e,) = _stagger_tables(rt['rb_e'], [rt['rb_e']], parts)

    def w_spec(p):
        return pl.BlockSpec((1, 1, f // parts, d), lambda i, et, nvb, pe: (layer, pe[p, i], p, 0))

    grid_spec = pltpu.PrefetchScalarGridSpec(
        num_scalar_prefetch=3,
        grid=(rt['n_rb'],),
        in_specs=([pl.BlockSpec((r, f), lambda i, et, nvb, pe: (jnp.minimum(i, nvb[0] - 1), 0))]
                  + [w_spec(p) for p in range(parts)]
                  + [pl.BlockSpec((1, 1, d), lambda i, et, nvb, pe: (et[i], 0, 0))]),
        out_specs=pl.BlockSpec((r, d), lambda i, et, nvb, pe: (i, 0)),
        scratch_shapes=[pltpu.VMEM((f, d), BF16)],
    )
    return pl.pallas_call(
        _moe_dn_kernel,
        grid_spec=grid_spec,
        out_shape=jax.ShapeDtypeStruct((n_slots, d), F32),
        compiler_params=_cparams(("arbitrary",)),
        name="moe_dn",
    )(rt['rb_e'], rt['n_valid_blocks'], part_e, act, *([w_dn] * parts), b_dn.reshape(n_exp, 1, d))


def _combine_kernel(dest_ref, x_ref, gate_ref, g_ref, fg_ref, y_hbm, o_ref, buf_ref, sem_ref, *, final):
    i = pl.program_id(0)
    n_steps = pl.num_programs(0)
    tc = COMBINE_TOKENS

    def issue(step, slot):
        def group(g, _):
            for s in range(DMA_UNROLL):
                t = g * DMA_UNROLL + s
                for k in range(TOP_K):
                    src = dest_ref[(step * tc + t) * TOP_K + k]
                    pltpu.make_async_copy(y_hbm.at[pl.ds(src, 1), :], buf_ref.at[slot, k, pl.ds(t, 1), :],
                                          sem_ref.at[slot]).start()
            return 0

        lax.fori_loop(0, tc // DMA_UNROLL, group, 0)

    @pl.when(i == 0)
    def _():
        issue(0, 0)

    @pl.when(i + 1 < n_steps)
    def _():
        issue(i + 1, (i + 1) % 2)

    slot = i % 2
    for k in range(TOP_K):
        pltpu.make_async_copy(y_hbm.at[pl.ds(0, tc), :], buf_ref.at[slot, k], sem_ref.at[slot]).wait()
    gates = gate_ref[...]
    y = gates[:, 0:1] * buf_ref[slot, 0]
    for k in range(1, TOP_K):
        y = y + gates[:, k:k + 1] * buf_ref[slot, k]
    x2 = x_ref[...] + g_ref[0] * y
    if final:
        x2 = x2 * lax.rsqrt(jnp.mean(x2 * x2, axis=-1, keepdims=True) + EPS) * fg_ref[...]
    o_ref[...] = x2


def _combine(x1, y_sorted, gates, rt, g_ffn, final_gain, seq, final):
    t, d = x1.shape
    tc = COMBINE_TOKENS
    per_seq = seq // tc
    grid_spec = pltpu.PrefetchScalarGridSpec(
        num_scalar_prefetch=1,
        grid=(t // tc,),
        in_specs=[pl.BlockSpec((tc, d), lambda i, dst: (i, 0)),
                  pl.BlockSpec((tc, ROUTER_PAD), lambda i, dst: (i, 0)),
                  pl.BlockSpec((1, 1, d), lambda i, dst: (i // per_seq, 0, 0)),
                  pl.BlockSpec((1, d), lambda i, dst: (0, 0)),
                  pl.BlockSpec(memory_space=pl.ANY)],
        out_specs=pl.BlockSpec((tc, d), lambda i, dst: (i, 0)),
        scratch_shapes=[pltpu.VMEM((2, TOP_K, tc, d), F32), pltpu.SemaphoreType.DMA((2,))],
    )
    return pl.pallas_call(
        functools.partial(_combine_kernel, final=final),
        grid_spec=grid_spec,
        out_shape=jax.ShapeDtypeStruct((t, d), F32),
        compiler_params=_cparams(("arbitrary",)),
        name="combine_final" if final else "combine",
    )(rt['dest'], x1, gates, g_ffn, final_gain.reshape(1, d), y_sorted)


def _moe(x1, h, idx, gates, counts, w_gu, b_gu, w_dn, b_dn, layer, g_ffn, final_gain, seq, final):
    n_tok = x1.shape[0]
    rt = _routing_tables(idx[:, :TOP_K], idx[:, TOP_K:2 * TOP_K], counts[0, :N_EXPERTS], n_tok)
    x_sorted = _moe_dispatch(h, n_tok, rt)
    act = _moe_up(x_sorted, w_gu, b_gu, layer, rt)
    y_sorted = _moe_dn(act, w_dn, b_dn, layer, rt)
    return _combine(x1, y_sorted, gates, rt, g_ffn, final_gain, seq, final)


def kernel(x, c, ada_w, ada_b, norm_mix_gain, norm_ffn_gain, ab_w_in, ab_w_out, hg_lb_logits, hg_norm_gain, s5_lam_re, s5_lam_im, s5_log_dt, s5_b_re, s5_b_im, s5_c_re, s5_c_im, s5_d, s5_glu_w, s5_glu_b, cd_w_in, cd_w_out, ret_norm_gain, router_w, router_b, moe_w_gu, moe_b_gu, moe_w_dn, moe_b_dn, final_gain):
    bsz, seq, d = x.shape
    depth = ada_w.shape[0]
    n_tok = bsz * seq
    hg_width = hg_lb_logits.shape[1]
    s5_width = s5_glu_w.shape[1]
    ret_width = ret_norm_gain.shape[1]
    fnet_width = cd_w_out.shape[1] - ret_width

    lower_bounds = jnp.cumsum(jax.nn.softmax(hg_lb_logits.astype(F32), axis=0), axis=0)
    mod = _ada_mod(c, ada_w, ada_b)
    xr = x.reshape(n_tok, d)
    for layer in range(depth):
        sh_mix, sc_mix, g_mix, sh_ffn, sc_ffn, g_ffn = (
            mod[layer, :, k * d:(k + 1) * d].reshape(bsz, 1, d) for k in range(6))
        j = layer // 2
        if layer % 2 == 0:
            proj = _in_proj(xr, norm_mix_gain[layer], sh_mix, sc_mix, ab_w_in[j].astype(BF16), seq)
            proj3 = proj.reshape(bsz, seq, proj.shape[1])
            mix_a = _hgrn2(proj3, lower_bounds[j], hg_norm_gain[j], hg_width)
            tables = _s5c_tables(s5_lam_re[j], s5_lam_im[j], s5_log_dt[j], s5_b_re[j], s5_b_im[j],
                                 s5_c_re[j], s5_c_im[j], s5_d[j])
            mix_b = _s5c(proj3, 5 * hg_width, s5_width, tables)
            x1, h, idx, gate, counts = _out_proj(
                xr, mix_a.reshape(n_tok, hg_width), mix_b.reshape(n_tok, s5_width), ab_w_out[j],
                g_mix, norm_ffn_gain[layer], sh_ffn, sc_ffn, router_w[layer], router_b[layer], seq,
                glu_w=s5_glu_w[j], glu_b=s5_glu_b[j])
        else:
            proj = _in_proj(xr, norm_mix_gain[layer], sh_mix, sc_mix, cd_w_in[j].astype(BF16), seq)
            proj3 = proj.reshape(bsz, seq, proj.shape[1])
            mix_a = _retention(proj3, ret_norm_gain[j], ret_width)
            mix_b = _fnet(proj3, 4 * ret_width, fnet_width)
            x1, h, idx, gate, counts = _out_proj(
                xr, mix_a.reshape(n_tok, ret_width), mix_b.reshape(n_tok, fnet_width), cd_w_out[j],
                g_mix, norm_ffn_gain[layer], sh_ffn, sc_ffn, router_w[layer], router_b[layer], seq)
        xr = _moe(x1, h, idx, gate, counts, moe_w_gu, moe_b_gu[layer], moe_w_dn, moe_b_dn[layer],
                  layer, g_ffn, final_gain, seq, final=(layer == depth - 1))
    return xr.reshape(bsz, seq, d)
```

```python
import functools
import math

import jax
import jax.numpy as jnp
from jax import lax
from jax.experimental import pallas as pl
from jax.experimental.pallas import tpu as pltpu

F32 = jnp.float32
BF16 = jnp.bfloat16
I32 = jnp.int32

EPS = 1e-6
LANES = 128
SUBLANES = 8
VMEM_LIMIT = 56 * 1024 * 1024

HG_HEAD_DIM = 128
HG_CHUNK = 64
HG_GROUP = 256
HG_EXP_CLAMP = 80.0

S5_GROUP = 16
S5_STATE = 64
S5_CHUNK = 16
S5_UNROLL = 8
S5_TILE_PAIRS = 4
S5_ROW_SPLIT = 4

RET_HEAD_DIM = 256
RET_Q_TILE = 256
ROPE_BASE = 10000.0

FNET_GROUPS = 4
FNET_ROW_TILE = 512

N_EXPERTS = 32
TOP_K = 4
SWIGLU_LIMIT = 7.0
SWIGLU_ALPHA = 1.702
MOE_ROWS = 256
MOE_UP_TN = 1024
MOE_UP_COLS = 256
MOE_W_PARTS = 4
DISPATCH_TOKENS = 512
COMBINE_TOKENS = 256
DMA_UNROLL = 8
ROUTER_PAD = LANES
NEG_BIG = -1e30


def _cparams(semantics):
    return pltpu.CompilerParams(dimension_semantics=semantics, vmem_limit_bytes=VMEM_LIMIT)


def _ada_kernel(c_ref, w_ref, b_ref, o_ref):
    c = c_ref[...]
    cond = c * jax.nn.sigmoid(c)
    o_ref[0] = jnp.dot(cond.astype(BF16), w_ref[0].astype(BF16),
                       preferred_element_type=F32) + b_ref[0]


def _ada_mod(c, ada_w, ada_b):
    depth, d, n = ada_w.shape
    bsz = c.shape[0]
    tn = 1024
    return pl.pallas_call(
        _ada_kernel,
        grid=(depth, n // tn),
        in_specs=[
            pl.BlockSpec((bsz, d), lambda l, j: (0, 0)),
            pl.BlockSpec((1, d, tn), lambda l, j: (l, 0, j)),
            pl.BlockSpec((1, 1, tn), lambda l, j: (l, 0, j)),
        ],
        out_specs=pl.BlockSpec((1, bsz, tn), lambda l, j: (l, 0, j)),
        out_shape=jax.ShapeDtypeStruct((depth, bsz, n), F32),
        compiler_params=_cparams(("arbitrary", "arbitrary")),
        name="ada_mod",
    )(c, ada_w, ada_b.reshape(depth, 1, n))


def _norm_modulate(x, gain, shift, scale):
    ms = jnp.mean(x * x, axis=-1, keepdims=True)
    y = x * lax.rsqrt(ms + EPS) * gain
    return y * (1.0 + scale) + shift


def _inproj_kernel(x_ref, gain_ref, sh_ref, sc_ref, w_ref, o_ref, h_ref):
    @pl.when(pl.program_id(1) == 0)
    def _():
        rows = 128

        def slab(c, _):
            sl = pl.ds(pl.multiple_of(c * rows, rows), rows)
            h_ref[sl, :] = _norm_modulate(x_ref[sl, :], gain_ref[...], sh_ref[0], sc_ref[0]).astype(BF16)
            return 0

        lax.fori_loop(0, h_ref.shape[0] // rows, slab, 0)

    o_ref[...] = jnp.dot(h_ref[...], w_ref[...], preferred_element_type=F32)


def _in_proj(x2d, gain, shift, scale, w_bf16, seq):
    t, d = x2d.shape
    n = w_bf16.shape[1]
    tm = 1024
    tn = 1024 if n % 1024 == 0 else 512
    per_seq = seq // tm
    return pl.pallas_call(
        _inproj_kernel,
        grid=(t // tm, n // tn),
        in_specs=[
            pl.BlockSpec((tm, d), lambda i, j: (i, 0)),
            pl.BlockSpec((1, d), lambda i, j: (0, 0)),
            pl.BlockSpec((1, 1, d), lambda i, j: (i // per_seq, 0, 0)),
            pl.BlockSpec((1, 1, d), lambda i, j: (i // per_seq, 0, 0)),
            pl.BlockSpec((d, tn), lambda i, j: (0, j)),
        ],
        out_specs=pl.BlockSpec((tm, tn), lambda i, j: (i, j)),
        out_shape=jax.ShapeDtypeStruct((t, n), F32),
        scratch_shapes=[pltpu.VMEM((tm, d), BF16)],
        compiler_params=_cparams(("arbitrary", "arbitrary")),
        name="in_proj",
    )(x2d, gain.reshape(1, d), shift, scale, w_bf16)


def _split3(a):
    hi = a.astype(BF16)
    r1 = a - hi.astype(F32)
    mid = r1.astype(BF16)
    lo = (r1 - mid.astype(F32)).astype(BF16)
    return hi, mid, lo


def _tri_sum(tri, a):
    hi, mid, lo = _split3(a)
    return (jnp.dot(tri, hi, preferred_element_type=F32)
            + jnp.dot(tri, mid, preferred_element_type=F32)
            + jnp.dot(tri, lo, preferred_element_type=F32))


def _dot_nt(a, b):
    return lax.dot_general(a, b, (((1,), (1,)), ((), ())), preferred_element_type=F32)


def _dot_tn(a, b):
    return lax.dot_general(a, b, (((0,), (0,)), ((), ())), preferred_element_type=F32)


def _hgrn2_kernel(q_ref, zf_ref, zb_ref, v_ref, g_ref, lb_ref, gain_ref, o_ref, acc_ref, accb_ref):
    seq = q_ref.shape[1]
    ln = HG_CHUNK
    gr = HG_GROUP
    n_groups = seq // gr
    per_group = gr // ln
    lb = lb_ref[...]
    gain = gain_ref[...]
    row = lax.broadcasted_iota(I32, (gr, gr), 0)
    col = lax.broadcasted_iota(I32, (gr, gr), 1)
    chunk_lo = (row // ln) * ln
    chunk_hi = chunk_lo + ln
    lower_incl = (col <= row) & (col >= chunk_lo)
    upper_strict = (col > row) & (col < chunk_hi)
    tri_prefix = lower_incl.astype(BF16)
    tri_suffix = ((col >= row) & (col < chunk_hi)).astype(BF16)
    mid = ln // 2

    def per_chunk_rows(a, offset):
        return jnp.concatenate(
            [jnp.broadcast_to(a[j * ln + offset:j * ln + offset + 1, :], (ln, a.shape[1]))
             for j in range(per_group)], axis=0)

    rows = [slice(j * ln, (j + 1) * ln) for j in range(per_group)]

    def load(gi, z_ref, forward):
        sl = pl.ds(pl.multiple_of(gi * gr, gr), gr)
        f = lb + (1.0 - lb) * jax.nn.sigmoid(z_ref[0, sl, :])
        log_f = jnp.log(f)
        d = dict(sl=sl, forward=forward, q=q_ref[0, sl, :], v=v_ref[0, sl, :].astype(BF16), k=1.0 - f)
        if forward:
            d['cum'] = _tri_sum(tri_prefix, log_f)
        else:
            d['cum'] = _tri_sum(tri_suffix, log_f)
        return d

    def local_states(d):
        edge_off = ln - 1 if d['forward'] else 0
        kd = (d['k'] * jnp.exp(per_chunk_rows(d['cum'], edge_off) - d['cum'])).astype(BF16)
        d['local'] = [_dot_tn(d['v'][rs], kd[rs]) for rs in rows]

    def scores(d):
        ref_rows = per_chunk_rows(d['cum'], mid - 1 if d['forward'] else mid)
        qe = d['q'] * jnp.exp(jnp.minimum(d['cum'] - ref_rows, HG_EXP_CLAMP))
        ke = d['k'] * jnp.exp(jnp.minimum(ref_rows - d['cum'], HG_EXP_CLAMP))
        mask = lower_incl if d['forward'] else upper_strict
        d['scores'] = jnp.where(mask, _dot_nt(qe.astype(BF16), ke.astype(BF16)), 0.0)

    def inter(d, state_t):
        edge_off = ln - 1 if d['forward'] else 0
        qc = (d['q'] * jnp.exp(d['cum'])).astype(BF16)
        entering = [None] * per_group
        order = range(per_group) if d['forward'] else range(per_group - 1, -1, -1)
        for j in order:
            entering[j] = state_t.astype(BF16)
            edge = d['cum'][j * ln + edge_off:j * ln + edge_off + 1, :]
            state_t = state_t * jnp.exp(edge) + d['local'][j]
        d['inter'] = jnp.concatenate([_dot_nt(qc[rs], entering[j]) for j, rs in enumerate(rows)], axis=0)
        return state_t

    def body(i, states):
        st_f, st_b = states
        both = [load(i, zf_ref, True), load(n_groups - 1 - i, zb_ref, False)]
        for d in both:
            local_states(d)
        for d in both:
            scores(d)
        st_f = inter(both[0], st_f)
        st_b = inter(both[1], st_b)
        for d, ref in zip(both, (acc_ref, accb_ref)):
            ref[d['sl'], :] = jnp.dot(d['scores'].astype(BF16), d['v'], preferred_element_type=F32) + d['inter']
        return st_f, st_b

    zero = jnp.zeros((HG_HEAD_DIM, HG_HEAD_DIM), F32)
    lax.fori_loop(0, n_groups, body, (zero, zero))

    def finish(gi, _):
        sl = pl.ds(pl.multiple_of(gi * gr, gr), gr)
        o = acc_ref[sl, :] + accb_ref[sl, :]
        y = o * lax.rsqrt(jnp.mean(o * o, axis=-1, keepdims=True) + EPS) * gain
        g = g_ref[0, sl, :]
        o_ref[0, sl, :] = (y * (g * jax.nn.sigmoid(g))).astype(o_ref.dtype)
        return 0

    lax.fori_loop(0, n_groups, finish, 0)


def _hgrn2(proj3, lower_bound, hg_gain, width):
    bsz, seq, _ = proj3.shape
    heads = width // HG_HEAD_DIM
    dh = HG_HEAD_DIM

    def col(k):
        return pl.BlockSpec((1, seq, dh), lambda b, h: (b, 0, k * heads + h))

    return pl.pallas_call(
        _hgrn2_kernel,
        grid=(bsz, heads),
        in_specs=[col(0), col(1), col(2), col(3), col(4),
                  pl.BlockSpec((1, dh), lambda b, h: (0, h)),
                  pl.BlockSpec((1, dh), lambda b, h: (0, 0))],
        out_specs=pl.BlockSpec((1, seq, dh), lambda b, h: (b, 0, h)),
        out_shape=jax.ShapeDtypeStruct((bsz, seq, width), BF16),
        scratch_shapes=[pltpu.VMEM((seq, dh), F32), pltpu.VMEM((seq, dh), F32)],
        compiler_params=_cparams(("arbitrary", "arbitrary")),
        name="hgrn2",
    )(proj3, proj3, proj3, proj3, proj3, lower_bound.reshape(1, width), hg_gain.reshape(1, dh))


def _s5c_kernel(u_ref, t_ref, w_ref, v_ref, a_ref, z_ref, uc_ref, e_ref, p_ref):
    ck = S5_CHUNK
    n_chunks = u_ref.shape[1] // ck
    slab = n_chunks // S5_ROW_SPLIT
    gpt = 2 * S5_TILE_PAIRS
    half_tok = ck // 2
    lane_grp = lax.broadcasted_iota(I32, (slab, LANES), 1) // S5_GROUP

    def token_rows(rq, tok):
        return pl.ds(rq * slab * ck + tok, slab, stride=ck)

    def roll_lanes(a, groups):
        shift = (groups * S5_GROUP) % LANES
        return pltpu.roll(a, shift, 1) if shift else a

    for half in range(2):
        for rq in range(S5_ROW_SPLIT):
            toks = [u_ref.at[0][token_rows(rq, half * half_tok + sl), :] for sl in range(half_tok)]
            for gl in range(gpt):
                acc = jnp.zeros((slab, LANES), F32)
                for sl in range(half_tok):
                    acc = jnp.where(lane_grp == sl, roll_lanes(toks[sl], sl - gl), acc)
                uc_ref[gl * 2 + half, rq * slab:(rq + 1) * slab, :] = acc

    seg = [pl.ds(k * LANES, LANES) for k in range(4)]

    def group_inputs(g):
        return jnp.concatenate([uc_ref[2 * g], uc_ref[2 * g + 1]], axis=-1).astype(BF16)

    for g in range(gpt):
        e_ref[:, g, :] = jnp.dot(group_inputs(g), w_ref[g], preferred_element_type=F32)

    af_r, af_i, ab_r, ab_i = a_ref[0, 0], a_ref[0, 1], a_ref[0, 2], a_ref[0, 3]

    def cmul(ar, ai, br, bi):
        return ar * br - ai * bi, ar * bi + ai * br

    def step(m, carry):
        xr, xi, yr, yi = carry
        mb = n_chunks - 1 - m
        p_ref[m, :, seg[0]] = xr
        p_ref[m, :, seg[1]] = xi
        p_ref[mb, :, seg[2]] = yr
        p_ref[mb, :, seg[3]] = yi
        dr, di = cmul(af_r, af_i, xr, xi)
        gr, gi = cmul(ab_r, ab_i, yr, yi)
        return (dr + e_ref[m, :, seg[0]], di + e_ref[m, :, seg[1]],
                gr + e_ref[mb, :, seg[2]], gi + e_ref[mb, :, seg[3]])

    zero = jnp.zeros((gpt, LANES), F32)
    lax.fori_loop(0, n_chunks, step, (zero, zero, zero, zero), unroll=S5_UNROLL)

    for g in range(gpt):
        y = (jnp.dot(group_inputs(g), t_ref[g], preferred_element_type=F32)
             + jnp.dot(p_ref[:, g, :].astype(BF16), v_ref[g], preferred_element_type=F32))
        zt = jax.nn.gelu(y)
        for k in range(2):
            uc_ref[2 * g + k] = zt[:, k * LANES:(k + 1) * LANES]

    for half in range(2):
        for rq in range(S5_ROW_SPLIT):
            cols = [uc_ref[gl * 2 + half, rq * slab:(rq + 1) * slab, :] for gl in range(gpt)]
            for sl in range(half_tok):
                acc = jnp.zeros((slab, LANES), F32)
                for gl in range(gpt):
                    acc = jnp.where(lane_grp == gl, roll_lanes(cols[gl], gl - sl), acc)
                z_ref.at[0][token_rows(rq, half * half_tok + sl), :] = acc


_einsum_f32 = functools.partial(jnp.einsum, precision=lax.Precision.HIGHEST)


def _s5c_tables(lam_re, lam_im, log_dt, b_re, b_im, c_re, c_im, d_skip):
    groups, state = lam_re.shape[1], lam_re.shape[2]
    chans = b_re.shape[2]
    ck = S5_CHUNK
    lag = jnp.arange(ck + 1, dtype=F32)[:, None, None]
    kern, w_parts, v_parts, a_parts = [], [], [], []
    for direction in (0, 1):
        lr, li = lam_re[direction].astype(F32), lam_im[direction].astype(F32)
        dt = jnp.exp(log_dt[direction].astype(F32))[:, None]
        mag = jnp.exp(lr * dt)
        abar_re = mag * jnp.cos(li * dt)
        abar_im = mag * jnp.sin(li * dt)
        den = lr * lr + li * li
        num_re = abar_re - 1.0
        coef_re = (num_re * lr + abar_im * li) / den
        coef_im = (abar_im * lr - num_re * li) / den
        bbar_re = coef_re[..., None] * b_re - coef_im[..., None] * b_im
        bbar_im = coef_re[..., None] * b_im + coef_im[..., None] * b_re
        pw_re = jnp.exp(lag * lr * dt) * jnp.cos(lag * li * dt)
        pw_im = jnp.exp(lag * lr * dt) * jnp.sin(lag * li * dt)
        ab_re = pw_re[..., None] * bbar_re - pw_im[..., None] * bbar_im
        ab_im = pw_re[..., None] * bbar_im + pw_im[..., None] * bbar_re
        cr, ci = c_re[direction].astype(F32), c_im[direction].astype(F32)
        kern.append(_einsum_f32('gcp,ngpd->ngcd', cr, ab_re[:ck]) - _einsum_f32('gcp,ngpd->ngcd', ci, ab_im[:ck]))
        order = jnp.arange(ck - 1, -1, -1) if direction == 0 else jnp.arange(ck)
        w_parts.append((ab_re[order], ab_im[order]))
        order = jnp.arange(1, ck + 1) if direction == 0 else jnp.arange(ck, 0, -1)
        a_r, a_i = pw_re[order], pw_im[order]
        v_from_re = jnp.einsum('gcp,jgp->gpjc', cr, a_r) - jnp.einsum('gcp,jgp->gpjc', ci, a_i)
        v_from_im = -(jnp.einsum('gcp,jgp->gpjc', cr, a_i) + jnp.einsum('gcp,jgp->gpjc', ci, a_r))
        v_parts.append((v_from_re, v_from_im))
        a_parts.append((pw_re[ck], pw_im[ck]))

    s_idx = jnp.arange(ck)[None, :, None]
    t_idx = jnp.arange(ck)[None, None, :]
    n_idx = jnp.arange(ck)[:, None, None]
    sel_f = (t_idx - s_idx == n_idx).astype(F32)
    sel_b = (s_idx - t_idx == n_idx).astype(F32)
    skip = jnp.eye(chans, dtype=F32)[None] * d_skip.reshape(groups, chans)[:, :, None]
    toep = (_einsum_f32('nst,ngcd->gsdtc', sel_f, kern[0]) + _einsum_f32('nst,ngcd->gsdtc', sel_b, kern[1])
            + jnp.einsum('st,gcd->gsdtc', jnp.eye(ck, dtype=F32), skip))
    t_mat = toep.reshape(groups, ck * chans, ck * chans)

    pad = LANES - state
    w_seg = [w_parts[0][0], w_parts[0][1], w_parts[1][0], w_parts[1][1]]
    w_stack = jnp.stack(w_seg, axis=0).transpose(2, 1, 4, 0, 3)
    w_mat = jnp.pad(w_stack, ((0, 0),) * 4 + ((0, pad),)).reshape(groups, ck * chans, 4 * LANES)

    v_seg = [v_parts[0][0], v_parts[0][1], v_parts[1][0], v_parts[1][1]]
    v_stack = jnp.stack(v_seg, axis=1).reshape(groups, 4, state, ck * chans)
    v_mat = jnp.pad(v_stack, ((0, 0), (0, 0), (0, pad), (0, 0))).reshape(groups, 4 * LANES, ck * chans)

    gpt = 2 * S5_TILE_PAIRS
    a_seg = jnp.stack([a_parts[0][0], a_parts[0][1], a_parts[1][0], a_parts[1][1]], axis=0)
    a_tbl = jnp.pad(a_seg, ((0, 0), (0, 0), (0, pad))).reshape(4, groups // gpt, gpt, LANES).transpose(1, 0, 2, 3)
    return t_mat.astype(BF16), w_mat.astype(BF16), v_mat.astype(BF16), a_tbl


def _s5c(proj3, col0, width, tables):
    bsz, seq, _ = proj3.shape
    t_mat, w_mat, v_mat, a_tbl = tables
    gpt = 2 * S5_TILE_PAIRS
    tiles = t_mat.shape[0] // gpt
    cols = t_mat.shape[1]
    sw = w_mat.shape[2]
    tc = width // tiles
    assert tc == LANES and cols == 2 * LANES
    n_chunks = seq // S5_CHUNK
    return pl.pallas_call(
        _s5c_kernel,
        grid=(tiles, bsz),
        in_specs=[
            pl.BlockSpec((1, seq, tc), lambda t, b: (b, 0, col0 // tc + t)),
            pl.BlockSpec((gpt, cols, cols), lambda t, b: (t, 0, 0)),
            pl.BlockSpec((gpt, cols, sw), lambda t, b: (t, 0, 0)),
            pl.BlockSpec((gpt, sw, cols), lambda t, b: (t, 0, 0)),
            pl.BlockSpec((1, 4, gpt, LANES), lambda t, b: (t, 0, 0, 0)),
        ],
        out_specs=pl.BlockSpec((1, seq, tc), lambda t, b: (b, 0, t)),
        out_shape=jax.ShapeDtypeStruct((bsz, seq, width), F32),
        scratch_shapes=[pltpu.VMEM((2 * gpt, n_chunks, LANES), F32),
                        pltpu.VMEM((n_chunks, gpt, sw), F32),
                        pltpu.VMEM((n_chunks, gpt, sw), F32)],
        compiler_params=_cparams(("arbitrary", "arbitrary")),
        name="s5",
    )(proj3, t_mat, w_mat, v_mat, a_tbl)


def _retention_kernel(q_ref, k_ref, v_ref, g_ref, cos_ref, sin_ref, lg_ref, gain_ref,
                      o_ref, qs_ref, ks_ref, vs_ref, decay_ref):
    seq = q_ref.shape[1]
    half = RET_HEAD_DIM // 2
    tq = RET_Q_TILE
    cos = cos_ref[...]
    sin = sin_ref[...]

    @pl.when(pl.program_id(1) == 0)
    def _():
        lg_fwd = lg_ref[0, 0:1, :]
        lg_bwd = lg_ref[0, 1:2, :]

        def fill(i, _):
            sl = pl.ds(pl.multiple_of(i * tq, tq), tq)
            t_idx = lax.broadcasted_iota(I32, (tq, seq), 0) + i * tq
            s_idx = lax.broadcasted_iota(I32, (tq, seq), 1)
            rel = (t_idx - s_idx).astype(F32)
            decay_ref[sl, :] = jnp.exp(jnp.where(rel >= 0.0, lg_fwd * rel, -lg_bwd * rel))
            return 0

        lax.fori_loop(0, seq // tq, fill, 0)

    def rot(t_ref, scale):
        t1 = t_ref[0, :, :half]
        t2 = t_ref[0, :, half:]
        return jnp.concatenate([(t1 * cos - t2 * sin) * scale, (t1 * sin + t2 * cos) * scale], axis=-1)

    qs_ref[...] = rot(q_ref, 1.0).astype(BF16)
    ks_ref[...] = rot(k_ref, RET_HEAD_DIM ** -0.5).astype(BF16)
    vs_ref[...] = v_ref[0].astype(BF16)
    gain = gain_ref[...]

    def q_tiles(i, _):
        sls = [pl.ds(pl.multiple_of((2 * i + a) * tq, tq), tq) for a in range(2)]
        scores = [_dot_nt(qs_ref[sl, :], ks_ref[...]) for sl in sls]
        ps = [(s * decay_ref[sl, :]).astype(BF16) for s, sl in zip(scores, sls)]
        outs = [jnp.dot(p, vs_ref[...], preferred_element_type=F32) for p in ps]
        for o, sl in zip(outs, sls):
            y = o * lax.rsqrt(jnp.mean(o * o, axis=-1, keepdims=True) + EPS) * gain
            g = g_ref[0, sl, :]
            o_ref[0, sl, :] = (y * (g * jax.nn.sigmoid(g))).astype(o_ref.dtype)
        return 0

    lax.fori_loop(0, seq // (2 * tq), q_tiles, 0)


def _retention(proj3, ret_gain, width):
    bsz, seq, _ = proj3.shape
    dh = RET_HEAD_DIM
    heads = width // dh
    inv_freq = ROPE_BASE ** (-jnp.arange(0, dh, 2, dtype=F32) / dh)
    ang = jnp.arange(seq, dtype=F32)[:, None] * inv_freq[None, :]
    cos, sin = jnp.cos(ang), jnp.sin(ang)
    log_gamma = jnp.log1p(-jnp.exp2(-5.0 - jnp.arange(heads, dtype=F32)))
    lg = jnp.stack([log_gamma, log_gamma[::-1]], axis=1)
    lg = jnp.broadcast_to(lg[:, :, None], (heads, 2, seq))

    def col(k):
        return pl.BlockSpec((1, seq, dh), lambda h, b: (b, 0, k * heads + h))

    return pl.pallas_call(
        _retention_kernel,
        grid=(heads, bsz),
        in_specs=[col(0), col(1), col(2), col(3),
                  pl.BlockSpec((seq, dh // 2), lambda h, b: (0, 0)),
                  pl.BlockSpec((seq, dh // 2), lambda h, b: (0, 0)),
                  pl.BlockSpec((1, 2, seq), lambda h, b: (h, 0, 0)),
                  pl.BlockSpec((1, dh), lambda h, b: (0, h))],
        out_specs=pl.BlockSpec((1, seq, dh), lambda h, b: (b, 0, h)),
        out_shape=jax.ShapeDtypeStruct((bsz, seq, width), BF16),
        scratch_shapes=[pltpu.VMEM((seq, dh), BF16)] * 3 + [pltpu.VMEM((seq, seq), F32)],
        compiler_params=_cparams(("arbitrary", "arbitrary")),
        name="retention",
    )(proj3, proj3, proj3, proj3, cos, sin, lg, ret_gain.reshape(1, width))


def _fnet_kernel(x_ref, cc_ref, sc_ref, cs_ref, ss_ref, o_ref, a1_ref, a2_ref):
    @pl.when(pl.program_id(1) == 0)
    def _():
        x = x_ref[0].astype(BF16)
        a1_ref[...] = jnp.dot(x, cc_ref[...], preferred_element_type=F32).astype(BF16)
        a2_ref[...] = jnp.dot(x, sc_ref[...], preferred_element_type=F32).astype(BF16)

    y = (jnp.dot(cs_ref[...], a1_ref[...], preferred_element_type=F32)
         - jnp.dot(ss_ref[...], a2_ref[...], preferred_element_type=F32))
    o_ref[0] = y.astype(o_ref.dtype)


def _dft_mats(n):
    idx = jnp.arange(n, dtype=I32)
    ang = (2.0 * math.pi / n) * ((idx[:, None] * idx[None, :]) % n).astype(F32)
    scale = n ** -0.5
    return jnp.cos(ang) * scale, jnp.sin(ang) * scale


def _fnet(proj3, col0, width):
    bsz, seq, _ = proj3.shape
    gw = width // FNET_GROUPS
    cs, ss = _dft_mats(seq)
    cg, sg = _dft_mats(gw)
    eye = jnp.eye(FNET_GROUPS, dtype=F32)
    cc = jnp.kron(eye, cg)
    sc = jnp.kron(eye, sg)
    tr = FNET_ROW_TILE
    return pl.pallas_call(
        _fnet_kernel,
        grid=(bsz, seq // tr),
        in_specs=[
            pl.BlockSpec((1, seq, width), lambda b, i: (b, 0, col0 // width)),
            pl.BlockSpec((width, width), lambda b, i: (0, 0)),
            pl.BlockSpec((width, width), lambda b, i: (0, 0)),
            pl.BlockSpec((tr, seq), lambda b, i: (i, 0)),
            pl.BlockSpec((tr, seq), lambda b, i: (i, 0)),
        ],
        out_specs=pl.BlockSpec((1, tr, width), lambda b, i: (b, i, 0)),
        out_shape=jax.ShapeDtypeStruct((bsz, seq, width), BF16),
        scratch_shapes=[pltpu.VMEM((seq, width), BF16)] * 2,
        compiler_params=_cparams(("arbitrary", "arbitrary")),
        name="fnet",
    )(proj3, cc.astype(BF16), sc.astype(BF16), cs.astype(BF16), ss.astype(BF16))


HIGH_HALF = -65536


def _slab_rows(s, rows, per_row):
    return pl.ds(s, rows, stride=per_row)


def _pack_rows(h, slab_ref):
    rows, d = h.shape
    half = d // 2
    per_row = half // LANES
    for s in range(per_row):
        lo = h[:, s * LANES:(s + 1) * LANES].astype(BF16).astype(F32)
        hi = h[:, half + s * LANES:half + (s + 1) * LANES].astype(BF16).astype(F32)
        lo_bits = lax.shift_right_logical(lax.bitcast_convert_type(lo, I32), jnp.int32(16))
        hi_bits = lax.bitcast_convert_type(hi, I32) & jnp.int32(HIGH_HALF)
        slab_ref[_slab_rows(s, rows, per_row), :] = hi_bits | lo_bits


def _unpack_rows(slab_ref, per_row):
    rows = slab_ref.shape[0] // per_row
    los, his = [], []
    for s in range(per_row):
        w = slab_ref[_slab_rows(s, rows, per_row), :]
        los.append(lax.bitcast_convert_type(lax.shift_left(w, jnp.int32(16)), F32))
        his.append(lax.bitcast_convert_type(w & jnp.int32(HIGH_HALF), F32))
    return jnp.concatenate(los, axis=-1).astype(BF16), jnp.concatenate(his, axis=-1).astype(BF16)


def _outproj_kernel(*refs, glu):
    if glu:
        (x_ref, a_ref, b_ref, gw_ref, gb_ref, wa_ref, wb_ref, gm_ref, gain_ref, sh_ref, sc_ref,
         rwh_ref, rwl_ref, rb_ref, x1_ref, h_ref, idx_ref, gate_ref, cnt_ref, base_ref) = refs
    else:
        (x_ref, a_ref, b_ref, wa_ref, wb_ref, gm_ref, gain_ref, sh_ref, sc_ref,
         rwh_ref, rwl_ref, rb_ref, x1_ref, h_ref, idx_ref, gate_ref, cnt_ref, base_ref) = refs

    @pl.when(pl.program_id(0) == 0)
    def _():
        base_ref[...] = jnp.zeros(base_ref.shape, F32)

    bm = b_ref[...]
    if glu:
        gl = jnp.dot(bm.astype(BF16), gw_ref[...], preferred_element_type=F32) + gb_ref[...]
        bm = (bm.astype(F32) * jax.nn.sigmoid(gl)).astype(BF16)
    y = (jnp.dot(a_ref[...], wa_ref[...], preferred_element_type=F32)
         + jnp.dot(bm, wb_ref[...], preferred_element_type=F32))
    x1 = x_ref[...] + gm_ref[0] * y
    x1_ref[...] = x1
    h = _norm_modulate(x1, gain_ref[...], sh_ref[0], sc_ref[0])
    _pack_rows(h, h_ref)

    h_hi = h.astype(BF16)
    h_lo = (h - h_hi.astype(F32)).astype(BF16)
    logits = (jnp.dot(h_hi, rwh_ref[...], preferred_element_type=F32)
              + jnp.dot(h_lo, rwh_ref[...], preferred_element_type=F32)
              + jnp.dot(h_hi, rwl_ref[...], preferred_element_type=F32)
              + rb_ref[...])
    lane = lax.broadcasted_iota(I32, logits.shape, 1)
    vals, idxs = [], []
    for _ in range(TOP_K):
        m = jnp.max(logits, axis=-1, keepdims=True)
        ik = jnp.min(jnp.where(logits == m, lane, ROUTER_PAD), axis=-1, keepdims=True)
        vals.append(m)
        idxs.append(ik)
        logits = jnp.where(lane == ik, -jnp.inf, logits)
    exps = [jnp.exp(v - vals[0]) for v in vals]
    denom = exps[0] + exps[1] + exps[2] + exps[3]

    tm = lane.shape[0]
    onehot = jnp.zeros(lane.shape, F32)
    for k in range(TOP_K):
        onehot = onehot + (lane == idxs[k]).astype(F32)
    before = (lax.broadcasted_iota(I32, (tm, tm), 1) < lax.broadcasted_iota(I32, (tm, tm), 0)).astype(BF16)
    count = jnp.dot(before, onehot.astype(BF16), preferred_element_type=F32) + base_ref[...]
    new_base = base_ref[...] + jnp.sum(onehot, axis=0, keepdims=True)
    base_ref[...] = new_base
    cnt_ref[...] = new_base.astype(I32)

    idx_out = jnp.zeros(lane.shape, I32)
    gate_out = jnp.zeros(lane.shape, F32)
    for k in range(TOP_K):
        rank = jnp.sum(jnp.where(lane == idxs[k], count, 0.0), axis=-1, keepdims=True).astype(I32)
        idx_out = jnp.where(lane == k, idxs[k], idx_out)
        idx_out = jnp.where(lane == TOP_K + k, rank, idx_out)
        gate_out = jnp.where(lane == k, exps[k] / denom, gate_out)
    idx_ref[...] = idx_out
    gate_ref[...] = gate_out


def _out_proj(x2d, mix_a, mix_b, w_out, g_mix, gain, shift, scale, router_w, router_b, seq,
              glu_w=None, glu_b=None):
    t, d = x2d.shape
    wa_rows = mix_a.shape[1]
    wb_rows = mix_b.shape[1]
    tm = 256
    per_seq = seq // tm
    n_exp = router_w.shape[1]
    rw = jnp.zeros((d, ROUTER_PAD), F32).at[:, :n_exp].set(router_w)
    rw_hi = rw.astype(BF16)
    rw_lo = (rw - rw_hi.astype(F32)).astype(BF16)
    rb = jnp.full((1, ROUTER_PAD), NEG_BIG, F32).at[0, :n_exp].set(router_b)
    w_bf = w_out.astype(BF16)
    glu = glu_w is not None

    def rows(width):
        return pl.BlockSpec((tm, width), lambda i: (i, 0))

    def full(r, c):
        return pl.BlockSpec((r, c), lambda i: (0, 0))

    def per_batch():
        return pl.BlockSpec((1, 1, d), lambda i: (i // per_seq, 0, 0))

    in_specs = [rows(d), rows(wa_rows), rows(wb_rows)]
    args = [x2d, mix_a, mix_b]
    if glu:
        in_specs += [full(wb_rows, wb_rows), full(1, wb_rows)]
        args += [glu_w.astype(BF16), glu_b.reshape(1, wb_rows)]
    in_specs += [pl.BlockSpec((wa_rows, d), lambda i: (0, 0)),
                 pl.BlockSpec((wb_rows, d), lambda i: (wa_rows // wb_rows, 0)),
                 per_batch(), full(1, d), per_batch(), per_batch(),
                 full(d, ROUTER_PAD), full(d, ROUTER_PAD), full(1, ROUTER_PAD)]
    args += [w_bf, w_bf, g_mix, gain.reshape(1, d), shift, scale, rw_hi, rw_lo, rb]
    return pl.pallas_call(
        functools.partial(_outproj_kernel, glu=glu),
        grid=(t // tm,),
        in_specs=in_specs,
        out_specs=[rows(d), pl.BlockSpec((tm * (d // (2 * LANES)), LANES), lambda i: (i, 0)),
                   rows(ROUTER_PAD), rows(ROUTER_PAD), full(1, ROUTER_PAD)],
        out_shape=[jax.ShapeDtypeStruct((t, d), F32), jax.ShapeDtypeStruct((t * (d // (2 * LANES)), LANES), I32),
                   jax.ShapeDtypeStruct((t, ROUTER_PAD), I32), jax.ShapeDtypeStruct((t, ROUTER_PAD), F32),
                   jax.ShapeDtypeStruct((1, ROUTER_PAD), I32)],
        scratch_shapes=[pltpu.VMEM((1, ROUTER_PAD), F32)],
        compiler_params=_cparams(("arbitrary",)),
        name="out_proj_glu" if glu else "out_proj",
    )(*args)


def _routing_tables(top_idx, rank, counts, n_tok):
    n_assign = n_tok * TOP_K
    r = MOE_ROWS
    n_rb = n_assign // r + N_EXPERTS
    experts = jnp.arange(N_EXPERTS, dtype=I32)
    nblk = (counts + r - 1) // r
    blk_end = jnp.cumsum(nblk)
    blk_start = blk_end - nblk
    row_start = jnp.sum(jnp.where(top_idx[:, :, None] == experts[None, None, :],
                                  (blk_start * r)[None, None, :], 0), axis=-1)
    dest = (row_start + rank).reshape(-1).astype(I32)
    n_valid_blocks = blk_end[-1]
    rb = jnp.arange(n_rb, dtype=I32)
    rb_e = jnp.sum((blk_end[None, :] <= rb[:, None]).astype(I32), axis=1)
    rb_e = jnp.minimum(rb_e, N_EXPERTS - 1).astype(I32)
    tail = jnp.where(nblk > 0, blk_end - 1, -1)
    spare = n_valid_blocks + experts
    spare = jnp.where(spare < n_rb, spare, -1)
    zero_blocks = jnp.concatenate([tail, spare]).astype(I32)
    return dict(dest=dest, rb_e=rb_e, zero_blocks=zero_blocks,
                n_valid_blocks=n_valid_blocks.reshape(1).astype(I32),
                nblk=nblk.astype(I32), blk_start=blk_start.astype(I32), n_rb=n_rb, n_slots=n_rb * r)


def _dispatch_kernel(dest_ref, zb_ref, h_ref, xs_hbm, zero_ref, zsem_ref, sem_ref):
    i = pl.program_id(0)
    td = DISPATCH_TOKENS
    sr = h_ref.shape[0] // td
    blk = MOE_ROWS * sr

    def slab(ref, row):
        return ref.at[pl.ds(pl.multiple_of(row * sr, sr), sr), :]

    @pl.when(i == 0)
    def _():
        zero_ref[...] = jnp.zeros(zero_ref.shape, zero_ref.dtype)

        def zero_copy(z):
            row0 = pl.multiple_of(zb_ref[z] * blk, blk)
            return pltpu.make_async_copy(zero_ref, xs_hbm.at[pl.ds(row0, blk), :], zsem_ref.at[0])

        def start(z, _):
            @pl.when(zb_ref[z] >= 0)
            def _():
                zero_copy(z).start()
            return 0

        def wait(z, _):
            @pl.when(zb_ref[z] >= 0)
            def _():
                zero_copy(z).wait()
            return 0

        lax.fori_loop(0, zb_ref.shape[0], start, 0)
        lax.fori_loop(0, zb_ref.shape[0], wait, 0)

    def group(g, _):
        for s in range(DMA_UNROLL):
            t = g * DMA_UNROLL + s
            for k in range(TOP_K):
                dst = dest_ref[(i * td + t) * TOP_K + k]
                pltpu.make_async_copy(slab(h_ref, t), slab(xs_hbm, dst), sem_ref.at[0]).start()
        return 0

    lax.fori_loop(0, td // DMA_UNROLL, group, 0)
    for _ in range(TOP_K):
        pltpu.make_async_copy(h_ref, xs_hbm.at[pl.ds(0, td * sr), :], sem_ref.at[0]).wait()


def _moe_dispatch(h_slabs, n_tok, rt):
    sr = h_slabs.shape[0] // n_tok
    td = DISPATCH_TOKENS
    grid_spec = pltpu.PrefetchScalarGridSpec(
        num_scalar_prefetch=2,
        grid=(n_tok // td,),
        in_specs=[pl.BlockSpec((td * sr, LANES), lambda i, dst, zb: (i, 0))],
        out_specs=pl.BlockSpec(memory_space=pl.ANY),
        scratch_shapes=[pltpu.VMEM((MOE_ROWS * sr, LANES), h_slabs.dtype), pltpu.SemaphoreType.DMA((1,)),
                        pltpu.SemaphoreType.DMA((1,))],
    )
    return pl.pallas_call(
        _dispatch_kernel,
        grid_spec=grid_spec,
        out_shape=jax.ShapeDtypeStruct((rt['n_slots'] * sr, LANES), h_slabs.dtype),
        compiler_params=_cparams(("arbitrary",)),
        name="moe_dispatch",
    )(rt['dest'], rt['zero_blocks'], h_slabs)


def _cast_rows(src_ref, dst_ref, chunk=256):
    chunk = min(chunk, dst_ref.shape[0])
    n = dst_ref.shape[0] // chunk

    def body(c, _):
        sl = pl.ds(pl.multiple_of(c * chunk, chunk), chunk)
        dst_ref[sl, :] = src_ref[sl, :].astype(dst_ref.dtype)
        return 0

    lax.fori_loop(0, n, body, 0)


def _stagger_tables(group, cols, parts):
    n = group.shape[0]
    idx = jnp.arange(n, dtype=I32)
    is_first = jnp.concatenate([jnp.ones((1,), bool), group[1:] != group[:-1]])
    first_cur = lax.cummax(jnp.where(is_first, idx, 0))
    nxt = jnp.where(is_first, idx, n)
    first_next = jnp.concatenate([lax.cummin(nxt[::-1])[::-1][1:], jnp.full((1,), n, I32)])
    out = [[] for _ in cols]
    for p in range(parts):
        switch = (idx >= jnp.maximum(first_next - p, first_cur + 1)) & (first_next < n)
        eff = jnp.where(switch, first_next, idx)
        for k, c in enumerate(cols):
            out[k].append(c[eff])
    return [jnp.stack(o) for o in out]


def _moe_up_kernel(rb_ref, j_ref, e_ref, first_ref, valid_ref, pe_ref, pj_ref, x_ref, *refs):
    parts = MOE_W_PARTS
    wg_refs = refs[:parts]
    wu_refs = refs[parts:2 * parts]
    bg_ref, bu_ref, o_ref, wgs_ref, wus_ref = refs[2 * parts:]
    i = pl.program_id(0)
    rows = wgs_ref.shape[0] // parts

    @pl.when(first_ref[i] == 1)
    def _():
        for p in range(parts):
            _cast_rows(wg_refs[p].at[0, 0], wgs_ref.at[pl.ds(p * rows, rows)])
            _cast_rows(wu_refs[p].at[0, 0], wus_ref.at[pl.ds(p * rows, rows)])

    @pl.when(valid_ref[i] == 1)
    def _():
        x_lo, x_hi = _unpack_rows(x_ref, x_ref.shape[0] // MOE_ROWS)
        x = jnp.concatenate([x_lo, x_hi], axis=-1)
        cw = MOE_UP_COLS
        for c in range(o_ref.shape[1] // cw):
            cs = slice(c * cw, (c + 1) * cw)
            gate = jnp.dot(x, wgs_ref[:, cs], preferred_element_type=F32) + bg_ref[0, :, cs]
            up = jnp.dot(x, wus_ref[:, cs], preferred_element_type=F32) + bu_ref[0, :, cs]
            gate = jnp.minimum(gate, SWIGLU_LIMIT)
            up = jnp.clip(up, -SWIGLU_LIMIT, SWIGLU_LIMIT)
            act = (up + 1.0) * gate * jax.nn.sigmoid(SWIGLU_ALPHA * gate)
            o_ref[:, cs] = act.astype(o_ref.dtype)

    @pl.when(valid_ref[i] == 0)
    def _():
        o_ref[...] = jnp.zeros(o_ref.shape, o_ref.dtype)


def _moe_up(x_sorted, w_gu, b_gu, layer, rt):
    _, n_exp, d, two_f = w_gu.shape
    n_slots = rt['n_slots']
    sr = x_sorted.shape[0] // n_slots
    f = two_f // 2
    r, tn = MOE_ROWS, MOE_UP_TN
    nj = f // tn
    n_rb = rt['n_rb']
    n_items = n_rb * nj
    rb = jnp.arange(n_rb, dtype=I32)
    e_of = rt['rb_e']
    nvb = rt['n_valid_blocks'][0]
    nblk = rt['nblk'].at[n_exp - 1].add(n_rb - nvb)
    q = rb - rt['blk_start'][e_of]
    pos = (nj * rt['blk_start'][e_of][:, None]
           + jnp.arange(nj, dtype=I32)[None, :] * nblk[e_of][:, None] + q[:, None]).reshape(-1)
    rb2 = jnp.broadcast_to(rb[:, None], (n_rb, nj)).reshape(-1)
    j2 = jnp.broadcast_to(jnp.arange(nj, dtype=I32)[None, :], (n_rb, nj)).reshape(-1)
    it_rb = jnp.zeros((n_items,), I32).at[pos].set(rb2)
    it_j = jnp.zeros((n_items,), I32).at[pos].set(j2)
    it_valid = (it_rb < nvb).astype(I32)
    it_e = e_of[it_rb]
    prev_e = jnp.concatenate([jnp.full((1,), -1, I32), it_e[:-1]])
    prev_j = jnp.concatenate([jnp.full((1,), -1, I32), it_j[:-1]])
    is_first = (it_e != prev_e) | (it_j != prev_j)
    it_first = is_first.astype(I32)
    parts = MOE_W_PARTS
    part_e, part_j = _stagger_tables(jnp.cumsum(it_first), [it_e, it_j], parts)

    def w_spec(p, col0):
        return pl.BlockSpec((1, 1, d // parts, tn),
                            lambda i, rbt, jt, et, ft, vt, pe, pj: (layer, pe[p, i], p, col0 + pj[p, i]))

    grid_spec = pltpu.PrefetchScalarGridSpec(
        num_scalar_prefetch=7,
        grid=(n_items,),
        in_specs=([pl.BlockSpec((r * sr, LANES), lambda i, rbt, jt, et, ft, vt, pe, pj: (rbt[i], 0))]
                  + [w_spec(p, 0) for p in range(parts)] + [w_spec(p, nj) for p in range(parts)]
                  + [pl.BlockSpec((1, 1, tn), lambda i, rbt, jt, et, ft, vt, pe, pj: (et[i], 0, jt[i])),
                     pl.BlockSpec((1, 1, tn), lambda i, rbt, jt, et, ft, vt, pe, pj: (et[i], 0, nj + jt[i]))]),
        out_specs=pl.BlockSpec((r, tn), lambda i, rbt, jt, et, ft, vt, pe, pj: (rbt[i], jt[i])),
        scratch_shapes=[pltpu.VMEM((d, tn), BF16), pltpu.VMEM((d, tn), BF16)],
    )
    b3 = b_gu.reshape(n_exp, 1, two_f)
    return pl.pallas_call(
        _moe_up_kernel,
        grid_spec=grid_spec,
        out_shape=jax.ShapeDtypeStruct((n_slots, f), BF16),
        compiler_params=_cparams(("arbitrary",)),
        name="moe_up",
    )(it_rb, it_j, it_e, it_first, it_valid, part_e, part_j, x_sorted,
      *([w_gu] * (2 * parts)), b3, b3)


def _moe_dn_kernel(e_ref, nvb_ref, pe_ref, a_ref, *refs):
    parts = MOE_W_PARTS
    w_refs = refs[:parts]
    b_ref, o_ref, ws_ref = refs[parts:]
    i = pl.program_id(0)
    valid = i < nvb_ref[0]
    e = e_ref[i]
    prev_e = e_ref[jnp.maximum(i - 1, 0)]
    rows = ws_ref.shape[0] // parts

    @pl.when(valid & ((i == 0) | (e != prev_e)))
    def _():
        for p in range(parts):
            _cast_rows(w_refs[p].at[0, 0], ws_ref.at[pl.ds(p * rows, rows)])

    @pl.when(valid)
    def _():
        o_ref[...] = jnp.dot(a_ref[...], ws_ref[...], preferred_element_type=F32) + b_ref[0]

    @pl.when(jnp.logical_not(valid))
    def _():
        o_ref[...] = jnp.zeros(o_ref.shape, o_ref.dtype)


def _moe_dn(act, w_dn, b_dn, layer, rt):
    n_slots, f = act.shape
    _, n_exp, _, d = w_dn.shape
    r = MOE_ROWS
    parts = MOE_W_PARTS
    (part_e,) = _stagger_tables(rt['rb_e'], [rt['rb_e']], parts)

    def w_spec(p):
        return pl.BlockSpec((1, 1, f // parts, d), lambda i, et, nvb, pe: (layer, pe[p, i], p, 0))

    grid_spec = pltpu.PrefetchScalarGridSpec(
        num_scalar_prefetch=3,
        grid=(rt['n_rb'],),
        in_specs=([pl.BlockSpec((r, f), lambda i, et, nvb, pe: (jnp.minimum(i, nvb[0] - 1), 0))]
                  + [w_spec(p) for p in range(parts)]
                  + [pl.BlockSpec((1, 1, d), lambda i, et, nvb, pe: (et[i], 0, 0))]),
        out_specs=pl.BlockSpec((r, d), lambda i, et, nvb, pe: (i, 0)),
        scratch_shapes=[pltpu.VMEM((f, d), BF16)],
    )
    return pl.pallas_call(
        _moe_dn_kernel,
        grid_spec=grid_spec,
        out_shape=jax.ShapeDtypeStruct((n_slots, d), F32),
        compiler_params=_cparams(("arbitrary",)),
        name="moe_dn",
    )(rt['rb_e'], rt['n_valid_blocks'], part_e, act, *([w_dn] * parts), b_dn.reshape(n_exp, 1, d))


def _combine_kernel(dest_ref, x_ref, gate_ref, g_ref, fg_ref, y_hbm, o_ref, buf_ref, sem_ref, *, final):
    i = pl.program_id(0)
    n_steps = pl.num_programs(0)
    tc = COMBINE_TOKENS

    def issue(step, slot):
        def group(g, _):
            for s in range(DMA_UNROLL):
                t = g * DMA_UNROLL + s
                for k in range(TOP_K):
                    src = dest_ref[(step * tc + t) * TOP_K + k]
                    pltpu.make_async_copy(y_hbm.at[pl.ds(src, 1), :], buf_ref.at[slot, k, pl.ds(t, 1), :],
                                          sem_ref.at[slot]).start()
            return 0

        lax.fori_loop(0, tc // DMA_UNROLL, group, 0)

    @pl.when(i == 0)
    def _():
        issue(0, 0)

    @pl.when(i + 1 < n_steps)
    def _():
        issue(i + 1, (i + 1) % 2)

    slot = i % 2
    for k in range(TOP_K):
        pltpu.make_async_copy(y_hbm.at[pl.ds(0, tc), :], buf_ref.at[slot, k], sem_ref.at[slot]).wait()
    gates = gate_ref[...]
    y = gates[:, 0:1] * buf_ref[slot, 0]
    for k in range(1, TOP_K):
        y = y + gates[:, k:k + 1] * buf_ref[slot, k]
    x2 = x_ref[...] + g_ref[0] * y
    if final:
        x2 = x2 * lax.rsqrt(jnp.mean(x2 * x2, axis=-1, keepdims=True) + EPS) * fg_ref[...]
    o_ref[...] = x2


def _combine(x1, y_sorted, gates, rt, g_ffn, final_gain, seq, final):
    t, d = x1.shape
    tc = COMBINE_TOKENS
    per_seq = seq // tc
    grid_spec = pltpu.PrefetchScalarGridSpec(
        num_scalar_prefetch=1,
        grid=(t // tc,),
        in_specs=[pl.BlockSpec((tc, d), lambda i, dst: (i, 0)),
                  pl.BlockSpec((tc, ROUTER_PAD), lambda i, dst: (i, 0)),
                  pl.BlockSpec((1, 1, d), lambda i, dst: (i // per_seq, 0, 0)),
                  pl.BlockSpec((1, d), lambda i, dst: (0, 0)),
                  pl.BlockSpec(memory_space=pl.ANY)],
        out_specs=pl.BlockSpec((tc, d), lambda i, dst: (i, 0)),
        scratch_shapes=[pltpu.VMEM((2, TOP_K, tc, d), F32), pltpu.SemaphoreType.DMA((2,))],
    )
    return pl.pallas_call(
        functools.partial(_combine_kernel, final=final),
        grid_spec=grid_spec,
        out_shape=jax.ShapeDtypeStruct((t, d), F32),
        compiler_params=_cparams(("arbitrary",)),
        name="combine_final" if final else "combine",
    )(rt['dest'], x1, gates, g_ffn, final_gain.reshape(1, d), y_sorted)


def _moe(x1, h, idx, gates, counts, w_gu, b_gu, w_dn, b_dn, layer, g_ffn, final_gain, seq, final):
    n_tok = x1.shape[0]
    rt = _routing_tables(idx[:, :TOP_K], idx[:, TOP_K:2 * TOP_K], counts[0, :N_EXPERTS], n_tok)
    x_sorted = _moe_dispatch(h, n_tok, rt)
    act = _moe_up(x_sorted, w_gu, b_gu, layer, rt)
    y_sorted = _moe_dn(act, w_dn, b_dn, layer, rt)
    return _combine(x1, y_sorted, gates, rt, g_ffn, final_gain, seq, final)


def kernel(x, c, ada_w, ada_b, norm_mix_gain, norm_ffn_gain, ab_w_in, ab_w_out, hg_lb_logits, hg_norm_gain, s5_lam_re, s5_lam_im, s5_log_dt, s5_b_re, s5_b_im, s5_c_re, s5_c_im, s5_d, s5_glu_w, s5_glu_b, cd_w_in, cd_w_out, ret_norm_gain, router_w, router_b, moe_w_gu, moe_b_gu, moe_w_dn, moe_b_dn, final_gain):
    bsz, seq, d = x.shape
    depth = ada_w.shape[0]
    n_tok = bsz * seq
    hg_width = hg_lb_logits.shape[1]
    s5_width = s5_glu_w.shape[1]
    ret_width = ret_norm_gain.shape[1]
    fnet_width = cd_w_out.shape[1] - ret_width

    lower_bounds = jnp.cumsum(jax.nn.softmax(hg_lb_logits.astype(F32), axis=0), axis=0)
    mod = _ada_mod(c, ada_w, ada_b)
    xr = x.reshape(n_tok, d)
    for layer in range(depth):
        sh_mix, sc_mix, g_mix, sh_ffn, sc_ffn, g_ffn = (
            mod[layer, :, k * d:(k + 1) * d].reshape(bsz, 1, d) for k in range(6))
        j = layer // 2
        if layer % 2 == 0:
            proj = _in_proj(xr, norm_mix_gain[layer], sh_mix, sc_mix, ab_w_in[j].astype(BF16), seq)
            proj3 = proj.reshape(bsz, seq, proj.shape[1])
            mix_a = _hgrn2(proj3, lower_bounds[j], hg_norm_gain[j], hg_width)
            tables = _s5c_tables(s5_lam_re[j], s5_lam_im[j], s5_log_dt[j], s5_b_re[j], s5_b_im[j],
                                 s5_c_re[j], s5_c_im[j], s5_d[j])
            mix_b = _s5c(proj3, 5 * hg_width, s5_width, tables)
            x1, h, idx, gate, counts = _out_proj(
                xr, mix_a.reshape(n_tok, hg_width), mix_b.reshape(n_tok, s5_width), ab_w_out[j],
                g_mix, norm_ffn_gain[layer], sh_ffn, sc_ffn, router_w[layer], router_b[layer], seq,
                glu_w=s5_glu_w[j], glu_b=s5_glu_b[j])
        else:
            proj = _in_proj(xr, norm_mix_gain[layer], sh_mix, sc_mix, cd_w_in[j].astype(BF16), seq)
            proj3 = proj.reshape(bsz, seq, proj.shape[1])
            mix_a = _retention(proj3, ret_norm_gain[j], ret_width)
            mix_b = _fnet(proj3, 4 * ret_width, fnet_width)
            x1, h, idx, gate, counts = _out_proj(
                xr, mix_a.reshape(n_tok, ret_width), mix_b.reshape(n_tok, fnet_width), cd_w_out[j],
                g_mix, norm_ffn_gain[layer], sh_ffn, sc_ffn, router_w[layer], router_b[layer], seq)
        xr = _moe(x1, h, idx, gate, counts, moe_w_gu, moe_b_gu[layer], moe_w_dn, moe_b_dn[layer],
                  layer, g_ffn, final_gain, seq, final=(layer == depth - 1))
    return xr.reshape(bsz, seq, d)
```

```python
import functools
import math

import jax
import jax.numpy as jnp
from jax import lax
from jax.experimental import pallas as pl
from jax.experimental.pallas import tpu as pltpu

F32 = jnp.float32
BF16 = jnp.bfloat16
I32 = jnp.int32

EPS = 1e-6
LANES = 128
SUBLANES = 8
VMEM_LIMIT = 56 * 1024 * 1024

HG_HEAD_DIM = 128
HG_CHUNK = 64
HG_GROUP = 256
HG_EXP_CLAMP = 80.0

S5_GROUP = 16
S5_STATE = 64
S5_CHUNK = 16
S5_UNROLL = 8
S5_TILE_PAIRS = 4
S5_ROW_SPLIT = 4

RET_HEAD_DIM = 256
RET_Q_TILE = 256
ROPE_BASE = 10000.0

FNET_GROUPS = 4
FNET_ROW_TILE = 512

N_EXPERTS = 32
TOP_K = 4
SWIGLU_LIMIT = 7.0
SWIGLU_ALPHA = 1.702
MOE_ROWS = 256
MOE_UP_TN = 1024
MOE_W_PARTS = 4
DISPATCH_TOKENS = 512
COMBINE_TOKENS = 256
DMA_UNROLL = 8
ROUTER_PAD = LANES
NEG_BIG = -1e30


def _cparams(semantics):
    return pltpu.CompilerParams(dimension_semantics=semantics, vmem_limit_bytes=VMEM_LIMIT)


def _ada_kernel(c_ref, w_ref, b_ref, o_ref):
    c = c_ref[...]
    cond = c * jax.nn.sigmoid(c)
    o_ref[0] = jnp.dot(cond.astype(BF16), w_ref[0].astype(BF16),
                       preferred_element_type=F32) + b_ref[0]


def _ada_mod(c, ada_w, ada_b):
    depth, d, n = ada_w.shape
    bsz = c.shape[0]
    tn = 1024
    return pl.pallas_call(
        _ada_kernel,
        grid=(depth, n // tn),
        in_specs=[
            pl.BlockSpec((bsz, d), lambda l, j: (0, 0)),
            pl.BlockSpec((1, d, tn), lambda l, j: (l, 0, j)),
            pl.BlockSpec((1, 1, tn), lambda l, j: (l, 0, j)),
        ],
        out_specs=pl.BlockSpec((1, bsz, tn), lambda l, j: (l, 0, j)),
        out_shape=jax.ShapeDtypeStruct((depth, bsz, n), F32),
        compiler_params=_cparams(("arbitrary", "arbitrary")),
        name="ada_mod",
    )(c, ada_w, ada_b.reshape(depth, 1, n))


def _norm_modulate(x, gain, shift, scale):
    ms = jnp.mean(x * x, axis=-1, keepdims=True)
    y = x * lax.rsqrt(ms + EPS) * gain
    return y * (1.0 + scale) + shift


def _inproj_kernel(x_ref, gain_ref, sh_ref, sc_ref, w_ref, o_ref, h_ref):
    @pl.when(pl.program_id(1) == 0)
    def _():
        rows = 128

        def slab(c, _):
            sl = pl.ds(pl.multiple_of(c * rows, rows), rows)
            h_ref[sl, :] = _norm_modulate(x_ref[sl, :], gain_ref[...], sh_ref[0], sc_ref[0]).astype(BF16)
            return 0

        lax.fori_loop(0, h_ref.shape[0] // rows, slab, 0)

    o_ref[...] = jnp.dot(h_ref[...], w_ref[...], preferred_element_type=F32)


def _in_proj(x2d, gain, shift, scale, w_bf16, seq):
    t, d = x2d.shape
    n = w_bf16.shape[1]
    tm = 1024
    tn = 1024 if n % 1024 == 0 else 512
    per_seq = seq // tm
    return pl.pallas_call(
        _inproj_kernel,
        grid=(t // tm, n // tn),
        in_specs=[
            pl.BlockSpec((tm, d), lambda i, j: (i, 0)),
            pl.BlockSpec((1, d), lambda i, j: (0, 0)),
            pl.BlockSpec((1, 1, d), lambda i, j: (i // per_seq, 0, 0)),
            pl.BlockSpec((1, 1, d), lambda i, j: (i // per_seq, 0, 0)),
            pl.BlockSpec((d, tn), lambda i, j: (0, j)),
        ],
        out_specs=pl.BlockSpec((tm, tn), lambda i, j: (i, j)),
        out_shape=jax.ShapeDtypeStruct((t, n), F32),
        scratch_shapes=[pltpu.VMEM((tm, d), BF16)],
        compiler_params=_cparams(("arbitrary", "arbitrary")),
        name="in_proj",
    )(x2d, gain.reshape(1, d), shift, scale, w_bf16)


def _split3(a):
    hi = a.astype(BF16)
    r1 = a - hi.astype(F32)
    mid = r1.astype(BF16)
    lo = (r1 - mid.astype(F32)).astype(BF16)
    return hi, mid, lo


def _tri_sum(tri, a):
    hi, mid, lo = _split3(a)
    return (jnp.dot(tri, hi, preferred_element_type=F32)
            + jnp.dot(tri, mid, preferred_element_type=F32)
            + jnp.dot(tri, lo, preferred_element_type=F32))


def _dot_nt(a, b):
    return lax.dot_general(a, b, (((1,), (1,)), ((), ())), preferred_element_type=F32)


def _dot_tn(a, b):
    return lax.dot_general(a, b, (((0,), (0,)), ((), ())), preferred_element_type=F32)


def _hgrn2_kernel(q_ref, zf_ref, zb_ref, v_ref, g_ref, lb_ref, gain_ref, o_ref, acc_ref, accb_ref):
    seq = q_ref.shape[1]
    ln = HG_CHUNK
    gr = HG_GROUP
    n_groups = seq // gr
    per_group = gr // ln
    lb = lb_ref[...]
    gain = gain_ref[...]
    row = lax.broadcasted_iota(I32, (gr, gr), 0)
    col = lax.broadcasted_iota(I32, (gr, gr), 1)
    chunk_lo = (row // ln) * ln
    chunk_hi = chunk_lo + ln
    lower_incl = (col <= row) & (col >= chunk_lo)
    upper_strict = (col > row) & (col < chunk_hi)
    tri_prefix = lower_incl.astype(BF16)
    tri_suffix = ((col >= row) & (col < chunk_hi)).astype(BF16)
    mid = ln // 2

    def per_chunk_rows(a, offset):
        return jnp.concatenate(
            [jnp.broadcast_to(a[j * ln + offset:j * ln + offset + 1, :], (ln, a.shape[1]))
             for j in range(per_group)], axis=0)

    rows = [slice(j * ln, (j + 1) * ln) for j in range(per_group)]

    def load(gi, z_ref, forward):
        sl = pl.ds(pl.multiple_of(gi * gr, gr), gr)
        f = lb + (1.0 - lb) * jax.nn.sigmoid(z_ref[0, sl, :])
        log_f = jnp.log(f)
        d = dict(sl=sl, forward=forward, q=q_ref[0, sl, :], v=v_ref[0, sl, :].astype(BF16), k=1.0 - f)
        if forward:
            d['cum'] = _tri_sum(tri_prefix, log_f)
        else:
            d['cum'] = _tri_sum(tri_suffix, log_f)
        return d

    def local_states(d):
        edge_off = ln - 1 if d['forward'] else 0
        kd = (d['k'] * jnp.exp(per_chunk_rows(d['cum'], edge_off) - d['cum'])).astype(BF16)
        d['local'] = [_dot_tn(d['v'][rs], kd[rs]) for rs in rows]

    def scores(d):
        ref_rows = per_chunk_rows(d['cum'], mid - 1 if d['forward'] else mid)
        qe = d['q'] * jnp.exp(jnp.minimum(d['cum'] - ref_rows, HG_EXP_CLAMP))
        ke = d['k'] * jnp.exp(jnp.minimum(ref_rows - d['cum'], HG_EXP_CLAMP))
        mask = lower_incl if d['forward'] else upper_strict
        d['scores'] = jnp.where(mask, _dot_nt(qe.astype(BF16), ke.astype(BF16)), 0.0)

    def inter(d, state_t):
        edge_off = ln - 1 if d['forward'] else 0
        qc = (d['q'] * jnp.exp(d['cum'])).astype(BF16)
        entering = [None] * per_group
        order = range(per_group) if d['forward'] else range(per_group - 1, -1, -1)
        for j in order:
            entering[j] = state_t.astype(BF16)
            edge = d['cum'][j * ln + edge_off:j * ln + edge_off + 1, :]
            state_t = state_t * jnp.exp(edge) + d['local'][j]
        d['inter'] = jnp.concatenate([_dot_nt(qc[rs], entering[j]) for j, rs in enumerate(rows)], axis=0)
        return state_t

    def body(i, states):
        st_f, st_b = states
        both = [load(i, zf_ref, True), load(n_groups - 1 - i, zb_ref, False)]
        for d in both:
            local_states(d)
        for d in both:
            scores(d)
        st_f = inter(both[0], st_f)
        st_b = inter(both[1], st_b)
        for d, ref in zip(both, (acc_ref, accb_ref)):
            ref[d['sl'], :] = jnp.dot(d['scores'].astype(BF16), d['v'], preferred_element_type=F32) + d['inter']
        return st_f, st_b

    zero = jnp.zeros((HG_HEAD_DIM, HG_HEAD_DIM), F32)
    lax.fori_loop(0, n_groups, body, (zero, zero))

    def finish(gi, _):
        sl = pl.ds(pl.multiple_of(gi * gr, gr), gr)
        o = acc_ref[sl, :] + accb_ref[sl, :]
        y = o * lax.rsqrt(jnp.mean(o * o, axis=-1, keepdims=True) + EPS) * gain
        g = g_ref[0, sl, :]
        o_ref[0, sl, :] = (y * (g * jax.nn.sigmoid(g))).astype(o_ref.dtype)
        return 0

    lax.fori_loop(0, n_groups, finish, 0)


def _hgrn2(proj3, lower_bound, hg_gain, width):
    bsz, seq, _ = proj3.shape
    heads = width // HG_HEAD_DIM
    dh = HG_HEAD_DIM

    def col(k):
        return pl.BlockSpec((1, seq, dh), lambda b, h: (b, 0, k * heads + h))

    return pl.pallas_call(
        _hgrn2_kernel,
        grid=(bsz, heads),
        in_specs=[col(0), col(1), col(2), col(3), col(4),
                  pl.BlockSpec((1, dh), lambda b, h: (0, h)),
                  pl.BlockSpec((1, dh), lambda b, h: (0, 0))],
        out_specs=pl.BlockSpec((1, seq, dh), lambda b, h: (b, 0, h)),
        out_shape=jax.ShapeDtypeStruct((bsz, seq, width), BF16),
        scratch_shapes=[pltpu.VMEM((seq, dh), F32), pltpu.VMEM((seq, dh), F32)],
        compiler_params=_cparams(("arbitrary", "arbitrary")),
        name="hgrn2",
    )(proj3, proj3, proj3, proj3, proj3, lower_bound.reshape(1, width), hg_gain.reshape(1, dh))


def _s5c_kernel(u_ref, t_ref, w_ref, v_ref, a_ref, z_ref, uc_ref, e_ref, p_ref):
    ck = S5_CHUNK
    n_chunks = u_ref.shape[1] // ck
    slab = n_chunks // S5_ROW_SPLIT
    gpt = 2 * S5_TILE_PAIRS
    half_tok = ck // 2
    lane_grp = lax.broadcasted_iota(I32, (slab, LANES), 1) // S5_GROUP

    def token_rows(rq, tok):
        return pl.ds(rq * slab * ck + tok, slab, stride=ck)

    def roll_lanes(a, groups):
        shift = (groups * S5_GROUP) % LANES
        return pltpu.roll(a, shift, 1) if shift else a

    for half in range(2):
        for rq in range(S5_ROW_SPLIT):
            toks = [u_ref.at[0][token_rows(rq, half * half_tok + sl), :] for sl in range(half_tok)]
            for gl in range(gpt):
                acc = jnp.zeros((slab, LANES), F32)
                for sl in range(half_tok):
                    acc = jnp.where(lane_grp == sl, roll_lanes(toks[sl], sl - gl), acc)
                uc_ref[gl * 2 + half, rq * slab:(rq + 1) * slab, :] = acc

    seg = [pl.ds(k * LANES, LANES) for k in range(4)]

    def group_inputs(g):
        return jnp.concatenate([uc_ref[2 * g], uc_ref[2 * g + 1]], axis=-1).astype(BF16)

    for g in range(gpt):
        e_ref[:, g, :] = jnp.dot(group_inputs(g), w_ref[g], preferred_element_type=F32)

    af_r, af_i, ab_r, ab_i = a_ref[0, 0], a_ref[0, 1], a_ref[0, 2], a_ref[0, 3]

    def cmul(ar, ai, br, bi):
        return ar * br - ai * bi, ar * bi + ai * br

    def step(m, carry):
        xr, xi, yr, yi = carry
        mb = n_chunks - 1 - m
        p_ref[m, :, seg[0]] = xr
        p_ref[m, :, seg[1]] = xi
        p_ref[mb, :, seg[2]] = yr
        p_ref[mb, :, seg[3]] = yi
        dr, di = cmul(af_r, af_i, xr, xi)
        gr, gi = cmul(ab_r, ab_i, yr, yi)
        return (dr + e_ref[m, :, seg[0]], di + e_ref[m, :, seg[1]],
                gr + e_ref[mb, :, seg[2]], gi + e_ref[mb, :, seg[3]])

    zero = jnp.zeros((gpt, LANES), F32)
    lax.fori_loop(0, n_chunks, step, (zero, zero, zero, zero), unroll=S5_UNROLL)

    for g in range(gpt):
        y = (jnp.dot(group_inputs(g), t_ref[g], preferred_element_type=F32)
             + jnp.dot(p_ref[:, g, :].astype(BF16), v_ref[g], preferred_element_type=F32))
        zt = jax.nn.gelu(y)
        for k in range(2):
            uc_ref[2 * g + k] = zt[:, k * LANES:(k + 1) * LANES]

    for half in range(2):
        for rq in range(S5_ROW_SPLIT):
            cols = [uc_ref[gl * 2 + half, rq * slab:(rq + 1) * slab, :] for gl in range(gpt)]
            for sl in range(half_tok):
                acc = jnp.zeros((slab, LANES), F32)
                for gl in range(gpt):
                    acc = jnp.where(lane_grp == gl, roll_lanes(cols[gl], gl - sl), acc)
                z_ref.at[0][token_rows(rq, half * half_tok + sl), :] = acc


_einsum_f32 = functools.partial(jnp.einsum, precision=lax.Precision.HIGHEST)


def _s5c_tables(lam_re, lam_im, log_dt, b_re, b_im, c_re, c_im, d_skip):
    groups, state = lam_re.shape[1], lam_re.shape[2]
    chans = b_re.shape[2]
    ck = S5_CHUNK
    lag = jnp.arange(ck + 1, dtype=F32)[:, None, None]
    kern, w_parts, v_parts, a_parts = [], [], [], []
    for direction in (0, 1):
        lr, li = lam_re[direction].astype(F32), lam_im[direction].astype(F32)
        dt = jnp.exp(log_dt[direction].astype(F32))[:, None]
        mag = jnp.exp(lr * dt)
        abar_re = mag * jnp.cos(li * dt)
        abar_im = mag * jnp.sin(li * dt)
        den = lr * lr + li * li
        num_re = abar_re - 1.0
        coef_re = (num_re * lr + abar_im * li) / den
        coef_im = (abar_im * lr - num_re * li) / den
        bbar_re = coef_re[..., None] * b_re - coef_im[..., None] * b_im
        bbar_im = coef_re[..., None] * b_im + coef_im[..., None] * b_re
        pw_re = jnp.exp(lag * lr * dt) * jnp.cos(lag * li * dt)
        pw_im = jnp.exp(lag * lr * dt) * jnp.sin(lag * li * dt)
        ab_re = pw_re[..., None] * bbar_re - pw_im[..., None] * bbar_im
        ab_im = pw_re[..., None] * bbar_im + pw_im[..., None] * bbar_re
        cr, ci = c_re[direction].astype(F32), c_im[direction].astype(F32)
        kern.append(_einsum_f32('gcp,ngpd->ngcd', cr, ab_re[:ck]) - _einsum_f32('gcp,ngpd->ngcd', ci, ab_im[:ck]))
        order = jnp.arange(ck - 1, -1, -1) if direction == 0 else jnp.arange(ck)
        w_parts.append((ab_re[order], ab_im[order]))
        order = jnp.arange(1, ck + 1) if direction == 0 else jnp.arange(ck, 0, -1)
        a_r, a_i = pw_re[order], pw_im[order]
        v_from_re = jnp.einsum('gcp,jgp->gpjc', cr, a_r) - jnp.einsum('gcp,jgp->gpjc', ci, a_i)
        v_from_im = -(jnp.einsum('gcp,jgp->gpjc', cr, a_i) + jnp.einsum('gcp,jgp->gpjc', ci, a_r))
        v_parts.append((v_from_re, v_from_im))
        a_parts.append((pw_re[ck], pw_im[ck]))

    s_idx = jnp.arange(ck)[None, :, None]
    t_idx = jnp.arange(ck)[None, None, :]
    n_idx = jnp.arange(ck)[:, None, None]
    sel_f = (t_idx - s_idx == n_idx).astype(F32)
    sel_b = (s_idx - t_idx == n_idx).astype(F32)
    skip = jnp.eye(chans, dtype=F32)[None] * d_skip.reshape(groups, chans)[:, :, None]
    toep = (_einsum_f32('nst,ngcd->gsdtc', sel_f, kern[0]) + _einsum_f32('nst,ngcd->gsdtc', sel_b, kern[1])
            + jnp.einsum('st,gcd->gsdtc', jnp.eye(ck, dtype=F32), skip))
    t_mat = toep.reshape(groups, ck * chans, ck * chans)

    pad = LANES - state
    w_seg = [w_parts[0][0], w_parts[0][1], w_parts[1][0], w_parts[1][1]]
    w_stack = jnp.stack(w_seg, axis=0).transpose(2, 1, 4, 0, 3)
    w_mat = jnp.pad(w_stack, ((0, 0),) * 4 + ((0, pad),)).reshape(groups, ck * chans, 4 * LANES)

    v_seg = [v_parts[0][0], v_parts[0][1], v_parts[1][0], v_parts[1][1]]
    v_stack = jnp.stack(v_seg, axis=1).reshape(groups, 4, state, ck * chans)
    v_mat = jnp.pad(v_stack, ((0, 0), (0, 0), (0, pad), (0, 0))).reshape(groups, 4 * LANES, ck * chans)

    gpt = 2 * S5_TILE_PAIRS
    a_seg = jnp.stack([a_parts[0][0], a_parts[0][1], a_parts[1][0], a_parts[1][1]], axis=0)
    a_tbl = jnp.pad(a_seg, ((0, 0), (0, 0), (0, pad))).reshape(4, groups // gpt, gpt, LANES).transpose(1, 0, 2, 3)
    return t_mat.astype(BF16), w_mat.astype(BF16), v_mat.astype(BF16), a_tbl


def _s5c(proj3, col0, width, tables):
    bsz, seq, _ = proj3.shape
    t_mat, w_mat, v_mat, a_tbl = tables
    gpt = 2 * S5_TILE_PAIRS
    tiles = t_mat.shape[0] // gpt
    cols = t_mat.shape[1]
    sw = w_mat.shape[2]
    tc = width // tiles
    assert tc == LANES and cols == 2 * LANES
    n_chunks = seq // S5_CHUNK
    return pl.pallas_call(
        _s5c_kernel,
        grid=(tiles, bsz),
        in_specs=[
            pl.BlockSpec((1, seq, tc), lambda t, b: (b, 0, col0 // tc + t)),
            pl.BlockSpec((gpt, cols, cols), lambda t, b: (t, 0, 0)),
            pl.BlockSpec((gpt, cols, sw), lambda t, b: (t, 0, 0)),
            pl.BlockSpec((gpt, sw, cols), lambda t, b: (t, 0, 0)),
            pl.BlockSpec((1, 4, gpt, LANES), lambda t, b: (t, 0, 0, 0)),
        ],
        out_specs=pl.BlockSpec((1, seq, tc), lambda t, b: (b, 0, t)),
        out_shape=jax.ShapeDtypeStruct((bsz, seq, width), F32),
        scratch_shapes=[pltpu.VMEM((2 * gpt, n_chunks, LANES), F32),
                        pltpu.VMEM((n_chunks, gpt, sw), F32),
                        pltpu.VMEM((n_chunks, gpt, sw), F32)],
        compiler_params=_cparams(("arbitrary", "arbitrary")),
        name="s5",
    )(proj3, t_mat, w_mat, v_mat, a_tbl)


def _retention_kernel(q_ref, k_ref, v_ref, g_ref, cos_ref, sin_ref, lg_ref, gain_ref,
                      o_ref, qs_ref, ks_ref, vs_ref, decay_ref):
    seq = q_ref.shape[1]
    half = RET_HEAD_DIM // 2
    tq = RET_Q_TILE
    cos = cos_ref[...]
    sin = sin_ref[...]

    @pl.when(pl.program_id(1) == 0)
    def _():
        lg_fwd = lg_ref[0, 0:1, :]
        lg_bwd = lg_ref[0, 1:2, :]

        def fill(i, _):
            sl = pl.ds(pl.multiple_of(i * tq, tq), tq)
            t_idx = lax.broadcasted_iota(I32, (tq, seq), 0) + i * tq
            s_idx = lax.broadcasted_iota(I32, (tq, seq), 1)
            rel = (t_idx - s_idx).astype(F32)
            decay_ref[sl, :] = jnp.exp(jnp.where(rel >= 0.0, lg_fwd * rel, -lg_bwd * rel))
            return 0

        lax.fori_loop(0, seq // tq, fill, 0)

    def rot(t_ref, scale):
        t1 = t_ref[0, :, :half]
        t2 = t_ref[0, :, half:]
        return jnp.concatenate([(t1 * cos - t2 * sin) * scale, (t1 * sin + t2 * cos) * scale], axis=-1)

    qs_ref[...] = rot(q_ref, 1.0).astype(BF16)
    ks_ref[...] = rot(k_ref, RET_HEAD_DIM ** -0.5).astype(BF16)
    vs_ref[...] = v_ref[0].astype(BF16)
    gain = gain_ref[...]

    def q_tiles(i, _):
        sls = [pl.ds(pl.multiple_of((2 * i + a) * tq, tq), tq) for a in range(2)]
        scores = [_dot_nt(qs_ref[sl, :], ks_ref[...]) for sl in sls]
        ps = [(s * decay_ref[sl, :]).astype(BF16) for s, sl in zip(scores, sls)]
        outs = [jnp.dot(p, vs_ref[...], preferred_element_type=F32) for p in ps]
        for o, sl in zip(outs, sls):
            y = o * lax.rsqrt(jnp.mean(o * o, axis=-1, keepdims=True) + EPS) * gain
            g = g_ref[0, sl, :]
            o_ref[0, sl, :] = (y * (g * jax.nn.sigmoid(g))).astype(o_ref.dtype)
        return 0

    lax.fori_loop(0, seq // (2 * tq), q_tiles, 0)


def _retention(proj3, ret_gain, width):
    bsz, seq, _ = proj3.shape
    dh = RET_HEAD_DIM
    heads = width // dh
    inv_freq = ROPE_BASE ** (-jnp.arange(0, dh, 2, dtype=F32) / dh)
    ang = jnp.arange(seq, dtype=F32)[:, None] * inv_freq[None, :]
    cos, sin = jnp.cos(ang), jnp.sin(ang)
    log_gamma = jnp.log1p(-jnp.exp2(-5.0 - jnp.arange(heads, dtype=F32)))
    lg = jnp.stack([log_gamma, log_gamma[::-1]], axis=1)
    lg = jnp.broadcast_to(lg[:, :, None], (heads, 2, seq))

    def col(k):
        return pl.BlockSpec((1, seq, dh), lambda h, b: (b, 0, k * heads + h))

    return pl.pallas_call(
        _retention_kernel,
        grid=(heads, bsz),
        in_specs=[col(0), col(1), col(2), col(3),
                  pl.BlockSpec((seq, dh // 2), lambda h, b: (0, 0)),
                  pl.BlockSpec((seq, dh // 2), lambda h, b: (0, 0)),
                  pl.BlockSpec((1, 2, seq), lambda h, b: (h, 0, 0)),
                  pl.BlockSpec((1, dh), lambda h, b: (0, h))],
        out_specs=pl.BlockSpec((1, seq, dh), lambda h, b: (b, 0, h)),
        out_shape=jax.ShapeDtypeStruct((bsz, seq, width), BF16),
        scratch_shapes=[pltpu.VMEM((seq, dh), BF16)] * 3 + [pltpu.VMEM((seq, seq), F32)],
        compiler_params=_cparams(("arbitrary", "arbitrary")),
        name="retention",
    )(proj3, proj3, proj3, proj3, cos, sin, lg, ret_gain.reshape(1, width))


def _fnet_kernel(x_ref, cc_ref, sc_ref, cs_ref, ss_ref, o_ref, a1_ref, a2_ref):
    @pl.when(pl.program_id(1) == 0)
    def _():
        x = x_ref[0].astype(BF16)
        a1_ref[...] = jnp.dot(x, cc_ref[...], preferred_element_type=F32).astype(BF16)
        a2_ref[...] = jnp.dot(x, sc_ref[...], preferred_element_type=F32).astype(BF16)

    y = (jnp.dot(cs_ref[...], a1_ref[...], preferred_element_type=F32)
         - jnp.dot(ss_ref[...], a2_ref[...], preferred_element_type=F32))
    o_ref[0] = y.astype(o_ref.dtype)


def _dft_mats(n):
    idx = jnp.arange(n, dtype=I32)
    ang = (2.0 * math.pi / n) * ((idx[:, None] * idx[None, :]) % n).astype(F32)
    scale = n ** -0.5
    return jnp.cos(ang) * scale, jnp.sin(ang) * scale


def _fnet(proj3, col0, width):
    bsz, seq, _ = proj3.shape
    gw = width // FNET_GROUPS
    cs, ss = _dft_mats(seq)
    cg, sg = _dft_mats(gw)
    eye = jnp.eye(FNET_GROUPS, dtype=F32)
    cc = jnp.kron(eye, cg)
    sc = jnp.kron(eye, sg)
    tr = FNET_ROW_TILE
    return pl.pallas_call(
        _fnet_kernel,
        grid=(bsz, seq // tr),
        in_specs=[
            pl.BlockSpec((1, seq, width), lambda b, i: (b, 0, col0 // width)),
            pl.BlockSpec((width, width), lambda b, i: (0, 0)),
            pl.BlockSpec((width, width), lambda b, i: (0, 0)),
            pl.BlockSpec((tr, seq), lambda b, i: (i, 0)),
            pl.BlockSpec((tr, seq), lambda b, i: (i, 0)),
        ],
        out_specs=pl.BlockSpec((1, tr, width), lambda b, i: (b, i, 0)),
        out_shape=jax.ShapeDtypeStruct((bsz, seq, width), BF16),
        scratch_shapes=[pltpu.VMEM((seq, width), BF16)] * 2,
        compiler_params=_cparams(("arbitrary", "arbitrary")),
        name="fnet",
    )(proj3, cc.astype(BF16), sc.astype(BF16), cs.astype(BF16), ss.astype(BF16))


HIGH_HALF = -65536


def _slab_rows(s, rows, per_row):
    return pl.ds(s, rows, stride=per_row)


def _pack_rows(h, slab_ref):
    rows, d = h.shape
    half = d // 2
    per_row = half // LANES
    for s in range(per_row):
        lo = h[:, s * LANES:(s + 1) * LANES].astype(BF16).astype(F32)
        hi = h[:, half + s * LANES:half + (s + 1) * LANES].astype(BF16).astype(F32)
        lo_bits = lax.shift_right_logical(lax.bitcast_convert_type(lo, I32), jnp.int32(16))
        hi_bits = lax.bitcast_convert_type(hi, I32) & jnp.int32(HIGH_HALF)
        slab_ref[_slab_rows(s, rows, per_row), :] = hi_bits | lo_bits


def _unpack_rows(slab_ref, per_row):
    rows = slab_ref.shape[0] // per_row
    los, his = [], []
    for s in range(per_row):
        w = slab_ref[_slab_rows(s, rows, per_row), :]
        los.append(lax.bitcast_convert_type(lax.shift_left(w, jnp.int32(16)), F32))
        his.append(lax.bitcast_convert_type(w & jnp.int32(HIGH_HALF), F32))
    return jnp.concatenate(los, axis=-1).astype(BF16), jnp.concatenate(his, axis=-1).astype(BF16)


def _outproj_kernel(*refs, glu):
    if glu:
        (x_ref, a_ref, b_ref, gw_ref, gb_ref, wa_ref, wb_ref, gm_ref, gain_ref, sh_ref, sc_ref,
         rwh_ref, rwl_ref, rb_ref, x1_ref, h_ref, idx_ref, gate_ref, cnt_ref, base_ref) = refs
    else:
        (x_ref, a_ref, b_ref, wa_ref, wb_ref, gm_ref, gain_ref, sh_ref, sc_ref,
         rwh_ref, rwl_ref, rb_ref, x1_ref, h_ref, idx_ref, gate_ref, cnt_ref, base_ref) = refs

    @pl.when(pl.program_id(0) == 0)
    def _():
        base_ref[...] = jnp.zeros(base_ref.shape, F32)

    bm = b_ref[...]
    if glu:
        gl = jnp.dot(bm.astype(BF16), gw_ref[...], preferred_element_type=F32) + gb_ref[...]
        bm = (bm.astype(F32) * jax.nn.sigmoid(gl)).astype(BF16)
    y = (jnp.dot(a_ref[...], wa_ref[...], preferred_element_type=F32)
         + jnp.dot(bm, wb_ref[...], preferred_element_type=F32))
    x1 = x_ref[...] + gm_ref[0] * y
    x1_ref[...] = x1
    h = _norm_modulate(x1, gain_ref[...], sh_ref[0], sc_ref[0])
    _pack_rows(h, h_ref)

    h_hi = h.astype(BF16)
    h_lo = (h - h_hi.astype(F32)).astype(BF16)
    logits = (jnp.dot(h_hi, rwh_ref[...], preferred_element_type=F32)
              + jnp.dot(h_lo, rwh_ref[...], preferred_element_type=F32)
              + jnp.dot(h_hi, rwl_ref[...], preferred_element_type=F32)
              + rb_ref[...])
    lane = lax.broadcasted_iota(I32, logits.shape, 1)
    vals, idxs = [], []
    for _ in range(TOP_K):
        m = jnp.max(logits, axis=-1, keepdims=True)
        ik = jnp.min(jnp.where(logits == m, lane, ROUTER_PAD), axis=-1, keepdims=True)
        vals.append(m)
        idxs.append(ik)
        logits = jnp.where(lane == ik, -jnp.inf, logits)
    exps = [jnp.exp(v - vals[0]) for v in vals]
    denom = exps[0] + exps[1] + exps[2] + exps[3]

    tm = lane.shape[0]
    onehot = jnp.zeros(lane.shape, F32)
    for k in range(TOP_K):
        onehot = onehot + (lane == idxs[k]).astype(F32)
    before = (lax.broadcasted_iota(I32, (tm, tm), 1) < lax.broadcasted_iota(I32, (tm, tm), 0)).astype(BF16)
    count = jnp.dot(before, onehot.astype(BF16), preferred_element_type=F32) + base_ref[...]
    new_base = base_ref[...] + jnp.sum(onehot, axis=0, keepdims=True)
    base_ref[...] = new_base
    cnt_ref[...] = new_base.astype(I32)

    idx_out = jnp.zeros(lane.shape, I32)
    gate_out = jnp.zeros(lane.shape, F32)
    for k in range(TOP_K):
        rank = jnp.sum(jnp.where(lane == idxs[k], count, 0.0), axis=-1, keepdims=True).astype(I32)
        idx_out = jnp.where(lane == k, idxs[k], idx_out)
        idx_out = jnp.where(lane == TOP_K + k, rank, idx_out)
        gate_out = jnp.where(lane == k, exps[k] / denom, gate_out)
    idx_ref[...] = idx_out
    gate_ref[...] = gate_out


def _out_proj(x2d, mix_a, mix_b, w_out, g_mix, gain, shift, scale, router_w, router_b, seq,
              glu_w=None, glu_b=None):
    t, d = x2d.shape
    wa_rows = mix_a.shape[1]
    wb_rows = mix_b.shape[1]
    tm = 256
    per_seq = seq // tm
    n_exp = router_w.shape[1]
    rw = jnp.zeros((d, ROUTER_PAD), F32).at[:, :n_exp].set(router_w)
    rw_hi = rw.astype(BF16)
    rw_lo = (rw - rw_hi.astype(F32)).astype(BF16)
    rb = jnp.full((1, ROUTER_PAD), NEG_BIG, F32).at[0, :n_exp].set(router_b)
    w_bf = w_out.astype(BF16)
    glu = glu_w is not None

    def rows(width):
        return pl.BlockSpec((tm, width), lambda i: (i, 0))

    def full(r, c):
        return pl.BlockSpec((r, c), lambda i: (0, 0))

    def per_batch():
        return pl.BlockSpec((1, 1, d), lambda i: (i // per_seq, 0, 0))

    in_specs = [rows(d), rows(wa_rows), rows(wb_rows)]
    args = [x2d, mix_a, mix_b]
    if glu:
        in_specs += [full(wb_rows, wb_rows), full(1, wb_rows)]
        args += [glu_w.astype(BF16), glu_b.reshape(1, wb_rows)]
    in_specs += [pl.BlockSpec((wa_rows, d), lambda i: (0, 0)),
                 pl.BlockSpec((wb_rows, d), lambda i: (wa_rows // wb_rows, 0)),
                 per_batch(), full(1, d), per_batch(), per_batch(),
                 full(d, ROUTER_PAD), full(d, ROUTER_PAD), full(1, ROUTER_PAD)]
    args += [w_bf, w_bf, g_mix, gain.reshape(1, d), shift, scale, rw_hi, rw_lo, rb]
    return pl.pallas_call(
        functools.partial(_outproj_kernel, glu=glu),
        grid=(t // tm,),
        in_specs=in_specs,
        out_specs=[rows(d), pl.BlockSpec((tm * (d // (2 * LANES)), LANES), lambda i: (i, 0)),
                   rows(ROUTER_PAD), rows(ROUTER_PAD), full(1, ROUTER_PAD)],
        out_shape=[jax.ShapeDtypeStruct((t, d), F32), jax.ShapeDtypeStruct((t * (d // (2 * LANES)), LANES), I32),
                   jax.ShapeDtypeStruct((t, ROUTER_PAD), I32), jax.ShapeDtypeStruct((t, ROUTER_PAD), F32),
                   jax.ShapeDtypeStruct((1, ROUTER_PAD), I32)],
        scratch_shapes=[pltpu.VMEM((1, ROUTER_PAD), F32)],
        compiler_params=_cparams(("arbitrary",)),
        name="out_proj_glu" if glu else "out_proj",
    )(*args)


def _routing_tables(top_idx, rank, counts, n_tok):
    n_assign = n_tok * TOP_K
    r = MOE_ROWS
    n_sb = n_assign // (2 * r) + N_EXPERTS
    n_rb = 2 * n_sb
    experts = jnp.arange(N_EXPERTS, dtype=I32)
    used = (counts + r - 1) // r
    nsb = (used + 1) // 2
    sb_end = jnp.cumsum(nsb)
    sb_start = sb_end - nsb
    blk_start = 2 * sb_start
    row_start = jnp.sum(jnp.where(top_idx[:, :, None] == experts[None, None, :],
                                  (blk_start * r)[None, None, :], 0), axis=-1)
    dest = (row_start + rank).reshape(-1).astype(I32)
    rb = jnp.arange(n_rb, dtype=I32)
    rb_e = jnp.sum((sb_end[None, :] <= (rb // 2)[:, None]).astype(I32), axis=1)
    in_range = rb_e < N_EXPERTS
    rb_e = jnp.minimum(rb_e, N_EXPERTS - 1).astype(I32)
    rb_valid = (in_range & (rb - blk_start[rb_e] < used[rb_e])).astype(I32)
    tail = jnp.where(used > 0, blk_start + used - 1, -1)
    empty = jnp.where(2 * nsb > used, blk_start + used, -1)
    spare = 2 * sb_end[-1] + jnp.arange(2 * N_EXPERTS, dtype=I32)
    spare = jnp.where(spare < n_rb, spare, -1)
    zero_blocks = jnp.concatenate([tail, empty, spare]).astype(I32)
    return dict(dest=dest, rb_e=rb_e, rb_valid=rb_valid, zero_blocks=zero_blocks,
                n_valid_sb=sb_end[-1], nsb=nsb.astype(I32), sb_start=sb_start.astype(I32),
                n_sb=n_sb, n_rb=n_rb, n_slots=n_rb * r)


def _dispatch_kernel(dest_ref, zb_ref, h_ref, xs_hbm, zero_ref, zsem_ref, sem_ref):
    i = pl.program_id(0)
    td = DISPATCH_TOKENS
    sr = h_ref.shape[0] // td
    blk = MOE_ROWS * sr

    def slab(ref, row):
        return ref.at[pl.ds(pl.multiple_of(row * sr, sr), sr), :]

    @pl.when(i == 0)
    def _():
        zero_ref[...] = jnp.zeros(zero_ref.shape, zero_ref.dtype)

        def zero_copy(z):
            row0 = pl.multiple_of(zb_ref[z] * blk, blk)
            return pltpu.make_async_copy(zero_ref, xs_hbm.at[pl.ds(row0, blk), :], zsem_ref.at[0])

        def start(z, _):
            @pl.when(zb_ref[z] >= 0)
            def _():
                zero_copy(z).start()
            return 0

        def wait(z, _):
            @pl.when(zb_ref[z] >= 0)
            def _():
                zero_copy(z).wait()
            return 0

        lax.fori_loop(0, zb_ref.shape[0], start, 0)
        lax.fori_loop(0, zb_ref.shape[0], wait, 0)

    def group(g, _):
        for s in range(DMA_UNROLL):
            t = g * DMA_UNROLL + s
            for k in range(TOP_K):
                dst = dest_ref[(i * td + t) * TOP_K + k]
                pltpu.make_async_copy(slab(h_ref, t), slab(xs_hbm, dst), sem_ref.at[0]).start()
        return 0

    lax.fori_loop(0, td // DMA_UNROLL, group, 0)
    for _ in range(TOP_K):
        pltpu.make_async_copy(h_ref, xs_hbm.at[pl.ds(0, td * sr), :], sem_ref.at[0]).wait()


def _moe_dispatch(h_slabs, n_tok, rt):
    sr = h_slabs.shape[0] // n_tok
    td = DISPATCH_TOKENS
    grid_spec = pltpu.PrefetchScalarGridSpec(
        num_scalar_prefetch=2,
        grid=(n_tok // td,),
        in_specs=[pl.BlockSpec((td * sr, LANES), lambda i, dst, zb: (i, 0))],
        out_specs=pl.BlockSpec(memory_space=pl.ANY),
        scratch_shapes=[pltpu.VMEM((MOE_ROWS * sr, LANES), h_slabs.dtype), pltpu.SemaphoreType.DMA((1,)),
                        pltpu.SemaphoreType.DMA((1,))],
    )
    return pl.pallas_call(
        _dispatch_kernel,
        grid_spec=grid_spec,
        out_shape=jax.ShapeDtypeStruct((rt['n_slots'] * sr, LANES), h_slabs.dtype),
        compiler_params=_cparams(("arbitrary",)),
        name="moe_dispatch",
    )(rt['dest'], rt['zero_blocks'], h_slabs)


def _cast_rows(src_ref, dst_ref, chunk=256):
    chunk = min(chunk, dst_ref.shape[0])
    n = dst_ref.shape[0] // chunk

    def body(c, _):
        sl = pl.ds(pl.multiple_of(c * chunk, chunk), chunk)
        dst_ref[sl, :] = src_ref[sl, :].astype(dst_ref.dtype)
        return 0

    lax.fori_loop(0, n, body, 0)


def _stagger_tables(group, cols, parts):
    n = group.shape[0]
    idx = jnp.arange(n, dtype=I32)
    is_first = jnp.concatenate([jnp.ones((1,), bool), group[1:] != group[:-1]])
    first_cur = lax.cummax(jnp.where(is_first, idx, 0))
    nxt = jnp.where(is_first, idx, n)
    first_next = jnp.concatenate([lax.cummin(nxt[::-1])[::-1][1:], jnp.full((1,), n, I32)])
    out = [[] for _ in cols]
    for p in range(parts):
        switch = (idx >= jnp.maximum(first_next - p, first_cur + 1)) & (first_next < n)
        eff = jnp.where(switch, first_next, idx)
        for k, c in enumerate(cols):
            out[k].append(c[eff])
    return [jnp.stack(o) for o in out]


def _moe_up_kernel(sb_ref, j_ref, e_ref, first_ref, va_ref, vb_ref, pe_ref, pj_ref, x_ref, *refs):
    parts = MOE_W_PARTS
    wg_refs = refs[:parts]
    wu_refs = refs[parts:2 * parts]
    bg_ref, bu_ref, o_ref, wgs_ref, wus_ref = refs[2 * parts:]
    i = pl.program_id(0)
    rows = wgs_ref.shape[0] // parts
    r = MOE_ROWS
    sr = x_ref.shape[0] // (2 * r)

    @pl.when(first_ref[i] == 1)
    def _():
        for p in range(parts):
            _cast_rows(wg_refs[p].at[0, 0], wgs_ref.at[pl.ds(p * rows, rows)])
            _cast_rows(wu_refs[p].at[0, 0], wus_ref.at[pl.ds(p * rows, rows)])

    def compute(slabs):
        x_lo, x_hi = _unpack_rows(slabs, sr)
        half = wgs_ref.shape[0] // 2

        def proj(ws_ref, b_ref):
            return (jnp.dot(x_lo, ws_ref[:half, :], preferred_element_type=F32)
                    + jnp.dot(x_hi, ws_ref[half:, :], preferred_element_type=F32) + b_ref[0])

        gate = jnp.minimum(proj(wgs_ref, bg_ref), SWIGLU_LIMIT)
        up = jnp.clip(proj(wus_ref, bu_ref), -SWIGLU_LIMIT, SWIGLU_LIMIT)
        return ((up + 1.0) * gate * jax.nn.sigmoid(SWIGLU_ALPHA * gate)).astype(o_ref.dtype)

    va = va_ref[i] == 1
    vb = vb_ref[i] == 1

    @pl.when(vb)
    def _():
        o_ref[...] = compute(x_ref)

    @pl.when(va & jnp.logical_not(vb))
    def _():
        o_ref[pl.ds(0, r), :] = compute(x_ref.at[pl.ds(0, r * sr), :])
        o_ref[pl.ds(r, r), :] = jnp.zeros((r, o_ref.shape[1]), o_ref.dtype)

    @pl.when(jnp.logical_not(va))
    def _():
        o_ref[...] = jnp.zeros(o_ref.shape, o_ref.dtype)


def _moe_up(x_sorted, w_gu, b_gu, layer, rt):
    _, n_exp, d, two_f = w_gu.shape
    n_slots = rt['n_slots']
    sr = x_sorted.shape[0] // n_slots
    f = two_f // 2
    r, tn = MOE_ROWS, MOE_UP_TN
    nj = f // tn
    n_sb = rt['n_sb']
    n_items = n_sb * nj
    sb = jnp.arange(n_sb, dtype=I32)
    e_of = rt['rb_e'][2 * sb]
    nsb = rt['nsb'].at[n_exp - 1].add(n_sb - rt['n_valid_sb'])
    q = sb - rt['sb_start'][e_of]
    pos = (nj * rt['sb_start'][e_of][:, None]
           + jnp.arange(nj, dtype=I32)[None, :] * nsb[e_of][:, None] + q[:, None]).reshape(-1)
    sb2 = jnp.broadcast_to(sb[:, None], (n_sb, nj)).reshape(-1)
    j2 = jnp.broadcast_to(jnp.arange(nj, dtype=I32)[None, :], (n_sb, nj)).reshape(-1)
    it_sb = jnp.zeros((n_items,), I32).at[pos].set(sb2)
    it_j = jnp.zeros((n_items,), I32).at[pos].set(j2)
    it_va = rt['rb_valid'][2 * it_sb]
    it_vb = rt['rb_valid'][2 * it_sb + 1]
    it_e = e_of[it_sb]
    prev_e = jnp.concatenate([jnp.full((1,), -1, I32), it_e[:-1]])
    prev_j = jnp.concatenate([jnp.full((1,), -1, I32), it_j[:-1]])
    it_first = ((it_e != prev_e) | (it_j != prev_j)).astype(I32)
    parts = MOE_W_PARTS
    part_e, part_j = _stagger_tables(jnp.cumsum(it_first), [it_e, it_j], parts)

    def w_spec(p, col0):
        return pl.BlockSpec((1, 1, d // parts, tn),
                            lambda i, sbt, jt, et, ft, va, vb, pe, pj: (layer, pe[p, i], p, col0 + pj[p, i]))

    grid_spec = pltpu.PrefetchScalarGridSpec(
        num_scalar_prefetch=8,
        grid=(n_items,),
        in_specs=([pl.BlockSpec((2 * r * sr, LANES), lambda i, sbt, jt, et, ft, va, vb, pe, pj: (sbt[i], 0))]
                  + [w_spec(p, 0) for p in range(parts)] + [w_spec(p, nj) for p in range(parts)]
                  + [pl.BlockSpec((1, 1, tn), lambda i, sbt, jt, et, ft, va, vb, pe, pj: (et[i], 0, jt[i])),
                     pl.BlockSpec((1, 1, tn), lambda i, sbt, jt, et, ft, va, vb, pe, pj: (et[i], 0, nj + jt[i]))]),
        out_specs=pl.BlockSpec((2 * r, tn), lambda i, sbt, jt, et, ft, va, vb, pe, pj: (sbt[i], jt[i])),
        scratch_shapes=[pltpu.VMEM((d, tn), BF16), pltpu.VMEM((d, tn), BF16)],
    )
    b3 = b_gu.reshape(n_exp, 1, two_f)
    return pl.pallas_call(
        _moe_up_kernel,
        grid_spec=grid_spec,
        out_shape=jax.ShapeDtypeStruct((n_slots, f), BF16),
        compiler_params=_cparams(("arbitrary",)),
        name="moe_up",
    )(it_sb, it_j, it_e, it_first, it_va, it_vb, part_e, part_j, x_sorted,
      *([w_gu] * (2 * parts)), b3, b3)


def _moe_dn_kernel(e_ref, valid_ref, pe_ref, a_ref, *refs):
    parts = MOE_W_PARTS
    w_refs = refs[:parts]
    b_ref, o_ref, ws_ref = refs[parts:]
    i = pl.program_id(0)
    valid = valid_ref[i] == 1
    e = e_ref[i]
    prev_e = e_ref[jnp.maximum(i - 1, 0)]
    rows = ws_ref.shape[0] // parts

    @pl.when(valid & ((i == 0) | (e != prev_e)))
    def _():
        for p in range(parts):
            _cast_rows(w_refs[p].at[0, 0], ws_ref.at[pl.ds(p * rows, rows)])

    @pl.when(valid)
    def _():
        o_ref[...] = jnp.dot(a_ref[...], ws_ref[...], preferred_element_type=F32) + b_ref[0]

    @pl.when(jnp.logical_not(valid))
    def _():
        o_ref[...] = jnp.zeros(o_ref.shape, o_ref.dtype)


def _moe_dn(act, w_dn, b_dn, layer, rt):
    n_slots, f = act.shape
    _, n_exp, _, d = w_dn.shape
    r = MOE_ROWS
    parts = MOE_W_PARTS
    (part_e,) = _stagger_tables(rt['rb_e'], [rt['rb_e']], parts)

    def w_spec(p):
        return pl.BlockSpec((1, 1, f // parts, d), lambda i, et, vt, pe: (layer, pe[p, i], p, 0))

    grid_spec = pltpu.PrefetchScalarGridSpec(
        num_scalar_prefetch=3,
        grid=(rt['n_rb'],),
        in_specs=([pl.BlockSpec((r, f), lambda i, et, vt, pe: (i, 0))]
                  + [w_spec(p) for p in range(parts)]
                  + [pl.BlockSpec((1, 1, d), lambda i, et, vt, pe: (et[i], 0, 0))]),
        out_specs=pl.BlockSpec((r, d), lambda i, et, vt, pe: (i, 0)),
        scratch_shapes=[pltpu.VMEM((f, d), BF16)],
    )
    return pl.pallas_call(
        _moe_dn_kernel,
        grid_spec=grid_spec,
        out_shape=jax.ShapeDtypeStruct((n_slots, d), F32),
        compiler_params=_cparams(("arbitrary",)),
        name="moe_dn",
    )(rt['rb_e'], rt['rb_valid'], part_e, act, *([w_dn] * parts), b_dn.reshape(n_exp, 1, d))


def _combine_kernel(dest_ref, x_ref, gate_ref, g_ref, fg_ref, y_hbm, o_ref, buf_ref, sem_ref, *, final):
    i = pl.program_id(0)
    n_steps = pl.num_programs(0)
    tc = COMBINE_TOKENS

    def issue(step, slot):
        def group(g, _):
            for s in range(DMA_UNROLL):
                t = g * DMA_UNROLL + s
                for k in range(TOP_K):
                    src = dest_ref[(step * tc + t) * TOP_K + k]
                    pltpu.make_async_copy(y_hbm.at[pl.ds(src, 1), :], buf_ref.at[slot, k, pl.ds(t, 1), :],
                                          sem_ref.at[slot]).start()
            return 0

        lax.fori_loop(0, tc // DMA_UNROLL, group, 0)

    @pl.when(i == 0)
    def _():
        issue(0, 0)

    @pl.when(i + 1 < n_steps)
    def _():
        issue(i + 1, (i + 1) % 2)

    slot = i % 2
    for k in range(TOP_K):
        pltpu.make_async_copy(y_hbm.at[pl.ds(0, tc), :], buf_ref.at[slot, k], sem_ref.at[slot]).wait()
    gates = gate_ref[...]
    y = gates[:, 0:1] * buf_ref[slot, 0]
    for k in range(1, TOP_K):
        y = y + gates[:, k:k + 1] * buf_ref[slot, k]
    x2 = x_ref[...] + g_ref[0] * y
    if final:
        x2 = x2 * lax.rsqrt(jnp.mean(x2 * x2, axis=-1, keepdims=True) + EPS) * fg_ref[...]
    o_ref[...] = x2


def _combine(x1, y_sorted, gates, rt, g_ffn, final_gain, seq, final):
    t, d = x1.shape
    tc = COMBINE_TOKENS
    per_seq = seq // tc
    grid_spec = pltpu.PrefetchScalarGridSpec(
        num_scalar_prefetch=1,
        grid=(t // tc,),
        in_specs=[pl.BlockSpec((tc, d), lambda i, dst: (i, 0)),
                  pl.BlockSpec((tc, ROUTER_PAD), lambda i, dst: (i, 0)),
                  pl.BlockSpec((1, 1, d), lambda i, dst: (i // per_seq, 0, 0)),
                  pl.BlockSpec((1, d), lambda i, dst: (0, 0)),
                  pl.BlockSpec(memory_space=pl.ANY)],
        out_specs=pl.BlockSpec((tc, d), lambda i, dst: (i, 0)),
        scratch_shapes=[pltpu.VMEM((2, TOP_K, tc, d), F32), pltpu.SemaphoreType.DMA((2,))],
    )
    return pl.pallas_call(
        functools.partial(_combine_kernel, final=final),
        grid_spec=grid_spec,
        out_shape=jax.ShapeDtypeStruct((t, d), F32),
        compiler_params=_cparams(("arbitrary",)),
        name="combine_final" if final else "combine",
    )(rt['dest'], x1, gates, g_ffn, final_gain.reshape(1, d), y_sorted)


def _moe(x1, h, idx, gates, counts, w_gu, b_gu, w_dn, b_dn, layer, g_ffn, final_gain, seq, final):
    n_tok = x1.shape[0]
    rt = _routing_tables(idx[:, :TOP_K], idx[:, TOP_K:2 * TOP_K], counts[0, :N_EXPERTS], n_tok)
    x_sorted = _moe_dispatch(h, n_tok, rt)
    act = _moe_up(x_sorted, w_gu, b_gu, layer, rt)
    y_sorted = _moe_dn(act, w_dn, b_dn, layer, rt)
    return _combine(x1, y_sorted, gates, rt, g_ffn, final_gain, seq, final)


def kernel(x, c, ada_w, ada_b, norm_mix_gain, norm_ffn_gain, ab_w_in, ab_w_out, hg_lb_logits, hg_norm_gain, s5_lam_re, s5_lam_im, s5_log_dt, s5_b_re, s5_b_im, s5_c_re, s5_c_im, s5_d, s5_glu_w, s5_glu_b, cd_w_in, cd_w_out, ret_norm_gain, router_w, router_b, moe_w_gu, moe_b_gu, moe_w_dn, moe_b_dn, final_gain):
    bsz, seq, d = x.shape
    depth = ada_w.shape[0]
    n_tok = bsz * seq
    hg_width = hg_lb_logits.shape[1]
    s5_width = s5_glu_w.shape[1]
    ret_width = ret_norm_gain.shape[1]
    fnet_width = cd_w_out.shape[1] - ret_width

    lower_bounds = jnp.cumsum(jax.nn.softmax(hg_lb_logits.astype(F32), axis=0), axis=0)
    mod = _ada_mod(c, ada_w, ada_b)
    xr = x.reshape(n_tok, d)
    for layer in range(depth):
        sh_mix, sc_mix, g_mix, sh_ffn, sc_ffn, g_ffn = (
            mod[layer, :, k * d:(k + 1) * d].reshape(bsz, 1, d) for k in range(6))
        j = layer // 2
        if layer % 2 == 0:
            proj = _in_proj(xr, norm_mix_gain[layer], sh_mix, sc_mix, ab_w_in[j].astype(BF16), seq)
            proj3 = proj.reshape(bsz, seq, proj.shape[1])
            mix_a = _hgrn2(proj3, lower_bounds[j], hg_norm_gain[j], hg_width)
            tables = _s5c_tables(s5_lam_re[j], s5_lam_im[j], s5_log_dt[j], s5_b_re[j], s5_b_im[j],
                                 s5_c_re[j], s5_c_im[j], s5_d[j])
            mix_b = _s5c(proj3, 5 * hg_width, s5_width, tables)
            x1, h, idx, gate, counts = _out_proj(
                xr, mix_a.reshape(n_tok, hg_width), mix_b.reshape(n_tok, s5_width), ab_w_out[j],
                g_mix, norm_ffn_gain[layer], sh_ffn, sc_ffn, router_w[layer], router_b[layer], seq,
                glu_w=s5_glu_w[j], glu_b=s5_glu_b[j])
        else:
            proj = _in_proj(xr, norm_mix_gain[layer], sh_mix, sc_mix, cd_w_in[j].astype(BF16), seq)
            proj3 = proj.reshape(bsz, seq, proj.shape[1])
            mix_a = _retention(proj3, ret_norm_gain[j], ret_width)
            mix_b = _fnet(proj3, 4 * ret_width, fnet_width)
            x1, h, idx, gate, counts = _out_proj(
                xr, mix_a.reshape(n_tok, ret_width), mix_b.reshape(n_tok, fnet_width), cd_w_out[j],
                g_mix, norm_ffn_gain[layer], sh_ffn, sc_ffn, router_w[layer], router_b[layer], seq)
        xr = _moe(x1, h, idx, gate, counts, moe_w_gu, moe_b_gu[layer], moe_w_dn, moe_b_dn[layer],
                  layer, g_ffn, final_gain, seq, final=(layer == depth - 1))
    return xr.reshape(bsz, seq, d)
```

```python
import functools
import math

import jax
import jax.numpy as jnp
from jax import lax
from jax.experimental import pallas as pl
from jax.experimental.pallas import tpu as pltpu

F32 = jnp.float32
BF16 = jnp.bfloat16
I32 = jnp.int32

EPS = 1e-6
LANES = 128
SUBLANES = 8
VMEM_LIMIT = 56 * 1024 * 1024

HG_HEAD_DIM = 128
HG_CHUNK = 64
HG_GROUP = 256
HG_EXP_CLAMP = 80.0

S5_GROUP = 16
S5_STATE = 64
S5_CHUNK = 16
S5_UNROLL = 8
S5_TILE_PAIRS = 4
S5_ROW_SPLIT = 4

RET_HEAD_DIM = 256
RET_Q_TILE = 256
ROPE_BASE = 10000.0

FNET_GROUPS = 4
FNET_ROW_TILE = 512

N_EXPERTS = 32
TOP_K = 4
SWIGLU_LIMIT = 7.0
SWIGLU_ALPHA = 1.702
MOE_ROWS = 256
MOE_UP_TN = 1024
MOE_DN_TN = 1024
MOE_W_PARTS = 4
DISPATCH_TOKENS = 512
COMBINE_TOKENS = 256
DMA_UNROLL = 8
ROUTER_PAD = LANES
NEG_BIG = -1e30


def _cparams(semantics):
    return pltpu.CompilerParams(dimension_semantics=semantics, vmem_limit_bytes=VMEM_LIMIT)


def _ada_kernel(c_ref, w_ref, b_ref, o_ref):
    c = c_ref[...]
    cond = c * jax.nn.sigmoid(c)
    o_ref[0] = jnp.dot(cond.astype(BF16), w_ref[0].astype(BF16),
                       preferred_element_type=F32) + b_ref[0]


def _ada_mod(c, ada_w, ada_b):
    depth, d, n = ada_w.shape
    bsz = c.shape[0]
    tn = 1024
    return pl.pallas_call(
        _ada_kernel,
        grid=(depth, n // tn),
        in_specs=[
            pl.BlockSpec((bsz, d), lambda l, j: (0, 0)),
            pl.BlockSpec((1, d, tn), lambda l, j: (l, 0, j)),
            pl.BlockSpec((1, 1, tn), lambda l, j: (l, 0, j)),
        ],
        out_specs=pl.BlockSpec((1, bsz, tn), lambda l, j: (l, 0, j)),
        out_shape=jax.ShapeDtypeStruct((depth, bsz, n), F32),
        compiler_params=_cparams(("arbitrary", "arbitrary")),
        name="ada_mod",
    )(c, ada_w, ada_b.reshape(depth, 1, n))


def _norm_modulate(x, gain, shift, scale):
    ms = jnp.mean(x * x, axis=-1, keepdims=True)
    y = x * lax.rsqrt(ms + EPS) * gain
    return y * (1.0 + scale) + shift


def _inproj_kernel(x_ref, gain_ref, sh_ref, sc_ref, w_ref, o_ref, h_ref):
    @pl.when(pl.program_id(1) == 0)
    def _():
        rows = 128

        def slab(c, _):
            sl = pl.ds(pl.multiple_of(c * rows, rows), rows)
            h_ref[sl, :] = _norm_modulate(x_ref[sl, :], gain_ref[...], sh_ref[0], sc_ref[0]).astype(BF16)
            return 0

        lax.fori_loop(0, h_ref.shape[0] // rows, slab, 0)

    o_ref[...] = jnp.dot(h_ref[...], w_ref[...], preferred_element_type=F32)


def _in_proj(x2d, gain, shift, scale, w_bf16, seq):
    t, d = x2d.shape
    n = w_bf16.shape[1]
    tm = 1024
    tn = 1024 if n % 1024 == 0 else 512
    per_seq = seq // tm
    return pl.pallas_call(
        _inproj_kernel,
        grid=(t // tm, n // tn),
        in_specs=[
            pl.BlockSpec((tm, d), lambda i, j: (i, 0)),
            pl.BlockSpec((1, d), lambda i, j: (0, 0)),
            pl.BlockSpec((1, 1, d), lambda i, j: (i // per_seq, 0, 0)),
            pl.BlockSpec((1, 1, d), lambda i, j: (i // per_seq, 0, 0)),
            pl.BlockSpec((d, tn), lambda i, j: (0, j)),
        ],
        out_specs=pl.BlockSpec((tm, tn), lambda i, j: (i, j)),
        out_shape=jax.ShapeDtypeStruct((t, n), F32),
        scratch_shapes=[pltpu.VMEM((tm, d), BF16)],
        compiler_params=_cparams(("arbitrary", "arbitrary")),
        name="in_proj",
    )(x2d, gain.reshape(1, d), shift, scale, w_bf16)


def _split3(a):
    hi = a.astype(BF16)
    r1 = a - hi.astype(F32)
    mid = r1.astype(BF16)
    lo = (r1 - mid.astype(F32)).astype(BF16)
    return hi, mid, lo


def _tri_sum(tri, a):
    hi, mid, lo = _split3(a)
    return (jnp.dot(tri, hi, preferred_element_type=F32)
            + jnp.dot(tri, mid, preferred_element_type=F32)
            + jnp.dot(tri, lo, preferred_element_type=F32))


def _dot_nt(a, b):
    return lax.dot_general(a, b, (((1,), (1,)), ((), ())), preferred_element_type=F32)


def _dot_tn(a, b):
    return lax.dot_general(a, b, (((0,), (0,)), ((), ())), preferred_element_type=F32)


def _hgrn2_kernel(q_ref, zf_ref, zb_ref, v_ref, g_ref, lb_ref, gain_ref, o_ref, acc_ref, accb_ref):
    seq = q_ref.shape[1]
    ln = HG_CHUNK
    gr = HG_GROUP
    n_groups = seq // gr
    per_group = gr // ln
    lb = lb_ref[...]
    gain = gain_ref[...]
    row = lax.broadcasted_iota(I32, (gr, gr), 0)
    col = lax.broadcasted_iota(I32, (gr, gr), 1)
    chunk_lo = (row // ln) * ln
    chunk_hi = chunk_lo + ln
    lower_incl = (col <= row) & (col >= chunk_lo)
    upper_strict = (col > row) & (col < chunk_hi)
    tri_prefix = lower_incl.astype(BF16)
    tri_suffix = ((col >= row) & (col < chunk_hi)).astype(BF16)
    mid = ln // 2

    def per_chunk_rows(a, offset):
        return jnp.concatenate(
            [jnp.broadcast_to(a[j * ln + offset:j * ln + offset + 1, :], (ln, a.shape[1]))
             for j in range(per_group)], axis=0)

    rows = [slice(j * ln, (j + 1) * ln) for j in range(per_group)]

    def load(gi, z_ref, forward):
        sl = pl.ds(pl.multiple_of(gi * gr, gr), gr)
        f = lb + (1.0 - lb) * jax.nn.sigmoid(z_ref[0, sl, :])
        log_f = jnp.log(f)
        d = dict(sl=sl, forward=forward, q=q_ref[0, sl, :], v=v_ref[0, sl, :].astype(BF16), k=1.0 - f)
        if forward:
            d['cum'] = _tri_sum(tri_prefix, log_f)
        else:
            d['cum'] = _tri_sum(tri_suffix, log_f)
        return d

    def local_states(d):
        edge_off = ln - 1 if d['forward'] else 0
        kd = (d['k'] * jnp.exp(per_chunk_rows(d['cum'], edge_off) - d['cum'])).astype(BF16)
        d['local'] = [_dot_tn(d['v'][rs], kd[rs]) for rs in rows]

    def scores(d):
        ref_rows = per_chunk_rows(d['cum'], mid - 1 if d['forward'] else mid)
        qe = d['q'] * jnp.exp(jnp.minimum(d['cum'] - ref_rows, HG_EXP_CLAMP))
        ke = d['k'] * jnp.exp(jnp.minimum(ref_rows - d['cum'], HG_EXP_CLAMP))
        mask = lower_incl if d['forward'] else upper_strict
        d['scores'] = jnp.where(mask, _dot_nt(qe.astype(BF16), ke.astype(BF16)), 0.0)

    def inter(d, state_t):
        edge_off = ln - 1 if d['forward'] else 0
        qc = (d['q'] * jnp.exp(d['cum'])).astype(BF16)
        entering = [None] * per_group
        order = range(per_group) if d['forward'] else range(per_group - 1, -1, -1)
        for j in order:
            entering[j] = state_t.astype(BF16)
            edge = d['cum'][j * ln + edge_off:j * ln + edge_off + 1, :]
            state_t = state_t * jnp.exp(edge) + d['local'][j]
        d['inter'] = jnp.concatenate([_dot_nt(qc[rs], entering[j]) for j, rs in enumerate(rows)], axis=0)
        return state_t

    def body(i, states):
        st_f, st_b = states
        both = [load(i, zf_ref, True), load(n_groups - 1 - i, zb_ref, False)]
        for d in both:
            local_states(d)
        for d in both:
            scores(d)
        st_f = inter(both[0], st_f)
        st_b = inter(both[1], st_b)
        for d, ref in zip(both, (acc_ref, accb_ref)):
            ref[d['sl'], :] = jnp.dot(d['scores'].astype(BF16), d['v'], preferred_element_type=F32) + d['inter']
        return st_f, st_b

    zero = jnp.zeros((HG_HEAD_DIM, HG_HEAD_DIM), F32)
    lax.fori_loop(0, n_groups, body, (zero, zero))

    def finish(gi, _):
        sl = pl.ds(pl.multiple_of(gi * gr, gr), gr)
        o = acc_ref[sl, :] + accb_ref[sl, :]
        y = o * lax.rsqrt(jnp.mean(o * o, axis=-1, keepdims=True) + EPS) * gain
        g = g_ref[0, sl, :]
        o_ref[0, sl, :] = (y * (g * jax.nn.sigmoid(g))).astype(o_ref.dtype)
        return 0

    lax.fori_loop(0, n_groups, finish, 0)


def _hgrn2(proj3, lower_bound, hg_gain, width):
    bsz, seq, _ = proj3.shape
    heads = width // HG_HEAD_DIM
    dh = HG_HEAD_DIM

    def col(k):
        return pl.BlockSpec((1, seq, dh), lambda b, h: (b, 0, k * heads + h))

    return pl.pallas_call(
        _hgrn2_kernel,
        grid=(bsz, heads),
        in_specs=[col(0), col(1), col(2), col(3), col(4),
                  pl.BlockSpec((1, dh), lambda b, h: (0, h)),
                  pl.BlockSpec((1, dh), lambda b, h: (0, 0))],
        out_specs=pl.BlockSpec((1, seq, dh), lambda b, h: (b, 0, h)),
        out_shape=jax.ShapeDtypeStruct((bsz, seq, width), BF16),
        scratch_shapes=[pltpu.VMEM((seq, dh), F32), pltpu.VMEM((seq, dh), F32)],
        compiler_params=_cparams(("arbitrary", "arbitrary")),
        name="hgrn2",
    )(proj3, proj3, proj3, proj3, proj3, lower_bound.reshape(1, width), hg_gain.reshape(1, dh))


def _s5c_kernel(u_ref, t_ref, w_ref, v_ref, a_ref, z_ref, uc_ref, e_ref, p_ref):
    ck = S5_CHUNK
    n_chunks = u_ref.shape[1] // ck
    slab = n_chunks // S5_ROW_SPLIT
    gpt = 2 * S5_TILE_PAIRS
    half_tok = ck // 2
    lane_grp = lax.broadcasted_iota(I32, (slab, LANES), 1) // S5_GROUP

    def token_rows(rq, tok):
        return pl.ds(rq * slab * ck + tok, slab, stride=ck)

    def roll_lanes(a, groups):
        shift = (groups * S5_GROUP) % LANES
        return pltpu.roll(a, shift, 1) if shift else a

    for half in range(2):
        for rq in range(S5_ROW_SPLIT):
            toks = [u_ref.at[0][token_rows(rq, half * half_tok + sl), :] for sl in range(half_tok)]
            for gl in range(gpt):
                acc = jnp.zeros((slab, LANES), F32)
                for sl in range(half_tok):
                    acc = jnp.where(lane_grp == sl, roll_lanes(toks[sl], sl - gl), acc)
                uc_ref[gl * 2 + half, rq * slab:(rq + 1) * slab, :] = acc

    seg = [pl.ds(k * LANES, LANES) for k in range(4)]

    def group_inputs(g):
        return jnp.concatenate([uc_ref[2 * g], uc_ref[2 * g + 1]], axis=-1).astype(BF16)

    for g in range(gpt):
        e_ref[:, g, :] = jnp.dot(group_inputs(g), w_ref[g], preferred_element_type=F32)

    af_r, af_i, ab_r, ab_i = a_ref[0, 0], a_ref[0, 1], a_ref[0, 2], a_ref[0, 3]

    def cmul(ar, ai, br, bi):
        return ar * br - ai * bi, ar * bi + ai * br

    def step(m, carry):
        xr, xi, yr, yi = carry
        mb = n_chunks - 1 - m
        p_ref[m, :, seg[0]] = xr
        p_ref[m, :, seg[1]] = xi
        p_ref[mb, :, seg[2]] = yr
        p_ref[mb, :, seg[3]] = yi
        dr, di = cmul(af_r, af_i, xr, xi)
        gr, gi = cmul(ab_r, ab_i, yr, yi)
        return (dr + e_ref[m, :, seg[0]], di + e_ref[m, :, seg[1]],
                gr + e_ref[mb, :, seg[2]], gi + e_ref[mb, :, seg[3]])

    zero = jnp.zeros((gpt, LANES), F32)
    lax.fori_loop(0, n_chunks, step, (zero, zero, zero, zero), unroll=S5_UNROLL)

    for g in range(gpt):
        y = (jnp.dot(group_inputs(g), t_ref[g], preferred_element_type=F32)
             + jnp.dot(p_ref[:, g, :].astype(BF16), v_ref[g], preferred_element_type=F32))
        zt = jax.nn.gelu(y)
        for k in range(2):
            uc_ref[2 * g + k] = zt[:, k * LANES:(k + 1) * LANES]

    for half in range(2):
        for rq in range(S5_ROW_SPLIT):
            cols = [uc_ref[gl * 2 + half, rq * slab:(rq + 1) * slab, :] for gl in range(gpt)]
            for sl in range(half_tok):
                acc = jnp.zeros((slab, LANES), F32)
                for gl in range(gpt):
                    acc = jnp.where(lane_grp == gl, roll_lanes(cols[gl], gl - sl), acc)
                z_ref.at[0][token_rows(rq, half * half_tok + sl), :] = acc


_einsum_f32 = functools.partial(jnp.einsum, precision=lax.Precision.HIGHEST)


def _s5c_tables(lam_re, lam_im, log_dt, b_re, b_im, c_re, c_im, d_skip):
    groups, state = lam_re.shape[1], lam_re.shape[2]
    chans = b_re.shape[2]
    ck = S5_CHUNK
    lag = jnp.arange(ck + 1, dtype=F32)[:, None, None]
    kern, w_parts, v_parts, a_parts = [], [], [], []
    for direction in (0, 1):
        lr, li = lam_re[direction].astype(F32), lam_im[direction].astype(F32)
        dt = jnp.exp(log_dt[direction].astype(F32))[:, None]
        mag = jnp.exp(lr * dt)
        abar_re = mag * jnp.cos(li * dt)
        abar_im = mag * jnp.sin(li * dt)
        den = lr * lr + li * li
        num_re = abar_re - 1.0
        coef_re = (num_re * lr + abar_im * li) / den
        coef_im = (abar_im * lr - num_re * li) / den
        bbar_re = coef_re[..., None] * b_re - coef_im[..., None] * b_im
        bbar_im = coef_re[..., None] * b_im + coef_im[..., None] * b_re
        pw_re = jnp.exp(lag * lr * dt) * jnp.cos(lag * li * dt)
        pw_im = jnp.exp(lag * lr * dt) * jnp.sin(lag * li * dt)
        ab_re = pw_re[..., None] * bbar_re - pw_im[..., None] * bbar_im
        ab_im = pw_re[..., None] * bbar_im + pw_im[..., None] * bbar_re
        cr, ci = c_re[direction].astype(F32), c_im[direction].astype(F32)
        kern.append(_einsum_f32('gcp,ngpd->ngcd', cr, ab_re[:ck]) - _einsum_f32('gcp,ngpd->ngcd', ci, ab_im[:ck]))
        order = jnp.arange(ck - 1, -1, -1) if direction == 0 else jnp.arange(ck)
        w_parts.append((ab_re[order], ab_im[order]))
        order = jnp.arange(1, ck + 1) if direction == 0 else jnp.arange(ck, 0, -1)
        a_r, a_i = pw_re[order], pw_im[order]
        v_from_re = jnp.einsum('gcp,jgp->gpjc', cr, a_r) - jnp.einsum('gcp,jgp->gpjc', ci, a_i)
        v_from_im = -(jnp.einsum('gcp,jgp->gpjc', cr, a_i) + jnp.einsum('gcp,jgp->gpjc', ci, a_r))
        v_parts.append((v_from_re, v_from_im))
        a_parts.append((pw_re[ck], pw_im[ck]))

    s_idx = jnp.arange(ck)[None, :, None]
    t_idx = jnp.arange(ck)[None, None, :]
    n_idx = jnp.arange(ck)[:, None, None]
    sel_f = (t_idx - s_idx == n_idx).astype(F32)
    sel_b = (s_idx - t_idx == n_idx).astype(F32)
    skip = jnp.eye(chans, dtype=F32)[None] * d_skip.reshape(groups, chans)[:, :, None]
    toep = (_einsum_f32('nst,ngcd->gsdtc', sel_f, kern[0]) + _einsum_f32('nst,ngcd->gsdtc', sel_b, kern[1])
            + jnp.einsum('st,gcd->gsdtc', jnp.eye(ck, dtype=F32), skip))
    t_mat = toep.reshape(groups, ck * chans, ck * chans)

    pad = LANES - state
    w_seg = [w_parts[0][0], w_parts[0][1], w_parts[1][0], w_parts[1][1]]
    w_stack = jnp.stack(w_seg, axis=0).transpose(2, 1, 4, 0, 3)
    w_mat = jnp.pad(w_stack, ((0, 0),) * 4 + ((0, pad),)).reshape(groups, ck * chans, 4 * LANES)

    v_seg = [v_parts[0][0], v_parts[0][1], v_parts[1][0], v_parts[1][1]]
    v_stack = jnp.stack(v_seg, axis=1).reshape(groups, 4, state, ck * chans)
    v_mat = jnp.pad(v_stack, ((0, 0), (0, 0), (0, pad), (0, 0))).reshape(groups, 4 * LANES, ck * chans)

    gpt = 2 * S5_TILE_PAIRS
    a_seg = jnp.stack([a_parts[0][0], a_parts[0][1], a_parts[1][0], a_parts[1][1]], axis=0)
    a_tbl = jnp.pad(a_seg, ((0, 0), (0, 0), (0, pad))).reshape(4, groups // gpt, gpt, LANES).transpose(1, 0, 2, 3)
    return t_mat.astype(BF16), w_mat.astype(BF16), v_mat.astype(BF16), a_tbl


def _s5c(proj3, col0, width, tables):
    bsz, seq, _ = proj3.shape
    t_mat, w_mat, v_mat, a_tbl = tables
    gpt = 2 * S5_TILE_PAIRS
    tiles = t_mat.shape[0] // gpt
    cols = t_mat.shape[1]
    sw = w_mat.shape[2]
    tc = width // tiles
    assert tc == LANES and cols == 2 * LANES
    n_chunks = seq // S5_CHUNK
    return pl.pallas_call(
        _s5c_kernel,
        grid=(tiles, bsz),
        in_specs=[
            pl.BlockSpec((1, seq, tc), lambda t, b: (b, 0, col0 // tc + t)),
            pl.BlockSpec((gpt, cols, cols), lambda t, b: (t, 0, 0)),
            pl.BlockSpec((gpt, cols, sw), lambda t, b: (t, 0, 0)),
            pl.BlockSpec((gpt, sw, cols), lambda t, b: (t, 0, 0)),
            pl.BlockSpec((1, 4, gpt, LANES), lambda t, b: (t, 0, 0, 0)),
        ],
        out_specs=pl.BlockSpec((1, seq, tc), lambda t, b: (b, 0, t)),
        out_shape=jax.ShapeDtypeStruct((bsz, seq, width), F32),
        scratch_shapes=[pltpu.VMEM((2 * gpt, n_chunks, LANES), F32),
                        pltpu.VMEM((n_chunks, gpt, sw), F32),
                        pltpu.VMEM((n_chunks, gpt, sw), F32)],
        compiler_params=_cparams(("arbitrary", "arbitrary")),
        name="s5",
    )(proj3, t_mat, w_mat, v_mat, a_tbl)


def _retention_kernel(q_ref, k_ref, v_ref, g_ref, cos_ref, sin_ref, lg_ref, gain_ref,
                      o_ref, qs_ref, ks_ref, vs_ref, decay_ref):
    seq = q_ref.shape[1]
    half = RET_HEAD_DIM // 2
    tq = RET_Q_TILE
    cos = cos_ref[...]
    sin = sin_ref[...]

    @pl.when(pl.program_id(1) == 0)
    def _():
        lg_fwd = lg_ref[0, 0:1, :]
        lg_bwd = lg_ref[0, 1:2, :]

        def fill(i, _):
            sl = pl.ds(pl.multiple_of(i * tq, tq), tq)
            t_idx = lax.broadcasted_iota(I32, (tq, seq), 0) + i * tq
            s_idx = lax.broadcasted_iota(I32, (tq, seq), 1)
            rel = (t_idx - s_idx).astype(F32)
            decay_ref[sl, :] = jnp.exp(jnp.where(rel >= 0.0, lg_fwd * rel, -lg_bwd * rel))
            return 0

        lax.fori_loop(0, seq // tq, fill, 0)

    def rot(t_ref, scale):
        t1 = t_ref[0, :, :half]
        t2 = t_ref[0, :, half:]
        return jnp.concatenate([(t1 * cos - t2 * sin) * scale, (t1 * sin + t2 * cos) * scale], axis=-1)

    qs_ref[...] = rot(q_ref, 1.0).astype(BF16)
    ks_ref[...] = rot(k_ref, RET_HEAD_DIM ** -0.5).astype(BF16)
    vs_ref[...] = v_ref[0].astype(BF16)
    gain = gain_ref[...]

    def q_tiles(i, _):
        sls = [pl.ds(pl.multiple_of((2 * i + a) * tq, tq), tq) for a in range(2)]
        scores = [_dot_nt(qs_ref[sl, :], ks_ref[...]) for sl in sls]
        ps = [(s * decay_ref[sl, :]).astype(BF16) for s, sl in zip(scores, sls)]
        outs = [jnp.dot(p, vs_ref[...], preferred_element_type=F32) for p in ps]
        for o, sl in zip(outs, sls):
            y = o * lax.rsqrt(jnp.mean(o * o, axis=-1, keepdims=True) + EPS) * gain
            g = g_ref[0, sl, :]
            o_ref[0, sl, :] = (y * (g * jax.nn.sigmoid(g))).astype(o_ref.dtype)
        return 0

    lax.fori_loop(0, seq // (2 * tq), q_tiles, 0)


def _retention(proj3, ret_gain, width):
    bsz, seq, _ = proj3.shape
    dh = RET_HEAD_DIM
    heads = width // dh
    inv_freq = ROPE_BASE ** (-jnp.arange(0, dh, 2, dtype=F32) / dh)
    ang = jnp.arange(seq, dtype=F32)[:, None] * inv_freq[None, :]
    cos, sin = jnp.cos(ang), jnp.sin(ang)
    log_gamma = jnp.log1p(-jnp.exp2(-5.0 - jnp.arange(heads, dtype=F32)))
    lg = jnp.stack([log_gamma, log_gamma[::-1]], axis=1)
    lg = jnp.broadcast_to(lg[:, :, None], (heads, 2, seq))

    def col(k):
        return pl.BlockSpec((1, seq, dh), lambda h, b: (b, 0, k * heads + h))

    return pl.pallas_call(
        _retention_kernel,
        grid=(heads, bsz),
        in_specs=[col(0), col(1), col(2), col(3),
                  pl.BlockSpec((seq, dh // 2), lambda h, b: (0, 0)),
                  pl.BlockSpec((seq, dh // 2), lambda h, b: (0, 0)),
                  pl.BlockSpec((1, 2, seq), lambda h, b: (h, 0, 0)),
                  pl.BlockSpec((1, dh), lambda h, b: (0, h))],
        out_specs=pl.BlockSpec((1, seq, dh), lambda h, b: (b, 0, h)),
        out_shape=jax.ShapeDtypeStruct((bsz, seq, width), BF16),
        scratch_shapes=[pltpu.VMEM((seq, dh), BF16)] * 3 + [pltpu.VMEM((seq, seq), F32)],
        compiler_params=_cparams(("arbitrary", "arbitrary")),
        name="retention",
    )(proj3, proj3, proj3, proj3, cos, sin, lg, ret_gain.reshape(1, width))


def _fnet_kernel(x_ref, cc_ref, sc_ref, cs_ref, ss_ref, o_ref, a1_ref, a2_ref):
    @pl.when(pl.program_id(1) == 0)
    def _():
        x = x_ref[0].astype(BF16)
        a1_ref[...] = jnp.dot(x, cc_ref[...], preferred_element_type=F32).astype(BF16)
        a2_ref[...] = jnp.dot(x, sc_ref[...], preferred_element_type=F32).astype(BF16)

    y = (jnp.dot(cs_ref[...], a1_ref[...], preferred_element_type=F32)
         - jnp.dot(ss_ref[...], a2_ref[...], preferred_element_type=F32))
    o_ref[0] = y.astype(o_ref.dtype)


def _dft_mats(n):
    idx = jnp.arange(n, dtype=I32)
    ang = (2.0 * math.pi / n) * ((idx[:, None] * idx[None, :]) % n).astype(F32)
    scale = n ** -0.5
    return jnp.cos(ang) * scale, jnp.sin(ang) * scale


def _fnet(proj3, col0, width):
    bsz, seq, _ = proj3.shape
    gw = width // FNET_GROUPS
    cs, ss = _dft_mats(seq)
    cg, sg = _dft_mats(gw)
    eye = jnp.eye(FNET_GROUPS, dtype=F32)
    cc = jnp.kron(eye, cg)
    sc = jnp.kron(eye, sg)
    tr = FNET_ROW_TILE
    return pl.pallas_call(
        _fnet_kernel,
        grid=(bsz, seq // tr),
        in_specs=[
            pl.BlockSpec((1, seq, width), lambda b, i: (b, 0, col0 // width)),
            pl.BlockSpec((width, width), lambda b, i: (0, 0)),
            pl.BlockSpec((width, width), lambda b, i: (0, 0)),
            pl.BlockSpec((tr, seq), lambda b, i: (i, 0)),
            pl.BlockSpec((tr, seq), lambda b, i: (i, 0)),
        ],
        out_specs=pl.BlockSpec((1, tr, width), lambda b, i: (b, i, 0)),
        out_shape=jax.ShapeDtypeStruct((bsz, seq, width), BF16),
        scratch_shapes=[pltpu.VMEM((seq, width), BF16)] * 2,
        compiler_params=_cparams(("arbitrary", "arbitrary")),
        name="fnet",
    )(proj3, cc.astype(BF16), sc.astype(BF16), cs.astype(BF16), ss.astype(BF16))


HIGH_HALF = -65536


def _slab_rows(s, rows, per_row):
    return pl.ds(s, rows, stride=per_row)


def _pack_rows(h, slab_ref):
    rows, d = h.shape
    half = d // 2
    per_row = half // LANES
    for s in range(per_row):
        lo = h[:, s * LANES:(s + 1) * LANES].astype(BF16).astype(F32)
        hi = h[:, half + s * LANES:half + (s + 1) * LANES].astype(BF16).astype(F32)
        lo_bits = lax.shift_right_logical(lax.bitcast_convert_type(lo, I32), jnp.int32(16))
        hi_bits = lax.bitcast_convert_type(hi, I32) & jnp.int32(HIGH_HALF)
        slab_ref[_slab_rows(s, rows, per_row), :] = hi_bits | lo_bits


def _unpack_rows(slab_ref, per_row):
    rows = slab_ref.shape[0] // per_row
    los, his = [], []
    for s in range(per_row):
        w = slab_ref[_slab_rows(s, rows, per_row), :]
        los.append(lax.bitcast_convert_type(lax.shift_left(w, jnp.int32(16)), F32))
        his.append(lax.bitcast_convert_type(w & jnp.int32(HIGH_HALF), F32))
    return jnp.concatenate(los, axis=-1).astype(BF16), jnp.concatenate(his, axis=-1).astype(BF16)


def _outproj_kernel(*refs, glu):
    if glu:
        (x_ref, a_ref, b_ref, gw_ref, gb_ref, wa_ref, wb_ref, gm_ref, gain_ref, sh_ref, sc_ref,
         rwh_ref, rwl_ref, rb_ref, x1_ref, h_ref, idx_ref, gate_ref, cnt_ref, base_ref) = refs
    else:
        (x_ref, a_ref, b_ref, wa_ref, wb_ref, gm_ref, gain_ref, sh_ref, sc_ref,
         rwh_ref, rwl_ref, rb_ref, x1_ref, h_ref, idx_ref, gate_ref, cnt_ref, base_ref) = refs

    @pl.when(pl.program_id(0) == 0)
    def _():
        base_ref[...] = jnp.zeros(base_ref.shape, F32)

    bm = b_ref[...]
    if glu:
        gl = jnp.dot(bm.astype(BF16), gw_ref[...], preferred_element_type=F32) + gb_ref[...]
        bm = (bm.astype(F32) * jax.nn.sigmoid(gl)).astype(BF16)
    y = (jnp.dot(a_ref[...], wa_ref[...], preferred_element_type=F32)
         + jnp.dot(bm, wb_ref[...], preferred_element_type=F32))
    x1 = x_ref[...] + gm_ref[0] * y
    x1_ref[...] = x1
    h = _norm_modulate(x1, gain_ref[...], sh_ref[0], sc_ref[0])
    _pack_rows(h, h_ref)

    h_hi = h.astype(BF16)
    h_lo = (h - h_hi.astype(F32)).astype(BF16)
    logits = (jnp.dot(h_hi, rwh_ref[...], preferred_element_type=F32)
              + jnp.dot(h_lo, rwh_ref[...], preferred_element_type=F32)
              + jnp.dot(h_hi, rwl_ref[...], preferred_element_type=F32)
              + rb_ref[...])
    lane = lax.broadcasted_iota(I32, logits.shape, 1)
    vals, idxs = [], []
    for _ in range(TOP_K):
        m = jnp.max(logits, axis=-1, keepdims=True)
        ik = jnp.min(jnp.where(logits == m, lane, ROUTER_PAD), axis=-1, keepdims=True)
        vals.append(m)
        idxs.append(ik)
        logits = jnp.where(lane == ik, -jnp.inf, logits)
    exps = [jnp.exp(v - vals[0]) for v in vals]
    denom = exps[0] + exps[1] + exps[2] + exps[3]

    tm = lane.shape[0]
    onehot = jnp.zeros(lane.shape, F32)
    for k in range(TOP_K):
        onehot = onehot + (lane == idxs[k]).astype(F32)
    before = (lax.broadcasted_iota(I32, (tm, tm), 1) < lax.broadcasted_iota(I32, (tm, tm), 0)).astype(BF16)
    count = jnp.dot(before, onehot.astype(BF16), preferred_element_type=F32) + base_ref[...]
    new_base = base_ref[...] + jnp.sum(onehot, axis=0, keepdims=True)
    base_ref[...] = new_base
    cnt_ref[...] = new_base.astype(I32)

    idx_out = jnp.zeros(lane.shape, I32)
    gate_out = jnp.zeros(lane.shape, F32)
    for k in range(TOP_K):
        rank = jnp.sum(jnp.where(lane == idxs[k], count, 0.0), axis=-1, keepdims=True).astype(I32)
        idx_out = jnp.where(lane == k, idxs[k], idx_out)
        idx_out = jnp.where(lane == TOP_K + k, rank, idx_out)
        gate_out = jnp.where(lane == k, exps[k] / denom, gate_out)
    idx_ref[...] = idx_out
    gate_ref[...] = gate_out


def _out_proj(x2d, mix_a, mix_b, w_out, g_mix, gain, shift, scale, router_w, router_b, seq,
              glu_w=None, glu_b=None):
    t, d = x2d.shape
    wa_rows = mix_a.shape[1]
    wb_rows = mix_b.shape[1]
    tm = 256
    per_seq = seq // tm
    n_exp = router_w.shape[1]
    rw = jnp.zeros((d, ROUTER_PAD), F32).at[:, :n_exp].set(router_w)
    rw_hi = rw.astype(BF16)
    rw_lo = (rw - rw_hi.astype(F32)).astype(BF16)
    rb = jnp.full((1, ROUTER_PAD), NEG_BIG, F32).at[0, :n_exp].set(router_b)
    w_bf = w_out.astype(BF16)
    glu = glu_w is not None

    def rows(width):
        return pl.BlockSpec((tm, width), lambda i: (i, 0))

    def full(r, c):
        return pl.BlockSpec((r, c), lambda i: (0, 0))

    def per_batch():
        return pl.BlockSpec((1, 1, d), lambda i: (i // per_seq, 0, 0))

    in_specs = [rows(d), rows(wa_rows), rows(wb_rows)]
    args = [x2d, mix_a, mix_b]
    if glu:
        in_specs += [full(wb_rows, wb_rows), full(1, wb_rows)]
        args += [glu_w.astype(BF16), glu_b.reshape(1, wb_rows)]
    in_specs += [pl.BlockSpec((wa_rows, d), lambda i: (0, 0)),
                 pl.BlockSpec((wb_rows, d), lambda i: (wa_rows // wb_rows, 0)),
                 per_batch(), full(1, d), per_batch(), per_batch(),
                 full(d, ROUTER_PAD), full(d, ROUTER_PAD), full(1, ROUTER_PAD)]
    args += [w_bf, w_bf, g_mix, gain.reshape(1, d), shift, scale, rw_hi, rw_lo, rb]
    return pl.pallas_call(
        functools.partial(_outproj_kernel, glu=glu),
        grid=(t // tm,),
        in_specs=in_specs,
        out_specs=[rows(d), pl.BlockSpec((tm * (d // (2 * LANES)), LANES), lambda i: (i, 0)),
                   rows(ROUTER_PAD), rows(ROUTER_PAD), full(1, ROUTER_PAD)],
        out_shape=[jax.ShapeDtypeStruct((t, d), F32), jax.ShapeDtypeStruct((t * (d // (2 * LANES)), LANES), I32),
                   jax.ShapeDtypeStruct((t, ROUTER_PAD), I32), jax.ShapeDtypeStruct((t, ROUTER_PAD), F32),
                   jax.ShapeDtypeStruct((1, ROUTER_PAD), I32)],
        scratch_shapes=[pltpu.VMEM((1, ROUTER_PAD), F32)],
        compiler_params=_cparams(("arbitrary",)),
        name="out_proj_glu" if glu else "out_proj",
    )(*args)


def _routing_tables(top_idx, rank, counts, n_tok):
    n_assign = n_tok * TOP_K
    r = MOE_ROWS
    n_sb = n_assign // (2 * r) + N_EXPERTS
    n_rb = 2 * n_sb
    experts = jnp.arange(N_EXPERTS, dtype=I32)
    used = (counts + r - 1) // r
    nsb = (used + 1) // 2
    sb_end = jnp.cumsum(nsb)
    sb_start = sb_end - nsb
    blk_start = 2 * sb_start
    row_start = jnp.sum(jnp.where(top_idx[:, :, None] == experts[None, None, :],
                                  (blk_start * r)[None, None, :], 0), axis=-1)
    dest = (row_start + rank).reshape(-1).astype(I32)
    rb = jnp.arange(n_rb, dtype=I32)
    rb_e = jnp.sum((sb_end[None, :] <= (rb // 2)[:, None]).astype(I32), axis=1)
    in_range = rb_e < N_EXPERTS
    rb_e = jnp.minimum(rb_e, N_EXPERTS - 1).astype(I32)
    rb_valid = (in_range & (rb - blk_start[rb_e] < used[rb_e])).astype(I32)
    tail = jnp.where(used > 0, blk_start + used - 1, -1)
    empty = jnp.where(2 * nsb > used, blk_start + used, -1)
    spare = 2 * sb_end[-1] + jnp.arange(2 * N_EXPERTS, dtype=I32)
    spare = jnp.where(spare < n_rb, spare, -1)
    zero_blocks = jnp.concatenate([tail, empty, spare]).astype(I32)
    return dict(dest=dest, rb_e=rb_e, rb_valid=rb_valid, zero_blocks=zero_blocks,
                n_valid_sb=sb_end[-1], nsb=nsb.astype(I32), sb_start=sb_start.astype(I32),
                n_sb=n_sb, n_rb=n_rb, n_slots=n_rb * r)


def _dispatch_kernel(dest_ref, zb_ref, h_ref, xs_hbm, zero_ref, zsem_ref, sem_ref):
    i = pl.program_id(0)
    td = DISPATCH_TOKENS
    sr = h_ref.shape[0] // td
    blk = MOE_ROWS * sr

    def slab(ref, row):
        return ref.at[pl.ds(pl.multiple_of(row * sr, sr), sr), :]

    @pl.when(i == 0)
    def _():
        zero_ref[...] = jnp.zeros(zero_ref.shape, zero_ref.dtype)

        def zero_copy(z):
            row0 = pl.multiple_of(zb_ref[z] * blk, blk)
            return pltpu.make_async_copy(zero_ref, xs_hbm.at[pl.ds(row0, blk), :], zsem_ref.at[0])

        def start(z, _):
            @pl.when(zb_ref[z] >= 0)
            def _():
                zero_copy(z).start()
            return 0

        def wait(z, _):
            @pl.when(zb_ref[z] >= 0)
            def _():
                zero_copy(z).wait()
            return 0

        lax.fori_loop(0, zb_ref.shape[0], start, 0)
        lax.fori_loop(0, zb_ref.shape[0], wait, 0)

    def group(g, _):
        for s in range(DMA_UNROLL):
            t = g * DMA_UNROLL + s
            for k in range(TOP_K):
                dst = dest_ref[(i * td + t) * TOP_K + k]
                pltpu.make_async_copy(slab(h_ref, t), slab(xs_hbm, dst), sem_ref.at[0]).start()
        return 0

    lax.fori_loop(0, td // DMA_UNROLL, group, 0)
    for _ in range(TOP_K):
        pltpu.make_async_copy(h_ref, xs_hbm.at[pl.ds(0, td * sr), :], sem_ref.at[0]).wait()


def _moe_dispatch(h_slabs, n_tok, rt):
    sr = h_slabs.shape[0] // n_tok
    td = DISPATCH_TOKENS
    grid_spec = pltpu.PrefetchScalarGridSpec(
        num_scalar_prefetch=2,
        grid=(n_tok // td,),
        in_specs=[pl.BlockSpec((td * sr, LANES), lambda i, dst, zb: (i, 0))],
        out_specs=pl.BlockSpec(memory_space=pl.ANY),
        scratch_shapes=[pltpu.VMEM((MOE_ROWS * sr, LANES), h_slabs.dtype), pltpu.SemaphoreType.DMA((1,)),
                        pltpu.SemaphoreType.DMA((1,))],
    )
    return pl.pallas_call(
        _dispatch_kernel,
        grid_spec=grid_spec,
        out_shape=jax.ShapeDtypeStruct((rt['n_slots'] * sr, LANES), h_slabs.dtype),
        compiler_params=_cparams(("arbitrary",)),
        name="moe_dispatch",
    )(rt['dest'], rt['zero_blocks'], h_slabs)


def _cast_rows(src_ref, dst_ref, chunk=256):
    chunk = min(chunk, dst_ref.shape[0])
    n = dst_ref.shape[0] // chunk

    def body(c, _):
        sl = pl.ds(pl.multiple_of(c * chunk, chunk), chunk)
        dst_ref[sl, :] = src_ref[sl, :].astype(dst_ref.dtype)
        return 0

    lax.fori_loop(0, n, body, 0)


def _stagger_tables(group, cols, parts):
    n = group.shape[0]
    idx = jnp.arange(n, dtype=I32)
    is_first = jnp.concatenate([jnp.ones((1,), bool), group[1:] != group[:-1]])
    first_cur = lax.cummax(jnp.where(is_first, idx, 0))
    nxt = jnp.where(is_first, idx, n)
    first_next = jnp.concatenate([lax.cummin(nxt[::-1])[::-1][1:], jnp.full((1,), n, I32)])
    out = [[] for _ in cols]
    for p in range(parts):
        switch = (idx >= jnp.maximum(first_next - p, first_cur + 1)) & (first_next < n)
        eff = jnp.where(switch, first_next, idx)
        for k, c in enumerate(cols):
            out[k].append(c[eff])
    return [jnp.stack(o) for o in out]


def _moe_up_kernel(sb_ref, j_ref, e_ref, first_ref, va_ref, vb_ref, pe_ref, pj_ref, x_ref, *refs):
    parts = MOE_W_PARTS
    wg_refs = refs[:parts]
    wu_refs = refs[parts:2 * parts]
    bg_ref, bu_ref, o_ref, wgs_ref, wus_ref = refs[2 * parts:]
    i = pl.program_id(0)
    rows = wgs_ref.shape[0] // parts
    r = MOE_ROWS
    sr = x_ref.shape[0] // (2 * r)

    @pl.when(first_ref[i] == 1)
    def _():
        for p in range(parts):
            _cast_rows(wg_refs[p].at[0, 0], wgs_ref.at[pl.ds(p * rows, rows)])
            _cast_rows(wu_refs[p].at[0, 0], wus_ref.at[pl.ds(p * rows, rows)])

    def compute(slabs):
        x_lo, x_hi = _unpack_rows(slabs, sr)
        half = wgs_ref.shape[0] // 2

        def proj(ws_ref, b_ref):
            return (jnp.dot(x_lo, ws_ref[:half, :], preferred_element_type=F32)
                    + jnp.dot(x_hi, ws_ref[half:, :], preferred_element_type=F32) + b_ref[0])

        gate = jnp.minimum(proj(wgs_ref, bg_ref), SWIGLU_LIMIT)
        up = jnp.clip(proj(wus_ref, bu_ref), -SWIGLU_LIMIT, SWIGLU_LIMIT)
        return ((up + 1.0) * gate * jax.nn.sigmoid(SWIGLU_ALPHA * gate)).astype(o_ref.dtype)

    va = va_ref[i] == 1
    vb = vb_ref[i] == 1

    @pl.when(vb)
    def _():
        o_ref[...] = compute(x_ref)

    @pl.when(va & jnp.logical_not(vb))
    def _():
        o_ref[pl.ds(0, r), :] = compute(x_ref.at[pl.ds(0, r * sr), :])
        o_ref[pl.ds(r, r), :] = jnp.zeros((r, o_ref.shape[1]), o_ref.dtype)

    @pl.when(jnp.logical_not(va))
    def _():
        o_ref[...] = jnp.zeros(o_ref.shape, o_ref.dtype)


def _moe_items(rt, nj, n_exp):
    n_sb = rt['n_sb']
    n_items = n_sb * nj
    sb = jnp.arange(n_sb, dtype=I32)
    e_of = rt['rb_e'][2 * sb]
    nsb = rt['nsb'].at[n_exp - 1].add(n_sb - rt['n_valid_sb'])
    q = sb - rt['sb_start'][e_of]
    pos = (nj * rt['sb_start'][e_of][:, None]
           + jnp.arange(nj, dtype=I32)[None, :] * nsb[e_of][:, None] + q[:, None]).reshape(-1)
    sb2 = jnp.broadcast_to(sb[:, None], (n_sb, nj)).reshape(-1)
    j2 = jnp.broadcast_to(jnp.arange(nj, dtype=I32)[None, :], (n_sb, nj)).reshape(-1)
    it_sb = jnp.zeros((n_items,), I32).at[pos].set(sb2)
    it_j = jnp.zeros((n_items,), I32).at[pos].set(j2)
    it_va = rt['rb_valid'][2 * it_sb]
    it_vb = rt['rb_valid'][2 * it_sb + 1]
    it_e = e_of[it_sb]
    prev_e = jnp.concatenate([jnp.full((1,), -1, I32), it_e[:-1]])
    prev_j = jnp.concatenate([jnp.full((1,), -1, I32), it_j[:-1]])
    it_first = ((it_e != prev_e) | (it_j != prev_j)).astype(I32)
    part_e, part_j = _stagger_tables(jnp.cumsum(it_first), [it_e, it_j], MOE_W_PARTS)
    return n_items, (it_sb, it_j, it_e, it_first, it_va, it_vb, part_e, part_j)


def _moe_up(x_sorted, w_gu, b_gu, layer, rt):
    _, n_exp, d, two_f = w_gu.shape
    n_slots = rt['n_slots']
    sr = x_sorted.shape[0] // n_slots
    f = two_f // 2
    r, tn = MOE_ROWS, MOE_UP_TN
    nj = f // tn
    parts = MOE_W_PARTS
    n_items, tables = _moe_items(rt, nj, n_exp)

    def w_spec(p, col0):
        return pl.BlockSpec((1, 1, d // parts, tn),
                            lambda i, sbt, jt, et, ft, va, vb, pe, pj: (layer, pe[p, i], p, col0 + pj[p, i]))

    grid_spec = pltpu.PrefetchScalarGridSpec(
        num_scalar_prefetch=8,
        grid=(n_items,),
        in_specs=([pl.BlockSpec((2 * r * sr, LANES), lambda i, sbt, jt, et, ft, va, vb, pe, pj: (sbt[i], 0))]
                  + [w_spec(p, 0) for p in range(parts)] + [w_spec(p, nj) for p in range(parts)]
                  + [pl.BlockSpec((1, 1, tn), lambda i, sbt, jt, et, ft, va, vb, pe, pj: (et[i], 0, jt[i])),
                     pl.BlockSpec((1, 1, tn), lambda i, sbt, jt, et, ft, va, vb, pe, pj: (et[i], 0, nj + jt[i]))]),
        out_specs=pl.BlockSpec((2 * r, tn), lambda i, sbt, jt, et, ft, va, vb, pe, pj: (sbt[i], jt[i])),
        scratch_shapes=[pltpu.VMEM((d, tn), BF16), pltpu.VMEM((d, tn), BF16)],
    )
    b3 = b_gu.reshape(n_exp, 1, two_f)
    return pl.pallas_call(
        _moe_up_kernel,
        grid_spec=grid_spec,
        out_shape=jax.ShapeDtypeStruct((n_slots, f), BF16),
        compiler_params=_cparams(("arbitrary",)),
        name="moe_up",
    )(*tables, x_sorted, *([w_gu] * (2 * parts)), b3, b3)


def _moe_dn_kernel(sb_ref, j_ref, e_ref, first_ref, va_ref, vb_ref, pe_ref, pj_ref, a_ref, *refs):
    parts = MOE_W_PARTS
    w_refs = refs[:parts]
    b_ref, o_ref, ws_ref = refs[parts:]
    i = pl.program_id(0)
    rows = ws_ref.shape[0] // parts
    r = MOE_ROWS

    @pl.when(first_ref[i] == 1)
    def _():
        for p in range(parts):
            _cast_rows(w_refs[p].at[0, 0], ws_ref.at[pl.ds(p * rows, rows)])

    va = va_ref[i] == 1
    vb = vb_ref[i] == 1

    @pl.when(vb)
    def _():
        o_ref[...] = jnp.dot(a_ref[...], ws_ref[...], preferred_element_type=F32) + b_ref[0]

    @pl.when(va & jnp.logical_not(vb))
    def _():
        o_ref[pl.ds(0, r), :] = (jnp.dot(a_ref[pl.ds(0, r), :], ws_ref[...], preferred_element_type=F32)
                                 + b_ref[0])
        o_ref[pl.ds(r, r), :] = jnp.zeros((r, o_ref.shape[1]), o_ref.dtype)

    @pl.when(jnp.logical_not(va))
    def _():
        o_ref[...] = jnp.zeros(o_ref.shape, o_ref.dtype)


def _moe_dn(act, w_dn, b_dn, layer, rt):
    n_slots, f = act.shape
    _, n_exp, _, d = w_dn.shape
    r, tn = MOE_ROWS, MOE_DN_TN
    nj = d // tn
    parts = MOE_W_PARTS
    n_items, tables = _moe_items(rt, nj, n_exp)

    def w_spec(p):
        return pl.BlockSpec((1, 1, f // parts, tn),
                            lambda i, sbt, jt, et, ft, va, vb, pe, pj: (layer, pe[p, i], p, pj[p, i]))

    grid_spec = pltpu.PrefetchScalarGridSpec(
        num_scalar_prefetch=8,
        grid=(n_items,),
        in_specs=([pl.BlockSpec((2 * r, f), lambda i, sbt, jt, et, ft, va, vb, pe, pj: (sbt[i], 0))]
                  + [w_spec(p) for p in range(parts)]
                  + [pl.BlockSpec((1, 1, tn), lambda i, sbt, jt, et, ft, va, vb, pe, pj: (et[i], 0, jt[i]))]),
        out_specs=pl.BlockSpec((2 * r, tn), lambda i, sbt, jt, et, ft, va, vb, pe, pj: (sbt[i], jt[i])),
        scratch_shapes=[pltpu.VMEM((f, tn), BF16)],
    )
    return pl.pallas_call(
        _moe_dn_kernel,
        grid_spec=grid_spec,
        out_shape=jax.ShapeDtypeStruct((n_slots, d), F32),
        compiler_params=_cparams(("arbitrary",)),
        name="moe_dn",
    )(*tables, act, *([w_dn] * parts), b_dn.reshape(n_exp, 1, d))


def _combine_kernel(dest_ref, x_ref, gate_ref, g_ref, fg_ref, y_hbm, o_ref, buf_ref, sem_ref, *, final):
    i = pl.program_id(0)
    n_steps = pl.num_programs(0)
    tc = COMBINE_TOKENS

    def issue(step, slot):
        def group(g, _):
            for s in range(DMA_UNROLL):
                t = g * DMA_UNROLL + s
                for k in range(TOP_K):
                    src = dest_ref[(step * tc + t) * TOP_K + k]
                    pltpu.make_async_copy(y_hbm.at[pl.ds(src, 1), :], buf_ref.at[slot, k, pl.ds(t, 1), :],
                                          sem_ref.at[slot]).start()
            return 0

        lax.fori_loop(0, tc // DMA_UNROLL, group, 0)

    @pl.when(i == 0)
    def _():
        issue(0, 0)

    @pl.when(i + 1 < n_steps)
    def _():
        issue(i + 1, (i + 1) % 2)

    slot = i % 2
    for k in range(TOP_K):
        pltpu.make_async_copy(y_hbm.at[pl.ds(0, tc), :], buf_ref.at[slot, k], sem_ref.at[slot]).wait()
    gates = gate_ref[...]
    y = gates[:, 0:1] * buf_ref[slot, 0]
    for k in range(1, TOP_K):
        y = y + gates[:, k:k + 1] * buf_ref[slot, k]
    x2 = x_ref[...] + g_ref[0] * y
    if final:
        x2 = x2 * lax.rsqrt(jnp.mean(x2 * x2, axis=-1, keepdims=True) + EPS) * fg_ref[...]
    o_ref[...] = x2


def _combine(x1, y_sorted, gates, rt, g_ffn, final_gain, seq, final):
    t, d = x1.shape
    tc = COMBINE_TOKENS
    per_seq = seq // tc
    grid_spec = pltpu.PrefetchScalarGridSpec(
        num_scalar_prefetch=1,
        grid=(t // tc,),
        in_specs=[pl.BlockSpec((tc, d), lambda i, dst: (i, 0)),
                  pl.BlockSpec((tc, ROUTER_PAD), lambda i, dst: (i, 0)),
                  pl.BlockSpec((1, 1, d), lambda i, dst: (i // per_seq, 0, 0)),
                  pl.BlockSpec((1, d), lambda i, dst: (0, 0)),
                  pl.BlockSpec(memory_space=pl.ANY)],
        out_specs=pl.BlockSpec((tc, d), lambda i, dst: (i, 0)),
        scratch_shapes=[pltpu.VMEM((2, TOP_K, tc, d), F32), pltpu.SemaphoreType.DMA((2,))],
    )
    return pl.pallas_call(
        functools.partial(_combine_kernel, final=final),
        grid_spec=grid_spec,
        out_shape=jax.ShapeDtypeStruct((t, d), F32),
        compiler_params=_cparams(("arbitrary",)),
        name="combine_final" if final else "combine",
    )(rt['dest'], x1, gates, g_ffn, final_gain.reshape(1, d), y_sorted)


def _moe(x1, h, idx, gates, counts, w_gu, b_gu, w_dn, b_dn, layer, g_ffn, final_gain, seq, final):
    n_tok = x1.shape[0]
    rt = _routing_tables(idx[:, :TOP_K], idx[:, TOP_K:2 * TOP_K], counts[0, :N_EXPERTS], n_tok)
    x_sorted = _moe_dispatch(h, n_tok, rt)
    act = _moe_up(x_sorted, w_gu, b_gu, layer, rt)
    y_sorted = _moe_dn(act, w_dn, b_dn, layer, rt)
    return _combine(x1, y_sorted, gates, rt, g_ffn, final_gain, seq, final)


def kernel(x, c, ada_w, ada_b, norm_mix_gain, norm_ffn_gain, ab_w_in, ab_w_out, hg_lb_logits, hg_norm_gain, s5_lam_re, s5_lam_im, s5_log_dt, s5_b_re, s5_b_im, s5_c_re, s5_c_im, s5_d, s5_glu_w, s5_glu_b, cd_w_in, cd_w_out, ret_norm_gain, router_w, router_b, moe_w_gu, moe_b_gu, moe_w_dn, moe_b_dn, final_gain):
    bsz, seq, d = x.shape
    depth = ada_w.shape[0]
    n_tok = bsz * seq
    hg_width = hg_lb_logits.shape[1]
    s5_width = s5_glu_w.shape[1]
    ret_width = ret_norm_gain.shape[1]
    fnet_width = cd_w_out.shape[1] - ret_width

    lower_bounds = jnp.cumsum(jax.nn.softmax(hg_lb_logits.astype(F32), axis=0), axis=0)
    mod = _ada_mod(c, ada_w, ada_b)
    xr = x.reshape(n_tok, d)
    for layer in range(depth):
        sh_mix, sc_mix, g_mix, sh_ffn, sc_ffn, g_ffn = (
            mod[layer, :, k * d:(k + 1) * d].reshape(bsz, 1, d) for k in range(6))
        j = layer // 2
        if layer % 2 == 0:
            proj = _in_proj(xr, norm_mix_gain[layer], sh_mix, sc_mix, ab_w_in[j].astype(BF16), seq)
            proj3 = proj.reshape(bsz, seq, proj.shape[1])
            mix_a = _hgrn2(proj3, lower_bounds[j], hg_norm_gain[j], hg_width)
            tables = _s5c_tables(s5_lam_re[j], s5_lam_im[j], s5_log_dt[j], s5_b_re[j], s5_b_im[j],
                                 s5_c_re[j], s5_c_im[j], s5_d[j])
            mix_b = _s5c(proj3, 5 * hg_width, s5_width, tables)
            x1, h, idx, gate, counts = _out_proj(
                xr, mix_a.reshape(n_tok, hg_width), mix_b.reshape(n_tok, s5_width), ab_w_out[j],
                g_mix, norm_ffn_gain[layer], sh_ffn, sc_ffn, router_w[layer], router_b[layer], seq,
                glu_w=s5_glu_w[j], glu_b=s5_glu_b[j])
        else:
            proj = _in_proj(xr, norm_mix_gain[layer], sh_mix, sc_mix, cd_w_in[j].astype(BF16), seq)
            proj3 = proj.reshape(bsz, seq, proj.shape[1])
            mix_a = _retention(proj3, ret_norm_gain[j], ret_width)
            mix_b = _fnet(proj3, 4 * ret_width, fnet_width)
            x1, h, idx, gate, counts = _out_proj(
                xr, mix_a.reshape(n_tok, ret_width), mix_b.reshape(n_tok, fnet_width), cd_w_out[j],
                g_mix, norm_ffn_gain[layer], sh_ffn, sc_ffn, router_w[layer], router_b[layer], seq)
        xr = _moe(x1, h, idx, gate, counts, moe_w_gu, moe_b_gu[layer], moe_w_dn, moe_b_dn[layer],
                  layer, g_ffn, final_gain, seq, final=(layer == depth - 1))
    return xr.reshape(bsz, seq, d)
```

```python
import functools
import math

import jax
import jax.numpy as jnp
from jax import lax
from jax.experimental import pallas as pl
from jax.experimental.pallas import tpu as pltpu

F32 = jnp.float32
BF16 = jnp.bfloat16
I32 = jnp.int32

EPS = 1e-6
LANES = 128
SUBLANES = 8
VMEM_LIMIT = 56 * 1024 * 1024

HG_HEAD_DIM = 128
HG_CHUNK = 64
HG_GROUP = 256
HG_EXP_CLAMP = 80.0

S5_GROUP = 16
S5_STATE = 64
S5_CHUNK = 16
S5_UNROLL = 8
S5_TILE_PAIRS = 4
S5_ROW_SPLIT = 4

RET_HEAD_DIM = 256
RET_Q_TILE = 256
ROPE_BASE = 10000.0

FNET_GROUPS = 4
FNET_ROW_TILE = 512

N_EXPERTS = 32
TOP_K = 4
SWIGLU_LIMIT = 7.0
SWIGLU_ALPHA = 1.702
MOE_ROWS = 256
MOE_UP_TN = 1024
MOE_DN_TN = 1024
MOE_W_PARTS = 4
DISPATCH_TOKENS = 512
COMBINE_TOKENS = 256
DMA_UNROLL = 8
ROUTER_PAD = LANES
NEG_BIG = -1e30


def _cparams(semantics):
    return pltpu.CompilerParams(dimension_semantics=semantics, vmem_limit_bytes=VMEM_LIMIT)


def _ada_kernel(c_ref, w_ref, b_ref, o_ref):
    c = c_ref[...]
    cond = c * jax.nn.sigmoid(c)
    o_ref[0] = jnp.dot(cond.astype(BF16), w_ref[0].astype(BF16),
                       preferred_element_type=F32) + b_ref[0]


def _ada_mod(c, ada_w, ada_b):
    depth, d, n = ada_w.shape
    bsz = c.shape[0]
    tn = 1024
    return pl.pallas_call(
        _ada_kernel,
        grid=(depth, n // tn),
        in_specs=[
            pl.BlockSpec((bsz, d), lambda l, j: (0, 0)),
            pl.BlockSpec((1, d, tn), lambda l, j: (l, 0, j)),
            pl.BlockSpec((1, 1, tn), lambda l, j: (l, 0, j)),
        ],
        out_specs=pl.BlockSpec((1, bsz, tn), lambda l, j: (l, 0, j)),
        out_shape=jax.ShapeDtypeStruct((depth, bsz, n), F32),
        compiler_params=_cparams(("arbitrary", "arbitrary")),
        name="ada_mod",
    )(c, ada_w, ada_b.reshape(depth, 1, n))


def _norm_modulate(x, gain, shift, scale):
    ms = jnp.mean(x * x, axis=-1, keepdims=True)
    y = x * lax.rsqrt(ms + EPS) * gain
    return y * (1.0 + scale) + shift


def _inproj_kernel(x_ref, gain_ref, sh_ref, sc_ref, w_ref, o_ref, h_ref):
    @pl.when(pl.program_id(1) == 0)
    def _():
        rows = 128

        def slab(c, _):
            sl = pl.ds(pl.multiple_of(c * rows, rows), rows)
            h_ref[sl, :] = _norm_modulate(x_ref[sl, :], gain_ref[...], sh_ref[0], sc_ref[0]).astype(BF16)
            return 0

        lax.fori_loop(0, h_ref.shape[0] // rows, slab, 0)

    o_ref[...] = jnp.dot(h_ref[...], w_ref[...], preferred_element_type=F32)


def _in_proj(x2d, gain, shift, scale, w_bf16, seq):
    t, d = x2d.shape
    n = w_bf16.shape[1]
    tm = 1024
    tn = 1024 if n % 1024 == 0 else 512
    per_seq = seq // tm
    return pl.pallas_call(
        _inproj_kernel,
        grid=(t // tm, n // tn),
        in_specs=[
            pl.BlockSpec((tm, d), lambda i, j: (i, 0)),
            pl.BlockSpec((1, d), lambda i, j: (0, 0)),
            pl.BlockSpec((1, 1, d), lambda i, j: (i // per_seq, 0, 0)),
            pl.BlockSpec((1, 1, d), lambda i, j: (i // per_seq, 0, 0)),
            pl.BlockSpec((d, tn), lambda i, j: (0, j)),
        ],
        out_specs=pl.BlockSpec((tm, tn), lambda i, j: (i, j)),
        out_shape=jax.ShapeDtypeStruct((t, n), F32),
        scratch_shapes=[pltpu.VMEM((tm, d), BF16)],
        compiler_params=_cparams(("arbitrary", "arbitrary")),
        name="in_proj",
    )(x2d, gain.reshape(1, d), shift, scale, w_bf16)


def _split3(a):
    hi = a.astype(BF16)
    r1 = a - hi.astype(F32)
    mid = r1.astype(BF16)
    lo = (r1 - mid.astype(F32)).astype(BF16)
    return hi, mid, lo


def _tri_sum(tri, a):
    hi, mid, lo = _split3(a)
    return (jnp.dot(tri, hi, preferred_element_type=F32)
            + jnp.dot(tri, mid, preferred_element_type=F32)
            + jnp.dot(tri, lo, preferred_element_type=F32))


def _dot_nt(a, b):
    return lax.dot_general(a, b, (((1,), (1,)), ((), ())), preferred_element_type=F32)


def _dot_tn(a, b):
    return lax.dot_general(a, b, (((0,), (0,)), ((), ())), preferred_element_type=F32)


def _hgrn2_kernel(q_ref, zf_ref, zb_ref, v_ref, g_ref, lb_ref, gain_ref, o_ref, acc_ref, accb_ref):
    seq = q_ref.shape[1]
    ln = HG_CHUNK
    gr = HG_GROUP
    n_groups = seq // gr
    per_group = gr // ln
    lb = lb_ref[...]
    gain = gain_ref[...]
    row = lax.broadcasted_iota(I32, (gr, gr), 0)
    col = lax.broadcasted_iota(I32, (gr, gr), 1)
    chunk_lo = (row // ln) * ln
    chunk_hi = chunk_lo + ln
    lower_incl = (col <= row) & (col >= chunk_lo)
    upper_strict = (col > row) & (col < chunk_hi)
    tri_prefix = lower_incl.astype(BF16)
    tri_suffix = ((col >= row) & (col < chunk_hi)).astype(BF16)
    mid = ln // 2

    def per_chunk_rows(a, offset):
        return jnp.concatenate(
            [jnp.broadcast_to(a[j * ln + offset:j * ln + offset + 1, :], (ln, a.shape[1]))
             for j in range(per_group)], axis=0)

    rows = [slice(j * ln, (j + 1) * ln) for j in range(per_group)]

    def load(gi, z_ref, forward):
        sl = pl.ds(pl.multiple_of(gi * gr, gr), gr)
        f = lb + (1.0 - lb) * jax.nn.sigmoid(z_ref[0, sl, :])
        log_f = jnp.log(f)
        d = dict(sl=sl, forward=forward, q=q_ref[0, sl, :], v=v_ref[0, sl, :].astype(BF16), k=1.0 - f)
        if forward:
            d['cum'] = _tri_sum(tri_prefix, log_f)
        else:
            d['cum'] = _tri_sum(tri_suffix, log_f)
        return d

    def local_states(d):
        edge_off = ln - 1 if d['forward'] else 0
        kd = (d['k'] * jnp.exp(per_chunk_rows(d['cum'], edge_off) - d['cum'])).astype(BF16)
        d['local'] = [_dot_tn(d['v'][rs], kd[rs]) for rs in rows]

    def scores(d):
        ref_rows = per_chunk_rows(d['cum'], mid - 1 if d['forward'] else mid)
        qe = d['q'] * jnp.exp(jnp.minimum(d['cum'] - ref_rows, HG_EXP_CLAMP))
        ke = d['k'] * jnp.exp(jnp.minimum(ref_rows - d['cum'], HG_EXP_CLAMP))
        mask = lower_incl if d['forward'] else upper_strict
        d['scores'] = jnp.where(mask, _dot_nt(qe.astype(BF16), ke.astype(BF16)), 0.0)

    def inter(d, state_t):
        edge_off = ln - 1 if d['forward'] else 0
        qc = (d['q'] * jnp.exp(d['cum'])).astype(BF16)
        entering = [None] * per_group
        order = range(per_group) if d['forward'] else range(per_group - 1, -1, -1)
        for j in order:
            entering[j] = state_t.astype(BF16)
            edge = d['cum'][j * ln + edge_off:j * ln + edge_off + 1, :]
            state_t = state_t * jnp.exp(edge) + d['local'][j]
        d['inter'] = jnp.concatenate([_dot_nt(qc[rs], entering[j]) for j, rs in enumerate(rows)], axis=0)
        return state_t

    def body(i, states):
        st_f, st_b = states
        both = [load(i, zf_ref, True), load(n_groups - 1 - i, zb_ref, False)]
        for d in both:
            local_states(d)
        for d in both:
            scores(d)
        st_f = inter(both[0], st_f)
        st_b = inter(both[1], st_b)
        for d, ref in zip(both, (acc_ref, accb_ref)):
            ref[d['sl'], :] = jnp.dot(d['scores'].astype(BF16), d['v'], preferred_element_type=F32) + d['inter']
        return st_f, st_b

    zero = jnp.zeros((HG_HEAD_DIM, HG_HEAD_DIM), F32)
    lax.fori_loop(0, n_groups, body, (zero, zero))

    def finish(gi, _):
        sl = pl.ds(pl.multiple_of(gi * gr, gr), gr)
        o = acc_ref[sl, :] + accb_ref[sl, :]
        y = o * lax.rsqrt(jnp.mean(o * o, axis=-1, keepdims=True) + EPS) * gain
        g = g_ref[0, sl, :]
        o_ref[0, sl, :] = (y * (g * jax.nn.sigmoid(g))).astype(o_ref.dtype)
        return 0

    lax.fori_loop(0, n_groups, finish, 0)


def _hgrn2(proj3, lower_bound, hg_gain, width):
    bsz, seq, _ = proj3.shape
    heads = width // HG_HEAD_DIM
    dh = HG_HEAD_DIM

    def col(k):
        return pl.BlockSpec((1, seq, dh), lambda b, h: (b, 0, k * heads + h))

    return pl.pallas_call(
        _hgrn2_kernel,
        grid=(bsz, heads),
        in_specs=[col(0), col(1), col(2), col(3), col(4),
                  pl.BlockSpec((1, dh), lambda b, h: (0, h)),
                  pl.BlockSpec((1, dh), lambda b, h: (0, 0))],
        out_specs=pl.BlockSpec((1, seq, dh), lambda b, h: (b, 0, h)),
        out_shape=jax.ShapeDtypeStruct((bsz, seq, width), BF16),
        scratch_shapes=[pltpu.VMEM((seq, dh), F32), pltpu.VMEM((seq, dh), F32)],
        compiler_params=_cparams(("arbitrary", "arbitrary")),
        name="hgrn2",
    )(proj3, proj3, proj3, proj3, proj3, lower_bound.reshape(1, width), hg_gain.reshape(1, dh))


def _s5c_kernel(u_ref, t_ref, w_ref, v_ref, a_ref, z_ref, uc_ref, e_ref, p_ref):
    ck = S5_CHUNK
    n_chunks = u_ref.shape[1] // ck
    slab = n_chunks // S5_ROW_SPLIT
    gpt = 2 * S5_TILE_PAIRS
    half_tok = ck // 2
    lane_grp = lax.broadcasted_iota(I32, (slab, LANES), 1) // S5_GROUP

    def token_rows(rq, tok):
        return pl.ds(rq * slab * ck + tok, slab, stride=ck)

    def roll_lanes(a, groups):
        shift = (groups * S5_GROUP) % LANES
        return pltpu.roll(a, shift, 1) if shift else a

    for half in range(2):
        for rq in range(S5_ROW_SPLIT):
            toks = [u_ref.at[0][token_rows(rq, half * half_tok + sl), :] for sl in range(half_tok)]
            for gl in range(gpt):
                acc = jnp.zeros((slab, LANES), F32)
                for sl in range(half_tok):
                    acc = jnp.where(lane_grp == sl, roll_lanes(toks[sl], sl - gl), acc)
                uc_ref[gl * 2 + half, rq * slab:(rq + 1) * slab, :] = acc

    seg = [pl.ds(k * LANES, LANES) for k in range(4)]

    def group_inputs(g):
        return jnp.concatenate([uc_ref[2 * g], uc_ref[2 * g + 1]], axis=-1).astype(BF16)

    for g in range(gpt):
        e_ref[:, g, :] = jnp.dot(group_inputs(g), w_ref[g], preferred_element_type=F32)

    af_r, af_i, ab_r, ab_i = a_ref[0, 0], a_ref[0, 1], a_ref[0, 2], a_ref[0, 3]

    def cmul(ar, ai, br, bi):
        return ar * br - ai * bi, ar * bi + ai * br

    def step(m, carry):
        xr, xi, yr, yi = carry
        mb = n_chunks - 1 - m
        p_ref[m, :, seg[0]] = xr
        p_ref[m, :, seg[1]] = xi
        p_ref[mb, :, seg[2]] = yr
        p_ref[mb, :, seg[3]] = yi
        dr, di = cmul(af_r, af_i, xr, xi)
        gr, gi = cmul(ab_r, ab_i, yr, yi)
        return (dr + e_ref[m, :, seg[0]], di + e_ref[m, :, seg[1]],
                gr + e_ref[mb, :, seg[2]], gi + e_ref[mb, :, seg[3]])

    zero = jnp.zeros((gpt, LANES), F32)
    lax.fori_loop(0, n_chunks, step, (zero, zero, zero, zero), unroll=S5_UNROLL)

    for g in range(gpt):
        y = (jnp.dot(group_inputs(g), t_ref[g], preferred_element_type=F32)
             + jnp.dot(p_ref[:, g, :].astype(BF16), v_ref[g], preferred_element_type=F32))
        zt = jax.nn.gelu(y)
        for k in range(2):
            uc_ref[2 * g + k] = zt[:, k * LANES:(k + 1) * LANES]

    for half in range(2):
        for rq in range(S5_ROW_SPLIT):
            cols = [uc_ref[gl * 2 + half, rq * slab:(rq + 1) * slab, :] for gl in range(gpt)]
            for sl in range(half_tok):
                acc = jnp.zeros((slab, LANES), F32)
                for gl in range(gpt):
                    acc = jnp.where(lane_grp == gl, roll_lanes(cols[gl], gl - sl), acc)
                z_ref.at[0][token_rows(rq, half * half_tok + sl), :] = acc


_einsum_f32 = functools.partial(jnp.einsum, precision=lax.Precision.HIGHEST)


def _s5c_tables(lam_re, lam_im, log_dt, b_re, b_im, c_re, c_im, d_skip):
    groups, state = lam_re.shape[1], lam_re.shape[2]
    chans = b_re.shape[2]
    ck = S5_CHUNK
    lag = jnp.arange(ck + 1, dtype=F32)[:, None, None]
    kern, w_parts, v_parts, a_parts = [], [], [], []
    for direction in (0, 1):
        lr, li = lam_re[direction].astype(F32), lam_im[direction].astype(F32)
        dt = jnp.exp(log_dt[direction].astype(F32))[:, None]
        mag = jnp.exp(lr * dt)
        abar_re = mag * jnp.cos(li * dt)
        abar_im = mag * jnp.sin(li * dt)
        den = lr * lr + li * li
        num_re = abar_re - 1.0
        coef_re = (num_re * lr + abar_im * li) / den
        coef_im = (abar_im * lr - num_re * li) / den
        bbar_re = coef_re[..., None] * b_re - coef_im[..., None] * b_im
        bbar_im = coef_re[..., None] * b_im + coef_im[..., None] * b_re
        pw_re = jnp.exp(lag * lr * dt) * jnp.cos(lag * li * dt)
        pw_im = jnp.exp(lag * lr * dt) * jnp.sin(lag * li * dt)
        ab_re = pw_re[..., None] * bbar_re - pw_im[..., None] * bbar_im
        ab_im = pw_re[..., None] * bbar_im + pw_im[..., None] * bbar_re
        cr, ci = c_re[direction].astype(F32), c_im[direction].astype(F32)
        kern.append(_einsum_f32('gcp,ngpd->ngcd', cr, ab_re[:ck]) - _einsum_f32('gcp,ngpd->ngcd', ci, ab_im[:ck]))
        order = jnp.arange(ck - 1, -1, -1) if direction == 0 else jnp.arange(ck)
        w_parts.append((ab_re[order], ab_im[order]))
        order = jnp.arange(1, ck + 1) if direction == 0 else jnp.arange(ck, 0, -1)
        a_r, a_i = pw_re[order], pw_im[order]
        v_from_re = jnp.einsum('gcp,jgp->gpjc', cr, a_r) - jnp.einsum('gcp,jgp->gpjc', ci, a_i)
        v_from_im = -(jnp.einsum('gcp,jgp->gpjc', cr, a_i) + jnp.einsum('gcp,jgp->gpjc', ci, a_r))
        v_parts.append((v_from_re, v_from_im))
        a_parts.append((pw_re[ck], pw_im[ck]))

    s_idx = jnp.arange(ck)[None, :, None]
    t_idx = jnp.arange(ck)[None, None, :]
    n_idx = jnp.arange(ck)[:, None, None]
    sel_f = (t_idx - s_idx == n_idx).astype(F32)
    sel_b = (s_idx - t_idx == n_idx).astype(F32)
    skip = jnp.eye(chans, dtype=F32)[None] * d_skip.reshape(groups, chans)[:, :, None]
    toep = (_einsum_f32('nst,ngcd->gsdtc', sel_f, kern[0]) + _einsum_f32('nst,ngcd->gsdtc', sel_b, kern[1])
            + jnp.einsum('st,gcd->gsdtc', jnp.eye(ck, dtype=F32), skip))
    t_mat = toep.reshape(groups, ck * chans, ck * chans)

    pad = LANES - state
    w_seg = [w_parts[0][0], w_parts[0][1], w_parts[1][0], w_parts[1][1]]
    w_stack = jnp.stack(w_seg, axis=0).transpose(2, 1, 4, 0, 3)
    w_mat = jnp.pad(w_stack, ((0, 0),) * 4 + ((0, pad),)).reshape(groups, ck * chans, 4 * LANES)

    v_seg = [v_parts[0][0], v_parts[0][1], v_parts[1][0], v_parts[1][1]]
    v_stack = jnp.stack(v_seg, axis=1).reshape(groups, 4, state, ck * chans)
    v_mat = jnp.pad(v_stack, ((0, 0), (0, 0), (0, pad), (0, 0))).reshape(groups, 4 * LANES, ck * chans)

    gpt = 2 * S5_TILE_PAIRS
    a_seg = jnp.stack([a_parts[0][0], a_parts[0][1], a_parts[1][0], a_parts[1][1]], axis=0)
    a_tbl = jnp.pad(a_seg, ((0, 0), (0, 0), (0, pad))).reshape(4, groups // gpt, gpt, LANES).transpose(1, 0, 2, 3)
    return t_mat.astype(BF16), w_mat.astype(BF16), v_mat.astype(BF16), a_tbl


def _s5c(proj3, col0, width, tables):
    bsz, seq, _ = proj3.shape
    t_mat, w_mat, v_mat, a_tbl = tables
    gpt = 2 * S5_TILE_PAIRS
    tiles = t_mat.shape[0] // gpt
    cols = t_mat.shape[1]
    sw = w_mat.shape[2]
    tc = width // tiles
    assert tc == LANES and cols == 2 * LANES
    n_chunks = seq // S5_CHUNK
    return pl.pallas_call(
        _s5c_kernel,
        grid=(tiles, bsz),
        in_specs=[
            pl.BlockSpec((1, seq, tc), lambda t, b: (b, 0, col0 // tc + t)),
            pl.BlockSpec((gpt, cols, cols), lambda t, b: (t, 0, 0)),
            pl.BlockSpec((gpt, cols, sw), lambda t, b: (t, 0, 0)),
            pl.BlockSpec((gpt, sw, cols), lambda t, b: (t, 0, 0)),
            pl.BlockSpec((1, 4, gpt, LANES), lambda t, b: (t, 0, 0, 0)),
        ],
        out_specs=pl.BlockSpec((1, seq, tc), lambda t, b: (b, 0, t)),
        out_shape=jax.ShapeDtypeStruct((bsz, seq, width), F32),
        scratch_shapes=[pltpu.VMEM((2 * gpt, n_chunks, LANES), F32),
                        pltpu.VMEM((n_chunks, gpt, sw), F32),
                        pltpu.VMEM((n_chunks, gpt, sw), F32)],
        compiler_params=_cparams(("arbitrary", "arbitrary")),
        name="s5",
    )(proj3, t_mat, w_mat, v_mat, a_tbl)


def _retention_kernel(q_ref, k_ref, v_ref, g_ref, cos_ref, sin_ref, lg_ref, gain_ref,
                      o_ref, qs_ref, ks_ref, vs_ref, decay_ref):
    seq = q_ref.shape[1]
    half = RET_HEAD_DIM // 2
    tq = RET_Q_TILE
    cos = cos_ref[...]
    sin = sin_ref[...]

    @pl.when(pl.program_id(1) == 0)
    def _():
        lg_fwd = lg_ref[0, 0:1, :]
        lg_bwd = lg_ref[0, 1:2, :]

        def fill(i, _):
            sl = pl.ds(pl.multiple_of(i * tq, tq), tq)
            t_idx = lax.broadcasted_iota(I32, (tq, seq), 0) + i * tq
            s_idx = lax.broadcasted_iota(I32, (tq, seq), 1)
            rel = (t_idx - s_idx).astype(F32)
            decay_ref[sl, :] = jnp.exp(jnp.where(rel >= 0.0, lg_fwd * rel, -lg_bwd * rel))
            return 0

        lax.fori_loop(0, seq // tq, fill, 0)

    def rot(t_ref, scale):
        t1 = t_ref[0, :, :half]
        t2 = t_ref[0, :, half:]
        return jnp.concatenate([(t1 * cos - t2 * sin) * scale, (t1 * sin + t2 * cos) * scale], axis=-1)

    qs_ref[...] = rot(q_ref, 1.0).astype(BF16)
    ks_ref[...] = rot(k_ref, RET_HEAD_DIM ** -0.5).astype(BF16)
    vs_ref[...] = v_ref[0].astype(BF16)
    gain = gain_ref[...]

    def q_tiles(i, _):
        sls = [pl.ds(pl.multiple_of((2 * i + a) * tq, tq), tq) for a in range(2)]
        scores = [_dot_nt(qs_ref[sl, :], ks_ref[...]) for sl in sls]
        ps = [(s * decay_ref[sl, :]).astype(BF16) for s, sl in zip(scores, sls)]
        outs = [jnp.dot(p, vs_ref[...], preferred_element_type=F32) for p in ps]
        for o, sl in zip(outs, sls):
            y = o * lax.rsqrt(jnp.mean(o * o, axis=-1, keepdims=True) + EPS) * gain
            g = g_ref[0, sl, :]
            o_ref[0, sl, :] = (y * (g * jax.nn.sigmoid(g))).astype(o_ref.dtype)
        return 0

    lax.fori_loop(0, seq // (2 * tq), q_tiles, 0)


def _retention(proj3, ret_gain, width):
    bsz, seq, _ = proj3.shape
    dh = RET_HEAD_DIM
    heads = width // dh
    inv_freq = ROPE_BASE ** (-jnp.arange(0, dh, 2, dtype=F32) / dh)
    ang = jnp.arange(seq, dtype=F32)[:, None] * inv_freq[None, :]
    cos, sin = jnp.cos(ang), jnp.sin(ang)
    log_gamma = jnp.log1p(-jnp.exp2(-5.0 - jnp.arange(heads, dtype=F32)))
    lg = jnp.stack([log_gamma, log_gamma[::-1]], axis=1)
    lg = jnp.broadcast_to(lg[:, :, None], (heads, 2, seq))

    def col(k):
        return pl.BlockSpec((1, seq, dh), lambda h, b: (b, 0, k * heads + h))

    return pl.pallas_call(
        _retention_kernel,
        grid=(heads, bsz),
        in_specs=[col(0), col(1), col(2), col(3),
                  pl.BlockSpec((seq, dh // 2), lambda h, b: (0, 0)),
                  pl.BlockSpec((seq, dh // 2), lambda h, b: (0, 0)),
                  pl.BlockSpec((1, 2, seq), lambda h, b: (h, 0, 0)),
                  pl.BlockSpec((1, dh), lambda h, b: (0, h))],
        out_specs=pl.BlockSpec((1, seq, dh), lambda h, b: (b, 0, h)),
        out_shape=jax.ShapeDtypeStruct((bsz, seq, width), BF16),
        scratch_shapes=[pltpu.VMEM((seq, dh), BF16)] * 3 + [pltpu.VMEM((seq, seq), F32)],
        compiler_params=_cparams(("arbitrary", "arbitrary")),
        name="retention",
    )(proj3, proj3, proj3, proj3, cos, sin, lg, ret_gain.reshape(1, width))


def _fnet_kernel(x_ref, cc_ref, sc_ref, cs_ref, ss_ref, o_ref, a1_ref, a2_ref):
    @pl.when(pl.program_id(1) == 0)
    def _():
        x = x_ref[0].astype(BF16)
        a1_ref[...] = jnp.dot(x, cc_ref[...], preferred_element_type=F32).astype(BF16)
        a2_ref[...] = jnp.dot(x, sc_ref[...], preferred_element_type=F32).astype(BF16)

    y = (jnp.dot(cs_ref[...], a1_ref[...], preferred_element_type=F32)
         - jnp.dot(ss_ref[...], a2_ref[...], preferred_element_type=F32))
    o_ref[0] = y.astype(o_ref.dtype)


def _dft_mats(n):
    idx = jnp.arange(n, dtype=I32)
    ang = (2.0 * math.pi / n) * ((idx[:, None] * idx[None, :]) % n).astype(F32)
    scale = n ** -0.5
    return jnp.cos(ang) * scale, jnp.sin(ang) * scale


def _fnet(proj3, col0, width):
    bsz, seq, _ = proj3.shape
    gw = width // FNET_GROUPS
    cs, ss = _dft_mats(seq)
    cg, sg = _dft_mats(gw)
    eye = jnp.eye(FNET_GROUPS, dtype=F32)
    cc = jnp.kron(eye, cg)
    sc = jnp.kron(eye, sg)
    tr = FNET_ROW_TILE
    return pl.pallas_call(
        _fnet_kernel,
        grid=(bsz, seq // tr),
        in_specs=[
            pl.BlockSpec((1, seq, width), lambda b, i: (b, 0, col0 // width)),
            pl.BlockSpec((width, width), lambda b, i: (0, 0)),
            pl.BlockSpec((width, width), lambda b, i: (0, 0)),
            pl.BlockSpec((tr, seq), lambda b, i: (i, 0)),
            pl.BlockSpec((tr, seq), lambda b, i: (i, 0)),
        ],
        out_specs=pl.BlockSpec((1, tr, width), lambda b, i: (b, i, 0)),
        out_shape=jax.ShapeDtypeStruct((bsz, seq, width), BF16),
        scratch_shapes=[pltpu.VMEM((seq, width), BF16)] * 2,
        compiler_params=_cparams(("arbitrary", "arbitrary")),
        name="fnet",
    )(proj3, cc.astype(BF16), sc.astype(BF16), cs.astype(BF16), ss.astype(BF16))


HIGH_HALF = -65536


def _slab_rows(s, rows, per_row):
    return pl.ds(s, rows, stride=per_row)


def _pack_rows(h, slab_ref):
    rows, d = h.shape
    half = d // 2
    per_row = half // LANES
    for s in range(per_row):
        lo = h[:, s * LANES:(s + 1) * LANES].astype(BF16).astype(F32)
        hi = h[:, half + s * LANES:half + (s + 1) * LANES].astype(BF16).astype(F32)
        lo_bits = lax.shift_right_logical(lax.bitcast_convert_type(lo, I32), jnp.int32(16))
        hi_bits = lax.bitcast_convert_type(hi, I32) & jnp.int32(HIGH_HALF)
        slab_ref[_slab_rows(s, rows, per_row), :] = hi_bits | lo_bits


def _unpack_rows(slab_ref, per_row):
    rows = slab_ref.shape[0] // per_row
    los, his = [], []
    for s in range(per_row):
        w = slab_ref[_slab_rows(s, rows, per_row), :]
        los.append(lax.bitcast_convert_type(lax.shift_left(w, jnp.int32(16)), F32))
        his.append(lax.bitcast_convert_type(w & jnp.int32(HIGH_HALF), F32))
    return jnp.concatenate(los, axis=-1).astype(BF16), jnp.concatenate(his, axis=-1).astype(BF16)


def _outproj_kernel(*refs, glu):
    if glu:
        (x_ref, a_ref, b_ref, gw_ref, gb_ref, wa_ref, wb_ref, gm_ref, gain_ref, sh_ref, sc_ref,
         rwh_ref, rwl_ref, rb_ref, x1_ref, h_ref, idx_ref, gate_ref, cnt_ref, base_ref) = refs
    else:
        (x_ref, a_ref, b_ref, wa_ref, wb_ref, gm_ref, gain_ref, sh_ref, sc_ref,
         rwh_ref, rwl_ref, rb_ref, x1_ref, h_ref, idx_ref, gate_ref, cnt_ref, base_ref) = refs

    @pl.when(pl.program_id(0) == 0)
    def _():
        base_ref[...] = jnp.zeros(base_ref.shape, F32)

    bm = b_ref[...]
    if glu:
        gl = jnp.dot(bm.astype(BF16), gw_ref[...], preferred_element_type=F32) + gb_ref[...]
        bm = (bm.astype(F32) * jax.nn.sigmoid(gl)).astype(BF16)
    y = (jnp.dot(a_ref[...], wa_ref[...], preferred_element_type=F32)
         + jnp.dot(bm, wb_ref[...], preferred_element_type=F32))
    x1 = x_ref[...] + gm_ref[0] * y
    x1_ref[...] = x1
    h = _norm_modulate(x1, gain_ref[...], sh_ref[0], sc_ref[0])
    _pack_rows(h, h_ref)

    h_hi = h.astype(BF16)
    h_lo = (h - h_hi.astype(F32)).astype(BF16)
    logits = (jnp.dot(h_hi, rwh_ref[...], preferred_element_type=F32)
              + jnp.dot(h_lo, rwh_ref[...], preferred_element_type=F32)
              + jnp.dot(h_hi, rwl_ref[...], preferred_element_type=F32)
              + rb_ref[...])
    lane = lax.broadcasted_iota(I32, logits.shape, 1)
    vals, idxs = [], []
    for _ in range(TOP_K):
        m = jnp.max(logits, axis=-1, keepdims=True)
        ik = jnp.min(jnp.where(logits == m, lane, ROUTER_PAD), axis=-1, keepdims=True)
        vals.append(m)
        idxs.append(ik)
        logits = jnp.where(lane == ik, -jnp.inf, logits)
    exps = [jnp.exp(v - vals[0]) for v in vals]
    denom = exps[0] + exps[1] + exps[2] + exps[3]

    tm = lane.shape[0]
    onehot = jnp.zeros(lane.shape, F32)
    for k in range(TOP_K):
        onehot = onehot + (lane == idxs[k]).astype(F32)
    before = (lax.broadcasted_iota(I32, (tm, tm), 1) < lax.broadcasted_iota(I32, (tm, tm), 0)).astype(BF16)
    count = jnp.dot(before, onehot.astype(BF16), preferred_element_type=F32) + base_ref[...]
    new_base = base_ref[...] + jnp.sum(onehot, axis=0, keepdims=True)
    base_ref[...] = new_base
    cnt_ref[...] = new_base.astype(I32)

    idx_out = jnp.zeros(lane.shape, I32)
    gate_out = jnp.zeros(lane.shape, F32)
    for k in range(TOP_K):
        rank = jnp.sum(jnp.where(lane == idxs[k], count, 0.0), axis=-1, keepdims=True).astype(I32)
        idx_out = jnp.where(lane == k, idxs[k], idx_out)
        idx_out = jnp.where(lane == TOP_K + k, rank, idx_out)
        gate_out = jnp.where(lane == k, exps[k] / denom, gate_out)
    idx_ref[...] = idx_out
    gate_ref[...] = gate_out


def _out_proj(x2d, mix_a, mix_b, w_out, g_mix, gain, shift, scale, router_w, router_b, seq,
              glu_w=None, glu_b=None):
    t, d = x2d.shape
    wa_rows = mix_a.shape[1]
    wb_rows = mix_b.shape[1]
    tm = 256
    per_seq = seq // tm
    n_exp = router_w.shape[1]
    rw = jnp.zeros((d, ROUTER_PAD), F32).at[:, :n_exp].set(router_w)
    rw_hi = rw.astype(BF16)
    rw_lo = (rw - rw_hi.astype(F32)).astype(BF16)
    rb = jnp.full((1, ROUTER_PAD), NEG_BIG, F32).at[0, :n_exp].set(router_b)
    w_bf = w_out.astype(BF16)
    glu = glu_w is not None

    def rows(width):
        return pl.BlockSpec((tm, width), lambda i: (i, 0))

    def full(r, c):
        return pl.BlockSpec((r, c), lambda i: (0, 0))

    def per_batch():
        return pl.BlockSpec((1, 1, d), lambda i: (i // per_seq, 0, 0))

    in_specs = [rows(d), rows(wa_rows), rows(wb_rows)]
    args = [x2d, mix_a, mix_b]
    if glu:
        in_specs += [full(wb_rows, wb_rows), full(1, wb_rows)]
        args += [glu_w.astype(BF16), glu_b.reshape(1, wb_rows)]
    in_specs += [pl.BlockSpec((wa_rows, d), lambda i: (0, 0)),
                 pl.BlockSpec((wb_rows, d), lambda i: (wa_rows // wb_rows, 0)),
                 per_batch(), full(1, d), per_batch(), per_batch(),
                 full(d, ROUTER_PAD), full(d, ROUTER_PAD), full(1, ROUTER_PAD)]
    args += [w_bf, w_bf, g_mix, gain.reshape(1, d), shift, scale, rw_hi, rw_lo, rb]
    return pl.pallas_call(
        functools.partial(_outproj_kernel, glu=glu),
        grid=(t // tm,),
        in_specs=in_specs,
        out_specs=[rows(d), pl.BlockSpec((tm * (d // (2 * LANES)), LANES), lambda i: (i, 0)),
                   rows(ROUTER_PAD), rows(ROUTER_PAD), full(1, ROUTER_PAD)],
        out_shape=[jax.ShapeDtypeStruct((t, d), F32), jax.ShapeDtypeStruct((t * (d // (2 * LANES)), LANES), I32),
                   jax.ShapeDtypeStruct((t, ROUTER_PAD), I32), jax.ShapeDtypeStruct((t, ROUTER_PAD), F32),
                   jax.ShapeDtypeStruct((1, ROUTER_PAD), I32)],
        scratch_shapes=[pltpu.VMEM((1, ROUTER_PAD), F32)],
        compiler_params=_cparams(("arbitrary",)),
        name="out_proj_glu" if glu else "out_proj",
    )(*args)


def _routing_tables(top_idx, rank, counts, n_tok):
    n_assign = n_tok * TOP_K
    r = MOE_ROWS
    n_sb = n_assign // (2 * r) + N_EXPERTS
    n_rb = 2 * n_sb
    experts = jnp.arange(N_EXPERTS, dtype=I32)
    used = (counts + r - 1) // r
    nsb = (used + 1) // 2
    sb_end = jnp.cumsum(nsb)
    sb_start = sb_end - nsb
    blk_start = 2 * sb_start
    row_start = jnp.sum(jnp.where(top_idx[:, :, None] == experts[None, None, :],
                                  (blk_start * r)[None, None, :], 0), axis=-1)
    dest = (row_start + rank).reshape(-1).astype(I32)
    rb = jnp.arange(n_rb, dtype=I32)
    rb_e = jnp.sum((sb_end[None, :] <= (rb // 2)[:, None]).astype(I32), axis=1)
    in_range = rb_e < N_EXPERTS
    rb_e = jnp.minimum(rb_e, N_EXPERTS - 1).astype(I32)
    rb_valid = (in_range & (rb - blk_start[rb_e] < used[rb_e])).astype(I32)
    tail = jnp.where(used > 0, blk_start + used - 1, -1)
    empty = jnp.where(2 * nsb > used, blk_start + used, -1)
    spare = 2 * sb_end[-1] + jnp.arange(2 * N_EXPERTS, dtype=I32)
    spare = jnp.where(spare < n_rb, spare, -1)
    zero_blocks = jnp.concatenate([tail, empty, spare]).astype(I32)
    return dict(dest=dest, rb_e=rb_e, rb_valid=rb_valid, zero_blocks=zero_blocks,
                n_valid_sb=sb_end[-1], nsb=nsb.astype(I32), sb_start=sb_start.astype(I32),
                n_sb=n_sb, n_rb=n_rb, n_slots=n_rb * r)


def _dispatch_kernel(dest_ref, zb_ref, h_ref, xs_hbm, zero_ref, zsem_ref, sem_ref):
    i = pl.program_id(0)
    td = DISPATCH_TOKENS
    sr = h_ref.shape[0] // td
    blk = MOE_ROWS * sr

    def slab(ref, row):
        return ref.at[pl.ds(pl.multiple_of(row * sr, sr), sr), :]

    @pl.when(i == 0)
    def _():
        zero_ref[...] = jnp.zeros(zero_ref.shape, zero_ref.dtype)

        def zero_copy(z):
            row0 = pl.multiple_of(zb_ref[z] * blk, blk)
            return pltpu.make_async_copy(zero_ref, xs_hbm.at[pl.ds(row0, blk), :], zsem_ref.at[0])

        def start(z, _):
            @pl.when(zb_ref[z] >= 0)
            def _():
                zero_copy(z).start()
            return 0

        def wait(z, _):
            @pl.when(zb_ref[z] >= 0)
            def _():
                zero_copy(z).wait()
            return 0

        lax.fori_loop(0, zb_ref.shape[0], start, 0)
        lax.fori_loop(0, zb_ref.shape[0], wait, 0)

    def group(g, _):
        for s in range(DMA_UNROLL):
            t = g * DMA_UNROLL + s
            for k in range(TOP_K):
                dst = dest_ref[(i * td + t) * TOP_K + k]
                pltpu.make_async_copy(slab(h_ref, t), slab(xs_hbm, dst), sem_ref.at[0]).start()
        return 0

    lax.fori_loop(0, td // DMA_UNROLL, group, 0)
    for _ in range(TOP_K):
        pltpu.make_async_copy(h_ref, xs_hbm.at[pl.ds(0, td * sr), :], sem_ref.at[0]).wait()


def _moe_dispatch(h_slabs, n_tok, rt):
    sr = h_slabs.shape[0] // n_tok
    td = DISPATCH_TOKENS
    grid_spec = pltpu.PrefetchScalarGridSpec(
        num_scalar_prefetch=2,
        grid=(n_tok // td,),
        in_specs=[pl.BlockSpec((td * sr, LANES), lambda i, dst, zb: (i, 0))],
        out_specs=pl.BlockSpec(memory_space=pl.ANY),
        scratch_shapes=[pltpu.VMEM((MOE_ROWS * sr, LANES), h_slabs.dtype), pltpu.SemaphoreType.DMA((1,)),
                        pltpu.SemaphoreType.DMA((1,))],
    )
    return pl.pallas_call(
        _dispatch_kernel,
        grid_spec=grid_spec,
        out_shape=jax.ShapeDtypeStruct((rt['n_slots'] * sr, LANES), h_slabs.dtype),
        compiler_params=_cparams(("arbitrary",)),
        name="moe_dispatch",
    )(rt['dest'], rt['zero_blocks'], h_slabs)


def _cast_rows(src_ref, dst_ref, chunk=256):
    chunk = min(chunk, dst_ref.shape[0])
    n = dst_ref.shape[0] // chunk

    def body(c, _):
        sl = pl.ds(pl.multiple_of(c * chunk, chunk), chunk)
        dst_ref[sl, :] = src_ref[sl, :].astype(dst_ref.dtype)
        return 0

    lax.fori_loop(0, n, body, 0)


def _stagger_tables(group, cols, parts):
    n = group.shape[0]
    idx = jnp.arange(n, dtype=I32)
    is_first = jnp.concatenate([jnp.ones((1,), bool), group[1:] != group[:-1]])
    first_cur = lax.cummax(jnp.where(is_first, idx, 0))
    nxt = jnp.where(is_first, idx, n)
    first_next = jnp.concatenate([lax.cummin(nxt[::-1])[::-1][1:], jnp.full((1,), n, I32)])
    out = [[] for _ in cols]
    for p in range(parts):
        switch = (idx >= jnp.maximum(first_next - p, first_cur + 1)) & (first_next < n)
        eff = jnp.where(switch, first_next, idx)
        for k, c in enumerate(cols):
            out[k].append(c[eff])
    return [jnp.stack(o) for o in out]


def _moe_up_kernel(sb_ref, j_ref, e_ref, first_ref, va_ref, vb_ref, pe_ref, pj_ref, x_ref, *refs):
    parts = MOE_W_PARTS
    wg_refs = refs[:parts]
    wu_refs = refs[parts:2 * parts]
    bg_ref, bu_ref, o_ref, wgs_ref, wus_ref = refs[2 * parts:]
    i = pl.program_id(0)
    rows = wgs_ref.shape[0] // parts
    r = MOE_ROWS
    sr = x_ref.shape[0] // (2 * r)

    @pl.when(first_ref[i] == 1)
    def _():
        for p in range(parts):
            _cast_rows(wg_refs[p].at[0, 0], wgs_ref.at[pl.ds(p * rows, rows)])
            _cast_rows(wu_refs[p].at[0, 0], wus_ref.at[pl.ds(p * rows, rows)])

    def compute(slabs):
        x_lo, x_hi = _unpack_rows(slabs, sr)
        half = wgs_ref.shape[0] // 2

        def proj(ws_ref, b_ref):
            return (jnp.dot(x_lo, ws_ref[:half, :], preferred_element_type=F32)
                    + jnp.dot(x_hi, ws_ref[half:, :], preferred_element_type=F32) + b_ref[0])

        gate = jnp.minimum(proj(wgs_ref, bg_ref), SWIGLU_LIMIT)
        up = jnp.clip(proj(wus_ref, bu_ref), -SWIGLU_LIMIT, SWIGLU_LIMIT)
        return ((up + 1.0) * gate * jax.nn.sigmoid(SWIGLU_ALPHA * gate)).astype(o_ref.dtype)

    va = va_ref[i] == 1
    vb = vb_ref[i] == 1

    @pl.when(vb)
    def _():
        o_ref[...] = compute(x_ref)

    @pl.when(va & jnp.logical_not(vb))
    def _():
        o_ref[pl.ds(0, r), :] = compute(x_ref.at[pl.ds(0, r * sr), :])
        o_ref[pl.ds(r, r), :] = jnp.zeros((r, o_ref.shape[1]), o_ref.dtype)

    @pl.when(jnp.logical_not(va))
    def _():
        o_ref[...] = jnp.zeros(o_ref.shape, o_ref.dtype)


def _moe_items(rt, nj, n_exp):
    n_sb = rt['n_sb']
    n_items = n_sb * nj
    sb = jnp.arange(n_sb, dtype=I32)
    e_of = rt['rb_e'][2 * sb]
    nsb = rt['nsb'].at[n_exp - 1].add(n_sb - rt['n_valid_sb'])
    q = sb - rt['sb_start'][e_of]
    pos = (nj * rt['sb_start'][e_of][:, None]
           + jnp.arange(nj, dtype=I32)[None, :] * nsb[e_of][:, None] + q[:, None]).reshape(-1)
    sb2 = jnp.broadcast_to(sb[:, None], (n_sb, nj)).reshape(-1)
    j2 = jnp.broadcast_to(jnp.arange(nj, dtype=I32)[None, :], (n_sb, nj)).reshape(-1)
    it_sb = jnp.zeros((n_items,), I32).at[pos].set(sb2)
    it_j = jnp.zeros((n_items,), I32).at[pos].set(j2)
    it_va = rt['rb_valid'][2 * it_sb]
    it_vb = rt['rb_valid'][2 * it_sb + 1]
    it_e = e_of[it_sb]
    prev_e = jnp.concatenate([jnp.full((1,), -1, I32), it_e[:-1]])
    prev_j = jnp.concatenate([jnp.full((1,), -1, I32), it_j[:-1]])
    it_first = ((it_e != prev_e) | (it_j != prev_j)).astype(I32)
    part_e, part_j = _stagger_tables(jnp.cumsum(it_first), [it_e, it_j], MOE_W_PARTS)
    return n_items, (it_sb, it_j, it_e, it_first, it_va, it_vb, part_e, part_j)


def _moe_up(x_sorted, w_gu, b_gu, layer, rt):
    _, n_exp, d, two_f = w_gu.shape
    n_slots = rt['n_slots']
    sr = x_sorted.shape[0] // n_slots
    f = two_f // 2
    r, tn = MOE_ROWS, MOE_UP_TN
    nj = f // tn
    parts = MOE_W_PARTS
    n_items, tables = _moe_items(rt, nj, n_exp)

    def w_spec(p, col0):
        return pl.BlockSpec((1, 1, d // parts, tn),
                            lambda i, sbt, jt, et, ft, va, vb, pe, pj: (layer, pe[p, i], p, col0 + pj[p, i]))

    grid_spec = pltpu.PrefetchScalarGridSpec(
        num_scalar_prefetch=8,
        grid=(n_items,),
        in_specs=([pl.BlockSpec((2 * r * sr, LANES), lambda i, sbt, jt, et, ft, va, vb, pe, pj: (sbt[i], 0))]
                  + [w_spec(p, 0) for p in range(parts)] + [w_spec(p, nj) for p in range(parts)]
                  + [pl.BlockSpec((1, 1, tn), lambda i, sbt, jt, et, ft, va, vb, pe, pj: (et[i], 0, jt[i])),
                     pl.BlockSpec((1, 1, tn), lambda i, sbt, jt, et, ft, va, vb, pe, pj: (et[i], 0, nj + jt[i]))]),
        out_specs=pl.BlockSpec((2 * r, tn), lambda i, sbt, jt, et, ft, va, vb, pe, pj: (sbt[i], jt[i])),
        scratch_shapes=[pltpu.VMEM((d, tn), BF16), pltpu.VMEM((d, tn), BF16)],
    )
    b3 = b_gu.reshape(n_exp, 1, two_f)
    return pl.pallas_call(
        _moe_up_kernel,
        grid_spec=grid_spec,
        out_shape=jax.ShapeDtypeStruct((n_slots, f), BF16),
        compiler_params=_cparams(("arbitrary",)),
        name="moe_up",
    )(*tables, x_sorted, *([w_gu] * (2 * parts)), b3, b3)


def _moe_dn_kernel(sb_ref, j_ref, e_ref, first_ref, va_ref, vb_ref, pe_ref, a_ref, *refs):
    parts = MOE_W_PARTS
    w_refs = refs[:parts]
    b_ref, o_ref, ws_ref = refs[parts:]
    i = pl.program_id(0)
    n_col, _, tn = ws_ref.shape
    rows = ws_ref.shape[1] // parts
    r = MOE_ROWS

    @pl.when(first_ref[i] == 1)
    def _():
        for p in range(parts):
            for c in range(n_col):
                _cast_rows(w_refs[p].at[0, 0, :, pl.ds(c * tn, tn)], ws_ref.at[c, pl.ds(p * rows, rows)])

    va = va_ref[i] == 1
    vb = vb_ref[i] == 1
    col = j_ref[i]

    @pl.when(vb)
    def _():
        o_ref[...] = jnp.dot(a_ref[...], ws_ref[col], preferred_element_type=F32) + b_ref[0]

    @pl.when(va & jnp.logical_not(vb))
    def _():
        o_ref[pl.ds(0, r), :] = (jnp.dot(a_ref[pl.ds(0, r), :], ws_ref[col], preferred_element_type=F32)
                                 + b_ref[0])
        o_ref[pl.ds(r, r), :] = jnp.zeros((r, o_ref.shape[1]), o_ref.dtype)

    @pl.when(jnp.logical_not(va))
    def _():
        o_ref[...] = jnp.zeros(o_ref.shape, o_ref.dtype)


def _moe_dn(act, w_dn, b_dn, layer, rt):
    n_slots, f = act.shape
    _, n_exp, _, d = w_dn.shape
    r, tn = MOE_ROWS, MOE_DN_TN
    nj = d // tn
    parts = MOE_W_PARTS
    n_items = rt['n_sb'] * nj
    it = jnp.arange(n_items, dtype=I32)
    it_sb = it // nj
    it_j = it % nj
    it_e = rt['rb_e'][2 * it_sb]
    it_va = rt['rb_valid'][2 * it_sb]
    it_vb = rt['rb_valid'][2 * it_sb + 1]
    prev_e = jnp.concatenate([jnp.full((1,), -1, I32), it_e[:-1]])
    it_first = (it_e != prev_e).astype(I32)
    (part_e,) = _stagger_tables(jnp.cumsum(it_first), [it_e], parts)

    def w_spec(p):
        return pl.BlockSpec((1, 1, f // parts, d),
                            lambda i, sbt, jt, et, ft, va, vb, pe: (layer, pe[p, i], p, 0))

    grid_spec = pltpu.PrefetchScalarGridSpec(
        num_scalar_prefetch=7,
        grid=(n_items,),
        in_specs=([pl.BlockSpec((2 * r, f), lambda i, sbt, jt, et, ft, va, vb, pe: (sbt[i], 0))]
                  + [w_spec(p) for p in range(parts)]
                  + [pl.BlockSpec((1, 1, tn), lambda i, sbt, jt, et, ft, va, vb, pe: (et[i], 0, jt[i]))]),
        out_specs=pl.BlockSpec((2 * r, tn), lambda i, sbt, jt, et, ft, va, vb, pe: (sbt[i], jt[i])),
        scratch_shapes=[pltpu.VMEM((nj, f, tn), BF16)],
    )
    return pl.pallas_call(
        _moe_dn_kernel,
        grid_spec=grid_spec,
        out_shape=jax.ShapeDtypeStruct((n_slots, d), F32),
        compiler_params=_cparams(("arbitrary",)),
        name="moe_dn",
    )(it_sb, it_j, it_e, it_first, it_va, it_vb, part_e, act, *([w_dn] * parts), b_dn.reshape(n_exp, 1, d))


def _combine_kernel(dest_ref, x_ref, gate_ref, g_ref, fg_ref, y_hbm, o_ref, buf_ref, sem_ref, *, final):
    i = pl.program_id(0)
    n_steps = pl.num_programs(0)
    tc = COMBINE_TOKENS

    def issue(step, slot):
        def group(g, _):
            for s in range(DMA_UNROLL):
                t = g * DMA_UNROLL + s
                for k in range(TOP_K):
                    src = dest_ref[(step * tc + t) * TOP_K + k]
                    pltpu.make_async_copy(y_hbm.at[pl.ds(src, 1), :], buf_ref.at[slot, k, pl.ds(t, 1), :],
                                          sem_ref.at[slot]).start()
            return 0

        lax.fori_loop(0, tc // DMA_UNROLL, group, 0)

    @pl.when(i == 0)
    def _():
        issue(0, 0)

    @pl.when(i + 1 < n_steps)
    def _():
        issue(i + 1, (i + 1) % 2)

    slot = i % 2
    for k in range(TOP_K):
        pltpu.make_async_copy(y_hbm.at[pl.ds(0, tc), :], buf_ref.at[slot, k], sem_ref.at[slot]).wait()
    gates = gate_ref[...]
    y = gates[:, 0:1] * buf_ref[slot, 0]
    for k in range(1, TOP_K):
        y = y + gates[:, k:k + 1] * buf_ref[slot, k]
    x2 = x_ref[...] + g_ref[0] * y
    if final:
        x2 = x2 * lax.rsqrt(jnp.mean(x2 * x2, axis=-1, keepdims=True) + EPS) * fg_ref[...]
    o_ref[...] = x2


def _combine(x1, y_sorted, gates, rt, g_ffn, final_gain, seq, final):
    t, d = x1.shape
    tc = COMBINE_TOKENS
    per_seq = seq // tc
    grid_spec = pltpu.PrefetchScalarGridSpec(
        num_scalar_prefetch=1,
        grid=(t // tc,),
        in_specs=[pl.BlockSpec((tc, d), lambda i, dst: (i, 0)),
                  pl.BlockSpec((tc, ROUTER_PAD), lambda i, dst: (i, 0)),
                  pl.BlockSpec((1, 1, d), lambda i, dst: (i // per_seq, 0, 0)),
                  pl.BlockSpec((1, d), lambda i, dst: (0, 0)),
                  pl.BlockSpec(memory_space=pl.ANY)],
        out_specs=pl.BlockSpec((tc, d), lambda i, dst: (i, 0)),
        scratch_shapes=[pltpu.VMEM((2, TOP_K, tc, d), F32), pltpu.SemaphoreType.DMA((2,))],
    )
    return pl.pallas_call(
        functools.partial(_combine_kernel, final=final),
        grid_spec=grid_spec,
        out_shape=jax.ShapeDtypeStruct((t, d), F32),
        compiler_params=_cparams(("arbitrary",)),
        name="combine_final" if final else "combine",
    )(rt['dest'], x1, gates, g_ffn, final_gain.reshape(1, d), y_sorted)


def _moe(x1, h, idx, gates, counts, w_gu, b_gu, w_dn, b_dn, layer, g_ffn, final_gain, seq, final):
    n_tok = x1.shape[0]
    rt = _routing_tables(idx[:, :TOP_K], idx[:, TOP_K:2 * TOP_K], counts[0, :N_EXPERTS], n_tok)
    x_sorted = _moe_dispatch(h, n_tok, rt)
    act = _moe_up(x_sorted, w_gu, b_gu, layer, rt)
    y_sorted = _moe_dn(act, w_dn, b_dn, layer, rt)
    return _combine(x1, y_sorted, gates, rt, g_ffn, final_gain, seq, final)


def kernel(x, c, ada_w, ada_b, norm_mix_gain, norm_ffn_gain, ab_w_in, ab_w_out, hg_lb_logits, hg_norm_gain, s5_lam_re, s5_lam_im, s5_log_dt, s5_b_re, s5_b_im, s5_c_re, s5_c_im, s5_d, s5_glu_w, s5_glu_b, cd_w_in, cd_w_out, ret_norm_gain, router_w, router_b, moe_w_gu, moe_b_gu, moe_w_dn, moe_b_dn, final_gain):
    bsz, seq, d = x.shape
    depth = ada_w.shape[0]
    n_tok = bsz * seq
    hg_width = hg_lb_logits.shape[1]
    s5_width = s5_glu_w.shape[1]
    ret_width = ret_norm_gain.shape[1]
    fnet_width = cd_w_out.shape[1] - ret_width

    lower_bounds = jnp.cumsum(jax.nn.softmax(hg_lb_logits.astype(F32), axis=0), axis=0)
    mod = _ada_mod(c, ada_w, ada_b)
    xr = x.reshape(n_tok, d)
    for layer in range(depth):
        sh_mix, sc_mix, g_mix, sh_ffn, sc_ffn, g_ffn = (
            mod[layer, :, k * d:(k + 1) * d].reshape(bsz, 1, d) for k in range(6))
        j = layer // 2
        if layer % 2 == 0:
            proj = _in_proj(xr, norm_mix_gain[layer], sh_mix, sc_mix, ab_w_in[j].astype(BF16), seq)
            proj3 = proj.reshape(bsz, seq, proj.shape[1])
            mix_a = _hgrn2(proj3, lower_bounds[j], hg_norm_gain[j], hg_width)
            tables = _s5c_tables(s5_lam_re[j], s5_lam_im[j], s5_log_dt[j], s5_b_re[j], s5_b_im[j],
                                 s5_c_re[j], s5_c_im[j], s5_d[j])
            mix_b = _s5c(proj3, 5 * hg_width, s5_width, tables)
            x1, h, idx, gate, counts = _out_proj(
                xr, mix_a.reshape(n_tok, hg_width), mix_b.reshape(n_tok, s5_width), ab_w_out[j],
                g_mix, norm_ffn_gain[layer], sh_ffn, sc_ffn, router_w[layer], router_b[layer], seq,
                glu_w=s5_glu_w[j], glu_b=s5_glu_b[j])
        else:
            proj = _in_proj(xr, norm_mix_gain[layer], sh_mix, sc_mix, cd_w_in[j].astype(BF16), seq)
            proj3 = proj.reshape(bsz, seq, proj.shape[1])
            mix_a = _retention(proj3, ret_norm_gain[j], ret_width)
            mix_b = _fnet(proj3, 4 * ret_width, fnet_width)
            x1, h, idx, gate, counts = _out_proj(
                xr, mix_a.reshape(n_tok, ret_width), mix_b.reshape(n_tok, fnet_width), cd_w_out[j],
                g_mix, norm_ffn_gain[layer], sh_ffn, sc_ffn, router_w[layer], router_b[layer], seq)
        xr = _moe(x1, h, idx, gate, counts, moe_w_gu, moe_b_gu[layer], moe_w_dn, moe_b_dn[layer],
                  layer, g_ffn, final_gain, seq, final=(layer == depth - 1))
    return xr.reshape(bsz, seq, d)
```

```python
import functools
import math

import jax
import jax.numpy as jnp
from jax import lax
from jax.experimental import pallas as pl
from jax.experimental.pallas import tpu as pltpu

F32 = jnp.float32
BF16 = jnp.bfloat16
I32 = jnp.int32

EPS = 1e-6
LANES = 128
SUBLANES = 8
VMEM_LIMIT = 56 * 1024 * 1024

HG_HEAD_DIM = 128
HG_CHUNK = 64
HG_GROUP = 256
HG_EXP_CLAMP = 80.0

S5_GROUP = 16
S5_STATE = 64
S5_CHUNK = 16
S5_UNROLL = 8
S5_TILE_PAIRS = 4
S5_ROW_SPLIT = 4

RET_HEAD_DIM = 256
RET_Q_TILE = 256
ROPE_BASE = 10000.0

FNET_GROUPS = 4
FNET_ROW_TILE = 512

N_EXPERTS = 32
TOP_K = 4
SWIGLU_LIMIT = 7.0
SWIGLU_ALPHA = 1.702
MOE_ROWS = 256
MOE_UP_TN = 1024
MOE_DN_TN = 1024
MOE_W_PARTS = 4
DISPATCH_TOKENS = 512
COMBINE_TOKENS = 256
DMA_UNROLL = 8
ROUTER_PAD = LANES
NEG_BIG = -1e30


def _cparams(semantics):
    return pltpu.CompilerParams(dimension_semantics=semantics, vmem_limit_bytes=VMEM_LIMIT)


def _ada_kernel(c_ref, w_ref, b_ref, o_ref):
    c = c_ref[...]
    cond = c * jax.nn.sigmoid(c)
    o_ref[0] = jnp.dot(cond.astype(BF16), w_ref[0].astype(BF16),
                       preferred_element_type=F32) + b_ref[0]


def _ada_mod(c, ada_w, ada_b):
    depth, d, n = ada_w.shape
    bsz = c.shape[0]
    tn = 1024
    return pl.pallas_call(
        _ada_kernel,
        grid=(depth, n // tn),
        in_specs=[
            pl.BlockSpec((bsz, d), lambda l, j: (0, 0)),
            pl.BlockSpec((1, d, tn), lambda l, j: (l, 0, j)),
            pl.BlockSpec((1, 1, tn), lambda l, j: (l, 0, j)),
        ],
        out_specs=pl.BlockSpec((1, bsz, tn), lambda l, j: (l, 0, j)),
        out_shape=jax.ShapeDtypeStruct((depth, bsz, n), F32),
        compiler_params=_cparams(("arbitrary", "arbitrary")),
        name="ada_mod",
    )(c, ada_w, ada_b.reshape(depth, 1, n))


def _norm_modulate(x, gain, shift, scale):
    ms = jnp.mean(x * x, axis=-1, keepdims=True)
    y = x * lax.rsqrt(ms + EPS) * gain
    return y * (1.0 + scale) + shift


def _inproj_kernel(x_ref, gain_ref, sh_ref, sc_ref, w_ref, o_ref, h_ref):
    @pl.when(pl.program_id(1) == 0)
    def _():
        rows = 128

        def slab(c, _):
            sl = pl.ds(pl.multiple_of(c * rows, rows), rows)
            h_ref[sl, :] = _norm_modulate(x_ref[sl, :], gain_ref[...], sh_ref[0], sc_ref[0]).astype(BF16)
            return 0

        lax.fori_loop(0, h_ref.shape[0] // rows, slab, 0)

    o_ref[...] = jnp.dot(h_ref[...], w_ref[...], preferred_element_type=F32)


def _in_proj(x2d, gain, shift, scale, w_bf16, seq):
    t, d = x2d.shape
    n = w_bf16.shape[1]
    tm = 1024
    tn = 1024 if n % 1024 == 0 else 512
    per_seq = seq // tm
    return pl.pallas_call(
        _inproj_kernel,
        grid=(t // tm, n // tn),
        in_specs=[
            pl.BlockSpec((tm, d), lambda i, j: (i, 0)),
            pl.BlockSpec((1, d), lambda i, j: (0, 0)),
            pl.BlockSpec((1, 1, d), lambda i, j: (i // per_seq, 0, 0)),
            pl.BlockSpec((1, 1, d), lambda i, j: (i // per_seq, 0, 0)),
            pl.BlockSpec((d, tn), lambda i, j: (0, j)),
        ],
        out_specs=pl.BlockSpec((tm, tn), lambda i, j: (i, j)),
        out_shape=jax.ShapeDtypeStruct((t, n), F32),
        scratch_shapes=[pltpu.VMEM((tm, d), BF16)],
        compiler_params=_cparams(("arbitrary", "arbitrary")),
        name="in_proj",
    )(x2d, gain.reshape(1, d), shift, scale, w_bf16)


def _split3(a):
    hi = a.astype(BF16)
    r1 = a - hi.astype(F32)
    mid = r1.astype(BF16)
    lo = (r1 - mid.astype(F32)).astype(BF16)
    return hi, mid, lo


def _tri_sum(tri, a):
    hi, mid, lo = _split3(a)
    return (jnp.dot(tri, hi, preferred_element_type=F32)
            + jnp.dot(tri, mid, preferred_element_type=F32)
            + jnp.dot(tri, lo, preferred_element_type=F32))


def _dot_nt(a, b):
    return lax.dot_general(a, b, (((1,), (1,)), ((), ())), preferred_element_type=F32)


def _dot_tn(a, b):
    return lax.dot_general(a, b, (((0,), (0,)), ((), ())), preferred_element_type=F32)


def _hgrn2_kernel(q_ref, zf_ref, zb_ref, v_ref, g_ref, lb_ref, gain_ref, o_ref, acc_ref, accb_ref):
    seq = q_ref.shape[1]
    ln = HG_CHUNK
    gr = HG_GROUP
    n_groups = seq // gr
    per_group = gr // ln
    lb = lb_ref[...]
    gain = gain_ref[...]
    row = lax.broadcasted_iota(I32, (gr, gr), 0)
    col = lax.broadcasted_iota(I32, (gr, gr), 1)
    chunk_lo = (row // ln) * ln
    chunk_hi = chunk_lo + ln
    lower_incl = (col <= row) & (col >= chunk_lo)
    upper_strict = (col > row) & (col < chunk_hi)
    tri_prefix = lower_incl.astype(BF16)
    tri_suffix = ((col >= row) & (col < chunk_hi)).astype(BF16)
    mid = ln // 2

    def per_chunk_rows(a, offset):
        return jnp.concatenate(
            [jnp.broadcast_to(a[j * ln + offset:j * ln + offset + 1, :], (ln, a.shape[1]))
             for j in range(per_group)], axis=0)

    rows = [slice(j * ln, (j + 1) * ln) for j in range(per_group)]

    def load(gi, z_ref, forward):
        sl = pl.ds(pl.multiple_of(gi * gr, gr), gr)
        f = lb + (1.0 - lb) * jax.nn.sigmoid(z_ref[0, sl, :])
        log_f = jnp.log(f)
        d = dict(sl=sl, forward=forward, q=q_ref[0, sl, :], v=v_ref[0, sl, :].astype(BF16), k=1.0 - f)
        if forward:
            d['cum'] = _tri_sum(tri_prefix, log_f)
        else:
            d['cum'] = _tri_sum(tri_suffix, log_f)
        return d

    def local_states(d):
        edge_off = ln - 1 if d['forward'] else 0
        kd = (d['k'] * jnp.exp(per_chunk_rows(d['cum'], edge_off) - d['cum'])).astype(BF16)
        d['local'] = [_dot_tn(d['v'][rs], kd[rs]) for rs in rows]

    def scores(d):
        ref_rows = per_chunk_rows(d['cum'], mid - 1 if d['forward'] else mid)
        qe = d['q'] * jnp.exp(jnp.minimum(d['cum'] - ref_rows, HG_EXP_CLAMP))
        ke = d['k'] * jnp.exp(jnp.minimum(ref_rows - d['cum'], HG_EXP_CLAMP))
        mask = lower_incl if d['forward'] else upper_strict
        d['scores'] = jnp.where(mask, _dot_nt(qe.astype(BF16), ke.astype(BF16)), 0.0)

    def inter(d, state_t):
        edge_off = ln - 1 if d['forward'] else 0
        qc = (d['q'] * jnp.exp(d['cum'])).astype(BF16)
        entering = [None] * per_group
        order = range(per_group) if d['forward'] else range(per_group - 1, -1, -1)
        for j in order:
            entering[j] = state_t.astype(BF16)
            edge = d['cum'][j * ln + edge_off:j * ln + edge_off + 1, :]
            state_t = state_t * jnp.exp(edge) + d['local'][j]
        d['inter'] = jnp.concatenate([_dot_nt(qc[rs], entering[j]) for j, rs in enumerate(rows)], axis=0)
        return state_t

    def body(i, states):
        st_f, st_b = states
        both = [load(i, zf_ref, True), load(n_groups - 1 - i, zb_ref, False)]
        for d in both:
            local_states(d)
        for d in both:
            scores(d)
        st_f = inter(both[0], st_f)
        st_b = inter(both[1], st_b)
        for d, ref in zip(both, (acc_ref, accb_ref)):
            ref[d['sl'], :] = jnp.dot(d['scores'].astype(BF16), d['v'], preferred_element_type=F32) + d['inter']
        return st_f, st_b

    zero = jnp.zeros((HG_HEAD_DIM, HG_HEAD_DIM), F32)
    lax.fori_loop(0, n_groups, body, (zero, zero))

    def finish(gi, _):
        sl = pl.ds(pl.multiple_of(gi * gr, gr), gr)
        o = acc_ref[sl, :] + accb_ref[sl, :]
        y = o * lax.rsqrt(jnp.mean(o * o, axis=-1, keepdims=True) + EPS) * gain
        g = g_ref[0, sl, :]
        o_ref[0, sl, :] = (y * (g * jax.nn.sigmoid(g))).astype(o_ref.dtype)
        return 0

    lax.fori_loop(0, n_groups, finish, 0)


def _hgrn2(proj3, lower_bound, hg_gain, width):
    bsz, seq, _ = proj3.shape
    heads = width // HG_HEAD_DIM
    dh = HG_HEAD_DIM

    def col(k):
        return pl.BlockSpec((1, seq, dh), lambda b, h: (b, 0, k * heads + h))

    return pl.pallas_call(
        _hgrn2_kernel,
        grid=(bsz, heads),
        in_specs=[col(0), col(1), col(2), col(3), col(4),
                  pl.BlockSpec((1, dh), lambda b, h: (0, h)),
                  pl.BlockSpec((1, dh), lambda b, h: (0, 0))],
        out_specs=pl.BlockSpec((1, seq, dh), lambda b, h: (b, 0, h)),
        out_shape=jax.ShapeDtypeStruct((bsz, seq, width), BF16),
        scratch_shapes=[pltpu.VMEM((seq, dh), F32), pltpu.VMEM((seq, dh), F32)],
        compiler_params=_cparams(("arbitrary", "arbitrary")),
        name="hgrn2",
    )(proj3, proj3, proj3, proj3, proj3, lower_bound.reshape(1, width), hg_gain.reshape(1, dh))


def _s5c_kernel(u_ref, t_ref, w_ref, v_ref, a_ref, z_ref, uc_ref, e_ref, p_ref):
    ck = S5_CHUNK
    n_chunks = u_ref.shape[1] // ck
    slab = n_chunks // S5_ROW_SPLIT
    gpt = 2 * S5_TILE_PAIRS
    half_tok = ck // 2
    lane_grp = lax.broadcasted_iota(I32, (slab, LANES), 1) // S5_GROUP

    def token_rows(rq, tok):
        return pl.ds(rq * slab * ck + tok, slab, stride=ck)

    def roll_lanes(a, groups):
        shift = (groups * S5_GROUP) % LANES
        return pltpu.roll(a, shift, 1) if shift else a

    for half in range(2):
        for rq in range(S5_ROW_SPLIT):
            toks = [u_ref.at[0][token_rows(rq, half * half_tok + sl), :] for sl in range(half_tok)]
            for gl in range(gpt):
                acc = jnp.zeros((slab, LANES), F32)
                for sl in range(half_tok):
                    acc = jnp.where(lane_grp == sl, roll_lanes(toks[sl], sl - gl), acc)
                uc_ref[gl * 2 + half, rq * slab:(rq + 1) * slab, :] = acc

    seg = [pl.ds(k * LANES, LANES) for k in range(4)]

    def group_inputs(g):
        return jnp.concatenate([uc_ref[2 * g], uc_ref[2 * g + 1]], axis=-1).astype(BF16)

    for g in range(gpt):
        e_ref[:, g, :] = jnp.dot(group_inputs(g), w_ref[g], preferred_element_type=F32)

    af_r, af_i, ab_r, ab_i = a_ref[0, 0], a_ref[0, 1], a_ref[0, 2], a_ref[0, 3]

    def cmul(ar, ai, br, bi):
        return ar * br - ai * bi, ar * bi + ai * br

    def step(m, carry):
        xr, xi, yr, yi = carry
        mb = n_chunks - 1 - m
        p_ref[m, :, seg[0]] = xr
        p_ref[m, :, seg[1]] = xi
        p_ref[mb, :, seg[2]] = yr
        p_ref[mb, :, seg[3]] = yi
        dr, di = cmul(af_r, af_i, xr, xi)
        gr, gi = cmul(ab_r, ab_i, yr, yi)
        return (dr + e_ref[m, :, seg[0]], di + e_ref[m, :, seg[1]],
                gr + e_ref[mb, :, seg[2]], gi + e_ref[mb, :, seg[3]])

    zero = jnp.zeros((gpt, LANES), F32)
    lax.fori_loop(0, n_chunks, step, (zero, zero, zero, zero), unroll=S5_UNROLL)

    for g in range(gpt):
        y = (jnp.dot(group_inputs(g), t_ref[g], preferred_element_type=F32)
             + jnp.dot(p_ref[:, g, :].astype(BF16), v_ref[g], preferred_element_type=F32))
        zt = jax.nn.gelu(y)
        for k in range(2):
            uc_ref[2 * g + k] = zt[:, k * LANES:(k + 1) * LANES]

    for half in range(2):
        for rq in range(S5_ROW_SPLIT):
            cols = [uc_ref[gl * 2 + half, rq * slab:(rq + 1) * slab, :] for gl in range(gpt)]
            for sl in range(half_tok):
                acc = jnp.zeros((slab, LANES), F32)
                for gl in range(gpt):
                    acc = jnp.where(lane_grp == gl, roll_lanes(cols[gl], gl - sl), acc)
                z_ref.at[0][token_rows(rq, half * half_tok + sl), :] = acc


_einsum_f32 = functools.partial(jnp.einsum, precision=lax.Precision.HIGHEST)


def _s5c_tables(lam_re, lam_im, log_dt, b_re, b_im, c_re, c_im, d_skip):
    groups, state = lam_re.shape[1], lam_re.shape[2]
    chans = b_re.shape[2]
    ck = S5_CHUNK
    lag = jnp.arange(ck + 1, dtype=F32)[:, None, None]
    kern, w_parts, v_parts, a_parts = [], [], [], []
    for direction in (0, 1):
        lr, li = lam_re[direction].astype(F32), lam_im[direction].astype(F32)
        dt = jnp.exp(log_dt[direction].astype(F32))[:, None]
        mag = jnp.exp(lr * dt)
        abar_re = mag * jnp.cos(li * dt)
        abar_im = mag * jnp.sin(li * dt)
        den = lr * lr + li * li
        num_re = abar_re - 1.0
        coef_re = (num_re * lr + abar_im * li) / den
        coef_im = (abar_im * lr - num_re * li) / den
        bbar_re = coef_re[..., None] * b_re - coef_im[..., None] * b_im
        bbar_im = coef_re[..., None] * b_im + coef_im[..., None] * b_re
        pw_re = jnp.exp(lag * lr * dt) * jnp.cos(lag * li * dt)
        pw_im = jnp.exp(lag * lr * dt) * jnp.sin(lag * li * dt)
        ab_re = pw_re[..., None] * bbar_re - pw_im[..., None] * bbar_im
        ab_im = pw_re[..., None] * bbar_im + pw_im[..., None] * bbar_re
        cr, ci = c_re[direction].astype(F32), c_im[direction].astype(F32)
        kern.append(_einsum_f32('gcp,ngpd->ngcd', cr, ab_re[:ck]) - _einsum_f32('gcp,ngpd->ngcd', ci, ab_im[:ck]))
        order = jnp.arange(ck - 1, -1, -1) if direction == 0 else jnp.arange(ck)
        w_parts.append((ab_re[order], ab_im[order]))
        order = jnp.arange(1, ck + 1) if direction == 0 else jnp.arange(ck, 0, -1)
        a_r, a_i = pw_re[order], pw_im[order]
        v_from_re = jnp.einsum('gcp,jgp->gpjc', cr, a_r) - jnp.einsum('gcp,jgp->gpjc', ci, a_i)
        v_from_im = -(jnp.einsum('gcp,jgp->gpjc', cr, a_i) + jnp.einsum('gcp,jgp->gpjc', ci, a_r))
        v_parts.append((v_from_re, v_from_im))
        a_parts.append((pw_re[ck], pw_im[ck]))

    s_idx = jnp.arange(ck)[None, :, None]
    t_idx = jnp.arange(ck)[None, None, :]
    n_idx = jnp.arange(ck)[:, None, None]
    sel_f = (t_idx - s_idx == n_idx).astype(F32)
    sel_b = (s_idx - t_idx == n_idx).astype(F32)
    skip = jnp.eye(chans, dtype=F32)[None] * d_skip.reshape(groups, chans)[:, :, None]
    toep = (_einsum_f32('nst,ngcd->gsdtc', sel_f, kern[0]) + _einsum_f32('nst,ngcd->gsdtc', sel_b, kern[1])
            + jnp.einsum('st,gcd->gsdtc', jnp.eye(ck, dtype=F32), skip))
    t_mat = toep.reshape(groups, ck * chans, ck * chans)

    pad = LANES - state
    w_seg = [w_parts[0][0], w_parts[0][1], w_parts[1][0], w_parts[1][1]]
    w_stack = jnp.stack(w_seg, axis=0).transpose(2, 1, 4, 0, 3)
    w_mat = jnp.pad(w_stack, ((0, 0),) * 4 + ((0, pad),)).reshape(groups, ck * chans, 4 * LANES)

    v_seg = [v_parts[0][0], v_parts[0][1], v_parts[1][0], v_parts[1][1]]
    v_stack = jnp.stack(v_seg, axis=1).reshape(groups, 4, state, ck * chans)
    v_mat = jnp.pad(v_stack, ((0, 0), (0, 0), (0, pad), (0, 0))).reshape(groups, 4 * LANES, ck * chans)

    gpt = 2 * S5_TILE_PAIRS
    a_seg = jnp.stack([a_parts[0][0], a_parts[0][1], a_parts[1][0], a_parts[1][1]], axis=0)
    a_tbl = jnp.pad(a_seg, ((0, 0), (0, 0), (0, pad))).reshape(4, groups // gpt, gpt, LANES).transpose(1, 0, 2, 3)
    return t_mat.astype(BF16), w_mat.astype(BF16), v_mat.astype(BF16), a_tbl


def _s5c(proj3, col0, width, tables):
    bsz, seq, _ = proj3.shape
    t_mat, w_mat, v_mat, a_tbl = tables
    gpt = 2 * S5_TILE_PAIRS
    tiles = t_mat.shape[0] // gpt
    cols = t_mat.shape[1]
    sw = w_mat.shape[2]
    tc = width // tiles
    assert tc == LANES and cols == 2 * LANES
    n_chunks = seq // S5_CHUNK
    return pl.pallas_call(
        _s5c_kernel,
        grid=(tiles, bsz),
        in_specs=[
            pl.BlockSpec((1, seq, tc), lambda t, b: (b, 0, col0 // tc + t)),
            pl.BlockSpec((gpt, cols, cols), lambda t, b: (t, 0, 0)),
            pl.BlockSpec((gpt, cols, sw), lambda t, b: (t, 0, 0)),
            pl.BlockSpec((gpt, sw, cols), lambda t, b: (t, 0, 0)),
            pl.BlockSpec((1, 4, gpt, LANES), lambda t, b: (t, 0, 0, 0)),
        ],
        out_specs=pl.BlockSpec((1, seq, tc), lambda t, b: (b, 0, t)),
        out_shape=jax.ShapeDtypeStruct((bsz, seq, width), F32),
        scratch_shapes=[pltpu.VMEM((2 * gpt, n_chunks, LANES), F32),
                        pltpu.VMEM((n_chunks, gpt, sw), F32),
                        pltpu.VMEM((n_chunks, gpt, sw), F32)],
        compiler_params=_cparams(("arbitrary", "arbitrary")),
        name="s5",
    )(proj3, t_mat, w_mat, v_mat, a_tbl)


def _retention_kernel(q_ref, k_ref, v_ref, g_ref, cos_ref, sin_ref, lg_ref, gain_ref,
                      o_ref, qs_ref, ks_ref, vs_ref, decay_ref):
    seq = q_ref.shape[1]
    half = RET_HEAD_DIM // 2
    tq = RET_Q_TILE
    cos = cos_ref[...]
    sin = sin_ref[...]

    @pl.when(pl.program_id(1) == 0)
    def _():
        lg_fwd = lg_ref[0, 0:1, :]
        lg_bwd = lg_ref[0, 1:2, :]

        def fill(i, _):
            sl = pl.ds(pl.multiple_of(i * tq, tq), tq)
            t_idx = lax.broadcasted_iota(I32, (tq, seq), 0) + i * tq
            s_idx = lax.broadcasted_iota(I32, (tq, seq), 1)
            rel = (t_idx - s_idx).astype(F32)
            decay_ref[sl, :] = jnp.exp(jnp.where(rel >= 0.0, lg_fwd * rel, -lg_bwd * rel))
            return 0

        lax.fori_loop(0, seq // tq, fill, 0)

    def rot(t_ref, scale):
        t1 = t_ref[0, :, :half]
        t2 = t_ref[0, :, half:]
        return jnp.concatenate([(t1 * cos - t2 * sin) * scale, (t1 * sin + t2 * cos) * scale], axis=-1)

    qs_ref[...] = rot(q_ref, 1.0).astype(BF16)
    ks_ref[...] = rot(k_ref, RET_HEAD_DIM ** -0.5).astype(BF16)
    vs_ref[...] = v_ref[0].astype(BF16)
    gain = gain_ref[...]

    def q_tiles(i, _):
        sls = [pl.ds(pl.multiple_of((2 * i + a) * tq, tq), tq) for a in range(2)]
        scores = [_dot_nt(qs_ref[sl, :], ks_ref[...]) for sl in sls]
        ps = [(s * decay_ref[sl, :]).astype(BF16) for s, sl in zip(scores, sls)]
        outs = [jnp.dot(p, vs_ref[...], preferred_element_type=F32) for p in ps]
        for o, sl in zip(outs, sls):
            y = o * lax.rsqrt(jnp.mean(o * o, axis=-1, keepdims=True) + EPS) * gain
            g = g_ref[0, sl, :]
            o_ref[0, sl, :] = (y * (g * jax.nn.sigmoid(g))).astype(o_ref.dtype)
        return 0

    lax.fori_loop(0, seq // (2 * tq), q_tiles, 0)


def _retention(proj3, ret_gain, width):
    bsz, seq, _ = proj3.shape
    dh = RET_HEAD_DIM
    heads = width // dh
    inv_freq = ROPE_BASE ** (-jnp.arange(0, dh, 2, dtype=F32) / dh)
    ang = jnp.arange(seq, dtype=F32)[:, None] * inv_freq[None, :]
    cos, sin = jnp.cos(ang), jnp.sin(ang)
    log_gamma = jnp.log1p(-jnp.exp2(-5.0 - jnp.arange(heads, dtype=F32)))
    lg = jnp.stack([log_gamma, log_gamma[::-1]], axis=1)
    lg = jnp.broadcast_to(lg[:, :, None], (heads, 2, seq))

    def col(k):
        return pl.BlockSpec((1, seq, dh), lambda h, b: (b, 0, k * heads + h))

    return pl.pallas_call(
        _retention_kernel,
        grid=(heads, bsz),
        in_specs=[col(0), col(1), col(2), col(3),
                  pl.BlockSpec((seq, dh // 2), lambda h, b: (0, 0)),
                  pl.BlockSpec((seq, dh // 2), lambda h, b: (0, 0)),
                  pl.BlockSpec((1, 2, seq), lambda h, b: (h, 0, 0)),
                  pl.BlockSpec((1, dh), lambda h, b: (0, h))],
        out_specs=pl.BlockSpec((1, seq, dh), lambda h, b: (b, 0, h)),
        out_shape=jax.ShapeDtypeStruct((bsz, seq, width), BF16),
        scratch_shapes=[pltpu.VMEM((seq, dh), BF16)] * 3 + [pltpu.VMEM((seq, seq), F32)],
        compiler_params=_cparams(("arbitrary", "arbitrary")),
        name="retention",
    )(proj3, proj3, proj3, proj3, cos, sin, lg, ret_gain.reshape(1, width))


def _fnet_kernel(x_ref, cc_ref, sc_ref, cs_ref, ss_ref, o_ref, a1_ref, a2_ref):
    @pl.when(pl.program_id(1) == 0)
    def _():
        x = x_ref[0].astype(BF16)
        a1_ref[...] = jnp.dot(x, cc_ref[...], preferred_element_type=F32).astype(BF16)
        a2_ref[...] = jnp.dot(x, sc_ref[...], preferred_element_type=F32).astype(BF16)

    y = (jnp.dot(cs_ref[...], a1_ref[...], preferred_element_type=F32)
         - jnp.dot(ss_ref[...], a2_ref[...], preferred_element_type=F32))
    o_ref[0] = y.astype(o_ref.dtype)


def _dft_mats(n):
    idx = jnp.arange(n, dtype=I32)
    ang = (2.0 * math.pi / n) * ((idx[:, None] * idx[None, :]) % n).astype(F32)
    scale = n ** -0.5
    return jnp.cos(ang) * scale, jnp.sin(ang) * scale


def _fnet(proj3, col0, width):
    bsz, seq, _ = proj3.shape
    gw = width // FNET_GROUPS
    cs, ss = _dft_mats(seq)
    cg, sg = _dft_mats(gw)
    eye = jnp.eye(FNET_GROUPS, dtype=F32)
    cc = jnp.kron(eye, cg)
    sc = jnp.kron(eye, sg)
    tr = FNET_ROW_TILE
    return pl.pallas_call(
        _fnet_kernel,
        grid=(bsz, seq // tr),
        in_specs=[
            pl.BlockSpec((1, seq, width), lambda b, i: (b, 0, col0 // width)),
            pl.BlockSpec((width, width), lambda b, i: (0, 0)),
            pl.BlockSpec((width, width), lambda b, i: (0, 0)),
            pl.BlockSpec((tr, seq), lambda b, i: (i, 0)),
            pl.BlockSpec((tr, seq), lambda b, i: (i, 0)),
        ],
        out_specs=pl.BlockSpec((1, tr, width), lambda b, i: (b, i, 0)),
        out_shape=jax.ShapeDtypeStruct((bsz, seq, width), BF16),
        scratch_shapes=[pltpu.VMEM((seq, width), BF16)] * 2,
        compiler_params=_cparams(("arbitrary", "arbitrary")),
        name="fnet",
    )(proj3, cc.astype(BF16), sc.astype(BF16), cs.astype(BF16), ss.astype(BF16))


HIGH_HALF = -65536


def _slab_rows(s, rows, per_row):
    return pl.ds(s, rows, stride=per_row)


def _pack_rows(h, slab_ref):
    rows, d = h.shape
    half = d // 2
    per_row = half // LANES
    for s in range(per_row):
        lo = h[:, s * LANES:(s + 1) * LANES].astype(BF16).astype(F32)
        hi = h[:, half + s * LANES:half + (s + 1) * LANES].astype(BF16).astype(F32)
        lo_bits = lax.shift_right_logical(lax.bitcast_convert_type(lo, I32), jnp.int32(16))
        hi_bits = lax.bitcast_convert_type(hi, I32) & jnp.int32(HIGH_HALF)
        slab_ref[_slab_rows(s, rows, per_row), :] = hi_bits | lo_bits


def _unpack_rows(slab_ref, per_row):
    rows = slab_ref.shape[0] // per_row
    los, his = [], []
    for s in range(per_row):
        w = slab_ref[_slab_rows(s, rows, per_row), :]
        los.append(lax.bitcast_convert_type(lax.shift_left(w, jnp.int32(16)), F32))
        his.append(lax.bitcast_convert_type(w & jnp.int32(HIGH_HALF), F32))
    return jnp.concatenate(los, axis=-1).astype(BF16), jnp.concatenate(his, axis=-1).astype(BF16)


def _outproj_kernel(*refs, glu):
    if glu:
        (x_ref, a_ref, b_ref, gw_ref, gb_ref, wa_ref, wb_ref, gm_ref, gain_ref, sh_ref, sc_ref,
         rwh_ref, rwl_ref, rb_ref, x1_ref, h_ref, idx_ref, gate_ref, cnt_ref, base_ref) = refs
    else:
        (x_ref, a_ref, b_ref, wa_ref, wb_ref, gm_ref, gain_ref, sh_ref, sc_ref,
         rwh_ref, rwl_ref, rb_ref, x1_ref, h_ref, idx_ref, gate_ref, cnt_ref, base_ref) = refs

    @pl.when(pl.program_id(0) == 0)
    def _():
        base_ref[...] = jnp.zeros(base_ref.shape, F32)

    bm = b_ref[...]
    if glu:
        gl = jnp.dot(bm.astype(BF16), gw_ref[...], preferred_element_type=F32) + gb_ref[...]
        bm = (bm.astype(F32) * jax.nn.sigmoid(gl)).astype(BF16)
    y = (jnp.dot(a_ref[...], wa_ref[...], preferred_element_type=F32)
         + jnp.dot(bm, wb_ref[...], preferred_element_type=F32))
    x1 = x_ref[...] + gm_ref[0] * y
    x1_ref[...] = x1
    h = _norm_modulate(x1, gain_ref[...], sh_ref[0], sc_ref[0])
    _pack_rows(h, h_ref)

    h_hi = h.astype(BF16)
    h_lo = (h - h_hi.astype(F32)).astype(BF16)
    rows = h.shape[0]
    prod = jnp.dot(jnp.concatenate([h_hi, h_lo], axis=0),
                   jnp.concatenate([rwh_ref[...], rwl_ref[...]], axis=-1), preferred_element_type=F32)
    logits = (prod[:rows, :ROUTER_PAD] + prod[rows:, :ROUTER_PAD] + prod[:rows, ROUTER_PAD:]
              + rb_ref[...])
    lane = lax.broadcasted_iota(I32, logits.shape, 1)
    vals, idxs = [], []
    for _ in range(TOP_K):
        m = jnp.max(logits, axis=-1, keepdims=True)
        ik = jnp.min(jnp.where(logits == m, lane, ROUTER_PAD), axis=-1, keepdims=True)
        vals.append(m)
        idxs.append(ik)
        logits = jnp.where(lane == ik, -jnp.inf, logits)
    exps = [jnp.exp(v - vals[0]) for v in vals]
    denom = exps[0] + exps[1] + exps[2] + exps[3]

    tm = lane.shape[0]
    onehot = jnp.zeros(lane.shape, F32)
    for k in range(TOP_K):
        onehot = onehot + (lane == idxs[k]).astype(F32)
    before = (lax.broadcasted_iota(I32, (tm, tm), 1) < lax.broadcasted_iota(I32, (tm, tm), 0)).astype(BF16)
    count = jnp.dot(before, onehot.astype(BF16), preferred_element_type=F32) + base_ref[...]
    new_base = base_ref[...] + jnp.sum(onehot, axis=0, keepdims=True)
    base_ref[...] = new_base
    cnt_ref[...] = new_base.astype(I32)

    idx_out = jnp.zeros(lane.shape, I32)
    gate_out = jnp.zeros(lane.shape, F32)
    for k in range(TOP_K):
        rank = jnp.sum(jnp.where(lane == idxs[k], count, 0.0), axis=-1, keepdims=True).astype(I32)
        idx_out = jnp.where(lane == k, idxs[k], idx_out)
        idx_out = jnp.where(lane == TOP_K + k, rank, idx_out)
        gate_out = jnp.where(lane == k, exps[k] / denom, gate_out)
    idx_ref[...] = idx_out
    gate_ref[...] = gate_out


def _out_proj(x2d, mix_a, mix_b, w_out, g_mix, gain, shift, scale, router_w, router_b, seq,
              glu_w=None, glu_b=None):
    t, d = x2d.shape
    wa_rows = mix_a.shape[1]
    wb_rows = mix_b.shape[1]
    tm = 256
    per_seq = seq // tm
    n_exp = router_w.shape[1]
    rw = jnp.zeros((d, ROUTER_PAD), F32).at[:, :n_exp].set(router_w)
    rw_hi = rw.astype(BF16)
    rw_lo = (rw - rw_hi.astype(F32)).astype(BF16)
    rb = jnp.full((1, ROUTER_PAD), NEG_BIG, F32).at[0, :n_exp].set(router_b)
    w_bf = w_out.astype(BF16)
    glu = glu_w is not None

    def rows(width):
        return pl.BlockSpec((tm, width), lambda i: (i, 0))

    def full(r, c):
        return pl.BlockSpec((r, c), lambda i: (0, 0))

    def per_batch():
        return pl.BlockSpec((1, 1, d), lambda i: (i // per_seq, 0, 0))

    in_specs = [rows(d), rows(wa_rows), rows(wb_rows)]
    args = [x2d, mix_a, mix_b]
    if glu:
        in_specs += [full(wb_rows, wb_rows), full(1, wb_rows)]
        args += [glu_w.astype(BF16), glu_b.reshape(1, wb_rows)]
    in_specs += [pl.BlockSpec((wa_rows, d), lambda i: (0, 0)),
                 pl.BlockSpec((wb_rows, d), lambda i: (wa_rows // wb_rows, 0)),
                 per_batch(), full(1, d), per_batch(), per_batch(),
                 full(d, ROUTER_PAD), full(d, ROUTER_PAD), full(1, ROUTER_PAD)]
    args += [w_bf, w_bf, g_mix, gain.reshape(1, d), shift, scale, rw_hi, rw_lo, rb]
    return pl.pallas_call(
        functools.partial(_outproj_kernel, glu=glu),
        grid=(t // tm,),
        in_specs=in_specs,
        out_specs=[rows(d), pl.BlockSpec((tm * (d // (2 * LANES)), LANES), lambda i: (i, 0)),
                   rows(ROUTER_PAD), rows(ROUTER_PAD), full(1, ROUTER_PAD)],
        out_shape=[jax.ShapeDtypeStruct((t, d), F32), jax.ShapeDtypeStruct((t * (d // (2 * LANES)), LANES), I32),
                   jax.ShapeDtypeStruct((t, ROUTER_PAD), I32), jax.ShapeDtypeStruct((t, ROUTER_PAD), F32),
                   jax.ShapeDtypeStruct((1, ROUTER_PAD), I32)],
        scratch_shapes=[pltpu.VMEM((1, ROUTER_PAD), F32)],
        compiler_params=_cparams(("arbitrary",)),
        name="out_proj_glu" if glu else "out_proj",
    )(*args)


def _routing_tables(top_idx, rank, counts, n_tok):
    n_assign = n_tok * TOP_K
    r = MOE_ROWS
    n_sb = n_assign // (2 * r) + N_EXPERTS
    n_rb = 2 * n_sb
    experts = jnp.arange(N_EXPERTS, dtype=I32)
    used = (counts + r - 1) // r
    nsb = (used + 1) // 2
    sb_end = jnp.cumsum(nsb)
    sb_start = sb_end - nsb
    blk_start = 2 * sb_start
    row_start = jnp.sum(jnp.where(top_idx[:, :, None] == experts[None, None, :],
                                  (blk_start * r)[None, None, :], 0), axis=-1)
    dest = (row_start + rank).reshape(-1).astype(I32)
    rb = jnp.arange(n_rb, dtype=I32)
    rb_e = jnp.sum((sb_end[None, :] <= (rb // 2)[:, None]).astype(I32), axis=1)
    in_range = rb_e < N_EXPERTS
    rb_e = jnp.minimum(rb_e, N_EXPERTS - 1).astype(I32)
    rb_valid = (in_range & (rb - blk_start[rb_e] < used[rb_e])).astype(I32)
    tail = jnp.where(used > 0, blk_start + used - 1, -1)
    empty = jnp.where(2 * nsb > used, blk_start + used, -1)
    spare = 2 * sb_end[-1] + jnp.arange(2 * N_EXPERTS, dtype=I32)
    spare = jnp.where(spare < n_rb, spare, -1)
    zero_blocks = jnp.concatenate([tail, empty, spare]).astype(I32)
    return dict(dest=dest, rb_e=rb_e, rb_valid=rb_valid, zero_blocks=zero_blocks,
                n_valid_sb=sb_end[-1], nsb=nsb.astype(I32), sb_start=sb_start.astype(I32),
                n_sb=n_sb, n_rb=n_rb, n_slots=n_rb * r)


def _dispatch_kernel(dest_ref, zb_ref, h_ref, xs_hbm, zero_ref, zsem_ref, sem_ref):
    i = pl.program_id(0)
    td = DISPATCH_TOKENS
    sr = h_ref.shape[0] // td
    blk = MOE_ROWS * sr

    def slab(ref, row):
        return ref.at[pl.ds(pl.multiple_of(row * sr, sr), sr), :]

    @pl.when(i == 0)
    def _():
        zero_ref[...] = jnp.zeros(zero_ref.shape, zero_ref.dtype)

        def zero_copy(z):
            row0 = pl.multiple_of(zb_ref[z] * blk, blk)
            return pltpu.make_async_copy(zero_ref, xs_hbm.at[pl.ds(row0, blk), :], zsem_ref.at[0])

        def start(z, _):
            @pl.when(zb_ref[z] >= 0)
            def _():
                zero_copy(z).start()
            return 0

        def wait(z, _):
            @pl.when(zb_ref[z] >= 0)
            def _():
                zero_copy(z).wait()
            return 0

        lax.fori_loop(0, zb_ref.shape[0], start, 0)
        lax.fori_loop(0, zb_ref.shape[0], wait, 0)

    def group(g, _):
        for s in range(DMA_UNROLL):
            t = g * DMA_UNROLL + s
            for k in range(TOP_K):
                dst = dest_ref[(i * td + t) * TOP_K + k]
                pltpu.make_async_copy(slab(h_ref, t), slab(xs_hbm, dst), sem_ref.at[0]).start()
        return 0

    lax.fori_loop(0, td // DMA_UNROLL, group, 0)
    for _ in range(TOP_K):
        pltpu.make_async_copy(h_ref, xs_hbm.at[pl.ds(0, td * sr), :], sem_ref.at[0]).wait()


def _moe_dispatch(h_slabs, n_tok, rt):
    sr = h_slabs.shape[0] // n_tok
    td = DISPATCH_TOKENS
    grid_spec = pltpu.PrefetchScalarGridSpec(
        num_scalar_prefetch=2,
        grid=(n_tok // td,),
        in_specs=[pl.BlockSpec((td * sr, LANES), lambda i, dst, zb: (i, 0))],
        out_specs=pl.BlockSpec(memory_space=pl.ANY),
        scratch_shapes=[pltpu.VMEM((MOE_ROWS * sr, LANES), h_slabs.dtype), pltpu.SemaphoreType.DMA((1,)),
                        pltpu.SemaphoreType.DMA((1,))],
    )
    return pl.pallas_call(
        _dispatch_kernel,
        grid_spec=grid_spec,
        out_shape=jax.ShapeDtypeStruct((rt['n_slots'] * sr, LANES), h_slabs.dtype),
        compiler_params=_cparams(("arbitrary",)),
        name="moe_dispatch",
    )(rt['dest'], rt['zero_blocks'], h_slabs)


def _cast_rows(src_ref, dst_ref, chunk=256):
    chunk = min(chunk, dst_ref.shape[0])
    n = dst_ref.shape[0] // chunk

    def body(c, _):
        sl = pl.ds(pl.multiple_of(c * chunk, chunk), chunk)
        dst_ref[sl, :] = src_ref[sl, :].astype(dst_ref.dtype)
        return 0

    lax.fori_loop(0, n, body, 0)


def _stagger_tables(group, cols, parts):
    n = group.shape[0]
    idx = jnp.arange(n, dtype=I32)
    is_first = jnp.concatenate([jnp.ones((1,), bool), group[1:] != group[:-1]])
    first_cur = lax.cummax(jnp.where(is_first, idx, 0))
    nxt = jnp.where(is_first, idx, n)
    first_next = jnp.concatenate([lax.cummin(nxt[::-1])[::-1][1:], jnp.full((1,), n, I32)])
    out = [[] for _ in cols]
    for p in range(parts):
        switch = (idx >= jnp.maximum(first_next - p, first_cur + 1)) & (first_next < n)
        eff = jnp.where(switch, first_next, idx)
        for k, c in enumerate(cols):
            out[k].append(c[eff])
    return [jnp.stack(o) for o in out]


def _moe_up_kernel(sb_ref, j_ref, e_ref, first_ref, va_ref, vb_ref, pe_ref, pj_ref, x_ref, *refs):
    parts = MOE_W_PARTS
    wg_refs = refs[:parts]
    wu_refs = refs[parts:2 * parts]
    bg_ref, bu_ref, o_ref, wgs_ref, wus_ref = refs[2 * parts:]
    i = pl.program_id(0)
    rows = wgs_ref.shape[0] // parts
    r = MOE_ROWS
    sr = x_ref.shape[0] // (2 * r)

    @pl.when(first_ref[i] == 1)
    def _():
        for p in range(parts):
            _cast_rows(wg_refs[p].at[0, 0], wgs_ref.at[pl.ds(p * rows, rows)])
            _cast_rows(wu_refs[p].at[0, 0], wus_ref.at[pl.ds(p * rows, rows)])

    def compute(slabs):
        x_lo, x_hi = _unpack_rows(slabs, sr)
        half = wgs_ref.shape[0] // 2

        def proj(ws_ref, b_ref):
            return (jnp.dot(x_lo, ws_ref[:half, :], preferred_element_type=F32)
                    + jnp.dot(x_hi, ws_ref[half:, :], preferred_element_type=F32) + b_ref[0])

        gate = jnp.minimum(proj(wgs_ref, bg_ref), SWIGLU_LIMIT)
        up = jnp.clip(proj(wus_ref, bu_ref), -SWIGLU_LIMIT, SWIGLU_LIMIT)
        return ((up + 1.0) * gate * jax.nn.sigmoid(SWIGLU_ALPHA * gate)).astype(o_ref.dtype)

    va = va_ref[i] == 1
    vb = vb_ref[i] == 1

    @pl.when(vb)
    def _():
        o_ref[...] = compute(x_ref)

    @pl.when(va & jnp.logical_not(vb))
    def _():
        o_ref[pl.ds(0, r), :] = compute(x_ref.at[pl.ds(0, r * sr), :])
        o_ref[pl.ds(r, r), :] = jnp.zeros((r, o_ref.shape[1]), o_ref.dtype)

    @pl.when(jnp.logical_not(va))
    def _():
        o_ref[...] = jnp.zeros(o_ref.shape, o_ref.dtype)


def _moe_items(rt, nj, n_exp):
    n_sb = rt['n_sb']
    n_items = n_sb * nj
    sb = jnp.arange(n_sb, dtype=I32)
    e_of = rt['rb_e'][2 * sb]
    nsb = rt['nsb'].at[n_exp - 1].add(n_sb - rt['n_valid_sb'])
    q = sb - rt['sb_start'][e_of]
    pos = (nj * rt['sb_start'][e_of][:, None]
           + jnp.arange(nj, dtype=I32)[None, :] * nsb[e_of][:, None] + q[:, None]).reshape(-1)
    sb2 = jnp.broadcast_to(sb[:, None], (n_sb, nj)).reshape(-1)
    j2 = jnp.broadcast_to(jnp.arange(nj, dtype=I32)[None, :], (n_sb, nj)).reshape(-1)
    it_sb = jnp.zeros((n_items,), I32).at[pos].set(sb2)
    it_j = jnp.zeros((n_items,), I32).at[pos].set(j2)
    it_va = rt['rb_valid'][2 * it_sb]
    it_vb = rt['rb_valid'][2 * it_sb + 1]
    it_e = e_of[it_sb]
    prev_e = jnp.concatenate([jnp.full((1,), -1, I32), it_e[:-1]])
    prev_j = jnp.concatenate([jnp.full((1,), -1, I32), it_j[:-1]])
    it_first = ((it_e != prev_e) | (it_j != prev_j)).astype(I32)
    part_e, part_j = _stagger_tables(jnp.cumsum(it_first), [it_e, it_j], MOE_W_PARTS)
    return n_items, (it_sb, it_j, it_e, it_first, it_va, it_vb, part_e, part_j)


def _moe_up(x_sorted, w_gu, b_gu, layer, rt):
    _, n_exp, d, two_f = w_gu.shape
    n_slots = rt['n_slots']
    sr = x_sorted.shape[0] // n_slots
    f = two_f // 2
    r, tn = MOE_ROWS, MOE_UP_TN
    nj = f // tn
    parts = MOE_W_PARTS
    n_items, tables = _moe_items(rt, nj, n_exp)

    def w_spec(p, col0):
        return pl.BlockSpec((1, 1, d // parts, tn),
                            lambda i, sbt, jt, et, ft, va, vb, pe, pj: (layer, pe[p, i], p, col0 + pj[p, i]))

    grid_spec = pltpu.PrefetchScalarGridSpec(
        num_scalar_prefetch=8,
        grid=(n_items,),
        in_specs=([pl.BlockSpec((2 * r * sr, LANES), lambda i, sbt, jt, et, ft, va, vb, pe, pj: (sbt[i], 0))]
                  + [w_spec(p, 0) for p in range(parts)] + [w_spec(p, nj) for p in range(parts)]
                  + [pl.BlockSpec((1, 1, tn), lambda i, sbt, jt, et, ft, va, vb, pe, pj: (et[i], 0, jt[i])),
                     pl.BlockSpec((1, 1, tn), lambda i, sbt, jt, et, ft, va, vb, pe, pj: (et[i], 0, nj + jt[i]))]),
        out_specs=pl.BlockSpec((2 * r, tn), lambda i, sbt, jt, et, ft, va, vb, pe, pj: (sbt[i], jt[i])),
        scratch_shapes=[pltpu.VMEM((d, tn), BF16), pltpu.VMEM((d, tn), BF16)],
    )
    b3 = b_gu.reshape(n_exp, 1, two_f)
    return pl.pallas_call(
        _moe_up_kernel,
        grid_spec=grid_spec,
        out_shape=jax.ShapeDtypeStruct((n_slots, f), BF16),
        compiler_params=_cparams(("arbitrary",)),
        name="moe_up",
    )(*tables, x_sorted, *([w_gu] * (2 * parts)), b3, b3)


def _moe_dn_kernel(sb_ref, j_ref, e_ref, first_ref, va_ref, vb_ref, pe_ref, pj_ref, a_ref, *refs):
    parts = MOE_W_PARTS
    w_refs = refs[:parts]
    b_ref, o_ref, ws_ref = refs[parts:]
    i = pl.program_id(0)
    rows = ws_ref.shape[0] // parts
    r = MOE_ROWS

    @pl.when(first_ref[i] == 1)
    def _():
        for p in range(parts):
            _cast_rows(w_refs[p].at[0, 0], ws_ref.at[pl.ds(p * rows, rows)])

    va = va_ref[i] == 1
    vb = vb_ref[i] == 1

    @pl.when(vb)
    def _():
        o_ref[...] = jnp.dot(a_ref[...], ws_ref[...], preferred_element_type=F32) + b_ref[0]

    @pl.when(va & jnp.logical_not(vb))
    def _():
        o_ref[pl.ds(0, r), :] = (jnp.dot(a_ref[pl.ds(0, r), :], ws_ref[...], preferred_element_type=F32)
                                 + b_ref[0])
        o_ref[pl.ds(r, r), :] = jnp.zeros((r, o_ref.shape[1]), o_ref.dtype)

    @pl.when(jnp.logical_not(va))
    def _():
        o_ref[...] = jnp.zeros(o_ref.shape, o_ref.dtype)


def _moe_dn(act, w_dn, b_dn, layer, rt):
    n_slots, f = act.shape
    _, n_exp, _, d = w_dn.shape
    r, tn = MOE_ROWS, MOE_DN_TN
    nj = d // tn
    parts = MOE_W_PARTS
    n_items, tables = _moe_items(rt, nj, n_exp)

    def w_spec(p):
        return pl.BlockSpec((1, 1, f // parts, tn),
                            lambda i, sbt, jt, et, ft, va, vb, pe, pj: (layer, pe[p, i], p, pj[p, i]))

    grid_spec = pltpu.PrefetchScalarGridSpec(
        num_scalar_prefetch=8,
        grid=(n_items,),
        in_specs=([pl.BlockSpec((2 * r, f), lambda i, sbt, jt, et, ft, va, vb, pe, pj: (sbt[i], 0))]
                  + [w_spec(p) for p in range(parts)]
                  + [pl.BlockSpec((1, 1, tn), lambda i, sbt, jt, et, ft, va, vb, pe, pj: (et[i], 0, jt[i]))]),
        out_specs=pl.BlockSpec((2 * r, tn), lambda i, sbt, jt, et, ft, va, vb, pe, pj: (sbt[i], jt[i])),
        scratch_shapes=[pltpu.VMEM((f, tn), BF16)],
    )
    return pl.pallas_call(
        _moe_dn_kernel,
        grid_spec=grid_spec,
        out_shape=jax.ShapeDtypeStruct((n_slots, d), F32),
        compiler_params=_cparams(("arbitrary",)),
        name="moe_dn",
    )(*tables, act, *([w_dn] * parts), b_dn.reshape(n_exp, 1, d))


def _combine_kernel(dest_ref, x_ref, gate_ref, g_ref, fg_ref, y_hbm, o_ref, buf_ref, sem_ref, *, final):
    i = pl.program_id(0)
    n_steps = pl.num_programs(0)
    tc = COMBINE_TOKENS

    def issue(step, slot):
        def group(g, _):
            for s in range(DMA_UNROLL):
                t = g * DMA_UNROLL + s
                for k in range(TOP_K):
                    src = dest_ref[(step * tc + t) * TOP_K + k]
                    pltpu.make_async_copy(y_hbm.at[pl.ds(src, 1), :], buf_ref.at[slot, k, pl.ds(t, 1), :],
                                          sem_ref.at[slot]).start()
            return 0

        lax.fori_loop(0, tc // DMA_UNROLL, group, 0)

    @pl.when(i == 0)
    def _():
        issue(0, 0)

    @pl.when(i + 1 < n_steps)
    def _():
        issue(i + 1, (i + 1) % 2)

    slot = i % 2
    for k in range(TOP_K):
        pltpu.make_async_copy(y_hbm.at[pl.ds(0, tc), :], buf_ref.at[slot, k], sem_ref.at[slot]).wait()
    gates = gate_ref[...]
    y = gates[:, 0:1] * buf_ref[slot, 0]
    for k in range(1, TOP_K):
        y = y + gates[:, k:k + 1] * buf_ref[slot, k]
    x2 = x_ref[...] + g_ref[0] * y
    if final:
        x2 = x2 * lax.rsqrt(jnp.mean(x2 * x2, axis=-1, keepdims=True) + EPS) * fg_ref[...]
    o_ref[...] = x2


def _combine(x1, y_sorted, gates, rt, g_ffn, final_gain, seq, final):
    t, d = x1.shape
    tc = COMBINE_TOKENS
    per_seq = seq // tc
    grid_spec = pltpu.PrefetchScalarGridSpec(
        num_scalar_prefetch=1,
        grid=(t // tc,),
        in_specs=[pl.BlockSpec((tc, d), lambda i, dst: (i, 0)),
                  pl.BlockSpec((tc, ROUTER_PAD), lambda i, dst: (i, 0)),
                  pl.BlockSpec((1, 1, d), lambda i, dst: (i // per_seq, 0, 0)),
                  pl.BlockSpec((1, d), lambda i, dst: (0, 0)),
                  pl.BlockSpec(memory_space=pl.ANY)],
        out_specs=pl.BlockSpec((tc, d), lambda i, dst: (i, 0)),
        scratch_shapes=[pltpu.VMEM((2, TOP_K, tc, d), F32), pltpu.SemaphoreType.DMA((2,))],
    )
    return pl.pallas_call(
        functools.partial(_combine_kernel, final=final),
        grid_spec=grid_spec,
        out_shape=jax.ShapeDtypeStruct((t, d), F32),
        compiler_params=_cparams(("arbitrary",)),
        name="combine_final" if final else "combine",
    )(rt['dest'], x1, gates, g_ffn, final_gain.reshape(1, d), y_sorted)


def _moe(x1, h, idx, gates, counts, w_gu, b_gu, w_dn, b_dn, layer, g_ffn, final_gain, seq, final):
    n_tok = x1.shape[0]
    rt = _routing_tables(idx[:, :TOP_K], idx[:, TOP_K:2 * TOP_K], counts[0, :N_EXPERTS], n_tok)
    x_sorted = _moe_dispatch(h, n_tok, rt)
    act = _moe_up(x_sorted, w_gu, b_gu, layer, rt)
    y_sorted = _moe_dn(act, w_dn, b_dn, layer, rt)
    return _combine(x1, y_sorted, gates, rt, g_ffn, final_gain, seq, final)


def kernel(x, c, ada_w, ada_b, norm_mix_gain, norm_ffn_gain, ab_w_in, ab_w_out, hg_lb_logits, hg_norm_gain, s5_lam_re, s5_lam_im, s5_log_dt, s5_b_re, s5_b_im, s5_c_re, s5_c_im, s5_d, s5_glu_w, s5_glu_b, cd_w_in, cd_w_out, ret_norm_gain, router_w, router_b, moe_w_gu, moe_b_gu, moe_w_dn, moe_b_dn, final_gain):
    bsz, seq, d = x.shape
    depth = ada_w.shape[0]
    n_tok = bsz * seq
    hg_width = hg_lb_logits.shape[1]
    s5_width = s5_glu_w.shape[1]
    ret_width = ret_norm_gain.shape[1]
    fnet_width = cd_w_out.shape[1] - ret_width

    lower_bounds = jnp.cumsum(jax.nn.softmax(hg_lb_logits.astype(F32), axis=0), axis=0)
    mod = _ada_mod(c, ada_w, ada_b)
    xr = x.reshape(n_tok, d)
    for layer in range(depth):
        sh_mix, sc_mix, g_mix, sh_ffn, sc_ffn, g_ffn = (
            mod[layer, :, k * d:(k + 1) * d].reshape(bsz, 1, d) for k in range(6))
        j = layer // 2
        if layer % 2 == 0:
            proj = _in_proj(xr, norm_mix_gain[layer], sh_mix, sc_mix, ab_w_in[j].astype(BF16), seq)
            proj3 = proj.reshape(bsz, seq, proj.shape[1])
            mix_a = _hgrn2(proj3, lower_bounds[j], hg_norm_gain[j], hg_width)
            tables = _s5c_tables(s5_lam_re[j], s5_lam_im[j], s5_log_dt[j], s5_b_re[j], s5_b_im[j],
                                 s5_c_re[j], s5_c_im[j], s5_d[j])
            mix_b = _s5c(proj3, 5 * hg_width, s5_width, tables)
            x1, h, idx, gate, counts = _out_proj(
                xr, mix_a.reshape(n_tok, hg_width), mix_b.reshape(n_tok, s5_width), ab_w_out[j],
                g_mix, norm_ffn_gain[layer], sh_ffn, sc_ffn, router_w[layer], router_b[layer], seq,
                glu_w=s5_glu_w[j], glu_b=s5_glu_b[j])
        else:
            proj = _in_proj(xr, norm_mix_gain[layer], sh_mix, sc_mix, cd_w_in[j].astype(BF16), seq)
            proj3 = proj.reshape(bsz, seq, proj.shape[1])
            mix_a = _retention(proj3, ret_norm_gain[j], ret_width)
            mix_b = _fnet(proj3, 4 * ret_width, fnet_width)
            x1, h, idx, gate, counts = _out_proj(
                xr, mix_a.reshape(n_tok, ret_width), mix_b.reshape(n_tok, fnet_width), cd_w_out[j],
                g_mix, norm_ffn_gain[layer], sh_ffn, sc_ffn, router_w[layer], router_b[layer], seq)
        xr = _moe(x1, h, idx, gate, counts, moe_w_gu, moe_b_gu[layer], moe_w_dn, moe_b_dn[layer],
                  layer, g_ffn, final_gain, seq, final=(layer == depth - 1))
    return xr.reshape(bsz, seq, d)
```

```python
import functools
import math

import jax
import jax.numpy as jnp
from jax import lax
from jax.experimental import pallas as pl
from jax.experimental.pallas import tpu as pltpu

F32 = jnp.float32
BF16 = jnp.bfloat16
I32 = jnp.int32

EPS = 1e-6
LANES = 128
SUBLANES = 8
VMEM_LIMIT = 56 * 1024 * 1024

HG_HEAD_DIM = 128
HG_CHUNK = 64
HG_GROUP = 256
HG_EXP_CLAMP = 80.0

S5_GROUP = 16
S5_STATE = 64
S5_CHUNK = 16
S5_UNROLL = 8
S5_TILE_PAIRS = 4
S5_ROW_SPLIT = 4

RET_HEAD_DIM = 256
RET_Q_TILE = 256
ROPE_BASE = 10000.0

FNET_GROUPS = 4
FNET_ROW_TILE = 512

N_EXPERTS = 32
TOP_K = 4
SWIGLU_LIMIT = 7.0
SWIGLU_ALPHA = 1.702
MOE_ROWS = 256
MOE_UP_TN = 1024
MOE_DN_TN = 1024
MOE_W_PARTS = 4
DISPATCH_TOKENS = 512
COMBINE_TOKENS = 256
DMA_UNROLL = 8
ROUTER_PAD = LANES
NEG_BIG = -1e30


def _cparams(semantics):
    return pltpu.CompilerParams(dimension_semantics=semantics, vmem_limit_bytes=VMEM_LIMIT)


def _ada_kernel(c_ref, w_ref, b_ref, o_ref):
    c = c_ref[...]
    cond = c * jax.nn.sigmoid(c)
    o_ref[0] = jnp.dot(cond.astype(BF16), w_ref[0].astype(BF16),
                       preferred_element_type=F32) + b_ref[0]


def _ada_mod(c, ada_w, ada_b):
    depth, d, n = ada_w.shape
    bsz = c.shape[0]
    tn = 1024
    return pl.pallas_call(
        _ada_kernel,
        grid=(depth, n // tn),
        in_specs=[
            pl.BlockSpec((bsz, d), lambda l, j: (0, 0)),
            pl.BlockSpec((1, d, tn), lambda l, j: (l, 0, j)),
            pl.BlockSpec((1, 1, tn), lambda l, j: (l, 0, j)),
        ],
        out_specs=pl.BlockSpec((1, bsz, tn), lambda l, j: (l, 0, j)),
        out_shape=jax.ShapeDtypeStruct((depth, bsz, n), F32),
        compiler_params=_cparams(("arbitrary", "arbitrary")),
        name="ada_mod",
    )(c, ada_w, ada_b.reshape(depth, 1, n))


def _norm_modulate(x, gain, shift, scale):
    ms = jnp.mean(x * x, axis=-1, keepdims=True)
    y = x * lax.rsqrt(ms + EPS) * gain
    return y * (1.0 + scale) + shift


def _inproj_kernel(x_ref, gain_ref, sh_ref, sc_ref, w_ref, o_ref, h_ref):
    @pl.when(pl.program_id(1) == 0)
    def _():
        rows = 128

        def slab(c, _):
            sl = pl.ds(pl.multiple_of(c * rows, rows), rows)
            h_ref[sl, :] = _norm_modulate(x_ref[sl, :], gain_ref[...], sh_ref[0], sc_ref[0]).astype(BF16)
            return 0

        lax.fori_loop(0, h_ref.shape[0] // rows, slab, 0)

    o_ref[...] = jnp.dot(h_ref[...], w_ref[...], preferred_element_type=F32)


def _in_proj(x2d, gain, shift, scale, w_bf16, seq):
    t, d = x2d.shape
    n = w_bf16.shape[1]
    tm = 1024
    tn = 1024 if n % 1024 == 0 else 512
    per_seq = seq // tm
    return pl.pallas_call(
        _inproj_kernel,
        grid=(t // tm, n // tn),
        in_specs=[
            pl.BlockSpec((tm, d), lambda i, j: (i, 0)),
            pl.BlockSpec((1, d), lambda i, j: (0, 0)),
            pl.BlockSpec((1, 1, d), lambda i, j: (i // per_seq, 0, 0)),
            pl.BlockSpec((1, 1, d), lambda i, j: (i // per_seq, 0, 0)),
            pl.BlockSpec((d, tn), lambda i, j: (0, j)),
        ],
        out_specs=pl.BlockSpec((tm, tn), lambda i, j: (i, j)),
        out_shape=jax.ShapeDtypeStruct((t, n), F32),
        scratch_shapes=[pltpu.VMEM((tm, d), BF16)],
        compiler_params=_cparams(("arbitrary", "arbitrary")),
        name="in_proj",
    )(x2d, gain.reshape(1, d), shift, scale, w_bf16)


def _split3(a):
    hi = a.astype(BF16)
    r1 = a - hi.astype(F32)
    mid = r1.astype(BF16)
    lo = (r1 - mid.astype(F32)).astype(BF16)
    return hi, mid, lo


def _tri_sum(tri, a):
    hi, mid, lo = _split3(a)
    return (jnp.dot(tri, hi, preferred_element_type=F32)
            + jnp.dot(tri, mid, preferred_element_type=F32)
            + jnp.dot(tri, lo, preferred_element_type=F32))


def _dot_nt(a, b):
    return lax.dot_general(a, b, (((1,), (1,)), ((), ())), preferred_element_type=F32)


def _dot_tn(a, b):
    return lax.dot_general(a, b, (((0,), (0,)), ((), ())), preferred_element_type=F32)


def _hgrn2_kernel(q_ref, zf_ref, zb_ref, v_ref, g_ref, lb_ref, gain_ref, o_ref, acc_ref, accb_ref):
    seq = q_ref.shape[1]
    ln = HG_CHUNK
    gr = HG_GROUP
    n_groups = seq // gr
    per_group = gr // ln
    lb = lb_ref[...]
    gain = gain_ref[...]
    row = lax.broadcasted_iota(I32, (gr, gr), 0)
    col = lax.broadcasted_iota(I32, (gr, gr), 1)
    chunk_lo = (row // ln) * ln
    chunk_hi = chunk_lo + ln
    lower_incl = (col <= row) & (col >= chunk_lo)
    upper_strict = (col > row) & (col < chunk_hi)
    tri_prefix = lower_incl.astype(BF16)
    tri_suffix = ((col >= row) & (col < chunk_hi)).astype(BF16)
    mid = ln // 2

    def per_chunk_rows(a, offset):
        return jnp.concatenate(
            [jnp.broadcast_to(a[j * ln + offset:j * ln + offset + 1, :], (ln, a.shape[1]))
             for j in range(per_group)], axis=0)

    rows = [slice(j * ln, (j + 1) * ln) for j in range(per_group)]

    def load(gi, z_ref, forward):
        sl = pl.ds(pl.multiple_of(gi * gr, gr), gr)
        f = lb + (1.0 - lb) * jax.nn.sigmoid(z_ref[0, sl, :])
        log_f = jnp.log(f)
        d = dict(sl=sl, forward=forward, q=q_ref[0, sl, :], v=v_ref[0, sl, :].astype(BF16), k=1.0 - f)
        if forward:
            d['cum'] = _tri_sum(tri_prefix, log_f)
        else:
            d['cum'] = _tri_sum(tri_suffix, log_f)
        return d

    def local_states(d):
        edge_off = ln - 1 if d['forward'] else 0
        kd = (d['k'] * jnp.exp(per_chunk_rows(d['cum'], edge_off) - d['cum'])).astype(BF16)
        d['local'] = [_dot_tn(d['v'][rs], kd[rs]) for rs in rows]

    def scores(d):
        ref_rows = per_chunk_rows(d['cum'], mid - 1 if d['forward'] else mid)
        qe = d['q'] * jnp.exp(jnp.minimum(d['cum'] - ref_rows, HG_EXP_CLAMP))
        ke = d['k'] * jnp.exp(jnp.minimum(ref_rows - d['cum'], HG_EXP_CLAMP))
        mask = lower_incl if d['forward'] else upper_strict
        d['scores'] = jnp.where(mask, _dot_nt(qe.astype(BF16), ke.astype(BF16)), 0.0)

    def inter(d, state_t):
        edge_off = ln - 1 if d['forward'] else 0
        qc = (d['q'] * jnp.exp(d['cum'])).astype(BF16)
        entering = [None] * per_group
        order = range(per_group) if d['forward'] else range(per_group - 1, -1, -1)
        for j in order:
            entering[j] = state_t.astype(BF16)
            edge = d['cum'][j * ln + edge_off:j * ln + edge_off + 1, :]
            state_t = state_t * jnp.exp(edge) + d['local'][j]
        d['inter'] = jnp.concatenate([_dot_nt(qc[rs], entering[j]) for j, rs in enumerate(rows)], axis=0)
        return state_t

    def body(i, states):
        st_f, st_b = states
        both = [load(i, zf_ref, True), load(n_groups - 1 - i, zb_ref, False)]
        for d in both:
            local_states(d)
        for d in both:
            scores(d)
        st_f = inter(both[0], st_f)
        st_b = inter(both[1], st_b)
        for d, ref in zip(both, (acc_ref, accb_ref)):
            ref[d['sl'], :] = jnp.dot(d['scores'].astype(BF16), d['v'], preferred_element_type=F32) + d['inter']
        return st_f, st_b

    zero = jnp.zeros((HG_HEAD_DIM, HG_HEAD_DIM), F32)
    lax.fori_loop(0, n_groups, body, (zero, zero))

    def finish(gi, _):
        sl = pl.ds(pl.multiple_of(gi * gr, gr), gr)
        o = acc_ref[sl, :] + accb_ref[sl, :]
        y = o * lax.rsqrt(jnp.mean(o * o, axis=-1, keepdims=True) + EPS) * gain
        g = g_ref[0, sl, :]
        o_ref[0, sl, :] = (y * (g * jax.nn.sigmoid(g))).astype(o_ref.dtype)
        return 0

    lax.fori_loop(0, n_groups, finish, 0)


def _hgrn2(proj3, lower_bound, hg_gain, width):
    bsz, seq, _ = proj3.shape
    heads = width // HG_HEAD_DIM
    dh = HG_HEAD_DIM

    def col(k):
        return pl.BlockSpec((1, seq, dh), lambda b, h: (b, 0, k * heads + h))

    return pl.pallas_call(
        _hgrn2_kernel,
        grid=(bsz, heads),
        in_specs=[col(0), col(1), col(2), col(3), col(4),
                  pl.BlockSpec((1, dh), lambda b, h: (0, h)),
                  pl.BlockSpec((1, dh), lambda b, h: (0, 0))],
        out_specs=pl.BlockSpec((1, seq, dh), lambda b, h: (b, 0, h)),
        out_shape=jax.ShapeDtypeStruct((bsz, seq, width), BF16),
        scratch_shapes=[pltpu.VMEM((seq, dh), F32), pltpu.VMEM((seq, dh), F32)],
        compiler_params=_cparams(("arbitrary", "arbitrary")),
        name="hgrn2",
    )(proj3, proj3, proj3, proj3, proj3, lower_bound.reshape(1, width), hg_gain.reshape(1, dh))


def _s5c_kernel(u_ref, t_ref, w_ref, v_ref, a_ref, z_ref, uc_ref, e_ref, p_ref):
    ck = S5_CHUNK
    n_chunks = u_ref.shape[1] // ck
    slab = n_chunks // S5_ROW_SPLIT
    gpt = 2 * S5_TILE_PAIRS
    half_tok = ck // 2
    lane_grp = lax.broadcasted_iota(I32, (slab, LANES), 1) // S5_GROUP

    def token_rows(rq, tok):
        return pl.ds(rq * slab * ck + tok, slab, stride=ck)

    def roll_lanes(a, groups):
        shift = (groups * S5_GROUP) % LANES
        return pltpu.roll(a, shift, 1) if shift else a

    for half in range(2):
        for rq in range(S5_ROW_SPLIT):
            toks = [u_ref.at[0][token_rows(rq, half * half_tok + sl), :] for sl in range(half_tok)]
            for gl in range(gpt):
                acc = jnp.zeros((slab, LANES), F32)
                for sl in range(half_tok):
                    acc = jnp.where(lane_grp == sl, roll_lanes(toks[sl], sl - gl), acc)
                uc_ref[gl * 2 + half, rq * slab:(rq + 1) * slab, :] = acc

    seg = [pl.ds(k * LANES, LANES) for k in range(4)]

    def group_inputs(g):
        return jnp.concatenate([uc_ref[2 * g], uc_ref[2 * g + 1]], axis=-1).astype(BF16)

    for g in range(gpt):
        e_ref[:, g, :] = jnp.dot(group_inputs(g), w_ref[g], preferred_element_type=F32)

    af_r, af_i, ab_r, ab_i = a_ref[0, 0], a_ref[0, 1], a_ref[0, 2], a_ref[0, 3]

    def cmul(ar, ai, br, bi):
        return ar * br - ai * bi, ar * bi + ai * br

    def step(m, carry):
        xr, xi, yr, yi = carry
        mb = n_chunks - 1 - m
        p_ref[m, :, seg[0]] = xr
        p_ref[m, :, seg[1]] = xi
        p_ref[mb, :, seg[2]] = yr
        p_ref[mb, :, seg[3]] = yi
        dr, di = cmul(af_r, af_i, xr, xi)
        gr, gi = cmul(ab_r, ab_i, yr, yi)
        return (dr + e_ref[m, :, seg[0]], di + e_ref[m, :, seg[1]],
                gr + e_ref[mb, :, seg[2]], gi + e_ref[mb, :, seg[3]])

    zero = jnp.zeros((gpt, LANES), F32)
    lax.fori_loop(0, n_chunks, step, (zero, zero, zero, zero), unroll=S5_UNROLL)

    for g in range(gpt):
        y = (jnp.dot(group_inputs(g), t_ref[g], preferred_element_type=F32)
             + jnp.dot(p_ref[:, g, :].astype(BF16), v_ref[g], preferred_element_type=F32))
        zt = jax.nn.gelu(y)
        for k in range(2):
            uc_ref[2 * g + k] = zt[:, k * LANES:(k + 1) * LANES]

    for half in range(2):
        for rq in range(S5_ROW_SPLIT):
            cols = [uc_ref[gl * 2 + half, rq * slab:(rq + 1) * slab, :] for gl in range(gpt)]
            for sl in range(half_tok):
                acc = jnp.zeros((slab, LANES), F32)
                for gl in range(gpt):
                    acc = jnp.where(lane_grp == gl, roll_lanes(cols[gl], gl - sl), acc)
                z_ref.at[0][token_rows(rq, half * half_tok + sl), :] = acc


_einsum_f32 = functools.partial(jnp.einsum, precision=lax.Precision.HIGHEST)


def _s5c_tables(lam_re, lam_im, log_dt, b_re, b_im, c_re, c_im, d_skip):
    groups, state = lam_re.shape[1], lam_re.shape[2]
    chans = b_re.shape[2]
    ck = S5_CHUNK
    lag = jnp.arange(ck + 1, dtype=F32)[:, None, None]
    kern, w_parts, v_parts, a_parts = [], [], [], []
    for direction in (0, 1):
        lr, li = lam_re[direction].astype(F32), lam_im[direction].astype(F32)
        dt = jnp.exp(log_dt[direction].astype(F32))[:, None]
        mag = jnp.exp(lr * dt)
        abar_re = mag * jnp.cos(li * dt)
        abar_im = mag * jnp.sin(li * dt)
        den = lr * lr + li * li
        num_re = abar_re - 1.0
        coef_re = (num_re * lr + abar_im * li) / den
        coef_im = (abar_im * lr - num_re * li) / den
        bbar_re = coef_re[..., None] * b_re - coef_im[..., None] * b_im
        bbar_im = coef_re[..., None] * b_im + coef_im[..., None] * b_re
        pw_re = jnp.exp(lag * lr * dt) * jnp.cos(lag * li * dt)
        pw_im = jnp.exp(lag * lr * dt) * jnp.sin(lag * li * dt)
        ab_re = pw_re[..., None] * bbar_re - pw_im[..., None] * bbar_im
        ab_im = pw_re[..., None] * bbar_im + pw_im[..., None] * bbar_re
        cr, ci = c_re[direction].astype(F32), c_im[direction].astype(F32)
        kern.append(_einsum_f32('gcp,ngpd->ngcd', cr, ab_re[:ck]) - _einsum_f32('gcp,ngpd->ngcd', ci, ab_im[:ck]))
        order = jnp.arange(ck - 1, -1, -1) if direction == 0 else jnp.arange(ck)
        w_parts.append((ab_re[order], ab_im[order]))
        order = jnp.arange(1, ck + 1) if direction == 0 else jnp.arange(ck, 0, -1)
        a_r, a_i = pw_re[order], pw_im[order]
        v_from_re = jnp.einsum('gcp,jgp->gpjc', cr, a_r) - jnp.einsum('gcp,jgp->gpjc', ci, a_i)
        v_from_im = -(jnp.einsum('gcp,jgp->gpjc', cr, a_i) + jnp.einsum('gcp,jgp->gpjc', ci, a_r))
        v_parts.append((v_from_re, v_from_im))
        a_parts.append((pw_re[ck], pw_im[ck]))

    s_idx = jnp.arange(ck)[None, :, None]
    t_idx = jnp.arange(ck)[None, None, :]
    n_idx = jnp.arange(ck)[:, None, None]
    sel_f = (t_idx - s_idx == n_idx).astype(F32)
    sel_b = (s_idx - t_idx == n_idx).astype(F32)
    skip = jnp.eye(chans, dtype=F32)[None] * d_skip.reshape(groups, chans)[:, :, None]
    toep = (_einsum_f32('nst,ngcd->gsdtc', sel_f, kern[0]) + _einsum_f32('nst,ngcd->gsdtc', sel_b, kern[1])
            + jnp.einsum('st,gcd->gsdtc', jnp.eye(ck, dtype=F32), skip))
    t_mat = toep.reshape(groups, ck * chans, ck * chans)

    pad = LANES - state
    w_seg = [w_parts[0][0], w_parts[0][1], w_parts[1][0], w_parts[1][1]]
    w_stack = jnp.stack(w_seg, axis=0).transpose(2, 1, 4, 0, 3)
    w_mat = jnp.pad(w_stack, ((0, 0),) * 4 + ((0, pad),)).reshape(groups, ck * chans, 4 * LANES)

    v_seg = [v_parts[0][0], v_parts[0][1], v_parts[1][0], v_parts[1][1]]
    v_stack = jnp.stack(v_seg, axis=1).reshape(groups, 4, state, ck * chans)
    v_mat = jnp.pad(v_stack, ((0, 0), (0, 0), (0, pad), (0, 0))).reshape(groups, 4 * LANES, ck * chans)

    gpt = 2 * S5_TILE_PAIRS
    a_seg = jnp.stack([a_parts[0][0], a_parts[0][1], a_parts[1][0], a_parts[1][1]], axis=0)
    a_tbl = jnp.pad(a_seg, ((0, 0), (0, 0), (0, pad))).reshape(4, groups // gpt, gpt, LANES).transpose(1, 0, 2, 3)
    return t_mat.astype(BF16), w_mat.astype(BF16), v_mat.astype(BF16), a_tbl


def _s5c(proj3, col0, width, tables):
    bsz, seq, _ = proj3.shape
    t_mat, w_mat, v_mat, a_tbl = tables
    gpt = 2 * S5_TILE_PAIRS
    tiles = t_mat.shape[0] // gpt
    cols = t_mat.shape[1]
    sw = w_mat.shape[2]
    tc = width // tiles
    assert tc == LANES and cols == 2 * LANES
    n_chunks = seq // S5_CHUNK
    return pl.pallas_call(
        _s5c_kernel,
        grid=(tiles, bsz),
        in_specs=[
            pl.BlockSpec((1, seq, tc), lambda t, b: (b, 0, col0 // tc + t)),
            pl.BlockSpec((gpt, cols, cols), lambda t, b: (t, 0, 0)),
            pl.BlockSpec((gpt, cols, sw), lambda t, b: (t, 0, 0)),
            pl.BlockSpec((gpt, sw, cols), lambda t, b: (t, 0, 0)),
            pl.BlockSpec((1, 4, gpt, LANES), lambda t, b: (t, 0, 0, 0)),
        ],
        out_specs=pl.BlockSpec((1, seq, tc), lambda t, b: (b, 0, t)),
        out_shape=jax.ShapeDtypeStruct((bsz, seq, width), F32),
        scratch_shapes=[pltpu.VMEM((2 * gpt, n_chunks, LANES), F32),
                        pltpu.VMEM((n_chunks, gpt, sw), F32),
                        pltpu.VMEM((n_chunks, gpt, sw), F32)],
        compiler_params=_cparams(("arbitrary", "arbitrary")),
        name="s5",
    )(proj3, t_mat, w_mat, v_mat, a_tbl)


def _retention_kernel(q_ref, k_ref, v_ref, g_ref, cos_ref, sin_ref, lg_ref, gain_ref,
                      o_ref, qs_ref, ks_ref, vs_ref, decay_ref):
    seq = q_ref.shape[1]
    half = RET_HEAD_DIM // 2
    tq = RET_Q_TILE
    cos = cos_ref[...]
    sin = sin_ref[...]

    @pl.when(pl.program_id(1) == 0)
    def _():
        lg_fwd = lg_ref[0, 0:1, :]
        lg_bwd = lg_ref[0, 1:2, :]

        def fill(i, _):
            sl = pl.ds(pl.multiple_of(i * tq, tq), tq)
            t_idx = lax.broadcasted_iota(I32, (tq, seq), 0) + i * tq
            s_idx = lax.broadcasted_iota(I32, (tq, seq), 1)
            rel = (t_idx - s_idx).astype(F32)
            decay_ref[sl, :] = jnp.exp(jnp.where(rel >= 0.0, lg_fwd * rel, -lg_bwd * rel))
            return 0

        lax.fori_loop(0, seq // tq, fill, 0)

    def rot(t_ref, scale):
        t1 = t_ref[0, :, :half]
        t2 = t_ref[0, :, half:]
        return jnp.concatenate([(t1 * cos - t2 * sin) * scale, (t1 * sin + t2 * cos) * scale], axis=-1)

    qs_ref[...] = rot(q_ref, 1.0).astype(BF16)
    ks_ref[...] = rot(k_ref, RET_HEAD_DIM ** -0.5).astype(BF16)
    vs_ref[...] = v_ref[0].astype(BF16)
    gain = gain_ref[...]

    def q_tiles(i, _):
        sls = [pl.ds(pl.multiple_of((2 * i + a) * tq, tq), tq) for a in range(2)]
        scores = [_dot_nt(qs_ref[sl, :], ks_ref[...]) for sl in sls]
        ps = [(s * decay_ref[sl, :]).astype(BF16) for s, sl in zip(scores, sls)]
        outs = [jnp.dot(p, vs_ref[...], preferred_element_type=F32) for p in ps]
        for o, sl in zip(outs, sls):
            y = o * lax.rsqrt(jnp.mean(o * o, axis=-1, keepdims=True) + EPS) * gain
            g = g_ref[0, sl, :]
            o_ref[0, sl, :] = (y * (g * jax.nn.sigmoid(g))).astype(o_ref.dtype)
        return 0

    lax.fori_loop(0, seq // (2 * tq), q_tiles, 0)


def _retention(proj3, ret_gain, width):
    bsz, seq, _ = proj3.shape
    dh = RET_HEAD_DIM
    heads = width // dh
    inv_freq = ROPE_BASE ** (-jnp.arange(0, dh, 2, dtype=F32) / dh)
    ang = jnp.arange(seq, dtype=F32)[:, None] * inv_freq[None, :]
    cos, sin = jnp.cos(ang), jnp.sin(ang)
    log_gamma = jnp.log1p(-jnp.exp2(-5.0 - jnp.arange(heads, dtype=F32)))
    lg = jnp.stack([log_gamma, log_gamma[::-1]], axis=1)
    lg = jnp.broadcast_to(lg[:, :, None], (heads, 2, seq))

    def col(k):
        return pl.BlockSpec((1, seq, dh), lambda h, b: (b, 0, k * heads + h))

    return pl.pallas_call(
        _retention_kernel,
        grid=(heads, bsz),
        in_specs=[col(0), col(1), col(2), col(3),
                  pl.BlockSpec((seq, dh // 2), lambda h, b: (0, 0)),
                  pl.BlockSpec((seq, dh // 2), lambda h, b: (0, 0)),
                  pl.BlockSpec((1, 2, seq), lambda h, b: (h, 0, 0)),
                  pl.BlockSpec((1, dh), lambda h, b: (0, h))],
        out_specs=pl.BlockSpec((1, seq, dh), lambda h, b: (b, 0, h)),
        out_shape=jax.ShapeDtypeStruct((bsz, seq, width), BF16),
        scratch_shapes=[pltpu.VMEM((seq, dh), BF16)] * 3 + [pltpu.VMEM((seq, seq), F32)],
        compiler_params=_cparams(("arbitrary", "arbitrary")),
        name="retention",
    )(proj3, proj3, proj3, proj3, cos, sin, lg, ret_gain.reshape(1, width))


def _fnet_kernel(x_ref, cc_ref, sc_ref, cs_ref, ss_ref, o_ref, a1_ref, a2_ref):
    @pl.when(pl.program_id(1) == 0)
    def _():
        x = x_ref[0].astype(BF16)
        a1_ref[...] = jnp.dot(x, cc_ref[...], preferred_element_type=F32).astype(BF16)
        a2_ref[...] = jnp.dot(x, sc_ref[...], preferred_element_type=F32).astype(BF16)

    y = (jnp.dot(cs_ref[...], a1_ref[...], preferred_element_type=F32)
         - jnp.dot(ss_ref[...], a2_ref[...], preferred_element_type=F32))
    o_ref[0] = y.astype(o_ref.dtype)


def _dft_mats(n):
    idx = jnp.arange(n, dtype=I32)
    ang = (2.0 * math.pi / n) * ((idx[:, None] * idx[None, :]) % n).astype(F32)
    scale = n ** -0.5
    return jnp.cos(ang) * scale, jnp.sin(ang) * scale


def _fnet(proj3, col0, width):
    bsz, seq, _ = proj3.shape
    gw = width // FNET_GROUPS
    cs, ss = _dft_mats(seq)
    cg, sg = _dft_mats(gw)
    eye = jnp.eye(FNET_GROUPS, dtype=F32)
    cc = jnp.kron(eye, cg)
    sc = jnp.kron(eye, sg)
    tr = FNET_ROW_TILE
    return pl.pallas_call(
        _fnet_kernel,
        grid=(bsz, seq // tr),
        in_specs=[
            pl.BlockSpec((1, seq, width), lambda b, i: (b, 0, col0 // width)),
            pl.BlockSpec((width, width), lambda b, i: (0, 0)),
            pl.BlockSpec((width, width), lambda b, i: (0, 0)),
            pl.BlockSpec((tr, seq), lambda b, i: (i, 0)),
            pl.BlockSpec((tr, seq), lambda b, i: (i, 0)),
        ],
        out_specs=pl.BlockSpec((1, tr, width), lambda b, i: (b, i, 0)),
        out_shape=jax.ShapeDtypeStruct((bsz, seq, width), BF16),
        scratch_shapes=[pltpu.VMEM((seq, width), BF16)] * 2,
        compiler_params=_cparams(("arbitrary", "arbitrary")),
        name="fnet",
    )(proj3, cc.astype(BF16), sc.astype(BF16), cs.astype(BF16), ss.astype(BF16))


HIGH_HALF = -65536


def _slab_rows(s, rows, per_row):
    return pl.ds(s, rows, stride=per_row)


def _pack_rows(h, slab_ref):
    rows, d = h.shape
    half = d // 2
    per_row = half // LANES
    for s in range(per_row):
        lo = h[:, s * LANES:(s + 1) * LANES].astype(BF16).astype(F32)
        hi = h[:, half + s * LANES:half + (s + 1) * LANES].astype(BF16).astype(F32)
        lo_bits = lax.shift_right_logical(lax.bitcast_convert_type(lo, I32), jnp.int32(16))
        hi_bits = lax.bitcast_convert_type(hi, I32) & jnp.int32(HIGH_HALF)
        slab_ref[_slab_rows(s, rows, per_row), :] = hi_bits | lo_bits


def _unpack_rows(slab_ref, per_row):
    rows = slab_ref.shape[0] // per_row
    los, his = [], []
    for s in range(per_row):
        w = slab_ref[_slab_rows(s, rows, per_row), :]
        los.append(lax.bitcast_convert_type(lax.shift_left(w, jnp.int32(16)), F32))
        his.append(lax.bitcast_convert_type(w & jnp.int32(HIGH_HALF), F32))
    return jnp.concatenate(los, axis=-1).astype(BF16), jnp.concatenate(his, axis=-1).astype(BF16)


def _outproj_kernel(*refs, glu):
    if glu:
        (x_ref, a_ref, b_ref, gw_ref, gb_ref, wa_ref, wb_ref, gm_ref, gain_ref, sh_ref, sc_ref,
         rwh_ref, rwl_ref, rb_ref, x1_ref, h_ref, idx_ref, gate_ref, cnt_ref, base_ref) = refs
    else:
        (x_ref, a_ref, b_ref, wa_ref, wb_ref, gm_ref, gain_ref, sh_ref, sc_ref,
         rwh_ref, rwl_ref, rb_ref, x1_ref, h_ref, idx_ref, gate_ref, cnt_ref, base_ref) = refs

    @pl.when(pl.program_id(0) == 0)
    def _():
        base_ref[...] = jnp.zeros(base_ref.shape, F32)

    bm = b_ref[...]
    if glu:
        gl = jnp.dot(bm.astype(BF16), gw_ref[...], preferred_element_type=F32) + gb_ref[...]
        bm = (bm.astype(F32) * jax.nn.sigmoid(gl)).astype(BF16)
    y = (jnp.dot(a_ref[...], wa_ref[...], preferred_element_type=F32)
         + jnp.dot(bm, wb_ref[...], preferred_element_type=F32))
    x1 = x_ref[...] + gm_ref[0] * y
    x1_ref[...] = x1
    h = _norm_modulate(x1, gain_ref[...], sh_ref[0], sc_ref[0])
    _pack_rows(h, h_ref)

    h_hi = h.astype(BF16)
    h_lo = (h - h_hi.astype(F32)).astype(BF16)
    rows = h.shape[0]
    prod = jnp.dot(jnp.concatenate([h_hi, h_lo], axis=0),
                   jnp.concatenate([rwh_ref[...], rwl_ref[...]], axis=-1), preferred_element_type=F32)
    logits = (prod[:rows, :ROUTER_PAD] + prod[rows:, :ROUTER_PAD] + prod[:rows, ROUTER_PAD:]
              + rb_ref[...])
    lane = lax.broadcasted_iota(I32, logits.shape, 1)
    vals, idxs = [], []
    for _ in range(TOP_K):
        m = jnp.max(logits, axis=-1, keepdims=True)
        ik = jnp.min(jnp.where(logits == m, lane, ROUTER_PAD), axis=-1, keepdims=True)
        vals.append(m)
        idxs.append(ik)
        logits = jnp.where(lane == ik, -jnp.inf, logits)
    exps = [jnp.exp(v - vals[0]) for v in vals]
    denom = exps[0] + exps[1] + exps[2] + exps[3]

    tm = lane.shape[0]
    onehot = jnp.zeros(lane.shape, F32)
    for k in range(TOP_K):
        onehot = onehot + (lane == idxs[k]).astype(F32)
    before = (lax.broadcasted_iota(I32, (tm, tm), 1) < lax.broadcasted_iota(I32, (tm, tm), 0)).astype(BF16)
    count = jnp.dot(before, onehot.astype(BF16), preferred_element_type=F32) + base_ref[...]
    new_base = base_ref[...] + jnp.sum(onehot, axis=0, keepdims=True)
    base_ref[...] = new_base
    cnt_ref[...] = new_base.astype(I32)

    idx_out = jnp.zeros(lane.shape, I32)
    gate_out = jnp.zeros(lane.shape, F32)
    for k in range(TOP_K):
        rank = jnp.sum(jnp.where(lane == idxs[k], count, 0.0), axis=-1, keepdims=True).astype(I32)
        idx_out = jnp.where(lane == k, idxs[k], idx_out)
        idx_out = jnp.where(lane == TOP_K + k, rank, idx_out)
        gate_out = jnp.where(lane == k, exps[k] / denom, gate_out)
    idx_ref[...] = idx_out
    gate_ref[...] = gate_out


def _out_proj(x2d, mix_a, mix_b, w_out, g_mix, gain, shift, scale, router_w, router_b, seq,
              glu_w=None, glu_b=None):
    t, d = x2d.shape
    wa_rows = mix_a.shape[1]
    wb_rows = mix_b.shape[1]
    tm = 256
    per_seq = seq // tm
    n_exp = router_w.shape[1]
    rw = jnp.zeros((d, ROUTER_PAD), F32).at[:, :n_exp].set(router_w)
    rw_hi = rw.astype(BF16)
    rw_lo = (rw - rw_hi.astype(F32)).astype(BF16)
    rb = jnp.full((1, ROUTER_PAD), NEG_BIG, F32).at[0, :n_exp].set(router_b)
    w_bf = w_out.astype(BF16)
    glu = glu_w is not None

    def rows(width):
        return pl.BlockSpec((tm, width), lambda i: (i, 0))

    def full(r, c):
        return pl.BlockSpec((r, c), lambda i: (0, 0))

    def per_batch():
        return pl.BlockSpec((1, 1, d), lambda i: (i // per_seq, 0, 0))

    in_specs = [rows(d), rows(wa_rows), rows(wb_rows)]
    args = [x2d, mix_a, mix_b]
    if glu:
        in_specs += [full(wb_rows, wb_rows), full(1, wb_rows)]
        args += [glu_w.astype(BF16), glu_b.reshape(1, wb_rows)]
    in_specs += [pl.BlockSpec((wa_rows, d), lambda i: (0, 0)),
                 pl.BlockSpec((wb_rows, d), lambda i: (wa_rows // wb_rows, 0)),
                 per_batch(), full(1, d), per_batch(), per_batch(),
                 full(d, ROUTER_PAD), full(d, ROUTER_PAD), full(1, ROUTER_PAD)]
    args += [w_bf, w_bf, g_mix, gain.reshape(1, d), shift, scale, rw_hi, rw_lo, rb]
    return pl.pallas_call(
        functools.partial(_outproj_kernel, glu=glu),
        grid=(t // tm,),
        in_specs=in_specs,
        out_specs=[rows(d), pl.BlockSpec((tm * (d // (2 * LANES)), LANES), lambda i: (i, 0)),
                   rows(ROUTER_PAD), rows(ROUTER_PAD), full(1, ROUTER_PAD)],
        out_shape=[jax.ShapeDtypeStruct((t, d), F32), jax.ShapeDtypeStruct((t * (d // (2 * LANES)), LANES), I32),
                   jax.ShapeDtypeStruct((t, ROUTER_PAD), I32), jax.ShapeDtypeStruct((t, ROUTER_PAD), F32),
                   jax.ShapeDtypeStruct((1, ROUTER_PAD), I32)],
        scratch_shapes=[pltpu.VMEM((1, ROUTER_PAD), F32)],
        compiler_params=_cparams(("arbitrary",)),
        name="out_proj_glu" if glu else "out_proj",
    )(*args)


def _routing_tables(top_idx, rank, counts, n_tok):
    n_assign = n_tok * TOP_K
    r = MOE_ROWS
    n_sb = n_assign // (2 * r) + N_EXPERTS
    n_rb = 2 * n_sb
    experts = jnp.arange(N_EXPERTS, dtype=I32)
    used = (counts + r - 1) // r
    nsb = (used + 1) // 2
    sb_end = jnp.cumsum(nsb)
    sb_start = sb_end - nsb
    blk_start = 2 * sb_start
    row_start = jnp.sum(jnp.where(top_idx[:, :, None] == experts[None, None, :],
                                  (blk_start * r)[None, None, :], 0), axis=-1)
    dest = (row_start + rank).reshape(-1).astype(I32)
    rb = jnp.arange(n_rb, dtype=I32)
    rb_e = jnp.sum((sb_end[None, :] <= (rb // 2)[:, None]).astype(I32), axis=1)
    in_range = rb_e < N_EXPERTS
    rb_e = jnp.minimum(rb_e, N_EXPERTS - 1).astype(I32)
    rb_valid = (in_range & (rb - blk_start[rb_e] < used[rb_e])).astype(I32)
    tail = jnp.where(used > 0, blk_start + used - 1, -1)
    empty = jnp.where(2 * nsb > used, blk_start + used, -1)
    spare = 2 * sb_end[-1] + jnp.arange(2 * N_EXPERTS, dtype=I32)
    spare = jnp.where(spare < n_rb, spare, -1)
    zero_blocks = jnp.concatenate([tail, empty, spare]).astype(I32)
    return dict(dest=dest, rb_e=rb_e, rb_valid=rb_valid, zero_blocks=zero_blocks,
                n_valid_sb=sb_end[-1], nsb=nsb.astype(I32), sb_start=sb_start.astype(I32),
                n_sb=n_sb, n_rb=n_rb, n_slots=n_rb * r)


def _dispatch_kernel(dest_ref, zb_ref, h_ref, xs_hbm, zero_ref, zsem_ref, sem_ref):
    i = pl.program_id(0)
    td = DISPATCH_TOKENS
    sr = h_ref.shape[0] // td
    blk = MOE_ROWS * sr

    def slab(ref, row):
        return ref.at[pl.ds(pl.multiple_of(row * sr, sr), sr), :]

    @pl.when(i == 0)
    def _():
        zero_ref[...] = jnp.zeros(zero_ref.shape, zero_ref.dtype)

        def zero_copy(z):
            row0 = pl.multiple_of(zb_ref[z] * blk, blk)
            return pltpu.make_async_copy(zero_ref, xs_hbm.at[pl.ds(row0, blk), :], zsem_ref.at[0])

        def start(z, _):
            @pl.when(zb_ref[z] >= 0)
            def _():
                zero_copy(z).start()
            return 0

        def wait(z, _):
            @pl.when(zb_ref[z] >= 0)
            def _():
                zero_copy(z).wait()
            return 0

        lax.fori_loop(0, zb_ref.shape[0], start, 0)
        lax.fori_loop(0, zb_ref.shape[0], wait, 0)

    def group(g, _):
        for s in range(DMA_UNROLL):
            t = g * DMA_UNROLL + s
            for k in range(TOP_K):
                dst = dest_ref[(i * td + t) * TOP_K + k]
                pltpu.make_async_copy(slab(h_ref, t), slab(xs_hbm, dst), sem_ref.at[0]).start(priority=k % 2)
        return 0

    lax.fori_loop(0, td // DMA_UNROLL, group, 0)
    for _ in range(TOP_K):
        pltpu.make_async_copy(h_ref, xs_hbm.at[pl.ds(0, td * sr), :], sem_ref.at[0]).wait()


def _moe_dispatch(h_slabs, n_tok, rt):
    sr = h_slabs.shape[0] // n_tok
    td = DISPATCH_TOKENS
    grid_spec = pltpu.PrefetchScalarGridSpec(
        num_scalar_prefetch=2,
        grid=(n_tok // td,),
        in_specs=[pl.BlockSpec((td * sr, LANES), lambda i, dst, zb: (i, 0))],
        out_specs=pl.BlockSpec(memory_space=pl.ANY),
        scratch_shapes=[pltpu.VMEM((MOE_ROWS * sr, LANES), h_slabs.dtype), pltpu.SemaphoreType.DMA((1,)),
                        pltpu.SemaphoreType.DMA((1,))],
    )
    return pl.pallas_call(
        _dispatch_kernel,
        grid_spec=grid_spec,
        out_shape=jax.ShapeDtypeStruct((rt['n_slots'] * sr, LANES), h_slabs.dtype),
        compiler_params=_cparams(("arbitrary",)),
        name="moe_dispatch",
    )(rt['dest'], rt['zero_blocks'], h_slabs)


def _cast_rows(src_ref, dst_ref, chunk=256):
    chunk = min(chunk, dst_ref.shape[0])
    n = dst_ref.shape[0] // chunk

    def body(c, _):
        sl = pl.ds(pl.multiple_of(c * chunk, chunk), chunk)
        dst_ref[sl, :] = src_ref[sl, :].astype(dst_ref.dtype)
        return 0

    lax.fori_loop(0, n, body, 0)


def _stagger_tables(group, cols, parts):
    n = group.shape[0]
    idx = jnp.arange(n, dtype=I32)
    is_first = jnp.concatenate([jnp.ones((1,), bool), group[1:] != group[:-1]])
    first_cur = lax.cummax(jnp.where(is_first, idx, 0))
    nxt = jnp.where(is_first, idx, n)
    first_next = jnp.concatenate([lax.cummin(nxt[::-1])[::-1][1:], jnp.full((1,), n, I32)])
    out = [[] for _ in cols]
    for p in range(parts):
        switch = (idx >= jnp.maximum(first_next - p, first_cur + 1)) & (first_next < n)
        eff = jnp.where(switch, first_next, idx)
        for k, c in enumerate(cols):
            out[k].append(c[eff])
    return [jnp.stack(o) for o in out]


def _moe_up_kernel(sb_ref, j_ref, e_ref, first_ref, va_ref, vb_ref, pe_ref, pj_ref, x_ref, *refs):
    parts = MOE_W_PARTS
    wg_refs = refs[:parts]
    wu_refs = refs[parts:2 * parts]
    bg_ref, bu_ref, o_ref, wgs_ref, wus_ref = refs[2 * parts:]
    i = pl.program_id(0)
    rows = wgs_ref.shape[0] // parts
    r = MOE_ROWS
    sr = x_ref.shape[0] // (2 * r)

    @pl.when(first_ref[i] == 1)
    def _():
        for p in range(parts):
            _cast_rows(wg_refs[p].at[0, 0], wgs_ref.at[pl.ds(p * rows, rows)])
            _cast_rows(wu_refs[p].at[0, 0], wus_ref.at[pl.ds(p * rows, rows)])

    def compute(slabs):
        x_lo, x_hi = _unpack_rows(slabs, sr)
        half = wgs_ref.shape[0] // 2

        def proj(ws_ref, b_ref):
            return (jnp.dot(x_lo, ws_ref[:half, :], preferred_element_type=F32)
                    + jnp.dot(x_hi, ws_ref[half:, :], preferred_element_type=F32) + b_ref[0])

        gate = jnp.minimum(proj(wgs_ref, bg_ref), SWIGLU_LIMIT)
        up = jnp.clip(proj(wus_ref, bu_ref), -SWIGLU_LIMIT, SWIGLU_LIMIT)
        return ((up + 1.0) * gate * jax.nn.sigmoid(SWIGLU_ALPHA * gate)).astype(o_ref.dtype)

    va = va_ref[i] == 1
    vb = vb_ref[i] == 1

    @pl.when(vb)
    def _():
        o_ref[...] = compute(x_ref)

    @pl.when(va & jnp.logical_not(vb))
    def _():
        o_ref[pl.ds(0, r), :] = compute(x_ref.at[pl.ds(0, r * sr), :])
        o_ref[pl.ds(r, r), :] = jnp.zeros((r, o_ref.shape[1]), o_ref.dtype)

    @pl.when(jnp.logical_not(va))
    def _():
        o_ref[...] = jnp.zeros(o_ref.shape, o_ref.dtype)


def _moe_items(rt, nj, n_exp):
    n_sb = rt['n_sb']
    n_items = n_sb * nj
    sb = jnp.arange(n_sb, dtype=I32)
    e_of = rt['rb_e'][2 * sb]
    nsb = rt['nsb'].at[n_exp - 1].add(n_sb - rt['n_valid_sb'])
    q = sb - rt['sb_start'][e_of]
    pos = (nj * rt['sb_start'][e_of][:, None]
           + jnp.arange(nj, dtype=I32)[None, :] * nsb[e_of][:, None] + q[:, None]).reshape(-1)
    sb2 = jnp.broadcast_to(sb[:, None], (n_sb, nj)).reshape(-1)
    j2 = jnp.broadcast_to(jnp.arange(nj, dtype=I32)[None, :], (n_sb, nj)).reshape(-1)
    it_sb = jnp.zeros((n_items,), I32).at[pos].set(sb2)
    it_j = jnp.zeros((n_items,), I32).at[pos].set(j2)
    it_va = rt['rb_valid'][2 * it_sb]
    it_vb = rt['rb_valid'][2 * it_sb + 1]
    it_e = e_of[it_sb]
    prev_e = jnp.concatenate([jnp.full((1,), -1, I32), it_e[:-1]])
    prev_j = jnp.concatenate([jnp.full((1,), -1, I32), it_j[:-1]])
    it_first = ((it_e != prev_e) | (it_j != prev_j)).astype(I32)
    part_e, part_j = _stagger_tables(jnp.cumsum(it_first), [it_e, it_j], MOE_W_PARTS)
    return n_items, (it_sb, it_j, it_e, it_first, it_va, it_vb, part_e, part_j)


def _moe_up(x_sorted, w_gu, b_gu, layer, rt):
    _, n_exp, d, two_f = w_gu.shape
    n_slots = rt['n_slots']
    sr = x_sorted.shape[0] // n_slots
    f = two_f // 2
    r, tn = MOE_ROWS, MOE_UP_TN
    nj = f // tn
    parts = MOE_W_PARTS
    n_items, tables = _moe_items(rt, nj, n_exp)

    def w_spec(p, col0):
        return pl.BlockSpec((1, 1, d // parts, tn),
                            lambda i, sbt, jt, et, ft, va, vb, pe, pj: (layer, pe[p, i], p, col0 + pj[p, i]))

    grid_spec = pltpu.PrefetchScalarGridSpec(
        num_scalar_prefetch=8,
        grid=(n_items,),
        in_specs=([pl.BlockSpec((2 * r * sr, LANES), lambda i, sbt, jt, et, ft, va, vb, pe, pj: (sbt[i], 0))]
                  + [w_spec(p, 0) for p in range(parts)] + [w_spec(p, nj) for p in range(parts)]
                  + [pl.BlockSpec((1, 1, tn), lambda i, sbt, jt, et, ft, va, vb, pe, pj: (et[i], 0, jt[i])),
                     pl.BlockSpec((1, 1, tn), lambda i, sbt, jt, et, ft, va, vb, pe, pj: (et[i], 0, nj + jt[i]))]),
        out_specs=pl.BlockSpec((2 * r, tn), lambda i, sbt, jt, et, ft, va, vb, pe, pj: (sbt[i], jt[i])),
        scratch_shapes=[pltpu.VMEM((d, tn), BF16), pltpu.VMEM((d, tn), BF16)],
    )
    b3 = b_gu.reshape(n_exp, 1, two_f)
    return pl.pallas_call(
        _moe_up_kernel,
        grid_spec=grid_spec,
        out_shape=jax.ShapeDtypeStruct((n_slots, f), BF16),
        compiler_params=_cparams(("arbitrary",)),
        name="moe_up",
    )(*tables, x_sorted, *([w_gu] * (2 * parts)), b3, b3)


def _moe_dn_kernel(sb_ref, j_ref, e_ref, first_ref, va_ref, vb_ref, pe_ref, pj_ref, a_ref, *refs):
    parts = MOE_W_PARTS
    w_refs = refs[:parts]
    b_ref, o_ref, ws_ref = refs[parts:]
    i = pl.program_id(0)
    rows = ws_ref.shape[0] // parts
    r = MOE_ROWS

    @pl.when(first_ref[i] == 1)
    def _():
        for p in range(parts):
            _cast_rows(w_refs[p].at[0, 0], ws_ref.at[pl.ds(p * rows, rows)])

    va = va_ref[i] == 1
    vb = vb_ref[i] == 1

    @pl.when(vb)
    def _():
        o_ref[...] = jnp.dot(a_ref[...], ws_ref[...], preferred_element_type=F32) + b_ref[0]

    @pl.when(va & jnp.logical_not(vb))
    def _():
        o_ref[pl.ds(0, r), :] = (jnp.dot(a_ref[pl.ds(0, r), :], ws_ref[...], preferred_element_type=F32)
                                 + b_ref[0])
        o_ref[pl.ds(r, r), :] = jnp.zeros((r, o_ref.shape[1]), o_ref.dtype)

    @pl.when(jnp.logical_not(va))
    def _():
        o_ref[...] = jnp.zeros(o_ref.shape, o_ref.dtype)


def _moe_dn(act, w_dn, b_dn, layer, rt):
    n_slots, f = act.shape
    _, n_exp, _, d = w_dn.shape
    r, tn = MOE_ROWS, MOE_DN_TN
    nj = d // tn
    parts = MOE_W_PARTS
    n_items, tables = _moe_items(rt, nj, n_exp)

    def w_spec(p):
        return pl.BlockSpec((1, 1, f // parts, tn),
                            lambda i, sbt, jt, et, ft, va, vb, pe, pj: (layer, pe[p, i], p, pj[p, i]))

    grid_spec = pltpu.PrefetchScalarGridSpec(
        num_scalar_prefetch=8,
        grid=(n_items,),
        in_specs=([pl.BlockSpec((2 * r, f), lambda i, sbt, jt, et, ft, va, vb, pe, pj: (sbt[i], 0))]
                  + [w_spec(p) for p in range(parts)]
                  + [pl.BlockSpec((1, 1, tn), lambda i, sbt, jt, et, ft, va, vb, pe, pj: (et[i], 0, jt[i]))]),
        out_specs=pl.BlockSpec((2 * r, tn), lambda i, sbt, jt, et, ft, va, vb, pe, pj: (sbt[i], jt[i])),
        scratch_shapes=[pltpu.VMEM((f, tn), BF16)],
    )
    return pl.pallas_call(
        _moe_dn_kernel,
        grid_spec=grid_spec,
        out_shape=jax.ShapeDtypeStruct((n_slots, d), F32),
        compiler_params=_cparams(("arbitrary",)),
        name="moe_dn",
    )(*tables, act, *([w_dn] * parts), b_dn.reshape(n_exp, 1, d))


def _combine_kernel(dest_ref, x_ref, gate_ref, g_ref, fg_ref, y_hbm, o_ref, buf_ref, sem_ref, *, final):
    i = pl.program_id(0)
    n_steps = pl.num_programs(0)
    tc = COMBINE_TOKENS

    def issue(step, slot):
        def group(g, _):
            for s in range(DMA_UNROLL):
                t = g * DMA_UNROLL + s
                for k in range(TOP_K):
                    src = dest_ref[(step * tc + t) * TOP_K + k]
                    pltpu.make_async_copy(y_hbm.at[pl.ds(src, 1), :], buf_ref.at[slot, k, pl.ds(t, 1), :],
                                          sem_ref.at[slot]).start(priority=k % 2)
            return 0

        lax.fori_loop(0, tc // DMA_UNROLL, group, 0)

    @pl.when(i == 0)
    def _():
        issue(0, 0)

    @pl.when(i + 1 < n_steps)
    def _():
        issue(i + 1, (i + 1) % 2)

    slot = i % 2
    for k in range(TOP_K):
        pltpu.make_async_copy(y_hbm.at[pl.ds(0, tc), :], buf_ref.at[slot, k], sem_ref.at[slot]).wait()
    gates = gate_ref[...]
    y = gates[:, 0:1] * buf_ref[slot, 0]
    for k in range(1, TOP_K):
        y = y + gates[:, k:k + 1] * buf_ref[slot, k]
    x2 = x_ref[...] + g_ref[0] * y
    if final:
        x2 = x2 * lax.rsqrt(jnp.mean(x2 * x2, axis=-1, keepdims=True) + EPS) * fg_ref[...]
    o_ref[...] = x2


def _combine(x1, y_sorted, gates, rt, g_ffn, final_gain, seq, final):
    t, d = x1.shape
    tc = COMBINE_TOKENS
    per_seq = seq // tc
    grid_spec = pltpu.PrefetchScalarGridSpec(
        num_scalar_prefetch=1,
        grid=(t // tc,),
        in_specs=[pl.BlockSpec((tc, d), lambda i, dst: (i, 0)),
                  pl.BlockSpec((tc, ROUTER_PAD), lambda i, dst: (i, 0)),
                  pl.BlockSpec((1, 1, d), lambda i, dst: (i // per_seq, 0, 0)),
                  pl.BlockSpec((1, d), lambda i, dst: (0, 0)),
                  pl.BlockSpec(memory_space=pl.ANY)],
        out_specs=pl.BlockSpec((tc, d), lambda i, dst: (i, 0)),
        scratch_shapes=[pltpu.VMEM((2, TOP_K, tc, d), F32), pltpu.SemaphoreType.DMA((2,))],
    )
    return pl.pallas_call(
        functools.partial(_combine_kernel, final=final),
        grid_spec=grid_spec,
        out_shape=jax.ShapeDtypeStruct((t, d), F32),
        compiler_params=_cparams(("arbitrary",)),
        name="combine_final" if final else "combine",
    )(rt['dest'], x1, gates, g_ffn, final_gain.reshape(1, d), y_sorted)


def _moe(x1, h, idx, gates, counts, w_gu, b_gu, w_dn, b_dn, layer, g_ffn, final_gain, seq, final):
    n_tok = x1.shape[0]
    rt = _routing_tables(idx[:, :TOP_K], idx[:, TOP_K:2 * TOP_K], counts[0, :N_EXPERTS], n_tok)
    x_sorted = _moe_dispatch(h, n_tok, rt)
    act = _moe_up(x_sorted, w_gu, b_gu, layer, rt)
    y_sorted = _moe_dn(act, w_dn, b_dn, layer, rt)
    return _combine(x1, y_sorted, gates, rt, g_ffn, final_gain, seq, final)


def kernel(x, c, ada_w, ada_b, norm_mix_gain, norm_ffn_gain, ab_w_in, ab_w_out, hg_lb_logits, hg_norm_gain, s5_lam_re, s5_lam_im, s5_log_dt, s5_b_re, s5_b_im, s5_c_re, s5_c_im, s5_d, s5_glu_w, s5_glu_b, cd_w_in, cd_w_out, ret_norm_gain, router_w, router_b, moe_w_gu, moe_b_gu, moe_w_dn, moe_b_dn, final_gain):
    bsz, seq, d = x.shape
    depth = ada_w.shape[0]
    n_tok = bsz * seq
    hg_width = hg_lb_logits.shape[1]
    s5_width = s5_glu_w.shape[1]
    ret_width = ret_norm_gain.shape[1]
    fnet_width = cd_w_out.shape[1] - ret_width

    lower_bounds = jnp.cumsum(jax.nn.softmax(hg_lb_logits.astype(F32), axis=0), axis=0)
    mod = _ada_mod(c, ada_w, ada_b)
    xr = x.reshape(n_tok, d)
    for layer in range(depth):
        sh_mix, sc_mix, g_mix, sh_ffn, sc_ffn, g_ffn = (
            mod[layer, :, k * d:(k + 1) * d].reshape(bsz, 1, d) for k in range(6))
        j = layer // 2
        if layer % 2 == 0:
            proj = _in_proj(xr, norm_mix_gain[layer], sh_mix, sc_mix, ab_w_in[j].astype(BF16), seq)
            proj3 = proj.reshape(bsz, seq, proj.shape[1])
            mix_a = _hgrn2(proj3, lower_bounds[j], hg_norm_gain[j], hg_width)
            tables = _s5c_tables(s5_lam_re[j], s5_lam_im[j], s5_log_dt[j], s5_b_re[j], s5_b_im[j],
                                 s5_c_re[j], s5_c_im[j], s5_d[j])
            mix_b = _s5c(proj3, 5 * hg_width, s5_width, tables)
            x1, h, idx, gate, counts = _out_proj(
                xr, mix_a.reshape(n_tok, hg_width), mix_b.reshape(n_tok, s5_width), ab_w_out[j],
                g_mix, norm_ffn_gain[layer], sh_ffn, sc_ffn, router_w[layer], router_b[layer], seq,
                glu_w=s5_glu_w[j], glu_b=s5_glu_b[j])
        else:
            proj = _in_proj(xr, norm_mix_gain[layer], sh_mix, sc_mix, cd_w_in[j].astype(BF16), seq)
            proj3 = proj.reshape(bsz, seq, proj.shape[1])
            mix_a = _retention(proj3, ret_norm_gain[j], ret_width)
            mix_b = _fnet(proj3, 4 * ret_width, fnet_width)
            x1, h, idx, gate, counts = _out_proj(
                xr, mix_a.reshape(n_tok, ret_width), mix_b.reshape(n_tok, fnet_width), cd_w_out[j],
                g_mix, norm_ffn_gain[layer], sh_ffn, sc_ffn, router_w[layer], router_b[layer], seq)
        xr = _moe(x1, h, idx, gate, counts, moe_w_gu, moe_b_gu[layer], moe_w_dn, moe_b_dn[layer],
                  layer, g_ffn, final_gain, seq, final=(layer == depth - 1))
    return xr.reshape(bsz, seq, d)
```
